```python
import jax, jax.numpy as jnp
from jax import lax
import numpy as np

D_MODEL = 1024
BATCH = 32
SEQ = 256
DEPTH = 1
DEC_BATCH = 4
DEC_SEQ = 1024
PAST_LEN = 256

GRID_W = 64
N_DIR = 2
HG_WIDTH = 512
HG_HEADS = 4
HG_DK = 128
HG_DV = HG_WIDTH // HG_HEADS
HG_CHUNK = 16
LRU_WIDTH = 512
LRU_BLOCKS = 8
LRU_BLOCK = LRU_WIDTH // LRU_BLOCKS
LRU_C = 8.0
CONV_W = 4
CONV_LEFT = 2
D_FF = 4 * D_MODEL
EPS = 1e-6
POS_BASE = 10000.0
HG_QF = N_DIR * HG_HEADS * HG_DK
SPLITS = (HG_QF, 2 * HG_QF, 2 * HG_QF + HG_WIDTH, 2 * HG_QF + 2 * HG_WIDTH,
          2 * HG_QF + 2 * HG_WIDTH + LRU_WIDTH)
IN_COLS = 2 * HG_QF + 2 * HG_WIDTH + 2 * LRU_WIDTH
MIX_WIDTH = HG_WIDTH + LRU_WIDTH

kernel_name = "hybrid_hgrn2_rglru_diffusion_step"


def rms_norm(x, gain):
    xf = x.astype(jnp.float32)
    y = xf * lax.rsqrt(jnp.mean(xf * xf, axis=-1, keepdims=True) + EPS)
    return (y * gain.astype(jnp.float32)).astype(x.dtype)


def grid_pos_embed(n_tok, dtype):
    rows = n_tok // GRID_W
    rr, cc = jnp.meshgrid(jnp.arange(rows), jnp.arange(GRID_W), indexing='ij')
    quarter = D_MODEL // 4
    omega = 1.0 / (POS_BASE ** (jnp.arange(quarter, dtype=jnp.float32) / quarter))

    def emb(pos):
        ang = pos.reshape(-1, 1).astype(jnp.float32) * omega
        return jnp.concatenate([jnp.sin(ang), jnp.cos(ang)], axis=-1)

    return jnp.concatenate([emb(rr), emb(cc)], axis=-1).astype(dtype)


def hgrn2_chunkwise(q, logf, v, s0):
    B, T, H, K = q.shape
    C = HG_CHUNK
    N = T // C
    k = -jnp.expm1(logf)
    qc, kc, vc, lf = [a.reshape(B, N, C, H, -1) for a in (q, k, v, logf)]
    b = jnp.cumsum(lf, axis=2)
    mask = jnp.tril(jnp.ones((C, C), dtype=bool))[None, None, :, :, None, None]
    rel = b[:, :, :, None] - b[:, :, None, :]
    decay = jnp.where(mask, jnp.exp(jnp.minimum(rel, 0.0)), 0.0)
    attn = jnp.einsum('bnthk,bnshk,bntshk->bnhts', qc, kc, decay)
    o_intra = jnp.einsum('bnhts,bnshv->bnthv', attn, vc)
    b_last = b[:, :, -1]
    k_dec = kc * jnp.exp(b_last[:, :, None] - b)
    d_state = jnp.einsum('bnshk,bnshv->bnhkv', k_dec, vc)
    chunk_decay = jnp.exp(b_last)

    def step(s, inp):
        dec, ds = inp
        return dec[..., None] * s + ds, s

    s_fin, s_in = lax.scan(step, s0, (jnp.moveaxis(chunk_decay, 1, 0),
                                      jnp.moveaxis(d_state, 1, 0)))
    s_in = jnp.moveaxis(s_in, 0, 1)
    o_inter = jnp.einsum('bnthk,bnhkv->bnthv', qc * jnp.exp(b), s_in)
    return (o_intra + o_inter).reshape(B, T, H, -1), s_fin


def block_diag(x, w, bias):
    xb = x.reshape(x.shape[0], x.shape[1], LRU_BLOCKS, LRU_BLOCK)
    return jnp.einsum('btnd,nde->btne', xb, w).reshape(x.shape) + bias


def rglru_scan(x, wa, ba, wx, bx, lam, h0):
    r = jax.nn.sigmoid(block_diag(x, wa, ba))
    i = jax.nn.sigmoid(block_diag(x, wx, bx))
    log_a = -LRU_C * r * jax.nn.softplus(-lam)
    a = jnp.exp(log_a)
    u = jnp.sqrt(-jnp.expm1(2.0 * log_a)) * (i * x)
    u = u.at[:, 0].add(a[:, 0] * h0)

    def comb(left, right):
        a1, b1 = left
        a2, b2 = right
        return a1 * a2, a2 * b1 + b2

    _, h = lax.associative_scan(comb, (a, u), axis=1)
    return h, h[:, -1]


def dwconv_centred(x, w, bias):
    T = x.shape[1]
    xp = jnp.pad(x, ((0, 0), (CONV_LEFT, CONV_W - 1 - CONV_LEFT), (0, 0)))
    out = bias
    for j in range(CONV_W):
        out = out + xp[:, j:j + T] * w[j]
    return out


def mixer(h, p, l, hg_s0, lru_s0):
    f32 = jnp.float32
    B, T, _ = h.shape
    proj = h @ p['w_in'][l]
    hq, hf, hi, hgate, lx, lg = jnp.split(proj, SPLITS, axis=-1)
    q = jax.nn.silu(hq.astype(f32)).reshape(B, T, N_DIR, HG_HEADS, HG_DK)
    lb = jnp.cumsum(jax.nn.softmax(p['hg_lb_logits'].astype(f32), axis=0), axis=0)[l]
    fz = hf.astype(f32).reshape(B, T, N_DIR, HG_HEADS, HG_DK)
    logf = jnp.log(lb + (1.0 - lb) * jax.nn.sigmoid(fz))
    v = hi.astype(f32).reshape(B, T, HG_HEADS, HG_DV)
    o_f, s_f = hgrn2_chunkwise(q[:, :, 0], logf[:, :, 0], v, hg_s0[:, 0])
    o_b, s_b = hgrn2_chunkwise(jnp.flip(q[:, :, 1], 1), jnp.flip(logf[:, :, 1], 1),
                               jnp.flip(v, 1), hg_s0[:, 1])
    o = o_f + jnp.flip(o_b, 1)
    o = rms_norm(o, p['hg_norm_gain'][l]) * jax.nn.silu(hgate.astype(f32)).reshape(B, T, HG_HEADS, HG_DV)
    hg_out = o.reshape(B, T, HG_WIDTH)
    xc = dwconv_centred(lx.astype(f32), p['conv_w'][l].astype(f32), p['conv_b'][l].astype(f32))
    h_f, e_f = rglru_scan(xc, p['lru_wa'][l, 0].astype(f32), p['lru_ba'][l, 0].astype(f32),
                          p['lru_wx'][l, 0].astype(f32), p['lru_bx'][l, 0].astype(f32),
                          p['lru_lambda'][l, 0].astype(f32), lru_s0[:, 0])
    h_b, e_b = rglru_scan(jnp.flip(xc, 1), p['lru_wa'][l, 1].astype(f32), p['lru_ba'][l, 1].astype(f32),
                          p['lru_wx'][l, 1].astype(f32), p['lru_bx'][l, 1].astype(f32),
                          p['lru_lambda'][l, 1].astype(f32), lru_s0[:, 1])
    lru_out = (h_f + jnp.flip(h_b, 1)) * jax.nn.gelu(lg.astype(f32))
    mix = jnp.concatenate([hg_out, lru_out], axis=-1).astype(h.dtype) @ p['w_out'][l]
    return mix, jnp.stack([s_f, s_b], axis=1), jnp.stack([e_f, e_b], axis=1)


def run_trunk(x, cond, p, hg_init, lru_init):
    hg_fin, lru_fin = [], []
    for l in range(DEPTH):
        m = (jax.nn.silu(cond.astype(jnp.float32)) @ p['w_ada'][l].astype(jnp.float32)
             + p['b_ada'][l].astype(jnp.float32))[:, None, :].astype(x.dtype)
        sh1, sc1, g1, sh2, sc2, g2 = jnp.split(m, 6, axis=-1)
        h = rms_norm(x, p['norm1_gain'][l]) * (1.0 + sc1) + sh1
        mix, s_hg, s_lru = mixer(h, p, l, hg_init[:, l].astype(jnp.float32),
                                 lru_init[:, l].astype(jnp.float32))
        x = x + g1 * mix
        h = rms_norm(x, p['norm2_gain'][l]) * (1.0 + sc2) + sh2
        ff = jnp.square(jax.nn.relu(h @ p['w_ff1'][l])) @ p['w_ff2'][l]
        x = x + g2 * ff
        hg_fin.append(s_hg)
        lru_fin.append(s_lru)
    y = rms_norm(x, p['final_gain'])
    return y, jnp.stack(hg_fin, axis=1), jnp.stack(lru_fin, axis=1)


def setup_inputs(seed: int = 0) -> dict:
    key = jax.random.key(seed)
    ks = jax.random.split(key, 26)

    def nrm(k, shape, s):
        return jax.random.normal(k, shape, jnp.float32) * s

    a8 = jax.random.uniform(ks[20], (DEPTH, N_DIR, LRU_WIDTH), jnp.float32, minval=0.9, maxval=0.999)
    a = a8 ** (1.0 / LRU_C)
    lam = jnp.log(a) - jnp.log1p(-a)
    return {
        'x_prompt': nrm(ks[0], (BATCH, SEQ, D_MODEL), 1.0),
        'x_sample': nrm(ks[1], (DEC_BATCH, DEC_SEQ, D_MODEL), 1.0),
        'c': nrm(ks[2], (DEC_BATCH, D_MODEL), 1.0),
        'state_hgrn': nrm(ks[3], (DEC_BATCH, DEPTH, N_DIR, HG_HEADS, HG_DK, HG_DV), 0.3),
        'state_rglru': nrm(ks[4], (DEC_BATCH, DEPTH, N_DIR, LRU_WIDTH), 0.5),
        'c_ctx': nrm(ks[5], (D_MODEL,), 1.0),
        'w_ada': nrm(ks[6], (DEPTH, D_MODEL, 6 * D_MODEL), 0.5 * D_MODEL ** -0.5),
        'b_ada': nrm(ks[7], (DEPTH, 6 * D_MODEL), 0.02),
        'norm1_gain': 1.0 + nrm(ks[8], (DEPTH, D_MODEL), 0.02),
        'norm2_gain': 1.0 + nrm(ks[9], (DEPTH, D_MODEL), 0.02),
        'w_in': nrm(ks[10], (DEPTH, D_MODEL, IN_COLS), D_MODEL ** -0.5),
        'hg_lb_logits': nrm(ks[11], (DEPTH + 1, N_DIR, HG_HEADS, HG_DK), 0.5),
        'hg_norm_gain': 1.0 + nrm(ks[12], (DEPTH, HG_HEADS, HG_DV), 0.02),
        'conv_w': nrm(ks[13], (DEPTH, CONV_W, LRU_WIDTH), CONV_W ** -0.5),
        'conv_b': nrm(ks[14], (DEPTH, LRU_WIDTH), 0.02),
        'lru_wa': nrm(ks[15], (DEPTH, N_DIR, LRU_BLOCKS, LRU_BLOCK, LRU_BLOCK), LRU_BLOCK ** -0.5),
        'lru_ba': nrm(ks[16], (DEPTH, N_DIR, LRU_WIDTH), 0.02),
        'lru_wx': nrm(ks[17], (DEPTH, N_DIR, LRU_BLOCKS, LRU_BLOCK, LRU_BLOCK), LRU_BLOCK ** -0.5),
        'lru_bx': nrm(ks[18], (DEPTH, N_DIR, LRU_WIDTH), 0.02),
        'lru_lambda': lam,
        'w_out': nrm(ks[21], (DEPTH, MIX_WIDTH, D_MODEL), MIX_WIDTH ** -0.5),
        'w_ff1': nrm(ks[22], (DEPTH, D_MODEL, D_FF), D_MODEL ** -0.5),
        'w_ff2': nrm(ks[23], (DEPTH, D_FF, D_MODEL), D_FF ** -0.5),
        'final_gain': 1.0 + nrm(ks[24], (D_MODEL,), 0.02),
    }


def reference(x_prompt, x_sample, c, state_hgrn, state_rglru, c_ctx, w_ada, b_ada,
              norm1_gain, norm2_gain, w_in, hg_lb_logits, hg_norm_gain, conv_w, conv_b,
              lru_wa, lru_ba, lru_wx, lru_bx, lru_lambda, w_out, w_ff1, w_ff2, final_gain):
    p = {'w_ada': w_ada, 'b_ada': b_ada, 'norm1_gain': norm1_gain, 'norm2_gain': norm2_gain,
         'w_in': w_in, 'hg_lb_logits': hg_lb_logits, 'hg_norm_gain': hg_norm_gain,
         'conv_w': conv_w, 'conv_b': conv_b, 'lru_wa': lru_wa, 'lru_ba': lru_ba,
         'lru_wx': lru_wx, 'lru_bx': lru_bx, 'lru_lambda': lru_lambda, 'w_out': w_out,
         'w_ff1': w_ff1, 'w_ff2': w_ff2, 'final_gain': final_gain}
    bp = x_prompt.shape[0]
    hg_zero = jnp.zeros((bp, DEPTH, N_DIR, HG_HEADS, HG_DK, HG_DV), jnp.float32)
    lru_zero = jnp.zeros((bp, DEPTH, N_DIR, LRU_WIDTH), jnp.float32)
    y_prompt, hg_ctx, lru_ctx = run_trunk(x_prompt, c_ctx[None, :], p, hg_zero, lru_zero)
    new_state_hgrn = hg_ctx.astype(x_prompt.dtype)
    new_state_rglru = lru_ctx.astype(x_prompt.dtype)
    xs = x_sample + grid_pos_embed(x_sample.shape[1], x_sample.dtype)[None]
    y_sample, _, _ = run_trunk(xs, c, p, state_hgrn, state_rglru)
    return (y_prompt, y_sample, new_state_hgrn, new_state_rglru)
```

```python
import functools

import numpy as np
import jax
import jax.numpy as jnp
from jax import lax
from jax.experimental import pallas as pl
from jax.experimental.pallas import tpu as pltpu

F32 = jnp.float32
BF16 = jnp.bfloat16

D_MODEL = 1024
N_DIR = 2
HG_HEADS = 4
HG_DK = 128
HG_WIDTH = 512
LRU_WIDTH = 512
LRU_BLOCKS = 8
LRU_BLOCK = 64
LRU_C = 8.0
D_FF = 4096
IN_COLS = 4096
EPS = 1e-6
GRID_W = 64
POS_BASE = 10000.0

COL_Q = 0
COL_F = 1024
COL_V = 2048
COL_GATE = 2560
COL_LX = 3072
COL_LG = 3584

CHUNK = 128
LEVELS = (64, 32, 16, 8, 4, 2, 1)
ROW_TILE = 256
CONV_TILE = 64
SCAN_TILE = 32
LRU_GROUP = 256
FFN_TILE = 512
FF_CHUNK = 1024
MOD_TILE = 1024
VMEM_LIMIT = 58 * 1024 * 1024


def _sig(x):
    return jax.nn.sigmoid(x)


def _nt_dot(a, b):
    return lax.dot_general(a, b, (((1,), (1,)), ((), ())), preferred_element_type=F32)


def _level_tables():
    t = np.arange(CHUNK)[:, None]
    s = np.arange(CHUNK)[None, :]
    masks = np.zeros((N_DIR, len(LEVELS) + 1, CHUNK, CHUNK), np.float32)
    for li, h in enumerate(LEVELS):
        same = (t // (2 * h)) == (s // (2 * h))
        t_hi = (t // h) % 2 == 1
        s_hi = (s // h) % 2 == 1
        masks[0, li] = same & t_hi & ~s_hi
        masks[1, li] = same & ~t_hi & s_hi
    masks[:, len(LEVELS)] = (t == s)
    tri = np.stack([(s <= t), (s >= t)]).astype(np.float32)
    return masks, tri


def _mod_kernel(c_ref, w_ref, b_ref, o_ref):
    c = c_ref[...]
    a = (c * _sig(c)).astype(BF16)
    o_ref[...] = jnp.dot(a, w_ref[...].astype(BF16), preferred_element_type=F32) + b_ref[...]


def _modulation(cond8, w_ada, b_ada):
    n = w_ada.shape[1]
    return pl.pallas_call(
        _mod_kernel,
        grid=(n // MOD_TILE,),
        in_specs=[
            pl.BlockSpec((8, D_MODEL), lambda j: (0, 0)),
            pl.BlockSpec((D_MODEL, MOD_TILE), lambda j: (0, j)),
            pl.BlockSpec((1, MOD_TILE), lambda j: (0, j)),
        ],
        out_specs=pl.BlockSpec((8, MOD_TILE), lambda j: (0, j)),
        out_shape=jax.ShapeDtypeStruct((8, n), F32),
        compiler_params=pltpu.CompilerParams(dimension_semantics=("arbitrary",)),
        name="adaln_modulation",
    )(cond8, w_ada, b_ada)


def _mixer_kernel(*refs, T, has_pos, has_state, emit_state):
    it = iter(refs)
    x_ref = next(it)
    m_ref = next(it)
    if has_pos:
        pos_ref = next(it)
    n1g_ref = next(it)
    win_ref = next(it)
    lbl_ref = next(it)
    hgg_ref = next(it)
    masks_ref = next(it)
    tri_ref = next(it)
    convw_ref = next(it)
    convb_ref = next(it)
    wg_ref = next(it)
    bg_ref = next(it)
    lam_ref = next(it)
    wout_ref = next(it)
    if has_state:
        hs0_ref = next(it)
        ls0_ref = next(it)
    x1_ref = next(it)
    if emit_state:
        hs_ref = next(it)
        ls_ref = next(it)
    proj = next(it)
    lxp = next(it)
    mixin = next(it)
    st = next(it)
    bs = next(it)
    xcb = next(it)

    L = CHUNK
    mrow = m_ref[0]
    sh1 = mrow[:, 0:D_MODEL]
    sc1 = mrow[:, D_MODEL:2 * D_MODEL]
    g1 = mrow[:, 2 * D_MODEL:3 * D_MODEL]
    gain1 = n1g_ref[...] * (1.0 + sc1)

    zrows = jnp.zeros((8, LRU_WIDTH), F32)
    lxp[0:8, :] = zrows
    lxp[T + 8:T + 16, :] = zrows

    def load_x(rows):
        xt = x_ref[0, rows, :]
        if has_pos:
            xt = xt + pos_ref[rows, :]
        return xt

    def proj_body(i, carry):
        r0 = pl.multiple_of(i * ROW_TILE, ROW_TILE)
        xt = load_x(pl.ds(r0, ROW_TILE))
        ms = jnp.mean(xt * xt, axis=-1, keepdims=True)
        hb = (xt * lax.rsqrt(ms + EPS) * gain1 + sh1).astype(BF16)
        for c in range(IN_COLS // 512):
            res = jnp.dot(hb, win_ref[:, c * 512:(c + 1) * 512], preferred_element_type=F32)
            if c * 512 == COL_LX:
                lxp[pl.ds(pl.multiple_of(r0 + 8, 8), ROW_TILE), :] = res
            else:
                proj[pl.ds(r0, ROW_TILE), c * 512:(c + 1) * 512] = res
        return carry

    lax.fori_loop(0, T // ROW_TILE, proj_body, 0)

    l0 = lbl_ref[0]
    l1 = lbl_ref[1]
    lmx = jnp.maximum(l0, l1)
    e0 = jnp.exp(l0 - lmx)
    e1 = jnp.exp(l1 - lmx)
    lb_all = e0 / (e0 + e1)

    rowi = lax.broadcasted_iota(jnp.int32, (8, HG_DK), 0)
    r4 = rowi & 3
    is_r0 = r4 == 0
    is_r3 = r4 == 3
    lt2 = r4 < 2
    odd = (rowi & 1) == 1
    hi4 = rowi >= 4

    for i in range(N_DIR * HG_HEADS):
        if has_state:
            st[i] = hs0_ref[0, i].T
        else:
            st[i] = jnp.zeros((HG_DK, HG_DK), F32)

    def hg_chunk(r0, d):
        rows = pl.ds(r0, L)
        tri = tri_ref[d]
        for hd in range(HG_HEADS):
            idx = d * HG_HEADS + hd
            cq = COL_Q + idx * HG_DK
            cf = COL_F + idx * HG_DK
            cv = COL_V + hd * HG_DK
            hq = proj[rows, cq:cq + HG_DK]
            fz = proj[rows, cf:cf + HG_DK]
            v = proj[rows, cv:cv + HG_DK]
            q = hq * _sig(hq)
            sg = _sig(fz)
            lb = lb_all[idx:idx + 1, :]
            oml = 1.0 - lb
            f = lb + oml * sg
            k = oml * (1.0 - sg)
            logf = jnp.log(f)
            p1 = logf.astype(BF16)
            r1 = logf - p1.astype(F32)
            p2 = r1.astype(BF16)
            p3 = (r1 - p2.astype(F32)).astype(BF16)
            bb = jnp.dot(tri, jnp.concatenate([p1, p2, p3], axis=1), preferred_element_type=F32)
            b = (bb[:, 0:HG_DK] + bb[:, HG_DK:2 * HG_DK]) + bb[:, 2 * HG_DK:3 * HG_DK]
            bs[idx] = b

            def brow(r, n, idx=idx):
                return jnp.broadcast_to(bs[idx, r:r + 1, :], (n, HG_DK))

            qf = q * f
            acc = masks_ref[d, len(LEVELS)] * _nt_dot(q.astype(BF16), k.astype(BF16))
            for li, h in enumerate(LEVELS[:4]):
                pieces = []
                for j in range(L // (2 * h)):
                    lo = j * 2 * h
                    mid = lo + h
                    hi = mid + h
                    if d == 0:
                        bm = brow(mid - 1, h)
                        pieces.append(k[lo:mid] * jnp.exp(bm - b[lo:mid]))
                        pieces.append(q[mid:hi] * jnp.exp(b[mid:hi] - bm))
                    else:
                        bm = brow(mid, h)
                        pieces.append(q[lo:mid] * jnp.exp(b[lo:mid] - bm))
                        pieces.append(k[mid:hi] * jnp.exp(bm - b[mid:hi]))
                xl = jnp.concatenate(pieces, axis=0).astype(BF16)
                acc = acc + masks_ref[d, li] * _nt_dot(xl, xl)
            x4, x2, x1 = [], [], []
            for g in range(L // 8):
                sl = slice(8 * g, 8 * g + 8)
                qg, kg, fg, bg, qfg = q[sl], k[sl], f[sl], b[sl], qf[sl]
                mid = 8 * g + (3 if d == 0 else 4)
                e4 = jnp.exp(-jnp.abs(bg - brow(mid, 8)))
                fnx = pltpu.roll(fg, 7, 0)
                fpv = pltpu.roll(fg, 1, 0)
                s0 = jnp.where(is_r0, fnx, 1.0)
                s3 = jnp.where(is_r3, fpv, 1.0)
                if d == 0:
                    x4.append(jnp.where(hi4, qg, kg) * e4)
                    x2.append(jnp.where(lt2, kg * s0, qfg * s3))
                    x1.append(jnp.where(odd, qfg, kg))
                else:
                    x4.append(jnp.where(hi4, kg, qg) * e4)
                    x2.append(jnp.where(lt2, qfg * s0, kg * s3))
                    x1.append(jnp.where(odd, kg, qfg))
            for li, pcs in ((4, x4), (5, x2), (6, x1)):
                xl = jnp.concatenate(pcs, axis=0).astype(BF16)
                acc = acc + masks_ref[d, li] * _nt_dot(xl, xl)

            o = jnp.dot(acc.astype(BF16), v.astype(BF16), preferred_element_type=F32)
            st_t = st[idx]
            qt = (q * jnp.exp(b)).astype(BF16)
            o = o + _nt_dot(qt, st_t.astype(BF16))
            last = L - 1 if d == 0 else 0
            btot = bs[idx, last:last + 1, :]
            kt = (k * jnp.exp(btot - b)).astype(BF16)
            vt = v.T.astype(BF16)
            st[idx] = st_t * jnp.exp(btot) + jnp.dot(vt, kt, preferred_element_type=F32)
            co = slice(COL_LX + hd * HG_DK, COL_LX + (hd + 1) * HG_DK)
            if d == 0:
                proj[rows, co] = o
            else:
                proj[rows, co] = proj[rows, co] + o

    n_chunks = T // L

    def hg_fwd(c, carry):
        hg_chunk(pl.multiple_of(c * L, L), 0)
        return carry

    def hg_bwd(c, carry):
        hg_chunk(pl.multiple_of((n_chunks - 1 - c) * L, L), 1)
        return carry

    lax.fori_loop(0, n_chunks, hg_fwd, 0)
    lax.fori_loop(0, n_chunks, hg_bwd, 0)

    if emit_state:
        for i in range(N_DIR * HG_HEADS):
            hs_ref[0, i] = st[i].T

    def hg_fin(i, carry):
        rows = pl.ds(pl.multiple_of(i * L, L), L)
        for hd in range(HG_HEADS):
            cs = slice(hd * HG_DK, (hd + 1) * HG_DK)
            o = proj[rows, COL_LX + hd * HG_DK:COL_LX + (hd + 1) * HG_DK]
            ms = jnp.mean(o * o, axis=-1, keepdims=True)
            y = o * lax.rsqrt(ms + EPS) * hgg_ref[:, cs]
            gz = proj[rows, COL_GATE + hd * HG_DK:COL_GATE + (hd + 1) * HG_DK]
            mixin[rows, cs] = (y * (gz * _sig(gz))).astype(BF16)
        return carry

    lax.fori_loop(0, n_chunks, hg_fin, 0)

    col_xc = COL_V
    cw = convw_ref[...]
    cb = convb_ref[...]

    def conv_body(i, carry):
        r0 = pl.multiple_of(i * CONV_TILE, CONV_TILE)
        win = lxp[pl.ds(r0, CONV_TILE + 16), :]
        xc = cb
        for j in range(4):
            xc = xc + win[6 + j:6 + j + CONV_TILE] * cw[j:j + 1]
        proj[pl.ds(r0, CONV_TILE), col_xc:col_xc + LRU_WIDTH] = xc
        xcb[pl.ds(r0, CONV_TILE), :] = xc.astype(BF16)
        return carry

    lax.fori_loop(0, T // CONV_TILE, conv_body, 0)

    n_grp = LRU_WIDTH // LRU_GROUP

    def gate_body(i, carry):
        rows = pl.ds(pl.multiple_of(i * ROW_TILE, ROW_TILE), ROW_TILE)
        for g in range(n_grp):
            proj[rows, g * 1024:(g + 1) * 1024] = jnp.dot(
                xcb[rows, g * LRU_GROUP:(g + 1) * LRU_GROUP], wg_ref[g], preferred_element_type=F32)
        return carry

    lax.fori_loop(0, T // ROW_TILE, gate_body, 0)

    lam = lam_ref[...]
    nl = -lam
    c8 = -LRU_C * (jnp.maximum(nl, 0.0) + jnp.log1p(jnp.exp(-jnp.abs(nl))))
    rowi2 = lax.broadcasted_iota(jnp.int32, (8, LRU_GROUP), 0)

    def lru_inputs(rows8, g, d):
        base = g * 1024 + d * 512
        ga = proj[rows8, base:base + LRU_GROUP] + bg_ref[g:g + 1, d * 512:d * 512 + LRU_GROUP]
        gx = proj[rows8, base + LRU_GROUP:base + 2 * LRU_GROUP] + bg_ref[
            g:g + 1, d * 512 + LRU_GROUP:d * 512 + 2 * LRU_GROUP]
        xc8 = proj[rows8, col_xc + g * LRU_GROUP:col_xc + (g + 1) * LRU_GROUP]
        log_a = c8[d:d + 1, g * LRU_GROUP:(g + 1) * LRU_GROUP] * _sig(ga)
        a = jnp.exp(log_a)
        mult = jnp.sqrt(jnp.tanh(-log_a) * (1.0 + a * a))
        return a, mult * (_sig(gx) * xc8)

    def scan8(a, u, d):
        for sft in (1, 2, 4):
            if d == 0:
                keep = rowi2 >= sft
                amt = sft
            else:
                keep = rowi2 < 8 - sft
                amt = 8 - sft
            ash = jnp.where(keep, pltpu.roll(a, amt, 0), 1.0)
            ush = jnp.where(keep, pltpu.roll(u, amt, 0), 0.0)
            u = a * ush + u
            a = a * ash
        return a, u

    def lru_scan(d):
        def body(i, carry):
            cs = list(carry)
            tile = i if d == 0 else (T // SCAN_TILE - 1 - i)
            r0 = tile * SCAN_TILE
            order = range(SCAN_TILE // 8) if d == 0 else range(SCAN_TILE // 8 - 1, -1, -1)
            for s in order:
                rows8 = pl.ds(pl.multiple_of(r0 + 8 * s, 8), 8)
                for g in range(n_grp):
                    a, u = lru_inputs(rows8, g, d)
                    a, u = scan8(a, u, d)
                    h = u + a * cs[g]
                    base = g * 1024 + d * 512
                    proj[rows8, base:base + LRU_GROUP] = h
                    cs[g] = h[7:8, :] if d == 0 else h[0:1, :]
            return tuple(cs)

        if has_state:
            init = tuple(ls0_ref[0, d:d + 1, g * LRU_GROUP:(g + 1) * LRU_GROUP] for g in range(n_grp))
        else:
            init = tuple(jnp.zeros((1, LRU_GROUP), F32) for _ in range(n_grp))
        fin = lax.fori_loop(0, T // SCAN_TILE, body, init)
        if emit_state:
            for g in range(n_grp):
                ls_ref[0, d:d + 1, g * LRU_GROUP:(g + 1) * LRU_GROUP] = fin[g]

    lru_scan(0)
    lru_scan(1)

    def lru_fin(i, carry):
        rows = pl.ds(pl.multiple_of(i * L, L), L)
        for g in range(n_grp):
            hf = proj[rows, g * 1024:g * 1024 + LRU_GROUP]
            hb = proj[rows, g * 1024 + 512:g * 1024 + 512 + LRU_GROUP]
            lg = proj[rows, COL_LG + g * LRU_GROUP:COL_LG + (g + 1) * LRU_GROUP]
            gl = lg * (0.5 * (1.0 + jnp.tanh(0.7978845608028654 * (lg + 0.044715 * (lg * lg * lg)))))
            mixin[rows, HG_WIDTH + g * LRU_GROUP:HG_WIDTH + (g + 1) * LRU_GROUP] = ((hf + hb) * gl).astype(BF16)
        return carry

    lax.fori_loop(0, n_chunks, lru_fin, 0)

    def out_body(i, carry):
        rows = pl.ds(pl.multiple_of(i * ROW_TILE, ROW_TILE), ROW_TILE)
        mix = jnp.dot(mixin[rows, :], wout_ref[...], preferred_element_type=F32)
        x1_ref[0, rows, :] = load_x(rows) + g1 * mix
        return carry

    lax.fori_loop(0, T // ROW_TILE, out_body, 0)


def _const_spec(shape):
    nd = len(shape)
    return pl.BlockSpec(shape, lambda b, _n=nd: (0,) * _n, pipeline_mode=pl.Buffered(1))


def _mixer(x, m3, m_off, m_step, pos, consts, states, emit_state):
    B, T, _ = x.shape
    has_pos = pos is not None
    has_state = states is not None
    in_specs = [
        pl.BlockSpec((1, T, D_MODEL), lambda b: (b, 0, 0)),
        pl.BlockSpec((1, 1, 6 * D_MODEL), lambda b: (m_off + m_step * b, 0, 0)),
    ]
    args = [x, m3]
    if has_pos:
        in_specs.append(_const_spec(pos.shape))
        args.append(pos)
    in_specs += [_const_spec(c.shape) for c in consts]
    args += list(consts)
    if has_state:
        hs0, ls0 = states
        in_specs += [
            pl.BlockSpec((1, 8, HG_DK, HG_DK), lambda b: (b, 0, 0, 0)),
            pl.BlockSpec((1, N_DIR, LRU_WIDTH), lambda b: (b, 0, 0)),
        ]
        args += [hs0, ls0]
    out_shape = [jax.ShapeDtypeStruct((B, T, D_MODEL), F32)]
    out_specs = [pl.BlockSpec((1, T, D_MODEL), lambda b: (b, 0, 0))]
    if emit_state:
        out_shape += [jax.ShapeDtypeStruct((B, 8, HG_DK, HG_DK), F32),
                      jax.ShapeDtypeStruct((B, N_DIR, LRU_WIDTH), F32)]
        out_specs += [pl.BlockSpec((1, 8, HG_DK, HG_DK), lambda b: (b, 0, 0, 0)),
                      pl.BlockSpec((1, N_DIR, LRU_WIDTH), lambda b: (b, 0, 0))]
    scratch = [
        pltpu.VMEM((T, IN_COLS), F32),
        pltpu.VMEM((T + 16, LRU_WIDTH), F32),
        pltpu.VMEM((T, D_MODEL), BF16),
        pltpu.VMEM((8, HG_DK, HG_DK), F32),
        pltpu.VMEM((8, CHUNK, HG_DK), F32),
        pltpu.VMEM((T, LRU_WIDTH), BF16),
    ]
    return pl.pallas_call(
        functools.partial(_mixer_kernel, T=T, has_pos=has_pos, has_state=has_state, emit_state=emit_state),
        grid=(B,),
        in_specs=in_specs,
        out_specs=out_specs,
        out_shape=out_shape,
        scratch_shapes=scratch,
        compiler_params=pltpu.CompilerParams(
            dimension_semantics=("arbitrary",), vmem_limit_bytes=VMEM_LIMIT),
        name=f"mixer_t{T}",
    )(*args)


def _ffn_kernel(x_ref, m_ref, n2g_ref, fg_ref, w1_ref, w2_ref, y_ref):
    mrow = m_ref[0]
    sh2 = mrow[:, 3 * D_MODEL:4 * D_MODEL]
    sc2 = mrow[:, 4 * D_MODEL:5 * D_MODEL]
    g2 = mrow[:, 5 * D_MODEL:6 * D_MODEL]
    x = x_ref[...]
    ms = jnp.mean(x * x, axis=-1, keepdims=True)
    hb = (x * lax.rsqrt(ms + EPS) * (n2g_ref[...] * (1.0 + sc2)) + sh2).astype(BF16)
    ff = jnp.zeros(x.shape, F32)
    for c in range(D_FF // FF_CHUNK):
        a = jnp.dot(hb, w1_ref[:, c * FF_CHUNK:(c + 1) * FF_CHUNK], preferred_element_type=F32)
        a = jnp.maximum(a, 0.0)
        ff = ff + jnp.dot((a * a).astype(BF16), w2_ref[c * FF_CHUNK:(c + 1) * FF_CHUNK, :],
                          preferred_element_type=F32)
    x2 = x + g2 * ff
    ms2 = jnp.mean(x2 * x2, axis=-1, keepdims=True)
    y_ref[...] = x2 * lax.rsqrt(ms2 + EPS) * fg_ref[...]


def _ffn(x1, m3, m_off, tiles_per_cond, n2g, fgain, w1, w2):
    n = x1.shape[0]

    def m_index(i):
        if tiles_per_cond is None:
            return (m_off, 0, 0)
        return (m_off + i // tiles_per_cond, 0, 0)

    return pl.pallas_call(
        _ffn_kernel,
        grid=(n // FFN_TILE,),
        in_specs=[
            pl.BlockSpec((FFN_TILE, D_MODEL), lambda i: (i, 0)),
            pl.BlockSpec((1, 1, 6 * D_MODEL), m_index),
            _const_spec(n2g.shape),
            _const_spec(fgain.shape),
            _const_spec(w1.shape),
            _const_spec(w2.shape),
        ],
        out_specs=pl.BlockSpec((FFN_TILE, D_MODEL), lambda i: (i, 0)),
        out_shape=jax.ShapeDtypeStruct((n, D_MODEL), F32),
        compiler_params=pltpu.CompilerParams(
            dimension_semantics=("arbitrary",), vmem_limit_bytes=VMEM_LIMIT),
        name="ffn",
    )(x1, m3, n2g, fgain, w1, w2)


def _grid_pos_embed(n_tok):
    rows = n_tok // GRID_W
    rr, cc = jnp.meshgrid(jnp.arange(rows), jnp.arange(GRID_W), indexing='ij')
    quarter = D_MODEL // 4
    omega = 1.0 / (POS_BASE ** (jnp.arange(quarter, dtype=F32) / quarter))

    def emb(pos):
        ang = pos.reshape(-1, 1).astype(F32) * omega
        return jnp.concatenate([jnp.sin(ang), jnp.cos(ang)], axis=-1)

    return jnp.concatenate([emb(rr), emb(cc)], axis=-1)


def _gate_weights(lru_wa, lru_wx):
    per_group = LRU_GROUP // LRU_BLOCK
    slabs = []
    for g in range(LRU_WIDTH // LRU_GROUP):
        cols = []
        for d in range(N_DIR):
            for w in (lru_wa, lru_wx):
                blk = jnp.zeros((LRU_GROUP, LRU_GROUP), F32)
                for n in range(per_group):
                    blk = blk.at[n * LRU_BLOCK:(n + 1) * LRU_BLOCK, n * LRU_BLOCK:(n + 1) * LRU_BLOCK].set(
                        w[d, g * per_group + n])
                cols.append(blk)
        slabs.append(jnp.concatenate(cols, axis=1))
    return jnp.stack(slabs).astype(BF16)


def kernel(x_prompt, x_sample, c, state_hgrn, state_rglru, c_ctx, w_ada, b_ada, norm1_gain, norm2_gain,
           w_in, hg_lb_logits, hg_norm_gain, conv_w, conv_b, lru_wa, lru_ba, lru_wx, lru_bx, lru_lambda,
           w_out, w_ff1, w_ff2, final_gain):
    bp, tp, _ = x_prompt.shape
    bs_, ts, _ = x_sample.shape

    cond8 = jnp.concatenate([c_ctx[None, :], c, jnp.zeros((8 - 1 - bs_, D_MODEL), F32)], axis=0)
    m3 = _modulation(cond8, w_ada[0], b_ada).reshape(8, 1, 6 * D_MODEL)

    masks_np, tri_np = _level_tables()
    grp = [slice(g * LRU_GROUP, (g + 1) * LRU_GROUP) for g in range(LRU_WIDTH // LRU_GROUP)]
    bias_g = jnp.stack([jnp.concatenate([lru_ba[0, 0, s], lru_bx[0, 0, s], lru_ba[0, 1, s], lru_bx[0, 1, s]])
                        for s in grp])
    consts = [
        norm1_gain,
        w_in[0].astype(BF16),
        hg_lb_logits.reshape(2, N_DIR * HG_HEADS, HG_DK),
        hg_norm_gain[0].reshape(1, HG_WIDTH),
        jnp.asarray(masks_np),
        jnp.asarray(tri_np, dtype=BF16),
        conv_w[0],
        conv_b,
        _gate_weights(lru_wa[0], lru_wx[0]),
        bias_g,
        lru_lambda[0],
        w_out[0].astype(BF16),
    ]
    w1 = w_ff1[0].astype(BF16)
    w2 = w_ff2[0].astype(BF16)
    fgain = final_gain.reshape(1, D_MODEL)

    x1p, hs, ls = _mixer(x_prompt, m3, 0, 0, None, consts, None, True)
    y_prompt = _ffn(x1p.reshape(bp * tp, D_MODEL), m3, 0, None, norm2_gain, fgain, w1, w2)

    x1s = _mixer(x_sample, m3, 1, 1, _grid_pos_embed(ts), consts,
                 (state_hgrn.reshape(bs_, N_DIR * HG_HEADS, HG_DK, HG_DK),
                  state_rglru.reshape(bs_, N_DIR, LRU_WIDTH)), False)[0]
    y_sample = _ffn(x1s.reshape(bs_ * ts, D_MODEL), m3, 1, ts // FFN_TILE, norm2_gain, fgain, w1, w2)

    return (y_prompt.reshape(bp, tp, D_MODEL),
            y_sample.reshape(bs_, ts, D_MODEL),
            hs.reshape(bp, 1, N_DIR, HG_HEADS, HG_DK, HG_DK),
            ls.reshape(bp, 1, N_DIR, LRU_WIDTH))
```

```python
import functools

import numpy as np
import jax
import jax.numpy as jnp
from jax import lax
from jax.experimental import pallas as pl
from jax.experimental.pallas import tpu as pltpu

F32 = jnp.float32
BF16 = jnp.bfloat16

D_MODEL = 1024
N_DIR = 2
HG_HEADS = 4
HG_DK = 128
HG_WIDTH = 512
LRU_WIDTH = 512
LRU_BLOCKS = 8
LRU_BLOCK = 64
LRU_C = 8.0
D_FF = 4096
IN_COLS = 4096
EPS = 1e-6
GRID_W = 64
POS_BASE = 10000.0

COL_Q = 0
COL_F = 1024
COL_V = 2048
COL_GATE = 2560
COL_LX = 3072
COL_LG = 3584

CHUNK = 128
LEVELS = (64, 32, 16, 8, 4, 2, 1)
ROW_TILE = 256
CONV_TILE = 64
SCAN_TILE = 32
LRU_GROUP = 256
FFN_TILE = 512
FF_CHUNK = 1024
MOD_TILE = 1024
VMEM_LIMIT = 58 * 1024 * 1024
VMEM_HEADROOM = 6 * 1024 * 1024


def _sig(x):
    return jax.nn.sigmoid(x)


def _nt_dot(a, b):
    return lax.dot_general(a, b, (((1,), (1,)), ((), ())), preferred_element_type=F32)


def _gram(x):
    xb = x.astype(BF16)
    return _nt_dot(xb, xb)


def _level_tables():
    t = np.arange(CHUNK)[:, None]
    s = np.arange(CHUNK)[None, :]
    masks = np.zeros((N_DIR, len(LEVELS) + 1, CHUNK, CHUNK), np.float32)
    for li, h in enumerate(LEVELS):
        same = (t // (2 * h)) == (s // (2 * h))
        t_hi = (t // h) % 2 == 1
        s_hi = (s // h) % 2 == 1
        masks[0, li] = same & t_hi & ~s_hi
        masks[1, li] = same & ~t_hi & s_hi
    masks[:, len(LEVELS)] = (t == s)
    tri = np.stack([(s <= t), (s >= t)]).astype(np.float32)
    return masks, tri


def _mod_kernel(c_ref, w_ref, b_ref, o_ref):
    c = c_ref[...]
    a = (c * _sig(c)).astype(BF16)
    o_ref[...] = jnp.dot(a, w_ref[...].astype(BF16), preferred_element_type=F32) + b_ref[...]


def _modulation(cond8, w_ada, b_ada):
    n = w_ada.shape[1]
    return pl.pallas_call(
        _mod_kernel,
        grid=(n // MOD_TILE,),
        in_specs=[
            pl.BlockSpec((8, D_MODEL), lambda j: (0, 0)),
            pl.BlockSpec((D_MODEL, MOD_TILE), lambda j: (0, j)),
            pl.BlockSpec((1, MOD_TILE), lambda j: (0, j)),
        ],
        out_specs=pl.BlockSpec((8, MOD_TILE), lambda j: (0, j)),
        out_shape=jax.ShapeDtypeStruct((8, n), F32),
        compiler_params=pltpu.CompilerParams(dimension_semantics=("arbitrary",)),
        name="adaln_modulation",
    )(cond8, w_ada, b_ada)


def _mixer_kernel(*refs, T, has_pos, has_state, emit_state):
    it = iter(refs)
    x_ref = next(it)
    m_ref = next(it)
    if has_pos:
        pos_ref = next(it)
    n1g_ref = next(it)
    win_ref = next(it)
    lbl_ref = next(it)
    hgg_ref = next(it)
    masks_ref = next(it)
    tri_ref = next(it)
    convw_ref = next(it)
    convb_ref = next(it)
    wg_ref = next(it)
    bg_ref = next(it)
    lam_ref = next(it)
    wout_ref = next(it)
    if has_state:
        hs0_ref = next(it)
        ls0_ref = next(it)
    x1_ref = next(it)
    if emit_state:
        hs_ref = next(it)
        ls_ref = next(it)
    proj = next(it)
    lxp = next(it)
    mixin = next(it)
    st = next(it)
    ksc = next(it)
    xcb = next(it)
    vts = next(it)

    L = CHUNK
    mrow = m_ref[0]
    sh1 = mrow[:, 0:D_MODEL]
    sc1 = mrow[:, D_MODEL:2 * D_MODEL]
    g1 = mrow[:, 2 * D_MODEL:3 * D_MODEL]
    gain1 = n1g_ref[...] * (1.0 + sc1)

    zrows = jnp.zeros((8, LRU_WIDTH), F32)
    lxp[0:8, :] = zrows
    lxp[T + 8:T + 16, :] = zrows

    def load_x(rows):
        xt = x_ref[0, rows, :]
        if has_pos:
            xt = xt + pos_ref[rows, :]
        return xt

    def proj_body(i, carry):
        r0 = pl.multiple_of(i * ROW_TILE, ROW_TILE)
        xt = load_x(pl.ds(r0, ROW_TILE))
        ms = jnp.mean(xt * xt, axis=-1, keepdims=True)
        hb = (xt * lax.rsqrt(ms + EPS) * gain1 + sh1).astype(BF16)
        for c in range(IN_COLS // 512):
            res = jnp.dot(hb, win_ref[:, c * 512:(c + 1) * 512], preferred_element_type=F32)
            if c * 512 == COL_LX:
                lxp[pl.ds(pl.multiple_of(r0 + 8, 8), ROW_TILE), :] = res
            else:
                proj[pl.ds(r0, ROW_TILE), c * 512:(c + 1) * 512] = res
        return carry

    lax.fori_loop(0, T // ROW_TILE, proj_body, 0)

    l0 = lbl_ref[0]
    l1 = lbl_ref[1]
    lmx = jnp.maximum(l0, l1)
    e0 = jnp.exp(l0 - lmx)
    e1 = jnp.exp(l1 - lmx)
    lb_all = e0 / (e0 + e1)

    rowi = lax.broadcasted_iota(jnp.int32, (8, HG_DK), 0)
    r4 = rowi & 3
    is_r0 = r4 == 0
    is_r3 = r4 == 3
    lt2 = r4 < 2
    odd = (rowi & 1) == 1
    hi4 = rowi >= 4

    for i in range(N_DIR * HG_HEADS):
        if has_state:
            st[i] = hs0_ref[0, i].T
        else:
            st[i] = jnp.zeros((HG_DK, HG_DK), F32)

    def hg_prep(c, carry):
        rows = pl.ds(pl.multiple_of(c * L, L), L)
        for hd in range(HG_HEADS):
            vts[c, hd] = proj[rows, COL_V + hd * HG_DK:COL_V + (hd + 1) * HG_DK].T.astype(BF16)
        for d in range(N_DIR):
            parts = []
            for hd in range(HG_HEADS):
                idx = d * HG_HEADS + hd
                cq = COL_Q + idx * HG_DK
                cf = COL_F + idx * HG_DK
                hq = proj[rows, cq:cq + HG_DK]
                fz = proj[rows, cf:cf + HG_DK]
                proj[rows, cq:cq + HG_DK] = hq * _sig(hq)
                sg = _sig(fz)
                lb = lb_all[idx:idx + 1, :]
                oml = 1.0 - lb
                ksc[rows, idx * HG_DK:(idx + 1) * HG_DK] = oml * (1.0 - sg)
                logf = jnp.log(lb + oml * sg)
                p1 = logf.astype(BF16)
                r1 = logf - p1.astype(F32)
                p2 = r1.astype(BF16)
                p3 = (r1 - p2.astype(F32)).astype(BF16)
                parts += [p1, p2, p3]
            bb = jnp.dot(tri_ref[d], jnp.concatenate(parts, axis=1), preferred_element_type=F32)
            for hd in range(HG_HEADS):
                cf = COL_F + (d * HG_HEADS + hd) * HG_DK
                o3 = 3 * hd * HG_DK
                proj[rows, cf:cf + HG_DK] = ((bb[:, o3:o3 + HG_DK] + bb[:, o3 + HG_DK:o3 + 2 * HG_DK])
                                             + bb[:, o3 + 2 * HG_DK:o3 + 3 * HG_DK])
        return carry

    def hg_chunk(c, d):
        r0 = pl.multiple_of(c * L, L)
        rows = pl.ds(r0, L)
        for hd in range(HG_HEADS):
            idx = d * HG_HEADS + hd
            cq = COL_Q + idx * HG_DK
            cf = COL_F + idx * HG_DK
            ck = idx * HG_DK

            def ldq(lo, n, cq=cq):
                return proj[pl.ds(pl.multiple_of(r0 + lo, 8), n), cq:cq + HG_DK]

            def ldb(lo, n, cf=cf):
                return proj[pl.ds(pl.multiple_of(r0 + lo, 8), n), cf:cf + HG_DK]

            def ldk(lo, n, ck=ck):
                return ksc[pl.ds(pl.multiple_of(r0 + lo, 8), n), ck:ck + HG_DK]

            def bline(r, cf=cf):
                grp = proj[pl.ds(pl.multiple_of(r0 + 8 * (r // 8), 8), 8), cf:cf + HG_DK]
                return grp[r % 8:r % 8 + 1, :]

            def brow(r, n):
                return jnp.broadcast_to(bline(r), (n, HG_DK))

            acc = masks_ref[d, len(LEVELS)] * _nt_dot(ldq(0, L).astype(BF16), ldk(0, L).astype(BF16))
            for li, h in enumerate(LEVELS[:4]):
                pieces = []
                for j in range(L // (2 * h)):
                    lo = j * 2 * h
                    mid = lo + h
                    if d == 0:
                        bm = brow(mid - 1, h)
                        pieces.append(ldk(lo, h) * jnp.exp(bm - ldb(lo, h)))
                        pieces.append(ldq(mid, h) * jnp.exp(ldb(mid, h) - bm))
                    else:
                        bm = brow(mid, h)
                        pieces.append(ldq(lo, h) * jnp.exp(ldb(lo, h) - bm))
                        pieces.append(ldk(mid, h) * jnp.exp(bm - ldb(mid, h)))
                acc = acc + masks_ref[d, li] * _gram(jnp.concatenate(pieces, axis=0))
            x4, x2, x1 = [], [], []
            for g in range(L // 8):
                qg, kg, bg = ldq(8 * g, 8), ldk(8 * g, 8), ldb(8 * g, 8)
                fg = 1.0 - kg
                qfg = qg * fg
                mid = 8 * g + (3 if d == 0 else 4)
                e4 = jnp.exp(-jnp.abs(bg - brow(mid, 8)))
                fnx = pltpu.roll(fg, 7, 0)
                fpv = pltpu.roll(fg, 1, 0)
                s0 = jnp.where(is_r0, fnx, 1.0)
                s3 = jnp.where(is_r3, fpv, 1.0)
                if d == 0:
                    x4.append(jnp.where(hi4, qg, kg) * e4)
                    x2.append(jnp.where(lt2, kg * s0, qfg * s3))
                    x1.append(jnp.where(odd, qfg, kg))
                else:
                    x4.append(jnp.where(hi4, kg, qg) * e4)
                    x2.append(jnp.where(lt2, qfg * s0, kg * s3))
                    x1.append(jnp.where(odd, kg, qfg))
            for li, pcs in ((4, x4), (5, x2), (6, x1)):
                acc = acc + masks_ref[d, li] * _gram(jnp.concatenate(pcs, axis=0))

            vt = vts[c, hd]
            st_t = st[idx]
            b = ldb(0, L)
            qt = (ldq(0, L) * jnp.exp(b)).astype(BF16)
            o = _nt_dot(jnp.concatenate([acc.astype(BF16), qt], axis=1),
                        jnp.concatenate([vt, st_t.astype(BF16)], axis=1))
            btot = bline(L - 1 if d == 0 else 0)
            kt = (ldk(0, L) * jnp.exp(btot - b)).astype(BF16)
            st[idx] = st_t * jnp.exp(btot) + jnp.dot(vt, kt, preferred_element_type=F32)
            x1_ref[0, rows, d * HG_WIDTH + hd * HG_DK:d * HG_WIDTH + (hd + 1) * HG_DK] = o

    n_chunks = T // L

    def hg_both(c, carry):
        hg_chunk(c, 0)
        hg_chunk(n_chunks - 1 - c, 1)
        return carry

    lax.fori_loop(0, n_chunks, hg_prep, 0)
    lax.fori_loop(0, n_chunks, hg_both, 0)

    if emit_state:
        for i in range(N_DIR * HG_HEADS):
            hs_ref[0, i] = st[i].T

    def hg_fin(i, carry):
        rows = pl.ds(pl.multiple_of(i * L, L), L)
        for hd in range(HG_HEADS):
            cs = slice(hd * HG_DK, (hd + 1) * HG_DK)
            o = x1_ref[0, rows, cs] + x1_ref[0, rows, HG_WIDTH + hd * HG_DK:HG_WIDTH + (hd + 1) * HG_DK]
            ms = jnp.mean(o * o, axis=-1, keepdims=True)
            y = o * lax.rsqrt(ms + EPS) * hgg_ref[:, cs]
            gz = proj[rows, COL_GATE + hd * HG_DK:COL_GATE + (hd + 1) * HG_DK]
            mixin[rows, cs] = (y * (gz * _sig(gz))).astype(BF16)
        return carry

    lax.fori_loop(0, n_chunks, hg_fin, 0)

    col_xc = COL_V
    cw = convw_ref[...]
    cb = convb_ref[...]

    def conv_body(i, carry):
        r0 = pl.multiple_of(i * CONV_TILE, CONV_TILE)
        win = lxp[pl.ds(r0, CONV_TILE + 16), :]
        xc = cb
        for j in range(4):
            xc = xc + win[6 + j:6 + j + CONV_TILE] * cw[j:j + 1]
        proj[pl.ds(r0, CONV_TILE), col_xc:col_xc + LRU_WIDTH] = xc
        xcb[pl.ds(r0, CONV_TILE), :] = xc.astype(BF16)
        return carry

    lax.fori_loop(0, T // CONV_TILE, conv_body, 0)

    n_grp = LRU_WIDTH // LRU_GROUP

    def gate_body(i, carry):
        rows = pl.ds(pl.multiple_of(i * ROW_TILE, ROW_TILE), ROW_TILE)
        for g in range(n_grp):
            proj[rows, g * 1024:(g + 1) * 1024] = jnp.dot(
                xcb[rows, g * LRU_GROUP:(g + 1) * LRU_GROUP], wg_ref[g], preferred_element_type=F32)
        return carry

    lax.fori_loop(0, T // ROW_TILE, gate_body, 0)

    lam = lam_ref[...]
    nl = -lam
    c8 = -LRU_C * (jnp.maximum(nl, 0.0) + jnp.log1p(jnp.exp(-jnp.abs(nl))))
    rowi2 = lax.broadcasted_iota(jnp.int32, (8, LRU_GROUP), 0)

    def lru_inputs(rows8, g, d):
        base = g * 1024 + d * 512
        ga = proj[rows8, base:base + LRU_GROUP] + bg_ref[g:g + 1, d * 512:d * 512 + LRU_GROUP]
        gx = proj[rows8, base + LRU_GROUP:base + 2 * LRU_GROUP] + bg_ref[
            g:g + 1, d * 512 + LRU_GROUP:d * 512 + 2 * LRU_GROUP]
        xc8 = proj[rows8, col_xc + g * LRU_GROUP:col_xc + (g + 1) * LRU_GROUP]
        log_a = c8[d:d + 1, g * LRU_GROUP:(g + 1) * LRU_GROUP] * _sig(ga)
        a = jnp.exp(log_a)
        mult = jnp.sqrt(jnp.tanh(-log_a) * (1.0 + a * a))
        return a, mult * (_sig(gx) * xc8)

    def scan8(a, u, d):
        for sft in (1, 2, 4):
            if d == 0:
                keep = rowi2 >= sft
                amt = sft
            else:
                keep = rowi2 < 8 - sft
                amt = 8 - sft
            ash = jnp.where(keep, pltpu.roll(a, amt, 0), 1.0)
            ush = jnp.where(keep, pltpu.roll(u, amt, 0), 0.0)
            u = a * ush + u
            a = a * ash
        return a, u

    def lru_scan(d):
        def body(i, carry):
            cs = list(carry)
            tile = i if d == 0 else (T // SCAN_TILE - 1 - i)
            r0 = tile * SCAN_TILE
            order = range(SCAN_TILE // 8) if d == 0 else range(SCAN_TILE // 8 - 1, -1, -1)
            for s in order:
                rows8 = pl.ds(pl.multiple_of(r0 + 8 * s, 8), 8)
                for g in range(n_grp):
                    a, u = lru_inputs(rows8, g, d)
                    a, u = scan8(a, u, d)
                    h = u + a * cs[g]
                    base = g * 1024 + d * 512
                    proj[rows8, base:base + LRU_GROUP] = h
                    cs[g] = h[7:8, :] if d == 0 else h[0:1, :]
            return tuple(cs)

        if has_state:
            init = tuple(ls0_ref[0, d:d + 1, g * LRU_GROUP:(g + 1) * LRU_GROUP] for g in range(n_grp))
        else:
            init = tuple(jnp.zeros((1, LRU_GROUP), F32) for _ in range(n_grp))
        fin = lax.fori_loop(0, T // SCAN_TILE, body, init)
        if emit_state:
            for g in range(n_grp):
                ls_ref[0, d:d + 1, g * LRU_GROUP:(g + 1) * LRU_GROUP] = fin[g]

    lru_scan(0)
    lru_scan(1)

    def lru_fin(i, carry):
        rows = pl.ds(pl.multiple_of(i * L, L), L)
        for g in range(n_grp):
            hf = proj[rows, g * 1024:g * 1024 + LRU_GROUP]
            hb = proj[rows, g * 1024 + 512:g * 1024 + 512 + LRU_GROUP]
            lg = proj[rows, COL_LG + g * LRU_GROUP:COL_LG + (g + 1) * LRU_GROUP]
            gl = lg * (0.5 * (1.0 + jnp.tanh(0.7978845608028654 * (lg + 0.044715 * (lg * lg * lg)))))
            mixin[rows, HG_WIDTH + g * LRU_GROUP:HG_WIDTH + (g + 1) * LRU_GROUP] = ((hf + hb) * gl).astype(BF16)
        return carry

    lax.fori_loop(0, n_chunks, lru_fin, 0)

    def out_body(i, carry):
        rows = pl.ds(pl.multiple_of(i * ROW_TILE, ROW_TILE), ROW_TILE)
        mix = jnp.dot(mixin[rows, :], wout_ref[...], preferred_element_type=F32)
        x1_ref[0, rows, :] = load_x(rows) + g1 * mix
        return carry

    lax.fori_loop(0, T // ROW_TILE, out_body, 0)


def _const_spec(shape):
    nd = len(shape)
    return pl.BlockSpec(shape, lambda b, _n=nd: (0,) * _n, pipeline_mode=pl.Buffered(1))


def _nbytes(shape, dtype):
    return int(np.prod(shape)) * jnp.dtype(dtype).itemsize


def _mixer(x, m3, m_off, m_step, pos, consts, states, emit_state):
    B, T, _ = x.shape
    has_pos = pos is not None
    has_state = states is not None
    scratch_shapes = [
        ((T, IN_COLS), F32),
        ((T + 16, LRU_WIDTH), F32),
        ((T, D_MODEL), BF16),
        ((8, HG_DK, HG_DK), F32),
        ((T, N_DIR * HG_HEADS * HG_DK), F32),
        ((T, LRU_WIDTH), BF16),
        ((T // CHUNK, HG_HEADS, HG_DK, CHUNK), BF16),
    ]
    resident = (sum(_nbytes(s, d) for s, d in scratch_shapes)
                + sum(_nbytes(c.shape, c.dtype) for c in consts)
                + (_nbytes(pos.shape, pos.dtype) if has_pos else 0))
    io_block = _nbytes((T, D_MODEL), F32)
    io_bufs = 2 if resident + 4 * io_block <= VMEM_LIMIT - VMEM_HEADROOM else 1
    io_mode = pl.Buffered(io_bufs)
    in_specs = [
        pl.BlockSpec((1, T, D_MODEL), lambda b: (b, 0, 0), pipeline_mode=io_mode),
        pl.BlockSpec((1, 1, 6 * D_MODEL), lambda b: (m_off + m_step * b, 0, 0)),
    ]
    args = [x, m3]
    if has_pos:
        in_specs.append(_const_spec(pos.shape))
        args.append(pos)
    in_specs += [_const_spec(c.shape) for c in consts]
    args += list(consts)
    if has_state:
        hs0, ls0 = states
        in_specs += [
            pl.BlockSpec((1, 8, HG_DK, HG_DK), lambda b: (b, 0, 0, 0)),
            pl.BlockSpec((1, N_DIR, LRU_WIDTH), lambda b: (b, 0, 0)),
        ]
        args += [hs0, ls0]
    out_shape = [jax.ShapeDtypeStruct((B, T, D_MODEL), F32)]
    out_specs = [pl.BlockSpec((1, T, D_MODEL), lambda b: (b, 0, 0), pipeline_mode=io_mode)]
    if emit_state:
        out_shape += [jax.ShapeDtypeStruct((B, 8, HG_DK, HG_DK), F32),
                      jax.ShapeDtypeStruct((B, N_DIR, LRU_WIDTH), F32)]
        out_specs += [pl.BlockSpec((1, 8, HG_DK, HG_DK), lambda b: (b, 0, 0, 0)),
                      pl.BlockSpec((1, N_DIR, LRU_WIDTH), lambda b: (b, 0, 0))]
    scratch = [pltpu.VMEM(s, d) for s, d in scratch_shapes]
    return pl.pallas_call(
        functools.partial(_mixer_kernel, T=T, has_pos=has_pos, has_state=has_state, emit_state=emit_state),
        grid=(B,),
        in_specs=in_specs,
        out_specs=out_specs,
        out_shape=out_shape,
        scratch_shapes=scratch,
        compiler_params=pltpu.CompilerParams(
            dimension_semantics=("arbitrary",), vmem_limit_bytes=VMEM_LIMIT),
        name=f"mixer_t{T}",
    )(*args)


def _ffn_kernel(x_ref, m_ref, n2g_ref, fg_ref, w1_ref, w2_ref, y_ref):
    mrow = m_ref[0]
    sh2 = mrow[:, 3 * D_MODEL:4 * D_MODEL]
    sc2 = mrow[:, 4 * D_MODEL:5 * D_MODEL]
    g2 = mrow[:, 5 * D_MODEL:6 * D_MODEL]
    x = x_ref[...]
    ms = jnp.mean(x * x, axis=-1, keepdims=True)
    hb = (x * lax.rsqrt(ms + EPS) * (n2g_ref[...] * (1.0 + sc2)) + sh2).astype(BF16)
    ff = jnp.zeros(x.shape, F32)
    for c in range(D_FF // FF_CHUNK):
        a = jnp.dot(hb, w1_ref[:, c * FF_CHUNK:(c + 1) * FF_CHUNK], preferred_element_type=F32)
        a = jnp.maximum(a, 0.0)
        ff = ff + jnp.dot((a * a).astype(BF16), w2_ref[c * FF_CHUNK:(c + 1) * FF_CHUNK, :],
                          preferred_element_type=F32)
    x2 = x + g2 * ff
    ms2 = jnp.mean(x2 * x2, axis=-1, keepdims=True)
    y_ref[...] = x2 * lax.rsqrt(ms2 + EPS) * fg_ref[...]


def _ffn(x1, m3, m_off, tiles_per_cond, n2g, fgain, w1, w2):
    n = x1.shape[0]

    def m_index(i):
        if tiles_per_cond is None:
            return (m_off, 0, 0)
        return (m_off + i // tiles_per_cond, 0, 0)

    return pl.pallas_call(
        _ffn_kernel,
        grid=(n // FFN_TILE,),
        in_specs=[
            pl.BlockSpec((FFN_TILE, D_MODEL), lambda i: (i, 0)),
            pl.BlockSpec((1, 1, 6 * D_MODEL), m_index),
            _const_spec(n2g.shape),
            _const_spec(fgain.shape),
            _const_spec(w1.shape),
            _const_spec(w2.shape),
        ],
        out_specs=pl.BlockSpec((FFN_TILE, D_MODEL), lambda i: (i, 0)),
        out_shape=jax.ShapeDtypeStruct((n, D_MODEL), F32),
        compiler_params=pltpu.CompilerParams(
            dimension_semantics=("arbitrary",), vmem_limit_bytes=VMEM_LIMIT),
        name="ffn",
    )(x1, m3, n2g, fgain, w1, w2)


def _grid_pos_embed(n_tok):
    rows = n_tok // GRID_W
    rr, cc = np.meshgrid(np.arange(rows), np.arange(GRID_W), indexing='ij')
    quarter = D_MODEL // 4
    omega = (1.0 / (np.float32(POS_BASE) ** (np.arange(quarter, dtype=np.float32) / np.float32(quarter)))
             ).astype(np.float32)

    def emb(pos):
        ang = pos.reshape(-1, 1).astype(np.float32) * omega
        return np.concatenate([np.sin(ang), np.cos(ang)], axis=-1)

    return jnp.asarray(np.concatenate([emb(rr), emb(cc)], axis=-1), dtype=F32)


def _gate_weights(lru_wa, lru_wx):
    per_group = LRU_GROUP // LRU_BLOCK
    slabs = []
    for g in range(LRU_WIDTH // LRU_GROUP):
        cols = []
        for d in range(N_DIR):
            for w in (lru_wa, lru_wx):
                blk = jnp.zeros((LRU_GROUP, LRU_GROUP), F32)
                for n in range(per_group):
                    blk = blk.at[n * LRU_BLOCK:(n + 1) * LRU_BLOCK, n * LRU_BLOCK:(n + 1) * LRU_BLOCK].set(
                        w[d, g * per_group + n])
                cols.append(blk)
        slabs.append(jnp.concatenate(cols, axis=1))
    return jnp.stack(slabs).astype(BF16)


def kernel(x_prompt, x_sample, c, state_hgrn, state_rglru, c_ctx, w_ada, b_ada, norm1_gain, norm2_gain,
           w_in, hg_lb_logits, hg_norm_gain, conv_w, conv_b, lru_wa, lru_ba, lru_wx, lru_bx, lru_lambda,
           w_out, w_ff1, w_ff2, final_gain):
    bp, tp, _ = x_prompt.shape
    bs_, ts, _ = x_sample.shape

    cond8 = jnp.concatenate([c_ctx[None, :], c, jnp.zeros((8 - 1 - bs_, D_MODEL), F32)], axis=0)
    m3 = _modulation(cond8, w_ada[0], b_ada).reshape(8, 1, 6 * D_MODEL)

    masks_np, tri_np = _level_tables()
    grp = [slice(g * LRU_GROUP, (g + 1) * LRU_GROUP) for g in range(LRU_WIDTH // LRU_GROUP)]
    bias_g = jnp.stack([jnp.concatenate([lru_ba[0, 0, s], lru_bx[0, 0, s], lru_ba[0, 1, s], lru_bx[0, 1, s]])
                        for s in grp])
    consts = [
        norm1_gain,
        w_in[0].astype(BF16),
        hg_lb_logits.reshape(2, N_DIR * HG_HEADS, HG_DK),
        hg_norm_gain[0].reshape(1, HG_WIDTH),
        jnp.asarray(masks_np),
        jnp.asarray(tri_np, dtype=BF16),
        conv_w[0],
        conv_b,
        _gate_weights(lru_wa[0], lru_wx[0]),
        bias_g,
        lru_lambda[0],
        w_out[0].astype(BF16),
    ]
    w1 = w_ff1[0].astype(BF16)
    w2 = w_ff2[0].astype(BF16)
    fgain = final_gain.reshape(1, D_MODEL)

    x1p, hs, ls = _mixer(x_prompt, m3, 0, 0, None, consts, None, True)
    y_prompt = _ffn(x1p.reshape(bp * tp, D_MODEL), m3, 0, None, norm2_gain, fgain, w1, w2)

    x1s = _mixer(x_sample, m3, 1, 1, _grid_pos_embed(ts), consts,
                 (state_hgrn.reshape(bs_, N_DIR * HG_HEADS, HG_DK, HG_DK),
                  state_rglru.reshape(bs_, N_DIR, LRU_WIDTH)), False)[0]
    y_sample = _ffn(x1s.reshape(bs_ * ts, D_MODEL), m3, 1, ts // FFN_TILE, norm2_gain, fgain, w1, w2)

    return (y_prompt.reshape(bp, tp, D_MODEL),
            y_sample.reshape(bs_, ts, D_MODEL),
            hs.reshape(bp, 1, N_DIR, HG_HEADS, HG_DK, HG_DK),
            ls.reshape(bp, 1, N_DIR, LRU_WIDTH))
```

```python
import functools

import numpy as np
import jax
import jax.numpy as jnp
from jax import lax
from jax.experimental import pallas as pl
from jax.experimental.pallas import tpu as pltpu

F32 = jnp.float32
BF16 = jnp.bfloat16

D_MODEL = 1024
N_DIR = 2
HG_HEADS = 4
HG_DK = 128
HG_WIDTH = 512
LRU_WIDTH = 512
LRU_BLOCKS = 8
LRU_BLOCK = 64
LRU_C = 8.0
D_FF = 4096
IN_COLS = 4096
EPS = 1e-6
GRID_W = 64
POS_BASE = 10000.0

COL_Q = 0
COL_F = 1024
COL_V = 2048
COL_GATE = 2560
COL_LX = 3072
COL_LG = 3584

CHUNK = 128
LEVELS = (64, 32, 16, 8, 4, 2, 1)
ROW_TILE = 256
LRU_SEGS = 8
CONV_TILE = 32
SCAN_UNROLL = 4
FIX_TILE = 64
LRU_GROUP = 256
FFN_TILE = 512
FF_CHUNK = 1024
MOD_TILE = 1024
VMEM_LIMIT = 58 * 1024 * 1024
VMEM_HEADROOM = 6 * 1024 * 1024


def _sig(x):
    return jax.nn.sigmoid(x)


def _nt_dot(a, b):
    return lax.dot_general(a, b, (((1,), (1,)), ((), ())), preferred_element_type=F32)


def _gram(x):
    xb = x.astype(BF16)
    return _nt_dot(xb, xb)


def _level_tables():
    t = np.arange(CHUNK)[:, None]
    s = np.arange(CHUNK)[None, :]
    masks = np.zeros((N_DIR, len(LEVELS) + 1, CHUNK, CHUNK), np.float32)
    for li, h in enumerate(LEVELS):
        same = (t // (2 * h)) == (s // (2 * h))
        t_hi = (t // h) % 2 == 1
        s_hi = (s // h) % 2 == 1
        masks[0, li] = same & t_hi & ~s_hi
        masks[1, li] = same & ~t_hi & s_hi
    masks[:, len(LEVELS)] = (t == s)
    tri = np.stack([(s <= t), (s >= t)]).astype(np.float32)
    return masks, tri


def _mod_kernel(c_ref, w_ref, b_ref, o_ref):
    c = c_ref[...]
    a = (c * _sig(c)).astype(BF16)
    o_ref[...] = jnp.dot(a, w_ref[...].astype(BF16), preferred_element_type=F32) + b_ref[...]


def _modulation(cond8, w_ada, b_ada):
    n = w_ada.shape[1]
    return pl.pallas_call(
        _mod_kernel,
        grid=(n // MOD_TILE,),
        in_specs=[
            pl.BlockSpec((8, D_MODEL), lambda j: (0, 0)),
            pl.BlockSpec((D_MODEL, MOD_TILE), lambda j: (0, j)),
            pl.BlockSpec((1, MOD_TILE), lambda j: (0, j)),
        ],
        out_specs=pl.BlockSpec((8, MOD_TILE), lambda j: (0, j)),
        out_shape=jax.ShapeDtypeStruct((8, n), F32),
        compiler_params=pltpu.CompilerParams(dimension_semantics=("arbitrary",)),
        name="adaln_modulation",
    )(cond8, w_ada, b_ada)


def _mixer_kernel(*refs, T, has_pos, has_state, emit_state):
    it = iter(refs)
    x_ref = next(it)
    m_ref = next(it)
    if has_pos:
        pos_ref = next(it)
    n1g_ref = next(it)
    win_ref = next(it)
    lbl_ref = next(it)
    hgg_ref = next(it)
    masks_ref = next(it)
    tri_ref = next(it)
    convw_ref = next(it)
    convb_ref = next(it)
    wg_ref = next(it)
    bg_ref = next(it)
    lam_ref = next(it)
    wout_ref = next(it)
    if has_state:
        hs0_ref = next(it)
        ls0_ref = next(it)
    x1_ref = next(it)
    if emit_state:
        hs_ref = next(it)
        ls_ref = next(it)
    proj = next(it)
    lxp = next(it)
    mixin = next(it)
    st = next(it)
    ksc = next(it)
    xci = next(it)
    vts = next(it)
    rowc = next(it)

    L = CHUNK
    mrow = m_ref[0]
    sh1 = mrow[:, 0:D_MODEL]
    sc1 = mrow[:, D_MODEL:2 * D_MODEL]
    g1 = mrow[:, 2 * D_MODEL:3 * D_MODEL]
    gain1 = n1g_ref[...] * (1.0 + sc1)

    zrows = jnp.zeros((8, LRU_WIDTH), F32)
    lxp[0:8, :] = zrows
    lxp[T + 8:T + 16, :] = zrows

    def load_x(rows):
        xt = x_ref[0, rows, :]
        if has_pos:
            xt = xt + pos_ref[rows, :]
        return xt

    def proj_body(i, carry):
        r0 = pl.multiple_of(i * ROW_TILE, ROW_TILE)
        xt = load_x(pl.ds(r0, ROW_TILE))
        ms = jnp.mean(xt * xt, axis=-1, keepdims=True)
        hb = (xt * lax.rsqrt(ms + EPS) * gain1 + sh1).astype(BF16)
        for c in range(IN_COLS // 512):
            res = jnp.dot(hb, win_ref[:, c * 512:(c + 1) * 512], preferred_element_type=F32)
            if c * 512 == COL_LX:
                lxp[pl.ds(pl.multiple_of(r0 + 8, 8), ROW_TILE), :] = res
            else:
                proj[pl.ds(r0, ROW_TILE), c * 512:(c + 1) * 512] = res
        return carry

    lax.fori_loop(0, T // ROW_TILE, proj_body, 0)

    l0 = lbl_ref[0]
    l1 = lbl_ref[1]
    lmx = jnp.maximum(l0, l1)
    e0 = jnp.exp(l0 - lmx)
    e1 = jnp.exp(l1 - lmx)
    lb_all = e0 / (e0 + e1)

    rowi = lax.broadcasted_iota(jnp.int32, (8, HG_DK), 0)
    r4 = rowi & 3
    is_r0 = r4 == 0
    is_r3 = r4 == 3
    lt2 = r4 < 2
    odd = (rowi & 1) == 1
    hi4 = rowi >= 4

    for i in range(N_DIR * HG_HEADS):
        if has_state:
            st[i] = hs0_ref[0, i].T
        else:
            st[i] = jnp.zeros((HG_DK, HG_DK), F32)

    def hg_prep(c, carry):
        rows = pl.ds(pl.multiple_of(c * L, L), L)
        for hd in range(HG_HEADS):
            vts[c, hd] = proj[rows, COL_V + hd * HG_DK:COL_V + (hd + 1) * HG_DK].T.astype(BF16)
        for d in range(N_DIR):
            parts = []
            for hd in range(HG_HEADS):
                idx = d * HG_HEADS + hd
                cq = COL_Q + idx * HG_DK
                cf = COL_F + idx * HG_DK
                hq = proj[rows, cq:cq + HG_DK]
                fz = proj[rows, cf:cf + HG_DK]
                proj[rows, cq:cq + HG_DK] = hq * _sig(hq)
                sg = _sig(fz)
                lb = lb_all[idx:idx + 1, :]
                oml = 1.0 - lb
                ksc[rows, idx * HG_DK:(idx + 1) * HG_DK] = oml * (1.0 - sg)
                logf = jnp.log(lb + oml * sg)
                p1 = logf.astype(BF16)
                r1 = logf - p1.astype(F32)
                p2 = r1.astype(BF16)
                p3 = (r1 - p2.astype(F32)).astype(BF16)
                parts += [p1, p2, p3]
            bb = jnp.dot(tri_ref[d], jnp.concatenate(parts, axis=1), preferred_element_type=F32)
            for hd in range(HG_HEADS):
                cf = COL_F + (d * HG_HEADS + hd) * HG_DK
                o3 = 3 * hd * HG_DK
                proj[rows, cf:cf + HG_DK] = ((bb[:, o3:o3 + HG_DK] + bb[:, o3 + HG_DK:o3 + 2 * HG_DK])
                                             + bb[:, o3 + 2 * HG_DK:o3 + 3 * HG_DK])
        return carry

    def hg_chunk(c, d):
        r0 = pl.multiple_of(c * L, L)
        rows = pl.ds(r0, L)
        for hd in range(HG_HEADS):
            idx = d * HG_HEADS + hd
            cq = COL_Q + idx * HG_DK
            cf = COL_F + idx * HG_DK
            ck = idx * HG_DK

            def ldq(lo, n, cq=cq):
                return proj[pl.ds(pl.multiple_of(r0 + lo, 8), n), cq:cq + HG_DK]

            def ldb(lo, n, cf=cf):
                return proj[pl.ds(pl.multiple_of(r0 + lo, 8), n), cf:cf + HG_DK]

            def ldk(lo, n, ck=ck):
                return ksc[pl.ds(pl.multiple_of(r0 + lo, 8), n), ck:ck + HG_DK]

            def bline(r, cf=cf):
                grp = proj[pl.ds(pl.multiple_of(r0 + 8 * (r // 8), 8), 8), cf:cf + HG_DK]
                return grp[r % 8:r % 8 + 1, :]

            def brow(r, n):
                return jnp.broadcast_to(bline(r), (n, HG_DK))

            acc = masks_ref[d, len(LEVELS)] * _nt_dot(ldq(0, L).astype(BF16), ldk(0, L).astype(BF16))
            for li, h in enumerate(LEVELS[:4]):
                pieces = []
                for j in range(L // (2 * h)):
                    lo = j * 2 * h
                    mid = lo + h
                    if d == 0:
                        bm = brow(mid - 1, h)
                        pieces.append(ldk(lo, h) * jnp.exp(bm - ldb(lo, h)))
                        pieces.append(ldq(mid, h) * jnp.exp(ldb(mid, h) - bm))
                    else:
                        bm = brow(mid, h)
                        pieces.append(ldq(lo, h) * jnp.exp(ldb(lo, h) - bm))
                        pieces.append(ldk(mid, h) * jnp.exp(bm - ldb(mid, h)))
                acc = acc + masks_ref[d, li] * _gram(jnp.concatenate(pieces, axis=0))
            x4, x2, x1 = [], [], []
            for g in range(L // 8):
                qg, kg, bg = ldq(8 * g, 8), ldk(8 * g, 8), ldb(8 * g, 8)
                fg = 1.0 - kg
                qfg = qg * fg
                mid = 8 * g + (3 if d == 0 else 4)
                e4 = jnp.exp(-jnp.abs(bg - brow(mid, 8)))
                fnx = pltpu.roll(fg, 7, 0)
                fpv = pltpu.roll(fg, 1, 0)
                s0 = jnp.where(is_r0, fnx, 1.0)
                s3 = jnp.where(is_r3, fpv, 1.0)
                if d == 0:
                    x4.append(jnp.where(hi4, qg, kg) * e4)
                    x2.append(jnp.where(lt2, kg * s0, qfg * s3))
                    x1.append(jnp.where(odd, qfg, kg))
                else:
                    x4.append(jnp.where(hi4, kg, qg) * e4)
                    x2.append(jnp.where(lt2, qfg * s0, kg * s3))
                    x1.append(jnp.where(odd, kg, qfg))
            for li, pcs in ((4, x4), (5, x2), (6, x1)):
                acc = acc + masks_ref[d, li] * _gram(jnp.concatenate(pcs, axis=0))

            vt = vts[c, hd]
            st_t = st[idx]
            b = ldb(0, L)
            qt = (ldq(0, L) * jnp.exp(b)).astype(BF16)
            o = _nt_dot(jnp.concatenate([acc.astype(BF16), qt], axis=1),
                        jnp.concatenate([vt, st_t.astype(BF16)], axis=1))
            btot = bline(L - 1 if d == 0 else 0)
            kt = (ldk(0, L) * jnp.exp(btot - b)).astype(BF16)
            st[idx] = st_t * jnp.exp(btot) + jnp.dot(vt, kt, preferred_element_type=F32)
            x1_ref[0, rows, d * HG_WIDTH + hd * HG_DK:d * HG_WIDTH + (hd + 1) * HG_DK] = o

    n_chunks = T // L

    def hg_both(c, carry):
        hg_chunk(c, 0)
        hg_chunk(n_chunks - 1 - c, 1)
        return carry

    lax.fori_loop(0, n_chunks, hg_prep, 0)
    lax.fori_loop(0, n_chunks, hg_both, 0)

    if emit_state:
        for i in range(N_DIR * HG_HEADS):
            hs_ref[0, i] = st[i].T

    def hg_fin(i, carry):
        rows = pl.ds(pl.multiple_of(i * L, L), L)
        for hd in range(HG_HEADS):
            cs = slice(hd * HG_DK, (hd + 1) * HG_DK)
            o = x1_ref[0, rows, cs] + x1_ref[0, rows, HG_WIDTH + hd * HG_DK:HG_WIDTH + (hd + 1) * HG_DK]
            ms = jnp.mean(o * o, axis=-1, keepdims=True)
            y = o * lax.rsqrt(ms + EPS) * hgg_ref[:, cs]
            gz = proj[rows, COL_GATE + hd * HG_DK:COL_GATE + (hd + 1) * HG_DK]
            mixin[rows, cs] = (y * (gz * _sig(gz))).astype(BF16)
        return carry

    lax.fori_loop(0, n_chunks, hg_fin, 0)

    S = T // LRU_SEGS
    n_grp = LRU_WIDTH // LRU_GROUP
    slabs = LRU_GROUP // 128
    cw = convw_ref[...]
    cb = convb_ref[...]

    def irows(i0, j, n):
        return pl.ds(pl.multiple_of(LRU_SEGS * i0, 8) + j, n, stride=LRU_SEGS)

    for j in range(LRU_SEGS):
        def conv_body(ti, carry, j=j):
            i0 = pl.multiple_of(ti * CONV_TILE, CONV_TILE)
            win = lxp[pl.ds(pl.multiple_of(j * S + i0, 8), CONV_TILE + 16), :]
            xc = cb
            for tap in range(4):
                xc = xc + win[6 + tap:6 + tap + CONV_TILE] * cw[tap:tap + 1]
            for l in range(LRU_WIDTH // 128):
                xci[l, irows(i0, j, CONV_TILE), :] = xc[:, l * 128:(l + 1) * 128]
            return carry

        lax.fori_loop(0, S // CONV_TILE, conv_body, 0)

    def xc_tile(rows, g):
        return jnp.concatenate([xci[g * slabs + l, rows, :] for l in range(slabs)], axis=1)

    def gate_body(i, carry):
        rows = pl.ds(pl.multiple_of(i * ROW_TILE, ROW_TILE), ROW_TILE)
        for g in range(n_grp):
            proj[rows, g * 1024:(g + 1) * 1024] = jnp.dot(
                xc_tile(rows, g).astype(BF16), wg_ref[g], preferred_element_type=F32)
        return carry

    lax.fori_loop(0, T // ROW_TILE, gate_body, 0)

    lam = lam_ref[...]
    nl = -lam
    c8 = -LRU_C * (jnp.maximum(nl, 0.0) + jnp.log1p(jnp.exp(-jnp.abs(nl))))
    rowi2 = lax.broadcasted_iota(jnp.int32, (8, LRU_GROUP), 0)

    for d in range(N_DIR):
        for g in range(n_grp):
            k3 = 3 * (d * n_grp + g)
            for r, row in enumerate((bg_ref[g:g + 1, d * 512:d * 512 + LRU_GROUP],
                                     bg_ref[g:g + 1, d * 512 + LRU_GROUP:d * 512 + 2 * LRU_GROUP],
                                     c8[d:d + 1, g * LRU_GROUP:(g + 1) * LRU_GROUP])):
                rowc[k3 + r] = jnp.broadcast_to(row, (LRU_SEGS, LRU_GROUP))

    def seg_out(d, rows, g, part):
        cols = slice((g * 2 + part) * LRU_GROUP, (g * 2 + part + 1) * LRU_GROUP)
        return (ksc.at[rows, cols] if d == 0 else x1_ref.at[0, rows, cols])

    def lru_inputs(rows8, g, d):
        base = g * 1024 + d * 512
        k3 = 3 * (d * n_grp + g)
        ga = proj[rows8, base:base + LRU_GROUP] + rowc[k3]
        gx = proj[rows8, base + LRU_GROUP:base + 2 * LRU_GROUP] + rowc[k3 + 1]
        xc8 = xc_tile(rows8, g)
        log_a = rowc[k3 + 2] * _sig(ga)
        a = jnp.exp(log_a)
        mult = jnp.sqrt(jnp.tanh(-log_a) * (1.0 + a * a))
        return a, mult * (_sig(gx) * xc8)

    def scan8(a, u, d):
        for sft in (1, 2, 4):
            if d == 0:
                keep = rowi2 >= sft
                amt = sft
            else:
                keep = rowi2 < 8 - sft
                amt = 8 - sft
            ash = jnp.where(keep, pltpu.roll(a, amt, 0), 1.0)
            ush = jnp.where(keep, pltpu.roll(u, amt, 0), 0.0)
            u = a * ush + u
            a = a * ash
        return a, u

    def scan_body(n, carry):
        hs, ds = [list(c) for c in carry[:2]], [list(c) for c in carry[2:]]
        for u in range(SCAN_UNROLL):
            i_f = n * SCAN_UNROLL + u
            for d, i in ((0, i_f), (1, S - 1 - i_f)):
                rows8 = pl.ds(pl.multiple_of(LRU_SEGS * i, 8), 8)
                for g in range(n_grp):
                    a, uu = lru_inputs(rows8, g, d)
                    hs[d][g] = a * hs[d][g] + uu
                    ds[d][g] = a * ds[d][g]
                    seg_out(d, rows8, g, 0)[...] = hs[d][g]
                    seg_out(d, rows8, g, 1)[...] = ds[d][g]
        return tuple(tuple(c) for c in hs + ds)

    zero8 = jnp.zeros((LRU_SEGS, LRU_GROUP), F32)
    one8 = jnp.ones((LRU_SEGS, LRU_GROUP), F32)
    fin = lax.fori_loop(0, S // SCAN_UNROLL, scan_body,
                        ((zero8,) * n_grp, (zero8,) * n_grp, (one8,) * n_grp, (one8,) * n_grp))
    h_end, d_end = fin[:2], fin[2:]

    carry_in = [[None] * n_grp for _ in range(N_DIR)]
    for d in range(N_DIR):
        for g in range(n_grp):
            if has_state:
                h0 = ls0_ref[0, d:d + 1, g * LRU_GROUP:(g + 1) * LRU_GROUP]
            else:
                h0 = jnp.zeros((1, LRU_GROUP), F32)
            dd, hh = scan8(d_end[d][g], h_end[d][g], d)
            seg_end = hh + dd * h0
            if d == 0:
                carry_in[d][g] = jnp.where(rowi2 >= 1, pltpu.roll(seg_end, 1, 0), h0)
                last = seg_end[LRU_SEGS - 1:LRU_SEGS, :]
            else:
                carry_in[d][g] = jnp.where(rowi2 < LRU_SEGS - 1, pltpu.roll(seg_end, LRU_SEGS - 1, 0), h0)
                last = seg_end[0:1, :]
            if emit_state:
                ls_ref[0, d:d + 1, g * LRU_GROUP:(g + 1) * LRU_GROUP] = last

    def fix_body(i, carry):
        rows = pl.ds(pl.multiple_of(i * FIX_TILE, FIX_TILE), FIX_TILE)
        for g in range(n_grp):
            tot = None
            for d in range(N_DIR):
                cin = jnp.concatenate([carry_in[d][g]] * (FIX_TILE // LRU_SEGS), axis=0)
                h = seg_out(d, rows, g, 0)[...] + seg_out(d, rows, g, 1)[...] * cin
                tot = h if tot is None else tot + h
            for l in range(slabs):
                xci[g * slabs + l, rows, :] = tot[:, l * 128:(l + 1) * 128]
        return carry

    lax.fori_loop(0, T // FIX_TILE, fix_body, 0)

    for j in range(LRU_SEGS):
        def lru_fin(ti, carry, j=j):
            i0 = pl.multiple_of(ti * CONV_TILE, CONV_TILE)
            rows = pl.ds(pl.multiple_of(j * S + i0, CONV_TILE), CONV_TILE)
            hsum = jnp.concatenate([xci[l, irows(i0, j, CONV_TILE), :] for l in range(LRU_WIDTH // 128)], axis=1)
            lg = proj[rows, COL_LG:COL_LG + LRU_WIDTH]
            gl = lg * (0.5 * (1.0 + jnp.tanh(0.7978845608028654 * (lg + 0.044715 * (lg * lg * lg)))))
            mixin[rows, HG_WIDTH:HG_WIDTH + LRU_WIDTH] = (hsum * gl).astype(BF16)
            return carry

        lax.fori_loop(0, S // CONV_TILE, lru_fin, 0)

    def out_body(i, carry):
        rows = pl.ds(pl.multiple_of(i * ROW_TILE, ROW_TILE), ROW_TILE)
        mix = jnp.dot(mixin[rows, :], wout_ref[...], preferred_element_type=F32)
        x1_ref[0, rows, :] = load_x(rows) + g1 * mix
        return carry

    lax.fori_loop(0, T // ROW_TILE, out_body, 0)


def _const_spec(shape):
    nd = len(shape)
    return pl.BlockSpec(shape, lambda b, _n=nd: (0,) * _n, pipeline_mode=pl.Buffered(1))


def _nbytes(shape, dtype):
    return int(np.prod(shape)) * jnp.dtype(dtype).itemsize


def _mixer(x, m3, m_off, m_step, pos, consts, states, emit_state):
    B, T, _ = x.shape
    has_pos = pos is not None
    has_state = states is not None
    scratch_shapes = [
        ((T, IN_COLS), F32),
        ((T + 16, LRU_WIDTH), F32),
        ((T, D_MODEL), BF16),
        ((8, HG_DK, HG_DK), F32),
        ((T, N_DIR * HG_HEADS * HG_DK), F32),
        ((LRU_WIDTH // 128, T, 128), F32),
        ((T // CHUNK, HG_HEADS, HG_DK, CHUNK), BF16),
        ((3 * N_DIR * (LRU_WIDTH // LRU_GROUP), LRU_SEGS, LRU_GROUP), F32),
    ]
    resident = (sum(_nbytes(s, d) for s, d in scratch_shapes)
                + sum(_nbytes(c.shape, c.dtype) for c in consts)
                + (_nbytes(pos.shape, pos.dtype) if has_pos else 0))
    io_block = _nbytes((T, D_MODEL), F32)
    io_bufs = 2 if resident + 4 * io_block <= VMEM_LIMIT - VMEM_HEADROOM else 1
    io_mode = pl.Buffered(io_bufs)
    in_specs = [
        pl.BlockSpec((1, T, D_MODEL), lambda b: (b, 0, 0), pipeline_mode=io_mode),
        pl.BlockSpec((1, 1, 6 * D_MODEL), lambda b: (m_off + m_step * b, 0, 0)),
    ]
    args = [x, m3]
    if has_pos:
        in_specs.append(_const_spec(pos.shape))
        args.append(pos)
    in_specs += [_const_spec(c.shape) for c in consts]
    args += list(consts)
    if has_state:
        hs0, ls0 = states
        in_specs += [
            pl.BlockSpec((1, 8, HG_DK, HG_DK), lambda b: (b, 0, 0, 0)),
            pl.BlockSpec((1, N_DIR, LRU_WIDTH), lambda b: (b, 0, 0)),
        ]
        args += [hs0, ls0]
    out_shape = [jax.ShapeDtypeStruct((B, T, D_MODEL), F32)]
    out_specs = [pl.BlockSpec((1, T, D_MODEL), lambda b: (b, 0, 0), pipeline_mode=io_mode)]
    if emit_state:
        out_shape += [jax.ShapeDtypeStruct((B, 8, HG_DK, HG_DK), F32),
                      jax.ShapeDtypeStruct((B, N_DIR, LRU_WIDTH), F32)]
        out_specs += [pl.BlockSpec((1, 8, HG_DK, HG_DK), lambda b: (b, 0, 0, 0)),
                      pl.BlockSpec((1, N_DIR, LRU_WIDTH), lambda b: (b, 0, 0))]
    scratch = [pltpu.VMEM(s, d) for s, d in scratch_shapes]
    return pl.pallas_call(
        functools.partial(_mixer_kernel, T=T, has_pos=has_pos, has_state=has_state, emit_state=emit_state),
        grid=(B,),
        in_specs=in_specs,
        out_specs=out_specs,
        out_shape=out_shape,
        scratch_shapes=scratch,
        compiler_params=pltpu.CompilerParams(
            dimension_semantics=("arbitrary",), vmem_limit_bytes=VMEM_LIMIT),
        name=f"mixer_t{T}",
    )(*args)


def _ffn_kernel(x_ref, m_ref, n2g_ref, fg_ref, w1_ref, w2_ref, y_ref):
    mrow = m_ref[0]
    sh2 = mrow[:, 3 * D_MODEL:4 * D_MODEL]
    sc2 = mrow[:, 4 * D_MODEL:5 * D_MODEL]
    g2 = mrow[:, 5 * D_MODEL:6 * D_MODEL]
    x = x_ref[...]
    ms = jnp.mean(x * x, axis=-1, keepdims=True)
    hb = (x * lax.rsqrt(ms + EPS) * (n2g_ref[...] * (1.0 + sc2)) + sh2).astype(BF16)
    ff = jnp.zeros(x.shape, F32)
    for c in range(D_FF // FF_CHUNK):
        a = jnp.dot(hb, w1_ref[:, c * FF_CHUNK:(c + 1) * FF_CHUNK], preferred_element_type=F32)
        a = jnp.maximum(a, 0.0)
        ff = ff + jnp.dot((a * a).astype(BF16), w2_ref[c * FF_CHUNK:(c + 1) * FF_CHUNK, :],
                          preferred_element_type=F32)
    x2 = x + g2 * ff
    ms2 = jnp.mean(x2 * x2, axis=-1, keepdims=True)
    y_ref[...] = x2 * lax.rsqrt(ms2 + EPS) * fg_ref[...]


def _ffn(x1, m3, m_off, tiles_per_cond, n2g, fgain, w1, w2):
    n = x1.shape[0]

    def m_index(i):
        if tiles_per_cond is None:
            return (m_off, 0, 0)
        return (m_off + i // tiles_per_cond, 0, 0)

    return pl.pallas_call(
        _ffn_kernel,
        grid=(n // FFN_TILE,),
        in_specs=[
            pl.BlockSpec((FFN_TILE, D_MODEL), lambda i: (i, 0)),
            pl.BlockSpec((1, 1, 6 * D_MODEL), m_index),
            _const_spec(n2g.shape),
            _const_spec(fgain.shape),
            _const_spec(w1.shape),
            _const_spec(w2.shape),
        ],
        out_specs=pl.BlockSpec((FFN_TILE, D_MODEL), lambda i: (i, 0)),
        out_shape=jax.ShapeDtypeStruct((n, D_MODEL), F32),
        compiler_params=pltpu.CompilerParams(
            dimension_semantics=("arbitrary",), vmem_limit_bytes=VMEM_LIMIT),
        name="ffn",
    )(x1, m3, n2g, fgain, w1, w2)


def _grid_pos_embed(n_tok):
    rows = n_tok // GRID_W
    rr, cc = np.meshgrid(np.arange(rows), np.arange(GRID_W), indexing='ij')
    quarter = D_MODEL // 4
    omega = (1.0 / (np.float32(POS_BASE) ** (np.arange(quarter, dtype=np.float32) / np.float32(quarter)))
             ).astype(np.float32)

    def emb(pos):
        ang = pos.reshape(-1, 1).astype(np.float32) * omega
        return np.concatenate([np.sin(ang), np.cos(ang)], axis=-1)

    return jnp.asarray(np.concatenate([emb(rr), emb(cc)], axis=-1), dtype=F32)


def _gate_weights(lru_wa, lru_wx):
    per_group = LRU_GROUP // LRU_BLOCK
    slabs = []
    for g in range(LRU_WIDTH // LRU_GROUP):
        cols = []
        for d in range(N_DIR):
            for w in (lru_wa, lru_wx):
                blk = jnp.zeros((LRU_GROUP, LRU_GROUP), F32)
                for n in range(per_group):
                    blk = blk.at[n * LRU_BLOCK:(n + 1) * LRU_BLOCK, n * LRU_BLOCK:(n + 1) * LRU_BLOCK].set(
                        w[d, g * per_group + n])
                cols.append(blk)
        slabs.append(jnp.concatenate(cols, axis=1))
    return jnp.stack(slabs).astype(BF16)


def kernel(x_prompt, x_sample, c, state_hgrn, state_rglru, c_ctx, w_ada, b_ada, norm1_gain, norm2_gain,
           w_in, hg_lb_logits, hg_norm_gain, conv_w, conv_b, lru_wa, lru_ba, lru_wx, lru_bx, lru_lambda,
           w_out, w_ff1, w_ff2, final_gain):
    bp, tp, _ = x_prompt.shape
    bs_, ts, _ = x_sample.shape

    cond8 = jnp.concatenate([c_ctx[None, :], c, jnp.zeros((8 - 1 - bs_, D_MODEL), F32)], axis=0)
    m3 = _modulation(cond8, w_ada[0], b_ada).reshape(8, 1, 6 * D_MODEL)

    masks_np, tri_np = _level_tables()
    grp = [slice(g * LRU_GROUP, (g + 1) * LRU_GROUP) for g in range(LRU_WIDTH // LRU_GROUP)]
    bias_g = jnp.stack([jnp.concatenate([lru_ba[0, 0, s], lru_bx[0, 0, s], lru_ba[0, 1, s], lru_bx[0, 1, s]])
                        for s in grp])
    consts = [
        norm1_gain,
        w_in[0].astype(BF16),
        hg_lb_logits.reshape(2, N_DIR * HG_HEADS, HG_DK),
        hg_norm_gain[0].reshape(1, HG_WIDTH),
        jnp.asarray(masks_np),
        jnp.asarray(tri_np, dtype=BF16),
        conv_w[0],
        conv_b,
        _gate_weights(lru_wa[0], lru_wx[0]),
        bias_g,
        lru_lambda[0],
        w_out[0].astype(BF16),
    ]
    w1 = w_ff1[0].astype(BF16)
    w2 = w_ff2[0].astype(BF16)
    fgain = final_gain.reshape(1, D_MODEL)

    x1p, hs, ls = _mixer(x_prompt, m3, 0, 0, None, consts, None, True)
    y_prompt = _ffn(x1p.reshape(bp * tp, D_MODEL), m3, 0, None, norm2_gain, fgain, w1, w2)

    x1s = _mixer(x_sample, m3, 1, 1, _grid_pos_embed(ts), consts,
                 (state_hgrn.reshape(bs_, N_DIR * HG_HEADS, HG_DK, HG_DK),
                  state_rglru.reshape(bs_, N_DIR, LRU_WIDTH)), False)[0]
    y_sample = _ffn(x1s.reshape(bs_ * ts, D_MODEL), m3, 1, ts // FFN_TILE, norm2_gain, fgain, w1, w2)

    return (y_prompt.reshape(bp, tp, D_MODEL),
            y_sample.reshape(bs_, ts, D_MODEL),
            hs.reshape(bp, 1, N_DIR, HG_HEADS, HG_DK, HG_DK),
            ls.reshape(bp, 1, N_DIR, LRU_WIDTH))
```

```python
import functools

import numpy as np
import jax
import jax.numpy as jnp
from jax import lax
from jax.experimental import pallas as pl
from jax.experimental.pallas import tpu as pltpu

F32 = jnp.float32
BF16 = jnp.bfloat16

D_MODEL = 1024
N_DIR = 2
HG_HEADS = 4
HG_DK = 128
HG_WIDTH = 512
LRU_WIDTH = 512
LRU_BLOCKS = 8
LRU_BLOCK = 64
LRU_C = 8.0
D_FF = 4096
IN_COLS = 4096
EPS = 1e-6
GRID_W = 64
POS_BASE = 10000.0

COL_Q = 0
COL_F = 1024
COL_V = 2048
COL_GATE = 2560
COL_LX = 3072
COL_LG = 3584

CHUNK = 128
LEVELS = (64, 32, 16, 8, 4, 2, 1)
ROW_TILE = 256
LRU_SEGS = 8
CONV_TILE = 32
SCAN_UNROLL = 4
FIX_TILE = 64
LRU_GROUP = 256
FFN_TILE = 512
FF_CHUNK = 1024
MOD_TILE = 1024
VMEM_LIMIT = 58 * 1024 * 1024
VMEM_HEADROOM = 6 * 1024 * 1024


def _sig(x):
    return jax.nn.sigmoid(x)


def _nt_dot(a, b):
    return lax.dot_general(a, b, (((1,), (1,)), ((), ())), preferred_element_type=F32)


def _staged_transpose(xb, slot_ref):
    slot_ref[...] = xb.T
    return slot_ref[...]


def _gram(x, slot_ref):
    xb = x.astype(BF16)
    return jnp.dot(xb, _staged_transpose(xb, slot_ref), preferred_element_type=F32)


def _level_tables():
    t = np.arange(CHUNK)[:, None]
    s = np.arange(CHUNK)[None, :]
    masks = np.zeros((N_DIR, len(LEVELS) + 1, CHUNK, CHUNK), np.float32)
    for li, h in enumerate(LEVELS):
        same = (t // (2 * h)) == (s // (2 * h))
        t_hi = (t // h) % 2 == 1
        s_hi = (s // h) % 2 == 1
        masks[0, li] = same & t_hi & ~s_hi
        masks[1, li] = same & ~t_hi & s_hi
    masks[:, len(LEVELS)] = (t == s)
    tri = np.stack([(s <= t), (s >= t)]).astype(np.float32)
    return masks, tri


def _mod_kernel(c_ref, w_ref, b_ref, o_ref):
    c = c_ref[...]
    a = (c * _sig(c)).astype(BF16)
    o_ref[...] = jnp.dot(a, w_ref[...].astype(BF16), preferred_element_type=F32) + b_ref[...]


def _modulation(cond8, w_ada, b_ada):
    n = w_ada.shape[1]
    return pl.pallas_call(
        _mod_kernel,
        grid=(n // MOD_TILE,),
        in_specs=[
            pl.BlockSpec((8, D_MODEL), lambda j: (0, 0)),
            pl.BlockSpec((D_MODEL, MOD_TILE), lambda j: (0, j)),
            pl.BlockSpec((1, MOD_TILE), lambda j: (0, j)),
        ],
        out_specs=pl.BlockSpec((8, MOD_TILE), lambda j: (0, j)),
        out_shape=jax.ShapeDtypeStruct((8, n), F32),
        compiler_params=pltpu.CompilerParams(dimension_semantics=("arbitrary",)),
        name="adaln_modulation",
    )(cond8, w_ada, b_ada)


def _mixer_kernel(*refs, T, has_pos, has_state, emit_state):
    it = iter(refs)
    x_ref = next(it)
    m_ref = next(it)
    if has_pos:
        pos_ref = next(it)
    n1g_ref = next(it)
    win_ref = next(it)
    lbl_ref = next(it)
    hgg_ref = next(it)
    masks_ref = next(it)
    tri_ref = next(it)
    convw_ref = next(it)
    convb_ref = next(it)
    wg_ref = next(it)
    bg_ref = next(it)
    lam_ref = next(it)
    wout_ref = next(it)
    if has_state:
        hs0_ref = next(it)
        ls0_ref = next(it)
    x1_ref = next(it)
    if emit_state:
        hs_ref = next(it)
        ls_ref = next(it)
    proj = next(it)
    lxp = next(it)
    mixin = next(it)
    st = next(it)
    ksc = next(it)
    xci = next(it)
    vts = next(it)
    rowc = next(it)
    tsl = next(it)

    L = CHUNK
    mrow = m_ref[0]
    sh1 = mrow[:, 0:D_MODEL]
    sc1 = mrow[:, D_MODEL:2 * D_MODEL]
    g1 = mrow[:, 2 * D_MODEL:3 * D_MODEL]
    gain1 = n1g_ref[...] * (1.0 + sc1)

    zrows = jnp.zeros((8, LRU_WIDTH), F32)
    lxp[0:8, :] = zrows
    lxp[T + 8:T + 16, :] = zrows

    def load_x(rows):
        xt = x_ref[0, rows, :]
        if has_pos:
            xt = xt + pos_ref[rows, :]
        return xt

    def proj_body(i, carry):
        r0 = pl.multiple_of(i * ROW_TILE, ROW_TILE)
        xt = load_x(pl.ds(r0, ROW_TILE))
        ms = jnp.mean(xt * xt, axis=-1, keepdims=True)
        hb = (xt * lax.rsqrt(ms + EPS) * gain1 + sh1).astype(BF16)
        for c in range(IN_COLS // 512):
            res = jnp.dot(hb, win_ref[:, c * 512:(c + 1) * 512], preferred_element_type=F32)
            if c * 512 == COL_LX:
                lxp[pl.ds(pl.multiple_of(r0 + 8, 8), ROW_TILE), :] = res
            else:
                proj[pl.ds(r0, ROW_TILE), c * 512:(c + 1) * 512] = res
        return carry

    lax.fori_loop(0, T // ROW_TILE, proj_body, 0)

    l0 = lbl_ref[0]
    l1 = lbl_ref[1]
    lmx = jnp.maximum(l0, l1)
    e0 = jnp.exp(l0 - lmx)
    e1 = jnp.exp(l1 - lmx)
    lb_all = e0 / (e0 + e1)

    rowi = lax.broadcasted_iota(jnp.int32, (8, HG_DK), 0)
    r4 = rowi & 3
    is_r0 = r4 == 0
    is_r3 = r4 == 3
    lt2 = r4 < 2
    odd = (rowi & 1) == 1
    hi4 = rowi >= 4

    for i in range(N_DIR * HG_HEADS):
        if has_state:
            st[i] = hs0_ref[0, i].T
        else:
            st[i] = jnp.zeros((HG_DK, HG_DK), F32)

    def hg_prep(c, carry):
        rows = pl.ds(pl.multiple_of(c * L, L), L)
        for hd in range(HG_HEADS):
            vts[c, hd] = proj[rows, COL_V + hd * HG_DK:COL_V + (hd + 1) * HG_DK].T.astype(BF16)
        for d in range(N_DIR):
            parts = []
            for hd in range(HG_HEADS):
                idx = d * HG_HEADS + hd
                cq = COL_Q + idx * HG_DK
                cf = COL_F + idx * HG_DK
                hq = proj[rows, cq:cq + HG_DK]
                fz = proj[rows, cf:cf + HG_DK]
                proj[rows, cq:cq + HG_DK] = hq * _sig(hq)
                sg = _sig(fz)
                lb = lb_all[idx:idx + 1, :]
                oml = 1.0 - lb
                ksc[rows, idx * HG_DK:(idx + 1) * HG_DK] = oml * (1.0 - sg)
                logf = jnp.log(lb + oml * sg)
                p1 = logf.astype(BF16)
                r1 = logf - p1.astype(F32)
                p2 = r1.astype(BF16)
                p3 = (r1 - p2.astype(F32)).astype(BF16)
                parts += [p1, p2, p3]
            bb = jnp.dot(tri_ref[d], jnp.concatenate(parts, axis=1), preferred_element_type=F32)
            for hd in range(HG_HEADS):
                cf = COL_F + (d * HG_HEADS + hd) * HG_DK
                o3 = 3 * hd * HG_DK
                proj[rows, cf:cf + HG_DK] = ((bb[:, o3:o3 + HG_DK] + bb[:, o3 + HG_DK:o3 + 2 * HG_DK])
                                             + bb[:, o3 + 2 * HG_DK:o3 + 3 * HG_DK])
        return carry

    def hg_chunk(c, d):
        r0 = pl.multiple_of(c * L, L)
        rows = pl.ds(r0, L)
        for hd in range(HG_HEADS):
            idx = d * HG_HEADS + hd
            cq = COL_Q + idx * HG_DK
            cf = COL_F + idx * HG_DK
            ck = idx * HG_DK

            def ldq(lo, n, cq=cq):
                return proj[pl.ds(pl.multiple_of(r0 + lo, 8), n), cq:cq + HG_DK]

            def ldb(lo, n, cf=cf):
                return proj[pl.ds(pl.multiple_of(r0 + lo, 8), n), cf:cf + HG_DK]

            def ldk(lo, n, ck=ck):
                return ksc[pl.ds(pl.multiple_of(r0 + lo, 8), n), ck:ck + HG_DK]

            def bline(r, cf=cf):
                grp = proj[pl.ds(pl.multiple_of(r0 + 8 * (r // 8), 8), 8), cf:cf + HG_DK]
                return grp[r % 8:r % 8 + 1, :]

            def brow(r, n):
                return jnp.broadcast_to(bline(r), (n, HG_DK))

            n_lv = len(LEVELS)
            acc_rows = [None] * (L // 8)

            def accumulate(li, p, row0):
                for i in range(p.shape[0] // 8):
                    g = row0 // 8 + i
                    term = masks_ref[d, li, 8 * g:8 * g + 8, :] * p[8 * i:8 * i + 8]
                    acc_rows[g] = term if acc_rows[g] is None else acc_rows[g] + term

            accumulate(n_lv, jnp.dot(
                ldq(0, L).astype(BF16), _staged_transpose(ldk(0, L).astype(BF16), tsl.at[idx, n_lv]),
                preferred_element_type=F32), 0)
            for li, h in enumerate(LEVELS[:4]):
                pieces, q_pieces, q_starts = [], [], []
                for j in range(L // (2 * h)):
                    lo = j * 2 * h
                    mid = lo + h
                    if d == 0:
                        bm = brow(mid - 1, h)
                        kp = ldk(lo, h) * jnp.exp(bm - ldb(lo, h))
                        qp = ldq(mid, h) * jnp.exp(ldb(mid, h) - bm)
                        pieces += [kp, qp]
                        q_starts.append(mid)
                    else:
                        bm = brow(mid, h)
                        qp = ldq(lo, h) * jnp.exp(ldb(lo, h) - bm)
                        kp = ldk(mid, h) * jnp.exp(bm - ldb(mid, h))
                        pieces += [qp, kp]
                        q_starts.append(lo)
                    q_pieces.append(qp)
                xt = _staged_transpose(jnp.concatenate(pieces, axis=0).astype(BF16), tsl.at[idx, li])
                p = jnp.dot(jnp.concatenate(q_pieces, axis=0).astype(BF16), xt, preferred_element_type=F32)
                for j, row0 in enumerate(q_starts):
                    accumulate(li, p[j * h:(j + 1) * h], row0)
            x4, x2, x1 = [], [], []
            for g in range(L // 8):
                qg, kg, bg = ldq(8 * g, 8), ldk(8 * g, 8), ldb(8 * g, 8)
                fg = 1.0 - kg
                qfg = qg * fg
                mid = 8 * g + (3 if d == 0 else 4)
                e4 = jnp.exp(-jnp.abs(bg - brow(mid, 8)))
                fnx = pltpu.roll(fg, 7, 0)
                fpv = pltpu.roll(fg, 1, 0)
                s0 = jnp.where(is_r0, fnx, 1.0)
                s3 = jnp.where(is_r3, fpv, 1.0)
                if d == 0:
                    x4.append(jnp.where(hi4, qg, kg) * e4)
                    x2.append(jnp.where(lt2, kg * s0, qfg * s3))
                    x1.append(jnp.where(odd, qfg, kg))
                else:
                    x4.append(jnp.where(hi4, kg, qg) * e4)
                    x2.append(jnp.where(lt2, qfg * s0, kg * s3))
                    x1.append(jnp.where(odd, kg, qfg))
            for li, pcs in ((4, x4), (5, x2), (6, x1)):
                accumulate(li, _gram(jnp.concatenate(pcs, axis=0), tsl.at[idx, li]), 0)
            acc = jnp.concatenate(acc_rows, axis=0)

            vt = vts[c, hd]
            st_t = st[idx]
            b = ldb(0, L)
            qt = (ldq(0, L) * jnp.exp(b)).astype(BF16)
            vb = proj[rows, COL_V + hd * HG_DK:COL_V + (hd + 1) * HG_DK].astype(BF16)
            o = jnp.dot(jnp.concatenate([acc.astype(BF16), qt], axis=1),
                        jnp.concatenate([vb, _staged_transpose(st_t.astype(BF16), tsl.at[idx, n_lv + 1])], axis=0),
                        preferred_element_type=F32)
            btot = bline(L - 1 if d == 0 else 0)
            kt = (ldk(0, L) * jnp.exp(btot - b)).astype(BF16)
            st[idx] = st_t * jnp.exp(btot) + jnp.dot(vt, kt, preferred_element_type=F32)
            x1_ref[0, rows, d * HG_WIDTH + hd * HG_DK:d * HG_WIDTH + (hd + 1) * HG_DK] = o

    n_chunks = T // L

    def hg_both(c, carry):
        hg_chunk(c, 0)
        hg_chunk(n_chunks - 1 - c, 1)
        return carry

    lax.fori_loop(0, n_chunks, hg_prep, 0)
    lax.fori_loop(0, n_chunks, hg_both, 0)

    if emit_state:
        for i in range(N_DIR * HG_HEADS):
            hs_ref[0, i] = st[i].T

    def hg_fin(i, carry):
        rows = pl.ds(pl.multiple_of(i * L, L), L)
        for hd in range(HG_HEADS):
            cs = slice(hd * HG_DK, (hd + 1) * HG_DK)
            o = x1_ref[0, rows, cs] + x1_ref[0, rows, HG_WIDTH + hd * HG_DK:HG_WIDTH + (hd + 1) * HG_DK]
            ms = jnp.mean(o * o, axis=-1, keepdims=True)
            y = o * lax.rsqrt(ms + EPS) * hgg_ref[:, cs]
            gz = proj[rows, COL_GATE + hd * HG_DK:COL_GATE + (hd + 1) * HG_DK]
            mixin[rows, cs] = (y * (gz * _sig(gz))).astype(BF16)
        return carry

    lax.fori_loop(0, n_chunks, hg_fin, 0)

    S = T // LRU_SEGS
    n_grp = LRU_WIDTH // LRU_GROUP
    slabs = LRU_GROUP // 128
    cw = convw_ref[...]
    cb = convb_ref[...]

    def irows(i0, j, n):
        return pl.ds(pl.multiple_of(LRU_SEGS * i0, 8) + j, n, stride=LRU_SEGS)

    for j in range(LRU_SEGS):
        def conv_body(ti, carry, j=j):
            i0 = pl.multiple_of(ti * CONV_TILE, CONV_TILE)
            win = lxp[pl.ds(pl.multiple_of(j * S + i0, 8), CONV_TILE + 16), :]
            xc = cb
            for tap in range(4):
                xc = xc + win[6 + tap:6 + tap + CONV_TILE] * cw[tap:tap + 1]
            for l in range(LRU_WIDTH // 128):
                xci[l, irows(i0, j, CONV_TILE), :] = xc[:, l * 128:(l + 1) * 128]
            return carry

        lax.fori_loop(0, S // CONV_TILE, conv_body, 0)

    def xc_tile(rows, g):
        return jnp.concatenate([xci[g * slabs + l, rows, :] for l in range(slabs)], axis=1)

    def gate_body(i, carry):
        rows = pl.ds(pl.multiple_of(i * ROW_TILE, ROW_TILE), ROW_TILE)
        for g in range(n_grp):
            proj[rows, g * 1024:(g + 1) * 1024] = jnp.dot(
                xc_tile(rows, g).astype(BF16), wg_ref[g], preferred_element_type=F32)
        return carry

    lax.fori_loop(0, T // ROW_TILE, gate_body, 0)

    lam = lam_ref[...]
    nl = -lam
    c8 = -LRU_C * (jnp.maximum(nl, 0.0) + jnp.log1p(jnp.exp(-jnp.abs(nl))))
    rowi2 = lax.broadcasted_iota(jnp.int32, (8, LRU_GROUP), 0)

    for d in range(N_DIR):
        for g in range(n_grp):
            k3 = 3 * (d * n_grp + g)
            for r, row in enumerate((bg_ref[g:g + 1, d * 512:d * 512 + LRU_GROUP],
                                     bg_ref[g:g + 1, d * 512 + LRU_GROUP:d * 512 + 2 * LRU_GROUP],
                                     c8[d:d + 1, g * LRU_GROUP:(g + 1) * LRU_GROUP])):
                rowc[k3 + r] = jnp.broadcast_to(row, (LRU_SEGS, LRU_GROUP))

    def seg_out(d, rows, g, part):
        cols = slice((g * 2 + part) * LRU_GROUP, (g * 2 + part + 1) * LRU_GROUP)
        return (ksc.at[rows, cols] if d == 0 else x1_ref.at[0, rows, cols])

    def lru_inputs(rows8, g, d):
        base = g * 1024 + d * 512
        k3 = 3 * (d * n_grp + g)
        ga = proj[rows8, base:base + LRU_GROUP] + rowc[k3]
        gx = proj[rows8, base + LRU_GROUP:base + 2 * LRU_GROUP] + rowc[k3 + 1]
        xc8 = xc_tile(rows8, g)
        log_a = rowc[k3 + 2] * _sig(ga)
        a = jnp.exp(log_a)
        mult = jnp.sqrt(jnp.tanh(-log_a) * (1.0 + a * a))
        return a, mult * (_sig(gx) * xc8)

    def scan8(a, u, d):
        for sft in (1, 2, 4):
            if d == 0:
                keep = rowi2 >= sft
                amt = sft
            else:
                keep = rowi2 < 8 - sft
                amt = 8 - sft
            ash = jnp.where(keep, pltpu.roll(a, amt, 0), 1.0)
            ush = jnp.where(keep, pltpu.roll(u, amt, 0), 0.0)
            u = a * ush + u
            a = a * ash
        return a, u

    def scan_body(n, carry):
        hs, ds = [list(c) for c in carry[:2]], [list(c) for c in carry[2:]]
        for u in range(SCAN_UNROLL):
            i_f = n * SCAN_UNROLL + u
            for d, i in ((0, i_f), (1, S - 1 - i_f)):
                rows8 = pl.ds(pl.multiple_of(LRU_SEGS * i, 8), 8)
                for g in range(n_grp):
                    a, uu = lru_inputs(rows8, g, d)
                    hs[d][g] = a * hs[d][g] + uu
                    ds[d][g] = a * ds[d][g]
                    seg_out(d, rows8, g, 0)[...] = hs[d][g]
                    seg_out(d, rows8, g, 1)[...] = ds[d][g]
        return tuple(tuple(c) for c in hs + ds)

    zero8 = jnp.zeros((LRU_SEGS, LRU_GROUP), F32)
    one8 = jnp.ones((LRU_SEGS, LRU_GROUP), F32)
    fin = lax.fori_loop(0, S // SCAN_UNROLL, scan_body,
                        ((zero8,) * n_grp, (zero8,) * n_grp, (one8,) * n_grp, (one8,) * n_grp))
    h_end, d_end = fin[:2], fin[2:]

    carry_in = [[None] * n_grp for _ in range(N_DIR)]
    for d in range(N_DIR):
        for g in range(n_grp):
            if has_state:
                h0 = ls0_ref[0, d:d + 1, g * LRU_GROUP:(g + 1) * LRU_GROUP]
            else:
                h0 = jnp.zeros((1, LRU_GROUP), F32)
            dd, hh = scan8(d_end[d][g], h_end[d][g], d)
            seg_end = hh + dd * h0
            if d == 0:
                carry_in[d][g] = jnp.where(rowi2 >= 1, pltpu.roll(seg_end, 1, 0), h0)
                last = seg_end[LRU_SEGS - 1:LRU_SEGS, :]
            else:
                carry_in[d][g] = jnp.where(rowi2 < LRU_SEGS - 1, pltpu.roll(seg_end, LRU_SEGS - 1, 0), h0)
                last = seg_end[0:1, :]
            if emit_state:
                ls_ref[0, d:d + 1, g * LRU_GROUP:(g + 1) * LRU_GROUP] = last

    def fix_body(i, carry):
        rows = pl.ds(pl.multiple_of(i * FIX_TILE, FIX_TILE), FIX_TILE)
        for g in range(n_grp):
            tot = None
            for d in range(N_DIR):
                cin = jnp.concatenate([carry_in[d][g]] * (FIX_TILE // LRU_SEGS), axis=0)
                h = seg_out(d, rows, g, 0)[...] + seg_out(d, rows, g, 1)[...] * cin
                tot = h if tot is None else tot + h
            for l in range(slabs):
                xci[g * slabs + l, rows, :] = tot[:, l * 128:(l + 1) * 128]
        return carry

    lax.fori_loop(0, T // FIX_TILE, fix_body, 0)

    for j in range(LRU_SEGS):
        def lru_fin(ti, carry, j=j):
            i0 = pl.multiple_of(ti * CONV_TILE, CONV_TILE)
            rows = pl.ds(pl.multiple_of(j * S + i0, CONV_TILE), CONV_TILE)
            hsum = jnp.concatenate([xci[l, irows(i0, j, CONV_TILE), :] for l in range(LRU_WIDTH // 128)], axis=1)
            lg = proj[rows, COL_LG:COL_LG + LRU_WIDTH]
            gl = lg * (0.5 * (1.0 + jnp.tanh(0.7978845608028654 * (lg + 0.044715 * (lg * lg * lg)))))
            mixin[rows, HG_WIDTH:HG_WIDTH + LRU_WIDTH] = (hsum * gl).astype(BF16)
            return carry

        lax.fori_loop(0, S // CONV_TILE, lru_fin, 0)

    def out_body(i, carry):
        rows = pl.ds(pl.multiple_of(i * ROW_TILE, ROW_TILE), ROW_TILE)
        mix = jnp.dot(mixin[rows, :], wout_ref[...], preferred_element_type=F32)
        x1_ref[0, rows, :] = load_x(rows) + g1 * mix
        return carry

    lax.fori_loop(0, T // ROW_TILE, out_body, 0)


def _const_spec(shape):
    nd = len(shape)
    return pl.BlockSpec(shape, lambda b, _n=nd: (0,) * _n, pipeline_mode=pl.Buffered(1))


def _nbytes(shape, dtype):
    return int(np.prod(shape)) * jnp.dtype(dtype).itemsize


def _mixer(x, m3, m_off, m_step, pos, consts, states, emit_state):
    B, T, _ = x.shape
    has_pos = pos is not None
    has_state = states is not None
    scratch_shapes = [
        ((T, IN_COLS), F32),
        ((T + 16, LRU_WIDTH), F32),
        ((T, D_MODEL), BF16),
        ((8, HG_DK, HG_DK), F32),
        ((T, N_DIR * HG_HEADS * HG_DK), F32),
        ((LRU_WIDTH // 128, T, 128), F32),
        ((T // CHUNK, HG_HEADS, HG_DK, CHUNK), BF16),
        ((3 * N_DIR * (LRU_WIDTH // LRU_GROUP), LRU_SEGS, LRU_GROUP), F32),
        ((N_DIR * HG_HEADS, len(LEVELS) + 2, HG_DK, CHUNK), BF16),
    ]
    resident = (sum(_nbytes(s, d) for s, d in scratch_shapes)
                + sum(_nbytes(c.shape, c.dtype) for c in consts)
                + (_nbytes(pos.shape, pos.dtype) if has_pos else 0))
    io_block = _nbytes((T, D_MODEL), F32)
    io_bufs = 2 if resident + 4 * io_block <= VMEM_LIMIT - VMEM_HEADROOM else 1
    io_mode = pl.Buffered(io_bufs)
    in_specs = [
        pl.BlockSpec((1, T, D_MODEL), lambda b: (b, 0, 0), pipeline_mode=io_mode),
        pl.BlockSpec((1, 1, 6 * D_MODEL), lambda b: (m_off + m_step * b, 0, 0)),
    ]
    args = [x, m3]
    if has_pos:
        in_specs.append(_const_spec(pos.shape))
        args.append(pos)
    in_specs += [_const_spec(c.shape) for c in consts]
    args += list(consts)
    if has_state:
        hs0, ls0 = states
        in_specs += [
            pl.BlockSpec((1, 8, HG_DK, HG_DK), lambda b: (b, 0, 0, 0)),
            pl.BlockSpec((1, N_DIR, LRU_WIDTH), lambda b: (b, 0, 0)),
        ]
        args += [hs0, ls0]
    out_shape = [jax.ShapeDtypeStruct((B, T, D_MODEL), F32)]
    out_specs = [pl.BlockSpec((1, T, D_MODEL), lambda b: (b, 0, 0), pipeline_mode=io_mode)]
    if emit_state:
        out_shape += [jax.ShapeDtypeStruct((B, 8, HG_DK, HG_DK), F32),
                      jax.ShapeDtypeStruct((B, N_DIR, LRU_WIDTH), F32)]
        out_specs += [pl.BlockSpec((1, 8, HG_DK, HG_DK), lambda b: (b, 0, 0, 0)),
                      pl.BlockSpec((1, N_DIR, LRU_WIDTH), lambda b: (b, 0, 0))]
    scratch = [pltpu.VMEM(s, d) for s, d in scratch_shapes]
    return pl.pallas_call(
        functools.partial(_mixer_kernel, T=T, has_pos=has_pos, has_state=has_state, emit_state=emit_state),
        grid=(B,),
        in_specs=in_specs,
        out_specs=out_specs,
        out_shape=out_shape,
        scratch_shapes=scratch,
        compiler_params=pltpu.CompilerParams(
            dimension_semantics=("arbitrary",), vmem_limit_bytes=VMEM_LIMIT),
        name=f"mixer_t{T}",
    )(*args)


def _ffn_kernel(x_ref, m_ref, n2g_ref, fg_ref, w1_ref, w2_ref, y_ref):
    mrow = m_ref[0]
    sh2 = mrow[:, 3 * D_MODEL:4 * D_MODEL]
    sc2 = mrow[:, 4 * D_MODEL:5 * D_MODEL]
    g2 = mrow[:, 5 * D_MODEL:6 * D_MODEL]
    x = x_ref[...]
    ms = jnp.mean(x * x, axis=-1, keepdims=True)
    hb = (x * lax.rsqrt(ms + EPS) * (n2g_ref[...] * (1.0 + sc2)) + sh2).astype(BF16)
    ff = jnp.zeros(x.shape, F32)
    for c in range(D_FF // FF_CHUNK):
        a = jnp.dot(hb, w1_ref[:, c * FF_CHUNK:(c + 1) * FF_CHUNK], preferred_element_type=F32)
        a = jnp.maximum(a, 0.0)
        ff = ff + jnp.dot((a * a).astype(BF16), w2_ref[c * FF_CHUNK:(c + 1) * FF_CHUNK, :],
                          preferred_element_type=F32)
    x2 = x + g2 * ff
    ms2 = jnp.mean(x2 * x2, axis=-1, keepdims=True)
    y_ref[...] = x2 * lax.rsqrt(ms2 + EPS) * fg_ref[...]


def _ffn(x1, m3, m_off, tiles_per_cond, n2g, fgain, w1, w2):
    n = x1.shape[0]

    def m_index(i):
        if tiles_per_cond is None:
            return (m_off, 0, 0)
        return (m_off + i // tiles_per_cond, 0, 0)

    return pl.pallas_call(
        _ffn_kernel,
        grid=(n // FFN_TILE,),
        in_specs=[
            pl.BlockSpec((FFN_TILE, D_MODEL), lambda i: (i, 0)),
            pl.BlockSpec((1, 1, 6 * D_MODEL), m_index),
            _const_spec(n2g.shape),
            _const_spec(fgain.shape),
            _const_spec(w1.shape),
            _const_spec(w2.shape),
        ],
        out_specs=pl.BlockSpec((FFN_TILE, D_MODEL), lambda i: (i, 0)),
        out_shape=jax.ShapeDtypeStruct((n, D_MODEL), F32),
        compiler_params=pltpu.CompilerParams(
            dimension_semantics=("arbitrary",), vmem_limit_bytes=VMEM_LIMIT),
        name="ffn",
    )(x1, m3, n2g, fgain, w1, w2)


def _grid_pos_embed(n_tok):
    rows = n_tok // GRID_W
    rr, cc = np.meshgrid(np.arange(rows), np.arange(GRID_W), indexing='ij')
    quarter = D_MODEL // 4
    omega = (1.0 / (np.float32(POS_BASE) ** (np.arange(quarter, dtype=np.float32) / np.float32(quarter)))
             ).astype(np.float32)

    def emb(pos):
        ang = pos.reshape(-1, 1).astype(np.float32) * omega
        return np.concatenate([np.sin(ang), np.cos(ang)], axis=-1)

    return jnp.asarray(np.concatenate([emb(rr), emb(cc)], axis=-1), dtype=F32)


def _gate_weights(lru_wa, lru_wx):
    per_group = LRU_GROUP // LRU_BLOCK
    slabs = []
    for g in range(LRU_WIDTH // LRU_GROUP):
        cols = []
        for d in range(N_DIR):
            for w in (lru_wa, lru_wx):
                blk = jnp.zeros((LRU_GROUP, LRU_GROUP), F32)
                for n in range(per_group):
                    blk = blk.at[n * LRU_BLOCK:(n + 1) * LRU_BLOCK, n * LRU_BLOCK:(n + 1) * LRU_BLOCK].set(
                        w[d, g * per_group + n])
                cols.append(blk)
        slabs.append(jnp.concatenate(cols, axis=1))
    return jnp.stack(slabs).astype(BF16)


def kernel(x_prompt, x_sample, c, state_hgrn, state_rglru, c_ctx, w_ada, b_ada, norm1_gain, norm2_gain,
           w_in, hg_lb_logits, hg_norm_gain, conv_w, conv_b, lru_wa, lru_ba, lru_wx, lru_bx, lru_lambda,
           w_out, w_ff1, w_ff2, final_gain):
    bp, tp, _ = x_prompt.shape
    bs_, ts, _ = x_sample.shape

    cond8 = jnp.concatenate([c_ctx[None, :], c, jnp.zeros((8 - 1 - bs_, D_MODEL), F32)], axis=0)
    m3 = _modulation(cond8, w_ada[0], b_ada).reshape(8, 1, 6 * D_MODEL)

    masks_np, tri_np = _level_tables()
    grp = [slice(g * LRU_GROUP, (g + 1) * LRU_GROUP) for g in range(LRU_WIDTH // LRU_GROUP)]
    bias_g = jnp.stack([jnp.concatenate([lru_ba[0, 0, s], lru_bx[0, 0, s], lru_ba[0, 1, s], lru_bx[0, 1, s]])
                        for s in grp])
    consts = [
        norm1_gain,
        w_in[0].astype(BF16),
        hg_lb_logits.reshape(2, N_DIR * HG_HEADS, HG_DK),
        hg_norm_gain[0].reshape(1, HG_WIDTH),
        jnp.asarray(masks_np),
        jnp.asarray(tri_np, dtype=BF16),
        conv_w[0],
        conv_b,
        _gate_weights(lru_wa[0], lru_wx[0]),
        bias_g,
        lru_lambda[0],
        w_out[0].astype(BF16),
    ]
    w1 = w_ff1[0].astype(BF16)
    w2 = w_ff2[0].astype(BF16)
    fgain = final_gain.reshape(1, D_MODEL)

    x1p, hs, ls = _mixer(x_prompt, m3, 0, 0, None, consts, None, True)
    y_prompt = _ffn(x1p.reshape(bp * tp, D_MODEL), m3, 0, None, norm2_gain, fgain, w1, w2)

    x1s = _mixer(x_sample, m3, 1, 1, _grid_pos_embed(ts), consts,
                 (state_hgrn.reshape(bs_, N_DIR * HG_HEADS, HG_DK, HG_DK),
                  state_rglru.reshape(bs_, N_DIR, LRU_WIDTH)), False)[0]
    y_sample = _ffn(x1s.reshape(bs_ * ts, D_MODEL), m3, 1, ts // FFN_TILE, norm2_gain, fgain, w1, w2)

    return (y_prompt.reshape(bp, tp, D_MODEL),
            y_sample.reshape(bs_, ts, D_MODEL),
            hs.reshape(bp, 1, N_DIR, HG_HEADS, HG_DK, HG_DK),
            ls.reshape(bp, 1, N_DIR, LRU_WIDTH))
```

```python
import functools

import numpy as np
import jax
import jax.numpy as jnp
from jax import lax
from jax.experimental import pallas as pl
from jax.experimental.pallas import tpu as pltpu

F32 = jnp.float32
BF16 = jnp.bfloat16

D_MODEL = 1024
N_DIR = 2
HG_HEADS = 4
HG_DK = 128
HG_WIDTH = 512
LRU_WIDTH = 512
LRU_BLOCKS = 8
LRU_BLOCK = 64
LRU_C = 8.0
D_FF = 4096
IN_COLS = 4096
EPS = 1e-6
LOG2E = 1.4426950408889634
GRID_W = 64
POS_BASE = 10000.0

COL_Q = 0
COL_F = 1024
COL_V = 2048
COL_GATE = 2560
COL_LX = 3072
COL_LG = 3584

CHUNK = 128
LEVELS = (64, 32, 16, 8, 4, 2, 1)
ROW_TILE = 256
LRU_SEGS = 8
CONV_TILE = 32
SCAN_UNROLL = 4
FIX_TILE = 64
LRU_GROUP = 256
FFN_TILE = 512
FF_CHUNK = 1024
MOD_TILE = 1024
VMEM_LIMIT = 58 * 1024 * 1024
VMEM_HEADROOM = 6 * 1024 * 1024


def _sig(x):
    return jax.nn.sigmoid(x)


def _nt_dot(a, b):
    return lax.dot_general(a, b, (((1,), (1,)), ((), ())), preferred_element_type=F32)


def _staged_transpose(xb, slot_ref):
    slot_ref[...] = xb.T
    return slot_ref[...]


def _gram(x, slot_ref):
    xb = x.astype(BF16)
    return jnp.dot(xb, _staged_transpose(xb, slot_ref), preferred_element_type=F32)


def _level_tables():
    t = np.arange(CHUNK)[:, None]
    s = np.arange(CHUNK)[None, :]
    masks = np.zeros((N_DIR, len(LEVELS) + 1, CHUNK, CHUNK), np.float32)
    for li, h in enumerate(LEVELS):
        same = (t // (2 * h)) == (s // (2 * h))
        t_hi = (t // h) % 2 == 1
        s_hi = (s // h) % 2 == 1
        masks[0, li] = same & t_hi & ~s_hi
        masks[1, li] = same & ~t_hi & s_hi
    masks[:, len(LEVELS)] = (t == s)
    tri = np.stack([(s <= t), (s >= t)]).astype(np.float32)
    return masks, tri


def _mod_kernel(c_ref, w_ref, b_ref, o_ref):
    c = c_ref[...]
    a = (c * _sig(c)).astype(BF16)
    o_ref[...] = jnp.dot(a, w_ref[...].astype(BF16), preferred_element_type=F32) + b_ref[...]


def _modulation(cond8, w_ada, b_ada):
    n = w_ada.shape[1]
    return pl.pallas_call(
        _mod_kernel,
        grid=(n // MOD_TILE,),
        in_specs=[
            pl.BlockSpec((8, D_MODEL), lambda j: (0, 0)),
            pl.BlockSpec((D_MODEL, MOD_TILE), lambda j: (0, j)),
            pl.BlockSpec((1, MOD_TILE), lambda j: (0, j)),
        ],
        out_specs=pl.BlockSpec((8, MOD_TILE), lambda j: (0, j)),
        out_shape=jax.ShapeDtypeStruct((8, n), F32),
        compiler_params=pltpu.CompilerParams(dimension_semantics=("arbitrary",)),
        name="adaln_modulation",
    )(cond8, w_ada, b_ada)


def _mixer_kernel(*refs, T, has_pos, has_state, emit_state):
    it = iter(refs)
    x_ref = next(it)
    m_ref = next(it)
    if has_pos:
        pos_ref = next(it)
    n1g_ref = next(it)
    win_ref = next(it)
    lbl_ref = next(it)
    hgg_ref = next(it)
    masks_ref = next(it)
    tri_ref = next(it)
    convw_ref = next(it)
    convb_ref = next(it)
    gw_ref = next(it)
    bg_ref = next(it)
    lam_ref = next(it)
    wout_ref = next(it)
    if has_state:
        hs0_ref = next(it)
        ls0_ref = next(it)
    x1_ref = next(it)
    if emit_state:
        hs_ref = next(it)
        ls_ref = next(it)
    proj = next(it)
    lxp = next(it)
    mixin = next(it)
    st = next(it)
    ksc = next(it)
    xci = next(it)
    vts = next(it)
    rowc = next(it)
    tsl = next(it)
    wg = next(it)

    L = CHUNK

    @pl.when(pl.program_id(0) == 0)
    def _build_gate_weights():
        per_group = LRU_GROUP // LRU_BLOCK
        wg[...] = jnp.zeros(wg.shape, BF16)
        for g in range(LRU_WIDTH // LRU_GROUP):
            for p in range(2 * N_DIR):
                for n in range(per_group):
                    r = n * LRU_BLOCK
                    col = p * LRU_GROUP + r
                    wg[g, r:r + LRU_BLOCK, col:col + LRU_BLOCK] = gw_ref[
                        p * LRU_BLOCKS + g * per_group + n].astype(BF16)

    mrow = m_ref[0]
    sh1 = mrow[:, 0:D_MODEL]
    sc1 = mrow[:, D_MODEL:2 * D_MODEL]
    g1 = mrow[:, 2 * D_MODEL:3 * D_MODEL]
    gain1 = n1g_ref[...] * (1.0 + sc1)

    zrows = jnp.zeros((8, LRU_WIDTH), F32)
    lxp[0:8, :] = zrows
    lxp[T + 8:T + 16, :] = zrows

    def load_x(rows):
        xt = x_ref[0, rows, :]
        if has_pos:
            xt = xt + pos_ref[rows, :]
        return xt

    def proj_body(i, carry):
        r0 = pl.multiple_of(i * ROW_TILE, ROW_TILE)
        xt = load_x(pl.ds(r0, ROW_TILE))
        ms = jnp.mean(xt * xt, axis=-1, keepdims=True)
        hb = (xt * lax.rsqrt(ms + EPS) * gain1 + sh1).astype(BF16)
        for c in range(IN_COLS // 512):
            res = jnp.dot(hb, win_ref[:, c * 512:(c + 1) * 512], preferred_element_type=F32)
            if c * 512 == COL_LX:
                lxp[pl.ds(pl.multiple_of(r0 + 8, 8), ROW_TILE), :] = res
            else:
                proj[pl.ds(r0, ROW_TILE), c * 512:(c + 1) * 512] = res
        return carry

    lax.fori_loop(0, T // ROW_TILE, proj_body, 0)

    l0 = lbl_ref[0]
    l1 = lbl_ref[1]
    lmx = jnp.maximum(l0, l1)
    e0 = jnp.exp(l0 - lmx)
    e1 = jnp.exp(l1 - lmx)
    lb_all = e0 / (e0 + e1)

    rowi = lax.broadcasted_iota(jnp.int32, (8, HG_DK), 0)
    r4 = rowi & 3
    is_r0 = r4 == 0
    is_r3 = r4 == 3
    lt2 = r4 < 2
    odd = (rowi & 1) == 1
    hi4 = rowi >= 4

    for i in range(N_DIR * HG_HEADS):
        if has_state:
            st[i] = hs0_ref[0, i].T
        else:
            st[i] = jnp.zeros((HG_DK, HG_DK), F32)

    def hg_prep(c, carry):
        rows = pl.ds(pl.multiple_of(c * L, L), L)
        for hd in range(HG_HEADS):
            vts[c, hd] = proj[rows, COL_V + hd * HG_DK:COL_V + (hd + 1) * HG_DK].T.astype(BF16)
        for d in range(N_DIR):
            parts = []
            for hd in range(HG_HEADS):
                idx = d * HG_HEADS + hd
                cq = COL_Q + idx * HG_DK
                cf = COL_F + idx * HG_DK
                hq = proj[rows, cq:cq + HG_DK]
                fz = proj[rows, cf:cf + HG_DK]
                proj[rows, cq:cq + HG_DK] = hq * _sig(hq)
                sg = _sig(fz)
                lb = lb_all[idx:idx + 1, :]
                oml = 1.0 - lb
                ksc[rows, idx * HG_DK:(idx + 1) * HG_DK] = oml * (1.0 - sg)
                logf = jnp.log(lb + oml * sg)
                p1 = logf.astype(BF16)
                p2 = (logf - p1.astype(F32)).astype(BF16)
                parts += [p1, p2]
            bb = jnp.dot(tri_ref[d], jnp.concatenate(parts, axis=1), preferred_element_type=F32)
            for hd in range(HG_HEADS):
                cf = COL_F + (d * HG_HEADS + hd) * HG_DK
                o2 = 2 * hd * HG_DK
                proj[rows, cf:cf + HG_DK] = (bb[:, o2:o2 + HG_DK] + bb[:, o2 + HG_DK:o2 + 2 * HG_DK]) * LOG2E
        return carry

    def hg_chunk(c, d):
        r0 = pl.multiple_of(c * L, L)
        rows = pl.ds(r0, L)
        for hd in range(HG_HEADS):
            idx = d * HG_HEADS + hd
            cq = COL_Q + idx * HG_DK
            cf = COL_F + idx * HG_DK
            ck = idx * HG_DK

            def ldq(lo, n, cq=cq):
                return proj[pl.ds(pl.multiple_of(r0 + lo, 8), n), cq:cq + HG_DK]

            def ldb(lo, n, cf=cf):
                return proj[pl.ds(pl.multiple_of(r0 + lo, 8), n), cf:cf + HG_DK]

            def ldk(lo, n, ck=ck):
                return ksc[pl.ds(pl.multiple_of(r0 + lo, 8), n), ck:ck + HG_DK]

            def bline(r, cf=cf):
                grp = proj[pl.ds(pl.multiple_of(r0 + 8 * (r // 8), 8), 8), cf:cf + HG_DK]
                return grp[r % 8:r % 8 + 1, :]

            def brow(r, n):
                return jnp.broadcast_to(bline(r), (n, HG_DK))

            n_lv = len(LEVELS)
            acc_rows = [None] * (L // 8)

            def accumulate(li, p, row0):
                for i in range(p.shape[0] // 8):
                    g = row0 // 8 + i
                    term = masks_ref[d, li, 8 * g:8 * g + 8, :] * p[8 * i:8 * i + 8]
                    acc_rows[g] = term if acc_rows[g] is None else acc_rows[g] + term

            accumulate(n_lv, jnp.dot(
                ldq(0, L).astype(BF16), _staged_transpose(ldk(0, L).astype(BF16), tsl.at[idx, n_lv]),
                preferred_element_type=F32), 0)
            for li, h in enumerate(LEVELS[:4]):
                pieces, q_pieces, q_starts = [], [], []
                for j in range(L // (2 * h)):
                    lo = j * 2 * h
                    mid = lo + h
                    if d == 0:
                        bm = brow(mid - 1, h)
                        kp = ldk(lo, h) * jnp.exp2(bm - ldb(lo, h))
                        qp = ldq(mid, h) * jnp.exp2(ldb(mid, h) - bm)
                        pieces += [kp, qp]
                        q_starts.append(mid)
                    else:
                        bm = brow(mid, h)
                        qp = ldq(lo, h) * jnp.exp2(ldb(lo, h) - bm)
                        kp = ldk(mid, h) * jnp.exp2(bm - ldb(mid, h))
                        pieces += [qp, kp]
                        q_starts.append(lo)
                    q_pieces.append(qp)
                xt = _staged_transpose(jnp.concatenate(pieces, axis=0).astype(BF16), tsl.at[idx, li])
                p = jnp.dot(jnp.concatenate(q_pieces, axis=0).astype(BF16), xt, preferred_element_type=F32)
                for j, row0 in enumerate(q_starts):
                    accumulate(li, p[j * h:(j + 1) * h], row0)
            x4, x2, x1 = [], [], []
            for g in range(L // 8):
                qg, kg, bg = ldq(8 * g, 8), ldk(8 * g, 8), ldb(8 * g, 8)
                fg = 1.0 - kg
                qfg = qg * fg
                mid = 8 * g + (3 if d == 0 else 4)
                e4 = jnp.exp2(-jnp.abs(bg - brow(mid, 8)))
                fnx = pltpu.roll(fg, 7, 0)
                fpv = pltpu.roll(fg, 1, 0)
                s0 = jnp.where(is_r0, fnx, 1.0)
                s3 = jnp.where(is_r3, fpv, 1.0)
                if d == 0:
                    x4.append(jnp.where(hi4, qg, kg) * e4)
                    x2.append(jnp.where(lt2, kg * s0, qfg * s3))
                    x1.append(jnp.where(odd, qfg, kg))
                else:
                    x4.append(jnp.where(hi4, kg, qg) * e4)
                    x2.append(jnp.where(lt2, qfg * s0, kg * s3))
                    x1.append(jnp.where(odd, kg, qfg))
            for li, pcs in ((4, x4), (5, x2), (6, x1)):
                accumulate(li, _gram(jnp.concatenate(pcs, axis=0), tsl.at[idx, li]), 0)
            acc = jnp.concatenate(acc_rows, axis=0)

            vt = vts[c, hd]
            st_t = st[idx]
            b = ldb(0, L)
            qt = (ldq(0, L) * jnp.exp2(b)).astype(BF16)
            vb = proj[rows, COL_V + hd * HG_DK:COL_V + (hd + 1) * HG_DK].astype(BF16)
            o = jnp.dot(jnp.concatenate([acc.astype(BF16), qt], axis=1),
                        jnp.concatenate([vb, _staged_transpose(st_t.astype(BF16), tsl.at[idx, n_lv + 1])], axis=0),
                        preferred_element_type=F32)
            btot = bline(L - 1 if d == 0 else 0)
            kt = (ldk(0, L) * jnp.exp2(btot - b)).astype(BF16)
            st[idx] = st_t * jnp.exp2(btot) + jnp.dot(vt, kt, preferred_element_type=F32)
            x1_ref[0, rows, d * HG_WIDTH + hd * HG_DK:d * HG_WIDTH + (hd + 1) * HG_DK] = o

    n_chunks = T // L

    def hg_both(c, carry):
        hg_chunk(c, 0)
        hg_chunk(n_chunks - 1 - c, 1)
        return carry

    lax.fori_loop(0, n_chunks, hg_prep, 0)
    lax.fori_loop(0, n_chunks, hg_both, 0)

    if emit_state:
        for i in range(N_DIR * HG_HEADS):
            hs_ref[0, i] = st[i].T

    def hg_fin(i, carry):
        rows = pl.ds(pl.multiple_of(i * L, L), L)
        for hd in range(HG_HEADS):
            cs = slice(hd * HG_DK, (hd + 1) * HG_DK)
            o = x1_ref[0, rows, cs] + x1_ref[0, rows, HG_WIDTH + hd * HG_DK:HG_WIDTH + (hd + 1) * HG_DK]
            ms = jnp.mean(o * o, axis=-1, keepdims=True)
            y = o * lax.rsqrt(ms + EPS) * hgg_ref[:, cs]
            gz = proj[rows, COL_GATE + hd * HG_DK:COL_GATE + (hd + 1) * HG_DK]
            mixin[rows, cs] = (y * (gz * _sig(gz))).astype(BF16)
        return carry

    lax.fori_loop(0, n_chunks, hg_fin, 0)

    S = T // LRU_SEGS
    n_grp = LRU_WIDTH // LRU_GROUP
    slabs = LRU_GROUP // 128
    cw = convw_ref[...]
    cb = convb_ref[...]

    def irows(i0, j, n):
        return pl.ds(pl.multiple_of(LRU_SEGS * i0, 8) + j, n, stride=LRU_SEGS)

    for j in range(LRU_SEGS):
        def conv_body(ti, carry, j=j):
            i0 = pl.multiple_of(ti * CONV_TILE, CONV_TILE)
            win = lxp[pl.ds(pl.multiple_of(j * S + i0, 8), CONV_TILE + 16), :]
            xc = cb
            for tap in range(4):
                xc = xc + win[6 + tap:6 + tap + CONV_TILE] * cw[tap:tap + 1]
            for l in range(LRU_WIDTH // 128):
                xci[l, irows(i0, j, CONV_TILE), :] = xc[:, l * 128:(l + 1) * 128]
            return carry

        lax.fori_loop(0, S // CONV_TILE, conv_body, 0)

    def xc_tile(rows, g):
        return jnp.concatenate([xci[g * slabs + l, rows, :] for l in range(slabs)], axis=1)

    def gate_body(i, carry):
        rows = pl.ds(pl.multiple_of(i * ROW_TILE, ROW_TILE), ROW_TILE)
        for g in range(n_grp):
            proj[rows, g * 1024:(g + 1) * 1024] = jnp.dot(
                xc_tile(rows, g).astype(BF16), wg[g], preferred_element_type=F32)
        return carry

    lax.fori_loop(0, T // ROW_TILE, gate_body, 0)

    lam = lam_ref[...]
    nl = -lam
    c8 = -LRU_C * (jnp.maximum(nl, 0.0) + jnp.log1p(jnp.exp(-jnp.abs(nl))))
    rowi2 = lax.broadcasted_iota(jnp.int32, (8, LRU_GROUP), 0)

    for d in range(N_DIR):
        for g in range(n_grp):
            k3 = 3 * (d * n_grp + g)
            for r, row in enumerate((bg_ref[g:g + 1, d * 512:d * 512 + LRU_GROUP],
                                     bg_ref[g:g + 1, d * 512 + LRU_GROUP:d * 512 + 2 * LRU_GROUP],
                                     c8[d:d + 1, g * LRU_GROUP:(g + 1) * LRU_GROUP])):
                rowc[k3 + r] = jnp.broadcast_to(row, (LRU_SEGS, LRU_GROUP))

    def seg_out(d, rows, g, part):
        cols = slice((g * 2 + part) * LRU_GROUP, (g * 2 + part + 1) * LRU_GROUP)
        return (ksc.at[rows, cols] if d == 0 else x1_ref.at[0, rows, cols])

    def lru_inputs(rows8, g, d):
        base = g * 1024 + d * 512
        k3 = 3 * (d * n_grp + g)
        ga = proj[rows8, base:base + LRU_GROUP] + rowc[k3]
        gx = proj[rows8, base + LRU_GROUP:base + 2 * LRU_GROUP] + rowc[k3 + 1]
        xc8 = xc_tile(rows8, g)
        log_a = rowc[k3 + 2] * _sig(ga)
        a = jnp.exp(log_a)
        mult = jnp.sqrt(jnp.tanh(-log_a) * (1.0 + a * a))
        return a, mult * (_sig(gx) * xc8)

    def scan8(a, u, d):
        for sft in (1, 2, 4):
            if d == 0:
                keep = rowi2 >= sft
                amt = sft
            else:
                keep = rowi2 < 8 - sft
                amt = 8 - sft
            ash = jnp.where(keep, pltpu.roll(a, amt, 0), 1.0)
            ush = jnp.where(keep, pltpu.roll(u, amt, 0), 0.0)
            u = a * ush + u
            a = a * ash
        return a, u

    def scan_body(n, carry):
        hs, ds = [list(c) for c in carry[:2]], [list(c) for c in carry[2:]]
        for u in range(SCAN_UNROLL):
            i_f = n * SCAN_UNROLL + u
            for d, i in ((0, i_f), (1, S - 1 - i_f)):
                rows8 = pl.ds(pl.multiple_of(LRU_SEGS * i, 8), 8)
                for g in range(n_grp):
                    a, uu = lru_inputs(rows8, g, d)
                    hs[d][g] = a * hs[d][g] + uu
                    ds[d][g] = a * ds[d][g]
                    seg_out(d, rows8, g, 0)[...] = hs[d][g]
                    seg_out(d, rows8, g, 1)[...] = ds[d][g]
        return tuple(tuple(c) for c in hs + ds)

    zero8 = jnp.zeros((LRU_SEGS, LRU_GROUP), F32)
    one8 = jnp.ones((LRU_SEGS, LRU_GROUP), F32)
    fin = lax.fori_loop(0, S // SCAN_UNROLL, scan_body,
                        ((zero8,) * n_grp, (zero8,) * n_grp, (one8,) * n_grp, (one8,) * n_grp))
    h_end, d_end = fin[:2], fin[2:]

    carry_in = [[None] * n_grp for _ in range(N_DIR)]
    for d in range(N_DIR):
        for g in range(n_grp):
            if has_state:
                h0 = ls0_ref[0, d:d + 1, g * LRU_GROUP:(g + 1) * LRU_GROUP]
            else:
                h0 = jnp.zeros((1, LRU_GROUP), F32)
            dd, hh = scan8(d_end[d][g], h_end[d][g], d)
            seg_end = hh + dd * h0
            if d == 0:
                carry_in[d][g] = jnp.where(rowi2 >= 1, pltpu.roll(seg_end, 1, 0), h0)
                last = seg_end[LRU_SEGS - 1:LRU_SEGS, :]
            else:
                carry_in[d][g] = jnp.where(rowi2 < LRU_SEGS - 1, pltpu.roll(seg_end, LRU_SEGS - 1, 0), h0)
                last = seg_end[0:1, :]
            if emit_state:
                ls_ref[0, d:d + 1, g * LRU_GROUP:(g + 1) * LRU_GROUP] = last

    def fix_body(i, carry):
        rows = pl.ds(pl.multiple_of(i * FIX_TILE, FIX_TILE), FIX_TILE)
        for g in range(n_grp):
            tot = None
            for d in range(N_DIR):
                cin = jnp.concatenate([carry_in[d][g]] * (FIX_TILE // LRU_SEGS), axis=0)
                h = seg_out(d, rows, g, 0)[...] + seg_out(d, rows, g, 1)[...] * cin
                tot = h if tot is None else tot + h
            for l in range(slabs):
                xci[g * slabs + l, rows, :] = tot[:, l * 128:(l + 1) * 128]
        return carry

    lax.fori_loop(0, T // FIX_TILE, fix_body, 0)

    for j in range(LRU_SEGS):
        def lru_fin(ti, carry, j=j):
            i0 = pl.multiple_of(ti * CONV_TILE, CONV_TILE)
            rows = pl.ds(pl.multiple_of(j * S + i0, CONV_TILE), CONV_TILE)
            hsum = jnp.concatenate([xci[l, irows(i0, j, CONV_TILE), :] for l in range(LRU_WIDTH // 128)], axis=1)
            lg = proj[rows, COL_LG:COL_LG + LRU_WIDTH]
            gl = lg * (0.5 * (1.0 + jnp.tanh(0.7978845608028654 * (lg + 0.044715 * (lg * lg * lg)))))
            mixin[rows, HG_WIDTH:HG_WIDTH + LRU_WIDTH] = (hsum * gl).astype(BF16)
            return carry

        lax.fori_loop(0, S // CONV_TILE, lru_fin, 0)

    def out_body(i, carry):
        rows = pl.ds(pl.multiple_of(i * ROW_TILE, ROW_TILE), ROW_TILE)
        mix = jnp.dot(mixin[rows, :], wout_ref[...], preferred_element_type=F32)
        x1_ref[0, rows, :] = load_x(rows) + g1 * mix
        return carry

    lax.fori_loop(0, T // ROW_TILE, out_body, 0)


def _const_spec(shape):
    nd = len(shape)
    return pl.BlockSpec(shape, lambda b, _n=nd: (0,) * _n, pipeline_mode=pl.Buffered(1))


def _nbytes(shape, dtype):
    return int(np.prod(shape)) * jnp.dtype(dtype).itemsize


def _mixer(x, m3, m_off, m_step, pos, consts, states, emit_state):
    B, T, _ = x.shape
    has_pos = pos is not None
    has_state = states is not None
    scratch_shapes = [
        ((T, IN_COLS), F32),
        ((T + 16, LRU_WIDTH), F32),
        ((T, D_MODEL), BF16),
        ((8, HG_DK, HG_DK), F32),
        ((T, N_DIR * HG_HEADS * HG_DK), F32),
        ((LRU_WIDTH // 128, T, 128), F32),
        ((T // CHUNK, HG_HEADS, HG_DK, CHUNK), BF16),
        ((3 * N_DIR * (LRU_WIDTH // LRU_GROUP), LRU_SEGS, LRU_GROUP), F32),
        ((N_DIR * HG_HEADS, len(LEVELS) + 2, HG_DK, CHUNK), BF16),
        ((LRU_WIDTH // LRU_GROUP, LRU_GROUP, 2 * N_DIR * LRU_GROUP), BF16),
    ]
    resident = (sum(_nbytes(s, d) for s, d in scratch_shapes)
                + sum(_nbytes(c.shape, c.dtype) for c in consts)
                + (_nbytes(pos.shape, pos.dtype) if has_pos else 0))
    io_block = _nbytes((T, D_MODEL), F32)
    io_bufs = 2 if resident + 4 * io_block <= VMEM_LIMIT - VMEM_HEADROOM else 1
    io_mode = pl.Buffered(io_bufs)
    in_specs = [
        pl.BlockSpec((1, T, D_MODEL), lambda b: (b, 0, 0), pipeline_mode=io_mode),
        pl.BlockSpec((1, 1, 6 * D_MODEL), lambda b: (m_off + m_step * b, 0, 0)),
    ]
    args = [x, m3]
    if has_pos:
        in_specs.append(_const_spec(pos.shape))
        args.append(pos)
    in_specs += [_const_spec(c.shape) for c in consts]
    args += list(consts)
    if has_state:
        hs0, ls0 = states
        in_specs += [
            pl.BlockSpec((1, 8, HG_DK, HG_DK), lambda b: (b, 0, 0, 0)),
            pl.BlockSpec((1, N_DIR, LRU_WIDTH), lambda b: (b, 0, 0)),
        ]
        args += [hs0, ls0]
    out_shape = [jax.ShapeDtypeStruct((B, T, D_MODEL), F32)]
    out_specs = [pl.BlockSpec((1, T, D_MODEL), lambda b: (b, 0, 0), pipeline_mode=io_mode)]
    if emit_state:
        out_shape += [jax.ShapeDtypeStruct((B, 8, HG_DK, HG_DK), F32),
                      jax.ShapeDtypeStruct((B, N_DIR, LRU_WIDTH), F32)]
        out_specs += [pl.BlockSpec((1, 8, HG_DK, HG_DK), lambda b: (b, 0, 0, 0)),
                      pl.BlockSpec((1, N_DIR, LRU_WIDTH), lambda b: (b, 0, 0))]
    scratch = [pltpu.VMEM(s, d) for s, d in scratch_shapes]
    return pl.pallas_call(
        functools.partial(_mixer_kernel, T=T, has_pos=has_pos, has_state=has_state, emit_state=emit_state),
        grid=(B,),
        in_specs=in_specs,
        out_specs=out_specs,
        out_shape=out_shape,
        scratch_shapes=scratch,
        compiler_params=pltpu.CompilerParams(
            dimension_semantics=("arbitrary",), vmem_limit_bytes=VMEM_LIMIT),
        name=f"mixer_t{T}",
    )(*args)


def _ffn_kernel(x_ref, m_ref, n2g_ref, fg_ref, w1_ref, w2_ref, y_ref):
    mrow = m_ref[0]
    sh2 = mrow[:, 3 * D_MODEL:4 * D_MODEL]
    sc2 = mrow[:, 4 * D_MODEL:5 * D_MODEL]
    g2 = mrow[:, 5 * D_MODEL:6 * D_MODEL]
    x = x_ref[...]
    ms = jnp.mean(x * x, axis=-1, keepdims=True)
    hb = (x * lax.rsqrt(ms + EPS) * (n2g_ref[...] * (1.0 + sc2)) + sh2).astype(BF16)
    ff = jnp.zeros(x.shape, F32)
    for c in range(D_FF // FF_CHUNK):
        a = jnp.dot(hb, w1_ref[:, c * FF_CHUNK:(c + 1) * FF_CHUNK], preferred_element_type=F32)
        a = jnp.maximum(a, 0.0)
        ff = ff + jnp.dot((a * a).astype(BF16), w2_ref[c * FF_CHUNK:(c + 1) * FF_CHUNK, :],
                          preferred_element_type=F32)
    x2 = x + g2 * ff
    ms2 = jnp.mean(x2 * x2, axis=-1, keepdims=True)
    y_ref[...] = x2 * lax.rsqrt(ms2 + EPS) * fg_ref[...]


def _ffn(x1, m3, m_off, tiles_per_cond, n2g, fgain, w1, w2):
    n = x1.shape[0]

    def m_index(i):
        if tiles_per_cond is None:
            return (m_off, 0, 0)
        return (m_off + i // tiles_per_cond, 0, 0)

    return pl.pallas_call(
        _ffn_kernel,
        grid=(n // FFN_TILE,),
        in_specs=[
            pl.BlockSpec((FFN_TILE, D_MODEL), lambda i: (i, 0)),
            pl.BlockSpec((1, 1, 6 * D_MODEL), m_index),
            _const_spec(n2g.shape),
            _const_spec(fgain.shape),
            _const_spec(w1.shape),
            _const_spec(w2.shape),
        ],
        out_specs=pl.BlockSpec((FFN_TILE, D_MODEL), lambda i: (i, 0)),
        out_shape=jax.ShapeDtypeStruct((n, D_MODEL), F32),
        compiler_params=pltpu.CompilerParams(
            dimension_semantics=("arbitrary",), vmem_limit_bytes=VMEM_LIMIT),
        name="ffn",
    )(x1, m3, n2g, fgain, w1, w2)


def _grid_pos_embed(n_tok):
    rows = n_tok // GRID_W
    rr, cc = np.meshgrid(np.arange(rows), np.arange(GRID_W), indexing='ij')
    quarter = D_MODEL // 4
    omega = (1.0 / (np.float32(POS_BASE) ** (np.arange(quarter, dtype=np.float32) / np.float32(quarter)))
             ).astype(np.float32)

    def emb(pos):
        ang = pos.reshape(-1, 1).astype(np.float32) * omega
        return np.concatenate([np.sin(ang), np.cos(ang)], axis=-1)

    return jnp.asarray(np.concatenate([emb(rr), emb(cc)], axis=-1), dtype=F32)


def kernel(x_prompt, x_sample, c, state_hgrn, state_rglru, c_ctx, w_ada, b_ada, norm1_gain, norm2_gain,
           w_in, hg_lb_logits, hg_norm_gain, conv_w, conv_b, lru_wa, lru_ba, lru_wx, lru_bx, lru_lambda,
           w_out, w_ff1, w_ff2, final_gain):
    bp, tp, _ = x_prompt.shape
    bs_, ts, _ = x_sample.shape

    cond8 = jnp.concatenate([c_ctx[None, :], c, jnp.zeros((8 - 1 - bs_, D_MODEL), F32)], axis=0)
    m3 = _modulation(cond8, w_ada[0], b_ada).reshape(8, 1, 6 * D_MODEL)

    masks_np, tri_np = _level_tables()
    grp = [slice(g * LRU_GROUP, (g + 1) * LRU_GROUP) for g in range(LRU_WIDTH // LRU_GROUP)]
    bias_g = jnp.stack([jnp.concatenate([lru_ba[0, 0, s], lru_bx[0, 0, s], lru_ba[0, 1, s], lru_bx[0, 1, s]])
                        for s in grp])
    consts = [
        norm1_gain,
        w_in[0].astype(BF16),
        hg_lb_logits.reshape(2, N_DIR * HG_HEADS, HG_DK),
        hg_norm_gain[0].reshape(1, HG_WIDTH),
        jnp.asarray(masks_np),
        jnp.asarray(tri_np, dtype=BF16),
        conv_w[0],
        conv_b,
        jnp.stack([lru_wa[0], lru_wx[0]], axis=1).reshape(
            N_DIR * 2 * LRU_BLOCKS, LRU_BLOCK, LRU_BLOCK),
        bias_g,
        lru_lambda[0],
        w_out[0].astype(BF16),
    ]
    w1 = w_ff1[0].astype(BF16)
    w2 = w_ff2[0].astype(BF16)
    fgain = final_gain.reshape(1, D_MODEL)

    x1p, hs, ls = _mixer(x_prompt, m3, 0, 0, None, consts, None, True)
    y_prompt = _ffn(x1p.reshape(bp * tp, D_MODEL), m3, 0, None, norm2_gain, fgain, w1, w2)

    x1s = _mixer(x_sample, m3, 1, 1, _grid_pos_embed(ts), consts,
                 (state_hgrn.reshape(bs_, N_DIR * HG_HEADS, HG_DK, HG_DK),
                  state_rglru.reshape(bs_, N_DIR, LRU_WIDTH)), False)[0]
    y_sample = _ffn(x1s.reshape(bs_ * ts, D_MODEL), m3, 1, ts // FFN_TILE, norm2_gain, fgain, w1, w2)

    return (y_prompt.reshape(bp, tp, D_MODEL),
            y_sample.reshape(bs_, ts, D_MODEL),
            hs.reshape(bp, 1, N_DIR, HG_HEADS, HG_DK, HG_DK),
            ls.reshape(bp, 1, N_DIR, LRU_WIDTH))
```

```python
import functools

import numpy as np
import jax
import jax.numpy as jnp
from jax import lax
from jax.experimental import pallas as pl
from jax.experimental.pallas import tpu as pltpu

F32 = jnp.float32
BF16 = jnp.bfloat16

D_MODEL = 1024
N_DIR = 2
HG_HEADS = 4
HG_DK = 128
HG_WIDTH = 512
LRU_WIDTH = 512
LRU_BLOCKS = 8
LRU_BLOCK = 64
LRU_C = 8.0
D_FF = 4096
IN_COLS = 4096
EPS = 1e-6
LOG2E = 1.4426950408889634
GRID_W = 64
POS_BASE = 10000.0

COL_Q = 0
COL_F = 1024
COL_V = 2048
COL_GATE = 2560
COL_LX = 3072
COL_LG = 3584

CHUNK = 128
LEVELS = (64, 32, 16, 8, 4, 2, 1)
ROW_TILE = 256
LRU_SEGS = 8
CONV_TILE = 32
SCAN_UNROLL = 4
FIX_TILE = 64
MAX_INLINE_TRIPS = 2
LRU_GROUP = 256
FFN_TILE = 512
FF_CHUNK = 1024
MOD_TILE = 1024
VMEM_LIMIT = 58 * 1024 * 1024
VMEM_HEADROOM = 6 * 1024 * 1024


def _sig(x):
    return jax.nn.sigmoid(x)


def _nt_dot(a, b):
    return lax.dot_general(a, b, (((1,), (1,)), ((), ())), preferred_element_type=F32)


def _staged_transpose(xb, slot_ref):
    slot_ref[...] = xb.T
    return slot_ref[...]


def _gram(x, slot_ref):
    xb = x.astype(BF16)
    return jnp.dot(xb, _staged_transpose(xb, slot_ref), preferred_element_type=F32)


def _level_tables():
    t = np.arange(CHUNK)[:, None]
    s = np.arange(CHUNK)[None, :]
    masks = np.zeros((N_DIR, len(LEVELS) + 1, CHUNK, CHUNK), np.float32)
    for li, h in enumerate(LEVELS):
        same = (t // (2 * h)) == (s // (2 * h))
        t_hi = (t // h) % 2 == 1
        s_hi = (s // h) % 2 == 1
        masks[0, li] = same & t_hi & ~s_hi
        masks[1, li] = same & ~t_hi & s_hi
    masks[:, len(LEVELS)] = (t == s)
    tri = np.stack([(s <= t), (s >= t)]).astype(np.float32)
    return masks, tri


def _mod_kernel(c_ref, w_ref, b_ref, o_ref):
    c = c_ref[...]
    a = (c * _sig(c)).astype(BF16)
    o_ref[...] = jnp.dot(a, w_ref[...].astype(BF16), preferred_element_type=F32) + b_ref[...]


def _modulation(cond8, w_ada, b_ada):
    n = w_ada.shape[1]
    return pl.pallas_call(
        _mod_kernel,
        grid=(n // MOD_TILE,),
        in_specs=[
            pl.BlockSpec((8, D_MODEL), lambda j: (0, 0)),
            pl.BlockSpec((D_MODEL, MOD_TILE), lambda j: (0, j)),
            pl.BlockSpec((1, MOD_TILE), lambda j: (0, j)),
        ],
        out_specs=pl.BlockSpec((8, MOD_TILE), lambda j: (0, j)),
        out_shape=jax.ShapeDtypeStruct((8, n), F32),
        compiler_params=pltpu.CompilerParams(dimension_semantics=("arbitrary",)),
        name="adaln_modulation",
    )(cond8, w_ada, b_ada)


def _mixer_kernel(*refs, T, has_pos, has_state, emit_state):
    it = iter(refs)
    x_ref = next(it)
    m_ref = next(it)
    if has_pos:
        posr_ref = next(it)
        posc_ref = next(it)
    n1g_ref = next(it)
    win_ref = next(it)
    lbl_ref = next(it)
    hgg_ref = next(it)
    masks_ref = next(it)
    tri_ref = next(it)
    convw_ref = next(it)
    convb_ref = next(it)
    gw_ref = next(it)
    bg_ref = next(it)
    lam_ref = next(it)
    wout_ref = next(it)
    if has_state:
        hs0_ref = next(it)
        ls0_ref = next(it)
    x1_ref = next(it)
    if emit_state:
        hs_ref = next(it)
        ls_ref = next(it)
    proj = next(it)
    lxp = next(it)
    mixin = next(it)
    st = next(it)
    ksc = next(it)
    xci = next(it)
    vts = next(it)
    rowc = next(it)
    tsl = next(it)
    wg = next(it)

    L = CHUNK

    @pl.when(pl.program_id(0) == 0)
    def _build_gate_weights():
        per_group = LRU_GROUP // LRU_BLOCK
        wg[...] = jnp.zeros(wg.shape, BF16)
        for g in range(LRU_WIDTH // LRU_GROUP):
            for p in range(2 * N_DIR):
                for n in range(per_group):
                    r = n * LRU_BLOCK
                    col = p * LRU_GROUP + r
                    wg[g, r:r + LRU_BLOCK, col:col + LRU_BLOCK] = gw_ref[
                        p * LRU_BLOCKS + g * per_group + n].astype(BF16)

    mrow = m_ref[0]
    sh1 = mrow[:, 0:D_MODEL]
    sc1 = mrow[:, D_MODEL:2 * D_MODEL]
    g1 = mrow[:, 2 * D_MODEL:3 * D_MODEL]
    gain1 = n1g_ref[...] * (1.0 + sc1)

    zrows = jnp.zeros((8, LRU_WIDTH), F32)
    lxp[0:8, :] = zrows
    lxp[T + 8:T + 16, :] = zrows

    def load_x(i):
        xt = x_ref[0, pl.ds(pl.multiple_of(i * ROW_TILE, ROW_TILE), ROW_TILE), :]
        if has_pos:
            per_tile = ROW_TILE // GRID_W
            tiles = []
            for s in range(per_tile):
                row_emb = jnp.concatenate([posr_ref[i * per_tile + s]] * (GRID_W // 8), axis=0)
                tiles.append(jnp.concatenate([row_emb, posc_ref[...]], axis=1))
            xt = xt + jnp.concatenate(tiles, axis=0)
        return xt

    def proj_body(i, carry):
        r0 = pl.multiple_of(i * ROW_TILE, ROW_TILE)
        xt = load_x(i)
        ms = jnp.mean(xt * xt, axis=-1, keepdims=True)
        hb = (xt * lax.rsqrt(ms + EPS) * gain1 + sh1).astype(BF16)
        n_col = IN_COLS // 512
        order = [COL_LX // 512] + [c for c in range(n_col) if c * 512 < COL_GATE] + [
            c for c in range(n_col) if c * 512 >= COL_GATE and c * 512 != COL_LX]
        for c in order:
            res = jnp.dot(hb, win_ref[:, c * 512:(c + 1) * 512], preferred_element_type=F32)
            if c * 512 == COL_LX:
                lxp[pl.ds(pl.multiple_of(r0 + 8, 8), ROW_TILE), :] = res
            else:
                proj[pl.ds(r0, ROW_TILE), c * 512:(c + 1) * 512] = res
        return carry

    lax.fori_loop(0, T // ROW_TILE, proj_body, 0)

    S = T // LRU_SEGS
    cw = convw_ref[...]
    cb = convb_ref[...]

    def irows(i0, j, n):
        return pl.ds(pl.multiple_of(LRU_SEGS * i0, 8) + j, n, stride=LRU_SEGS)

    for j in range(LRU_SEGS):
        def conv_body(ti, carry, j=j):
            i0 = pl.multiple_of(ti * CONV_TILE, CONV_TILE)
            win = lxp[pl.ds(pl.multiple_of(j * S + i0, 8), CONV_TILE + 16), :]
            xc = cb
            for tap in range(4):
                xc = xc + win[6 + tap:6 + tap + CONV_TILE] * cw[tap:tap + 1]
            for l in range(LRU_WIDTH // 128):
                xci[l, irows(i0, j, CONV_TILE), :] = xc[:, l * 128:(l + 1) * 128]
            return carry

        lax.fori_loop(0, S // CONV_TILE, conv_body, 0)

    l0 = lbl_ref[0]
    l1 = lbl_ref[1]
    lmx = jnp.maximum(l0, l1)
    e0 = jnp.exp(l0 - lmx)
    e1 = jnp.exp(l1 - lmx)
    lb_all = e0 / (e0 + e1)

    rowi = lax.broadcasted_iota(jnp.int32, (8, HG_DK), 0)
    r4 = rowi & 3
    is_r0 = r4 == 0
    is_r3 = r4 == 3
    lt2 = r4 < 2
    odd = (rowi & 1) == 1
    hi4 = rowi >= 4

    for i in range(N_DIR * HG_HEADS):
        if has_state:
            st[i] = hs0_ref[0, i].T
        else:
            st[i] = jnp.zeros((HG_DK, HG_DK), F32)

    def hg_prep(c, carry):
        rows = pl.ds(pl.multiple_of(c * L, L), L)
        for hd in range(HG_HEADS):
            vts[c, hd] = proj[rows, COL_V + hd * HG_DK:COL_V + (hd + 1) * HG_DK].T.astype(BF16)
        for d in range(N_DIR):
            parts = []
            for hd in range(HG_HEADS):
                idx = d * HG_HEADS + hd
                cq = COL_Q + idx * HG_DK
                cf = COL_F + idx * HG_DK
                hq = proj[rows, cq:cq + HG_DK]
                fz = proj[rows, cf:cf + HG_DK]
                proj[rows, cq:cq + HG_DK] = hq * _sig(hq)
                sg = _sig(fz)
                lb = lb_all[idx:idx + 1, :]
                oml = 1.0 - lb
                ksc[rows, idx * HG_DK:(idx + 1) * HG_DK] = oml * (1.0 - sg)
                logf = jnp.log(lb + oml * sg)
                p1 = logf.astype(BF16)
                p2 = (logf - p1.astype(F32)).astype(BF16)
                parts += [p1, p2]
            bb = jnp.dot(tri_ref[d], jnp.concatenate(parts, axis=1), preferred_element_type=F32)
            for hd in range(HG_HEADS):
                cf = COL_F + (d * HG_HEADS + hd) * HG_DK
                o2 = 2 * hd * HG_DK
                proj[rows, cf:cf + HG_DK] = (bb[:, o2:o2 + HG_DK] + bb[:, o2 + HG_DK:o2 + 2 * HG_DK]) * LOG2E
        return carry

    def hg_chunk(c, d):
        r0 = pl.multiple_of(c * L, L)
        rows = pl.ds(r0, L)
        for hd in range(HG_HEADS):
            idx = d * HG_HEADS + hd
            cq = COL_Q + idx * HG_DK
            cf = COL_F + idx * HG_DK
            ck = idx * HG_DK

            def ldq(lo, n, cq=cq):
                return proj[pl.ds(pl.multiple_of(r0 + lo, 8), n), cq:cq + HG_DK]

            def ldb(lo, n, cf=cf):
                return proj[pl.ds(pl.multiple_of(r0 + lo, 8), n), cf:cf + HG_DK]

            def ldk(lo, n, ck=ck):
                return ksc[pl.ds(pl.multiple_of(r0 + lo, 8), n), ck:ck + HG_DK]

            def bline(r, cf=cf):
                grp = proj[pl.ds(pl.multiple_of(r0 + 8 * (r // 8), 8), 8), cf:cf + HG_DK]
                return grp[r % 8:r % 8 + 1, :]

            def brow(r, n):
                return jnp.broadcast_to(bline(r), (n, HG_DK))

            n_lv = len(LEVELS)
            acc_rows = [None] * (L // 8)

            def accumulate(li, p, row0):
                for i in range(p.shape[0] // 8):
                    g = row0 // 8 + i
                    term = masks_ref[d, li, 8 * g:8 * g + 8, :] * p[8 * i:8 * i + 8]
                    acc_rows[g] = term if acc_rows[g] is None else acc_rows[g] + term

            accumulate(n_lv, jnp.dot(
                ldq(0, L).astype(BF16), _staged_transpose(ldk(0, L).astype(BF16), tsl.at[idx, n_lv]),
                preferred_element_type=F32), 0)
            for li, h in enumerate(LEVELS[:4]):
                pieces, q_pieces, q_starts = [], [], []
                for j in range(L // (2 * h)):
                    lo = j * 2 * h
                    mid = lo + h
                    if d == 0:
                        bm = brow(mid - 1, h)
                        kp = ldk(lo, h) * jnp.exp2(bm - ldb(lo, h))
                        qp = ldq(mid, h) * jnp.exp2(ldb(mid, h) - bm)
                        pieces += [kp, qp]
                        q_starts.append(mid)
                    else:
                        bm = brow(mid, h)
                        qp = ldq(lo, h) * jnp.exp2(ldb(lo, h) - bm)
                        kp = ldk(mid, h) * jnp.exp2(bm - ldb(mid, h))
                        pieces += [qp, kp]
                        q_starts.append(lo)
                    q_pieces.append(qp)
                xt = _staged_transpose(jnp.concatenate(pieces, axis=0).astype(BF16), tsl.at[idx, li])
                p = jnp.dot(jnp.concatenate(q_pieces, axis=0).astype(BF16), xt, preferred_element_type=F32)
                for j, row0 in enumerate(q_starts):
                    accumulate(li, p[j * h:(j + 1) * h], row0)
            x4, x2, x1 = [], [], []
            for g in range(L // 8):
                qg, kg, bg = ldq(8 * g, 8), ldk(8 * g, 8), ldb(8 * g, 8)
                fg = 1.0 - kg
                qfg = qg * fg
                mid = 8 * g + (3 if d == 0 else 4)
                e4 = jnp.exp2(-jnp.abs(bg - brow(mid, 8)))
                fnx = pltpu.roll(fg, 7, 0)
                fpv = pltpu.roll(fg, 1, 0)
                s0 = jnp.where(is_r0, fnx, 1.0)
                s3 = jnp.where(is_r3, fpv, 1.0)
                if d == 0:
                    x4.append(jnp.where(hi4, qg, kg) * e4)
                    x2.append(jnp.where(lt2, kg * s0, qfg * s3))
                    x1.append(jnp.where(odd, qfg, kg))
                else:
                    x4.append(jnp.where(hi4, kg, qg) * e4)
                    x2.append(jnp.where(lt2, qfg * s0, kg * s3))
                    x1.append(jnp.where(odd, kg, qfg))
            for li, pcs in ((4, x4), (5, x2), (6, x1)):
                accumulate(li, _gram(jnp.concatenate(pcs, axis=0), tsl.at[idx, li]), 0)
            acc = jnp.concatenate(acc_rows, axis=0)

            vt = vts[c, hd]
            st_t = st[idx]
            b = ldb(0, L)
            qt = (ldq(0, L) * jnp.exp2(b)).astype(BF16)
            vb = proj[rows, COL_V + hd * HG_DK:COL_V + (hd + 1) * HG_DK].astype(BF16)
            o = jnp.dot(jnp.concatenate([acc.astype(BF16), qt], axis=1),
                        jnp.concatenate([vb, _staged_transpose(st_t.astype(BF16), tsl.at[idx, n_lv + 1])], axis=0),
                        preferred_element_type=F32)
            btot = bline(L - 1 if d == 0 else 0)
            kt = (ldk(0, L) * jnp.exp2(btot - b)).astype(BF16)
            st[idx] = st_t * jnp.exp2(btot) + jnp.dot(vt, kt, preferred_element_type=F32)
            x1_ref[0, rows, d * HG_WIDTH + hd * HG_DK:d * HG_WIDTH + (hd + 1) * HG_DK] = o

    n_chunks = T // L

    def hg_both(c, carry):
        hg_chunk(c, 0)
        hg_chunk(n_chunks - 1 - c, 1)
        return carry

    lax.fori_loop(0, n_chunks, hg_prep, 0, unroll=n_chunks <= MAX_INLINE_TRIPS)
    lax.fori_loop(0, n_chunks, hg_both, 0)

    if emit_state:
        for i in range(N_DIR * HG_HEADS):
            hs_ref[0, i] = st[i].T

    def hg_fin(i, carry):
        rows = pl.ds(pl.multiple_of(i * L, L), L)
        for hd in range(HG_HEADS):
            cs = slice(hd * HG_DK, (hd + 1) * HG_DK)
            o = x1_ref[0, rows, cs] + x1_ref[0, rows, HG_WIDTH + hd * HG_DK:HG_WIDTH + (hd + 1) * HG_DK]
            ms = jnp.mean(o * o, axis=-1, keepdims=True)
            y = o * lax.rsqrt(ms + EPS) * hgg_ref[:, cs]
            gz = proj[rows, COL_GATE + hd * HG_DK:COL_GATE + (hd + 1) * HG_DK]
            mixin[rows, cs] = (y * (gz * _sig(gz))).astype(BF16)
        return carry

    lax.fori_loop(0, n_chunks, hg_fin, 0, unroll=n_chunks <= MAX_INLINE_TRIPS)

    n_grp = LRU_WIDTH // LRU_GROUP
    slabs = LRU_GROUP // 128

    def xc_tile(rows, g):
        return jnp.concatenate([xci[g * slabs + l, rows, :] for l in range(slabs)], axis=1)

    def gate_body(i, carry):
        rows = pl.ds(pl.multiple_of(i * ROW_TILE, ROW_TILE), ROW_TILE)
        for g in range(n_grp):
            proj[rows, g * 1024:(g + 1) * 1024] = jnp.dot(
                xc_tile(rows, g).astype(BF16), wg[g], preferred_element_type=F32)
        return carry

    lax.fori_loop(0, T // ROW_TILE, gate_body, 0)

    lam = lam_ref[...]
    nl = -lam
    c8 = -LRU_C * (jnp.maximum(nl, 0.0) + jnp.log1p(jnp.exp(-jnp.abs(nl))))
    rowi2 = lax.broadcasted_iota(jnp.int32, (8, LRU_GROUP), 0)

    for d in range(N_DIR):
        for g in range(n_grp):
            k3 = 3 * (d * n_grp + g)
            for r, row in enumerate((bg_ref[g:g + 1, d * 512:d * 512 + LRU_GROUP],
                                     bg_ref[g:g + 1, d * 512 + LRU_GROUP:d * 512 + 2 * LRU_GROUP],
                                     c8[d:d + 1, g * LRU_GROUP:(g + 1) * LRU_GROUP])):
                rowc[k3 + r] = jnp.broadcast_to(row, (LRU_SEGS, LRU_GROUP))

    def seg_out(d, rows, g, part):
        cols = slice((g * 2 + part) * LRU_GROUP, (g * 2 + part + 1) * LRU_GROUP)
        return (ksc.at[rows, cols] if d == 0 else x1_ref.at[0, rows, cols])

    def lru_inputs(rows8, g, d):
        base = g * 1024 + d * 512
        k3 = 3 * (d * n_grp + g)
        ga = proj[rows8, base:base + LRU_GROUP] + rowc[k3]
        gx = proj[rows8, base + LRU_GROUP:base + 2 * LRU_GROUP] + rowc[k3 + 1]
        xc8 = xc_tile(rows8, g)
        log_a = rowc[k3 + 2] * _sig(ga)
        a = jnp.exp(log_a)
        mult = jnp.sqrt(jnp.tanh(-log_a) * (1.0 + a * a))
        return a, mult * (_sig(gx) * xc8)

    def scan8(a, u, d):
        for sft in (1, 2, 4):
            if d == 0:
                keep = rowi2 >= sft
                amt = sft
            else:
                keep = rowi2 < 8 - sft
                amt = 8 - sft
            ash = jnp.where(keep, pltpu.roll(a, amt, 0), 1.0)
            ush = jnp.where(keep, pltpu.roll(u, amt, 0), 0.0)
            u = a * ush + u
            a = a * ash
        return a, u

    def scan_body(n, carry):
        hs, ds = [list(c) for c in carry[:2]], [list(c) for c in carry[2:]]
        for u in range(SCAN_UNROLL):
            i_f = n * SCAN_UNROLL + u
            for d, i in ((0, i_f), (1, S - 1 - i_f)):
                rows8 = pl.ds(pl.multiple_of(LRU_SEGS * i, 8), 8)
                for g in range(n_grp):
                    a, uu = lru_inputs(rows8, g, d)
                    hs[d][g] = a * hs[d][g] + uu
                    ds[d][g] = a * ds[d][g]
                    seg_out(d, rows8, g, 0)[...] = hs[d][g]
                    seg_out(d, rows8, g, 1)[...] = ds[d][g]
        return tuple(tuple(c) for c in hs + ds)

    zero8 = jnp.zeros((LRU_SEGS, LRU_GROUP), F32)
    one8 = jnp.ones((LRU_SEGS, LRU_GROUP), F32)
    fin = lax.fori_loop(0, S // SCAN_UNROLL, scan_body,
                        ((zero8,) * n_grp, (zero8,) * n_grp, (one8,) * n_grp, (one8,) * n_grp))
    h_end, d_end = fin[:2], fin[2:]

    carry_in = [[None] * n_grp for _ in range(N_DIR)]
    for d in range(N_DIR):
        for g in range(n_grp):
            if has_state:
                h0 = ls0_ref[0, d:d + 1, g * LRU_GROUP:(g + 1) * LRU_GROUP]
            else:
                h0 = jnp.zeros((1, LRU_GROUP), F32)
            dd, hh = scan8(d_end[d][g], h_end[d][g], d)
            seg_end = hh + dd * h0
            if d == 0:
                carry_in[d][g] = jnp.where(rowi2 >= 1, pltpu.roll(seg_end, 1, 0), h0)
                last = seg_end[LRU_SEGS - 1:LRU_SEGS, :]
            else:
                carry_in[d][g] = jnp.where(rowi2 < LRU_SEGS - 1, pltpu.roll(seg_end, LRU_SEGS - 1, 0), h0)
                last = seg_end[0:1, :]
            if emit_state:
                ls_ref[0, d:d + 1, g * LRU_GROUP:(g + 1) * LRU_GROUP] = last

    def fix_body(i, carry):
        rows = pl.ds(pl.multiple_of(i * FIX_TILE, FIX_TILE), FIX_TILE)
        for g in range(n_grp):
            tot = None
            for d in range(N_DIR):
                cin = jnp.concatenate([carry_in[d][g]] * (FIX_TILE // LRU_SEGS), axis=0)
                h = seg_out(d, rows, g, 0)[...] + seg_out(d, rows, g, 1)[...] * cin
                tot = h if tot is None else tot + h
            for l in range(slabs):
                xci[g * slabs + l, rows, :] = tot[:, l * 128:(l + 1) * 128]
        return carry

    lax.fori_loop(0, T // FIX_TILE, fix_body, 0)

    for j in range(LRU_SEGS):
        def lru_fin(ti, carry, j=j):
            i0 = pl.multiple_of(ti * CONV_TILE, CONV_TILE)
            rows = pl.ds(pl.multiple_of(j * S + i0, CONV_TILE), CONV_TILE)
            hsum = jnp.concatenate([xci[l, irows(i0, j, CONV_TILE), :] for l in range(LRU_WIDTH // 128)], axis=1)
            lg = proj[rows, COL_LG:COL_LG + LRU_WIDTH]
            gl = lg * (0.5 * (1.0 + jnp.tanh(0.7978845608028654 * (lg + 0.044715 * (lg * lg * lg)))))
            mixin[rows, HG_WIDTH:HG_WIDTH + LRU_WIDTH] = (hsum * gl).astype(BF16)
            return carry

        lax.fori_loop(0, S // CONV_TILE, lru_fin, 0)

    def out_body(i, carry):
        rows = pl.ds(pl.multiple_of(i * ROW_TILE, ROW_TILE), ROW_TILE)
        mix = jnp.dot(mixin[rows, :], wout_ref[...], preferred_element_type=F32)
        x1_ref[0, rows, :] = load_x(i) + g1 * mix
        return carry

    lax.fori_loop(0, T // ROW_TILE, out_body, 0)


def _const_spec(shape):
    nd = len(shape)
    return pl.BlockSpec(shape, lambda b, _n=nd: (0,) * _n, pipeline_mode=pl.Buffered(1))


def _nbytes(shape, dtype):
    return int(np.prod(shape)) * jnp.dtype(dtype).itemsize


def _mixer(x, m3, m_off, m_step, pos, consts, states, emit_state):
    B, T, _ = x.shape
    has_pos = pos is not None
    has_state = states is not None
    scratch_shapes = [
        ((T, IN_COLS), F32),
        ((T + 16, LRU_WIDTH), F32),
        ((T, D_MODEL), BF16),
        ((8, HG_DK, HG_DK), F32),
        ((T, N_DIR * HG_HEADS * HG_DK), F32),
        ((LRU_WIDTH // 128, T, 128), F32),
        ((T // CHUNK, HG_HEADS, HG_DK, CHUNK), BF16),
        ((3 * N_DIR * (LRU_WIDTH // LRU_GROUP), LRU_SEGS, LRU_GROUP), F32),
        ((N_DIR * HG_HEADS, len(LEVELS) + 2, HG_DK, CHUNK), BF16),
        ((LRU_WIDTH // LRU_GROUP, LRU_GROUP, 2 * N_DIR * LRU_GROUP), BF16),
    ]
    resident = (sum(_nbytes(s, d) for s, d in scratch_shapes)
                + sum(_nbytes(c.shape, c.dtype) for c in consts)
                + (sum(_nbytes(p.shape, p.dtype) for p in pos) if has_pos else 0))
    io_block = _nbytes((T, D_MODEL), F32)
    io_bufs = 2 if resident + 4 * io_block <= VMEM_LIMIT - VMEM_HEADROOM else 1
    io_mode = pl.Buffered(io_bufs)
    in_specs = [
        pl.BlockSpec((1, T, D_MODEL), lambda b: (b, 0, 0), pipeline_mode=io_mode),
        pl.BlockSpec((1, 1, 6 * D_MODEL), lambda b: (m_off + m_step * b, 0, 0)),
    ]
    args = [x, m3]
    if has_pos:
        in_specs += [_const_spec(p.shape) for p in pos]
        args += list(pos)
    in_specs += [_const_spec(c.shape) for c in consts]
    args += list(consts)
    if has_state:
        hs0, ls0 = states
        in_specs += [
            pl.BlockSpec((1, 8, HG_DK, HG_DK), lambda b: (b, 0, 0, 0)),
            pl.BlockSpec((1, N_DIR, LRU_WIDTH), lambda b: (b, 0, 0)),
        ]
        args += [hs0, ls0]
    out_shape = [jax.ShapeDtypeStruct((B, T, D_MODEL), F32)]
    out_specs = [pl.BlockSpec((1, T, D_MODEL), lambda b: (b, 0, 0), pipeline_mode=io_mode)]
    if emit_state:
        out_shape += [jax.ShapeDtypeStruct((B, 8, HG_DK, HG_DK), F32),
                      jax.ShapeDtypeStruct((B, N_DIR, LRU_WIDTH), F32)]
        out_specs += [pl.BlockSpec((1, 8, HG_DK, HG_DK), lambda b: (b, 0, 0, 0)),
                      pl.BlockSpec((1, N_DIR, LRU_WIDTH), lambda b: (b, 0, 0))]
    scratch = [pltpu.VMEM(s, d) for s, d in scratch_shapes]
    return pl.pallas_call(
        functools.partial(_mixer_kernel, T=T, has_pos=has_pos, has_state=has_state, emit_state=emit_state),
        grid=(B,),
        in_specs=in_specs,
        out_specs=out_specs,
        out_shape=out_shape,
        scratch_shapes=scratch,
        compiler_params=pltpu.CompilerParams(
            dimension_semantics=("arbitrary",), vmem_limit_bytes=VMEM_LIMIT),
        name=f"mixer_t{T}",
    )(*args)


def _ffn_kernel(x_ref, m_ref, n2g_ref, fg_ref, w1_ref, w2_ref, y_ref):
    mrow = m_ref[0]
    sh2 = mrow[:, 3 * D_MODEL:4 * D_MODEL]
    sc2 = mrow[:, 4 * D_MODEL:5 * D_MODEL]
    g2 = mrow[:, 5 * D_MODEL:6 * D_MODEL]
    x = x_ref[...]
    ms = jnp.mean(x * x, axis=-1, keepdims=True)
    hb = (x * lax.rsqrt(ms + EPS) * (n2g_ref[...] * (1.0 + sc2)) + sh2).astype(BF16)
    ff = jnp.zeros(x.shape, F32)
    for c in range(D_FF // FF_CHUNK):
        a = jnp.dot(hb, w1_ref[:, c * FF_CHUNK:(c + 1) * FF_CHUNK], preferred_element_type=F32)
        a = jnp.maximum(a, 0.0)
        ff = ff + jnp.dot((a * a).astype(BF16), w2_ref[c * FF_CHUNK:(c + 1) * FF_CHUNK, :],
                          preferred_element_type=F32)
    x2 = x + g2 * ff
    ms2 = jnp.mean(x2 * x2, axis=-1, keepdims=True)
    y_ref[...] = x2 * lax.rsqrt(ms2 + EPS) * fg_ref[...]


def _ffn(x1, m3, m_off, tiles_per_cond, n2g, fgain, w1, w2):
    n = x1.shape[0]

    def m_index(i):
        if tiles_per_cond is None:
            return (m_off, 0, 0)
        return (m_off + i // tiles_per_cond, 0, 0)

    return pl.pallas_call(
        _ffn_kernel,
        grid=(n // FFN_TILE,),
        in_specs=[
            pl.BlockSpec((FFN_TILE, D_MODEL), lambda i: (i, 0)),
            pl.BlockSpec((1, 1, 6 * D_MODEL), m_index),
            _const_spec(n2g.shape),
            _const_spec(fgain.shape),
            _const_spec(w1.shape),
            _const_spec(w2.shape),
        ],
        out_specs=pl.BlockSpec((FFN_TILE, D_MODEL), lambda i: (i, 0)),
        out_shape=jax.ShapeDtypeStruct((n, D_MODEL), F32),
        compiler_params=pltpu.CompilerParams(
            dimension_semantics=("arbitrary",), vmem_limit_bytes=VMEM_LIMIT),
        name="ffn",
    )(x1, m3, n2g, fgain, w1, w2)


def _grid_pos_tables(n_tok):
    quarter = D_MODEL // 4
    omega = (1.0 / (np.float32(POS_BASE) ** (np.arange(quarter, dtype=np.float32) / np.float32(quarter)))
             ).astype(np.float32)

    def emb(n):
        ang = np.arange(n).reshape(-1, 1).astype(np.float32) * omega
        return np.concatenate([np.sin(ang), np.cos(ang)], axis=-1)

    rows = np.repeat(emb(n_tok // GRID_W)[:, None, :], 8, axis=1)
    return jnp.asarray(rows, dtype=F32), jnp.asarray(emb(GRID_W), dtype=F32)


def kernel(x_prompt, x_sample, c, state_hgrn, state_rglru, c_ctx, w_ada, b_ada, norm1_gain, norm2_gain,
           w_in, hg_lb_logits, hg_norm_gain, conv_w, conv_b, lru_wa, lru_ba, lru_wx, lru_bx, lru_lambda,
           w_out, w_ff1, w_ff2, final_gain):
    bp, tp, _ = x_prompt.shape
    bs_, ts, _ = x_sample.shape

    cond8 = jnp.concatenate([c_ctx[None, :], c, jnp.zeros((8 - 1 - bs_, D_MODEL), F32)], axis=0)
    m3 = _modulation(cond8, w_ada[0], b_ada).reshape(8, 1, 6 * D_MODEL)

    masks_np, tri_np = _level_tables()
    grp = [slice(g * LRU_GROUP, (g + 1) * LRU_GROUP) for g in range(LRU_WIDTH // LRU_GROUP)]
    bias_g = jnp.stack([jnp.concatenate([lru_ba[0, 0, s], lru_bx[0, 0, s], lru_ba[0, 1, s], lru_bx[0, 1, s]])
                        for s in grp])
    consts = [
        norm1_gain,
        w_in[0].astype(BF16),
        hg_lb_logits.reshape(2, N_DIR * HG_HEADS, HG_DK),
        hg_norm_gain[0].reshape(1, HG_WIDTH),
        jnp.asarray(masks_np),
        jnp.asarray(tri_np, dtype=BF16),
        conv_w[0],
        conv_b,
        jnp.stack([lru_wa[0], lru_wx[0]], axis=1).reshape(
            N_DIR * 2 * LRU_BLOCKS, LRU_BLOCK, LRU_BLOCK),
        bias_g,
        lru_lambda[0],
        w_out[0].astype(BF16),
    ]
    w1 = w_ff1[0].astype(BF16)
    w2 = w_ff2[0].astype(BF16)
    fgain = final_gain.reshape(1, D_MODEL)

    x1p, hs, ls = _mixer(x_prompt, m3, 0, 0, None, consts, None, True)
    y_prompt = _ffn(x1p.reshape(bp * tp, D_MODEL), m3, 0, None, norm2_gain, fgain, w1, w2)

    x1s = _mixer(x_sample, m3, 1, 1, _grid_pos_tables(ts), consts,
                 (state_hgrn.reshape(bs_, N_DIR * HG_HEADS, HG_DK, HG_DK),
                  state_rglru.reshape(bs_, N_DIR, LRU_WIDTH)), False)[0]
    y_sample = _ffn(x1s.reshape(bs_ * ts, D_MODEL), m3, 1, ts // FFN_TILE, norm2_gain, fgain, w1, w2)

    return (y_prompt.reshape(bp, tp, D_MODEL),
            y_sample.reshape(bs_, ts, D_MODEL),
            hs.reshape(bp, 1, N_DIR, HG_HEADS, HG_DK, HG_DK),
            ls.reshape(bp, 1, N_DIR, LRU_WIDTH))
```

```python
import functools

import numpy as np
import jax
import jax.numpy as jnp
from jax import lax
from jax.experimental import pallas as pl
from jax.experimental.pallas import tpu as pltpu

F32 = jnp.float32
BF16 = jnp.bfloat16

D_MODEL = 1024
N_DIR = 2
HG_HEADS = 4
HG_DK = 128
HG_WIDTH = 512
LRU_WIDTH = 512
LRU_BLOCKS = 8
LRU_BLOCK = 64
LRU_C = 8.0
D_FF = 4096
IN_COLS = 4096
EPS = 1e-6
LOG2E = 1.4426950408889634
GRID_W = 64
POS_BASE = 10000.0

COL_Q = 0
COL_F = 1024
COL_V = 2048
COL_GATE = 2560
COL_LX = 3072
COL_LG = 3584

CHUNK = 128
LEVELS = (64, 32, 16, 8, 4, 2, 1)
ROW_TILE = 256
LRU_SEGS = 8
CONV_TILE = 32
SCAN_UNROLL = 4
FIX_TILE = 64
MAX_INLINE_TRIPS = 2
LRU_GROUP = 256
LRS_SEG_COL = 2 * N_DIR * LRU_WIDTH
FFN_TILE = 512
FF_CHUNK = 1024
MOD_TILE = 1024
VMEM_LIMIT = 58 * 1024 * 1024
VMEM_HEADROOM = 6 * 1024 * 1024


def _sig(x):
    return jax.nn.sigmoid(x)


def _nt_dot(a, b):
    return lax.dot_general(a, b, (((1,), (1,)), ((), ())), preferred_element_type=F32)


def _staged_transpose(xb, slot_ref):
    slot_ref[...] = xb.T
    return slot_ref[...]


def _gram(x, slot_ref):
    xb = x.astype(BF16)
    return jnp.dot(xb, _staged_transpose(xb, slot_ref), preferred_element_type=F32)


def _level_tables():
    t = np.arange(CHUNK)[:, None]
    s = np.arange(CHUNK)[None, :]
    masks = np.zeros((N_DIR, len(LEVELS) + 1, CHUNK, CHUNK), np.float32)
    for li, h in enumerate(LEVELS):
        same = (t // (2 * h)) == (s // (2 * h))
        t_hi = (t // h) % 2 == 1
        s_hi = (s // h) % 2 == 1
        masks[0, li] = same & t_hi & ~s_hi
        masks[1, li] = same & ~t_hi & s_hi
    masks[:, len(LEVELS)] = (t == s)
    tri = np.stack([(s <= t), (s >= t)]).astype(np.float32)
    return masks, tri


def _mod_kernel(c_ref, w_ref, b_ref, o_ref):
    c = c_ref[...]
    a = (c * _sig(c)).astype(BF16)
    o_ref[...] = jnp.dot(a, w_ref[...].astype(BF16), preferred_element_type=F32) + b_ref[...]


def _modulation(cond8, w_ada, b_ada):
    n = w_ada.shape[1]
    return pl.pallas_call(
        _mod_kernel,
        grid=(n // MOD_TILE,),
        in_specs=[
            pl.BlockSpec((8, D_MODEL), lambda j: (0, 0)),
            pl.BlockSpec((D_MODEL, MOD_TILE), lambda j: (0, j)),
            pl.BlockSpec((1, MOD_TILE), lambda j: (0, j)),
        ],
        out_specs=pl.BlockSpec((8, MOD_TILE), lambda j: (0, j)),
        out_shape=jax.ShapeDtypeStruct((8, n), F32),
        compiler_params=pltpu.CompilerParams(dimension_semantics=("arbitrary",)),
        name="adaln_modulation",
    )(cond8, w_ada, b_ada)


def _mixer_kernel(*refs, T, has_pos, has_state, emit_state, fuse_scan):
    it = iter(refs)
    x_ref = next(it)
    m_ref = next(it)
    if has_pos:
        posr_ref = next(it)
        posc_ref = next(it)
    n1g_ref = next(it)
    win_ref = next(it)
    lbl_ref = next(it)
    hgg_ref = next(it)
    masks_ref = next(it)
    tri_ref = next(it)
    convw_ref = next(it)
    convb_ref = next(it)
    gw_ref = next(it)
    bg_ref = next(it)
    lam_ref = next(it)
    wout_ref = next(it)
    if has_state:
        hs0_ref = next(it)
        ls0_ref = next(it)
    x1_ref = next(it)
    if emit_state:
        hs_ref = next(it)
        ls_ref = next(it)
    proj = next(it)
    lxp = next(it)
    mixin = next(it)
    st = next(it)
    ksc = next(it)
    xci = next(it)
    vts = next(it)
    rowc = next(it)
    tsl = next(it)
    wg = next(it)
    if fuse_scan:
        lrs = next(it)

    L = CHUNK

    @pl.when(pl.program_id(0) == 0)
    def _build_gate_weights():
        per_group = LRU_GROUP // LRU_BLOCK
        wg[...] = jnp.zeros(wg.shape, BF16)
        for g in range(LRU_WIDTH // LRU_GROUP):
            for p in range(2 * N_DIR):
                for n in range(per_group):
                    r = n * LRU_BLOCK
                    col = p * LRU_GROUP + r
                    wg[g, r:r + LRU_BLOCK, col:col + LRU_BLOCK] = gw_ref[
                        p * LRU_BLOCKS + g * per_group + n].astype(BF16)

    mrow = m_ref[0]
    sh1 = mrow[:, 0:D_MODEL]
    sc1 = mrow[:, D_MODEL:2 * D_MODEL]
    g1 = mrow[:, 2 * D_MODEL:3 * D_MODEL]
    gain1 = n1g_ref[...] * (1.0 + sc1)

    zrows = jnp.zeros((8, LRU_WIDTH), F32)
    lxp[0:8, :] = zrows
    lxp[T + 8:T + 16, :] = zrows

    def load_x(i):
        xt = x_ref[0, pl.ds(pl.multiple_of(i * ROW_TILE, ROW_TILE), ROW_TILE), :]
        if has_pos:
            per_tile = ROW_TILE // GRID_W
            tiles = []
            for s in range(per_tile):
                row_emb = jnp.concatenate([posr_ref[i * per_tile + s]] * (GRID_W // 8), axis=0)
                tiles.append(jnp.concatenate([row_emb, posc_ref[...]], axis=1))
            xt = xt + jnp.concatenate(tiles, axis=0)
        return xt

    def proj_body(i, carry):
        r0 = pl.multiple_of(i * ROW_TILE, ROW_TILE)
        xt = load_x(i)
        ms = jnp.mean(xt * xt, axis=-1, keepdims=True)
        hb = (xt * lax.rsqrt(ms + EPS) * gain1 + sh1).astype(BF16)
        n_col = IN_COLS // 512
        order = [COL_LX // 512] + [c for c in range(n_col) if c * 512 < COL_GATE] + [
            c for c in range(n_col) if c * 512 >= COL_GATE and c * 512 != COL_LX]
        for c in order:
            res = jnp.dot(hb, win_ref[:, c * 512:(c + 1) * 512], preferred_element_type=F32)
            if c * 512 == COL_LX:
                lxp[pl.ds(pl.multiple_of(r0 + 8, 8), ROW_TILE), :] = res
            else:
                proj[pl.ds(r0, ROW_TILE), c * 512:(c + 1) * 512] = res
        return carry

    lax.fori_loop(0, T // ROW_TILE, proj_body, 0)

    S = T // LRU_SEGS
    cw = convw_ref[...]
    cb = convb_ref[...]

    def irows(i0, j, n):
        return pl.ds(pl.multiple_of(LRU_SEGS * i0, 8) + j, n, stride=LRU_SEGS)

    for j in range(LRU_SEGS):
        def conv_body(ti, carry, j=j):
            i0 = pl.multiple_of(ti * CONV_TILE, CONV_TILE)
            win = lxp[pl.ds(pl.multiple_of(j * S + i0, 8), CONV_TILE + 16), :]
            xc = cb
            for tap in range(4):
                xc = xc + win[6 + tap:6 + tap + CONV_TILE] * cw[tap:tap + 1]
            for l in range(LRU_WIDTH // 128):
                xci[l, irows(i0, j, CONV_TILE), :] = xc[:, l * 128:(l + 1) * 128]
            return carry

        lax.fori_loop(0, S // CONV_TILE, conv_body, 0)

    n_grp = LRU_WIDTH // LRU_GROUP
    slabs = LRU_GROUP // 128
    gates = lrs if fuse_scan else proj

    def xc_tile(rows, g):
        return jnp.concatenate([xci[g * slabs + l, rows, :] for l in range(slabs)], axis=1)

    def gate_body(i, carry):
        rows = pl.ds(pl.multiple_of(i * ROW_TILE, ROW_TILE), ROW_TILE)
        for g in range(n_grp):
            gates[rows, g * 1024:(g + 1) * 1024] = jnp.dot(
                xc_tile(rows, g).astype(BF16), wg[g], preferred_element_type=F32)
        return carry

    lam = lam_ref[...]
    nl = -lam
    c8 = -LRU_C * (jnp.maximum(nl, 0.0) + jnp.log1p(jnp.exp(-jnp.abs(nl))))
    rowi2 = lax.broadcasted_iota(jnp.int32, (8, LRU_GROUP), 0)

    for d in range(N_DIR):
        for g in range(n_grp):
            k3 = 3 * (d * n_grp + g)
            for r, row in enumerate((bg_ref[g:g + 1, d * 512:d * 512 + LRU_GROUP],
                                     bg_ref[g:g + 1, d * 512 + LRU_GROUP:d * 512 + 2 * LRU_GROUP],
                                     c8[d:d + 1, g * LRU_GROUP:(g + 1) * LRU_GROUP])):
                rowc[k3 + r] = jnp.broadcast_to(row, (LRU_SEGS, LRU_GROUP))

    def seg_out(d, rows, g, part):
        if fuse_scan:
            col = LRS_SEG_COL + ((d * n_grp + g) * 2 + part) * LRU_GROUP
            return lrs.at[rows, col:col + LRU_GROUP]
        cols = slice((g * 2 + part) * LRU_GROUP, (g * 2 + part + 1) * LRU_GROUP)
        return (ksc.at[rows, cols] if d == 0 else x1_ref.at[0, rows, cols])

    def lru_inputs(rows8, g, d):
        base = g * 1024 + d * 512
        k3 = 3 * (d * n_grp + g)
        ga = gates[rows8, base:base + LRU_GROUP] + rowc[k3]
        gx = gates[rows8, base + LRU_GROUP:base + 2 * LRU_GROUP] + rowc[k3 + 1]
        xc8 = xc_tile(rows8, g)
        log_a = rowc[k3 + 2] * _sig(ga)
        a = jnp.exp(log_a)
        mult = jnp.sqrt(jnp.tanh(-log_a) * (1.0 + a * a))
        return a, mult * (_sig(gx) * xc8)

    def scan8(a, u, d):
        for sft in (1, 2, 4):
            if d == 0:
                keep = rowi2 >= sft
                amt = sft
            else:
                keep = rowi2 < 8 - sft
                amt = 8 - sft
            ash = jnp.where(keep, pltpu.roll(a, amt, 0), 1.0)
            ush = jnp.where(keep, pltpu.roll(u, amt, 0), 0.0)
            u = a * ush + u
            a = a * ash
        return a, u

    def scan_steps(first, count, carry):
        hs, ds = [list(c) for c in carry[:2]], [list(c) for c in carry[2:]]
        for u in range(count):
            i_f = first + u
            for d, i in ((0, i_f), (1, S - 1 - i_f)):
                rows8 = pl.ds(pl.multiple_of(LRU_SEGS * i, 8), 8)
                for g in range(n_grp):
                    a, uu = lru_inputs(rows8, g, d)
                    hs[d][g] = a * hs[d][g] + uu
                    ds[d][g] = a * ds[d][g]
                    seg_out(d, rows8, g, 0)[...] = hs[d][g]
                    seg_out(d, rows8, g, 1)[...] = ds[d][g]
        return tuple(tuple(c) for c in hs + ds)

    zero8 = jnp.zeros((LRU_SEGS, LRU_GROUP), F32)
    one8 = jnp.ones((LRU_SEGS, LRU_GROUP), F32)
    scan_init = ((zero8,) * n_grp, (zero8,) * n_grp, (one8,) * n_grp, (one8,) * n_grp)
    if fuse_scan:
        lax.fori_loop(0, T // ROW_TILE, gate_body, 0)

    l0 = lbl_ref[0]
    l1 = lbl_ref[1]
    lmx = jnp.maximum(l0, l1)
    e0 = jnp.exp(l0 - lmx)
    e1 = jnp.exp(l1 - lmx)
    lb_all = e0 / (e0 + e1)

    rowi = lax.broadcasted_iota(jnp.int32, (8, HG_DK), 0)
    r4 = rowi & 3
    is_r0 = r4 == 0
    is_r3 = r4 == 3
    lt2 = r4 < 2
    odd = (rowi & 1) == 1
    hi4 = rowi >= 4

    for i in range(N_DIR * HG_HEADS):
        if has_state:
            st[i] = hs0_ref[0, i].T
        else:
            st[i] = jnp.zeros((HG_DK, HG_DK), F32)

    def hg_prep(c, carry):
        rows = pl.ds(pl.multiple_of(c * L, L), L)
        for hd in range(HG_HEADS):
            vts[c, hd] = proj[rows, COL_V + hd * HG_DK:COL_V + (hd + 1) * HG_DK].T.astype(BF16)
        for d in range(N_DIR):
            parts = []
            for hd in range(HG_HEADS):
                idx = d * HG_HEADS + hd
                cq = COL_Q + idx * HG_DK
                cf = COL_F + idx * HG_DK
                hq = proj[rows, cq:cq + HG_DK]
                fz = proj[rows, cf:cf + HG_DK]
                proj[rows, cq:cq + HG_DK] = hq * _sig(hq)
                sg = _sig(fz)
                lb = lb_all[idx:idx + 1, :]
                oml = 1.0 - lb
                ksc[rows, idx * HG_DK:(idx + 1) * HG_DK] = oml * (1.0 - sg)
                logf = jnp.log(lb + oml * sg)
                p1 = logf.astype(BF16)
                p2 = (logf - p1.astype(F32)).astype(BF16)
                parts += [p1, p2]
            bb = jnp.dot(tri_ref[d], jnp.concatenate(parts, axis=1), preferred_element_type=F32)
            for hd in range(HG_HEADS):
                cf = COL_F + (d * HG_HEADS + hd) * HG_DK
                o2 = 2 * hd * HG_DK
                proj[rows, cf:cf + HG_DK] = (bb[:, o2:o2 + HG_DK] + bb[:, o2 + HG_DK:o2 + 2 * HG_DK]) * LOG2E
        return carry

    def hg_chunk(c, d):
        r0 = pl.multiple_of(c * L, L)
        rows = pl.ds(r0, L)
        for hd in range(HG_HEADS):
            idx = d * HG_HEADS + hd
            cq = COL_Q + idx * HG_DK
            cf = COL_F + idx * HG_DK
            ck = idx * HG_DK

            def ldq(lo, n, cq=cq):
                return proj[pl.ds(pl.multiple_of(r0 + lo, 8), n), cq:cq + HG_DK]

            def ldb(lo, n, cf=cf):
                return proj[pl.ds(pl.multiple_of(r0 + lo, 8), n), cf:cf + HG_DK]

            def ldk(lo, n, ck=ck):
                return ksc[pl.ds(pl.multiple_of(r0 + lo, 8), n), ck:ck + HG_DK]

            def bline(r, cf=cf):
                grp = proj[pl.ds(pl.multiple_of(r0 + 8 * (r // 8), 8), 8), cf:cf + HG_DK]
                return grp[r % 8:r % 8 + 1, :]

            def brow(r, n):
                return jnp.broadcast_to(bline(r), (n, HG_DK))

            n_lv = len(LEVELS)
            acc_rows = [None] * (L // 8)

            def accumulate(li, p, row0):
                for i in range(p.shape[0] // 8):
                    g = row0 // 8 + i
                    term = masks_ref[d, li, 8 * g:8 * g + 8, :] * p[8 * i:8 * i + 8]
                    acc_rows[g] = term if acc_rows[g] is None else acc_rows[g] + term

            accumulate(n_lv, jnp.dot(
                ldq(0, L).astype(BF16), _staged_transpose(ldk(0, L).astype(BF16), tsl.at[idx, n_lv]),
                preferred_element_type=F32), 0)
            for li, h in enumerate(LEVELS[:4]):
                pieces, q_pieces, q_starts = [], [], []
                for j in range(L // (2 * h)):
                    lo = j * 2 * h
                    mid = lo + h
                    if d == 0:
                        bm = brow(mid - 1, h)
                        kp = ldk(lo, h) * jnp.exp2(bm - ldb(lo, h))
                        qp = ldq(mid, h) * jnp.exp2(ldb(mid, h) - bm)
                        pieces += [kp, qp]
                        q_starts.append(mid)
                    else:
                        bm = brow(mid, h)
                        qp = ldq(lo, h) * jnp.exp2(ldb(lo, h) - bm)
                        kp = ldk(mid, h) * jnp.exp2(bm - ldb(mid, h))
                        pieces += [qp, kp]
                        q_starts.append(lo)
                    q_pieces.append(qp)
                xt = _staged_transpose(jnp.concatenate(pieces, axis=0).astype(BF16), tsl.at[idx, li])
                p = jnp.dot(jnp.concatenate(q_pieces, axis=0).astype(BF16), xt, preferred_element_type=F32)
                for j, row0 in enumerate(q_starts):
                    accumulate(li, p[j * h:(j + 1) * h], row0)
            x4, x2, x1 = [], [], []
            for g in range(L // 8):
                qg, kg, bg = ldq(8 * g, 8), ldk(8 * g, 8), ldb(8 * g, 8)
                fg = 1.0 - kg
                qfg = qg * fg
                mid = 8 * g + (3 if d == 0 else 4)
                e4 = jnp.exp2(-jnp.abs(bg - brow(mid, 8)))
                fnx = pltpu.roll(fg, 7, 0)
                fpv = pltpu.roll(fg, 1, 0)
                s0 = jnp.where(is_r0, fnx, 1.0)
                s3 = jnp.where(is_r3, fpv, 1.0)
                if d == 0:
                    x4.append(jnp.where(hi4, qg, kg) * e4)
                    x2.append(jnp.where(lt2, kg * s0, qfg * s3))
                    x1.append(jnp.where(odd, qfg, kg))
                else:
                    x4.append(jnp.where(hi4, kg, qg) * e4)
                    x2.append(jnp.where(lt2, qfg * s0, kg * s3))
                    x1.append(jnp.where(odd, kg, qfg))
            for li, pcs in ((4, x4), (5, x2), (6, x1)):
                accumulate(li, _gram(jnp.concatenate(pcs, axis=0), tsl.at[idx, li]), 0)
            acc = jnp.concatenate(acc_rows, axis=0)

            vt = vts[c, hd]
            st_t = st[idx]
            b = ldb(0, L)
            qt = (ldq(0, L) * jnp.exp2(b)).astype(BF16)
            vb = proj[rows, COL_V + hd * HG_DK:COL_V + (hd + 1) * HG_DK].astype(BF16)
            o = jnp.dot(jnp.concatenate([acc.astype(BF16), qt], axis=1),
                        jnp.concatenate([vb, _staged_transpose(st_t.astype(BF16), tsl.at[idx, n_lv + 1])], axis=0),
                        preferred_element_type=F32)
            btot = bline(L - 1 if d == 0 else 0)
            kt = (ldk(0, L) * jnp.exp2(btot - b)).astype(BF16)
            st[idx] = st_t * jnp.exp2(btot) + jnp.dot(vt, kt, preferred_element_type=F32)
            x1_ref[0, rows, d * HG_WIDTH + hd * HG_DK:d * HG_WIDTH + (hd + 1) * HG_DK] = o

    n_chunks = T // L

    def hg_both(c, carry):
        hg_chunk(c, 0)
        hg_chunk(n_chunks - 1 - c, 1)
        if fuse_scan:
            steps = S // n_chunks
            carry = scan_steps(c * steps, steps, carry)
        return carry

    lax.fori_loop(0, n_chunks, hg_prep, 0, unroll=n_chunks <= MAX_INLINE_TRIPS)
    scan_state = lax.fori_loop(0, n_chunks, hg_both, scan_init if fuse_scan else 0)

    if emit_state:
        for i in range(N_DIR * HG_HEADS):
            hs_ref[0, i] = st[i].T

    def hg_fin(i, carry):
        rows = pl.ds(pl.multiple_of(i * L, L), L)
        for hd in range(HG_HEADS):
            cs = slice(hd * HG_DK, (hd + 1) * HG_DK)
            o = x1_ref[0, rows, cs] + x1_ref[0, rows, HG_WIDTH + hd * HG_DK:HG_WIDTH + (hd + 1) * HG_DK]
            ms = jnp.mean(o * o, axis=-1, keepdims=True)
            y = o * lax.rsqrt(ms + EPS) * hgg_ref[:, cs]
            gz = proj[rows, COL_GATE + hd * HG_DK:COL_GATE + (hd + 1) * HG_DK]
            mixin[rows, cs] = (y * (gz * _sig(gz))).astype(BF16)
        return carry

    lax.fori_loop(0, n_chunks, hg_fin, 0, unroll=n_chunks <= MAX_INLINE_TRIPS)

    if not fuse_scan:
        lax.fori_loop(0, T // ROW_TILE, gate_body, 0)
        scan_state = lax.fori_loop(
            0, S // SCAN_UNROLL, lambda n, carry: scan_steps(n * SCAN_UNROLL, SCAN_UNROLL, carry), scan_init)
    h_end, d_end = scan_state[:2], scan_state[2:]

    carry_in = [[None] * n_grp for _ in range(N_DIR)]
    for d in range(N_DIR):
        for g in range(n_grp):
            if has_state:
                h0 = ls0_ref[0, d:d + 1, g * LRU_GROUP:(g + 1) * LRU_GROUP]
            else:
                h0 = jnp.zeros((1, LRU_GROUP), F32)
            dd, hh = scan8(d_end[d][g], h_end[d][g], d)
            seg_end = hh + dd * h0
            if d == 0:
                carry_in[d][g] = jnp.where(rowi2 >= 1, pltpu.roll(seg_end, 1, 0), h0)
                last = seg_end[LRU_SEGS - 1:LRU_SEGS, :]
            else:
                carry_in[d][g] = jnp.where(rowi2 < LRU_SEGS - 1, pltpu.roll(seg_end, LRU_SEGS - 1, 0), h0)
                last = seg_end[0:1, :]
            if emit_state:
                ls_ref[0, d:d + 1, g * LRU_GROUP:(g + 1) * LRU_GROUP] = last

    def fix_body(i, carry):
        rows = pl.ds(pl.multiple_of(i * FIX_TILE, FIX_TILE), FIX_TILE)
        for g in range(n_grp):
            tot = None
            for d in range(N_DIR):
                cin = jnp.concatenate([carry_in[d][g]] * (FIX_TILE // LRU_SEGS), axis=0)
                h = seg_out(d, rows, g, 0)[...] + seg_out(d, rows, g, 1)[...] * cin
                tot = h if tot is None else tot + h
            for l in range(slabs):
                xci[g * slabs + l, rows, :] = tot[:, l * 128:(l + 1) * 128]
        return carry

    lax.fori_loop(0, T // FIX_TILE, fix_body, 0)

    for j in range(LRU_SEGS):
        def lru_fin(ti, carry, j=j):
            i0 = pl.multiple_of(ti * CONV_TILE, CONV_TILE)
            rows = pl.ds(pl.multiple_of(j * S + i0, CONV_TILE), CONV_TILE)
            hsum = jnp.concatenate([xci[l, irows(i0, j, CONV_TILE), :] for l in range(LRU_WIDTH // 128)], axis=1)
            lg = proj[rows, COL_LG:COL_LG + LRU_WIDTH]
            gl = lg * (0.5 * (1.0 + jnp.tanh(0.7978845608028654 * (lg + 0.044715 * (lg * lg * lg)))))
            mixin[rows, HG_WIDTH:HG_WIDTH + LRU_WIDTH] = (hsum * gl).astype(BF16)
            return carry

        lax.fori_loop(0, S // CONV_TILE, lru_fin, 0)

    def out_body(i, carry):
        rows = pl.ds(pl.multiple_of(i * ROW_TILE, ROW_TILE), ROW_TILE)
        mix = jnp.dot(mixin[rows, :], wout_ref[...], preferred_element_type=F32)
        x1_ref[0, rows, :] = load_x(i) + g1 * mix
        return carry

    lax.fori_loop(0, T // ROW_TILE, out_body, 0)


def _const_spec(shape):
    nd = len(shape)
    return pl.BlockSpec(shape, lambda b, _n=nd: (0,) * _n, pipeline_mode=pl.Buffered(1))


def _nbytes(shape, dtype):
    return int(np.prod(shape)) * jnp.dtype(dtype).itemsize


def _mixer(x, m3, m_off, m_step, pos, consts, states, emit_state):
    B, T, _ = x.shape
    has_pos = pos is not None
    has_state = states is not None
    scratch_shapes = [
        ((T, IN_COLS), F32),
        ((T + 16, LRU_WIDTH), F32),
        ((T, D_MODEL), BF16),
        ((8, HG_DK, HG_DK), F32),
        ((T, N_DIR * HG_HEADS * HG_DK), F32),
        ((LRU_WIDTH // 128, T, 128), F32),
        ((T // CHUNK, HG_HEADS, HG_DK, CHUNK), BF16),
        ((3 * N_DIR * (LRU_WIDTH // LRU_GROUP), LRU_SEGS, LRU_GROUP), F32),
        ((N_DIR * HG_HEADS, len(LEVELS) + 2, HG_DK, CHUNK), BF16),
        ((LRU_WIDTH // LRU_GROUP, LRU_GROUP, 2 * N_DIR * LRU_GROUP), BF16),
    ]
    resident = (sum(_nbytes(s, d) for s, d in scratch_shapes)
                + sum(_nbytes(c.shape, c.dtype) for c in consts)
                + (sum(_nbytes(p.shape, p.dtype) for p in pos) if has_pos else 0))
    io_block = _nbytes((T, D_MODEL), F32)
    budget = VMEM_LIMIT - VMEM_HEADROOM
    lrs_shape = ((T, 2 * LRS_SEG_COL), F32)
    fuse_scan = resident + _nbytes(*lrs_shape) + 4 * io_block <= budget
    if fuse_scan:
        scratch_shapes.append(lrs_shape)
        resident += _nbytes(*lrs_shape)
    io_bufs = 2 if resident + 4 * io_block <= budget else 1
    io_mode = pl.Buffered(io_bufs)
    in_specs = [
        pl.BlockSpec((1, T, D_MODEL), lambda b: (b, 0, 0), pipeline_mode=io_mode),
        pl.BlockSpec((1, 1, 6 * D_MODEL), lambda b: (m_off + m_step * b, 0, 0)),
    ]
    args = [x, m3]
    if has_pos:
        in_specs += [_const_spec(p.shape) for p in pos]
        args += list(pos)
    in_specs += [_const_spec(c.shape) for c in consts]
    args += list(consts)
    if has_state:
        hs0, ls0 = states
        in_specs += [
            pl.BlockSpec((1, 8, HG_DK, HG_DK), lambda b: (b, 0, 0, 0)),
            pl.BlockSpec((1, N_DIR, LRU_WIDTH), lambda b: (b, 0, 0)),
        ]
        args += [hs0, ls0]
    out_shape = [jax.ShapeDtypeStruct((B, T, D_MODEL), F32)]
    out_specs = [pl.BlockSpec((1, T, D_MODEL), lambda b: (b, 0, 0), pipeline_mode=io_mode)]
    if emit_state:
        out_shape += [jax.ShapeDtypeStruct((B, 8, HG_DK, HG_DK), F32),
                      jax.ShapeDtypeStruct((B, N_DIR, LRU_WIDTH), F32)]
        out_specs += [pl.BlockSpec((1, 8, HG_DK, HG_DK), lambda b: (b, 0, 0, 0)),
                      pl.BlockSpec((1, N_DIR, LRU_WIDTH), lambda b: (b, 0, 0))]
    scratch = [pltpu.VMEM(s, d) for s, d in scratch_shapes]
    return pl.pallas_call(
        functools.partial(_mixer_kernel, T=T, has_pos=has_pos, has_state=has_state, emit_state=emit_state,
                          fuse_scan=fuse_scan),
        grid=(B,),
        in_specs=in_specs,
        out_specs=out_specs,
        out_shape=out_shape,
        scratch_shapes=scratch,
        compiler_params=pltpu.CompilerParams(
            dimension_semantics=("arbitrary",), vmem_limit_bytes=VMEM_LIMIT),
        name=f"mixer_t{T}",
    )(*args)


def _ffn_kernel(x_ref, m_ref, n2g_ref, fg_ref, w1_ref, w2_ref, y_ref):
    mrow = m_ref[0]
    sh2 = mrow[:, 3 * D_MODEL:4 * D_MODEL]
    sc2 = mrow[:, 4 * D_MODEL:5 * D_MODEL]
    g2 = mrow[:, 5 * D_MODEL:6 * D_MODEL]
    x = x_ref[...]
    ms = jnp.mean(x * x, axis=-1, keepdims=True)
    hb = (x * lax.rsqrt(ms + EPS) * (n2g_ref[...] * (1.0 + sc2)) + sh2).astype(BF16)
    ff = jnp.zeros(x.shape, F32)
    for c in range(D_FF // FF_CHUNK):
        a = jnp.dot(hb, w1_ref[:, c * FF_CHUNK:(c + 1) * FF_CHUNK], preferred_element_type=F32)
        a = jnp.maximum(a, 0.0)
        ff = ff + jnp.dot((a * a).astype(BF16), w2_ref[c * FF_CHUNK:(c + 1) * FF_CHUNK, :],
                          preferred_element_type=F32)
    x2 = x + g2 * ff
    ms2 = jnp.mean(x2 * x2, axis=-1, keepdims=True)
    y_ref[...] = x2 * lax.rsqrt(ms2 + EPS) * fg_ref[...]


def _ffn(x1, m3, m_off, tiles_per_cond, n2g, fgain, w1, w2):
    n = x1.shape[0]

    def m_index(i):
        if tiles_per_cond is None:
            return (m_off, 0, 0)
        return (m_off + i // tiles_per_cond, 0, 0)

    return pl.pallas_call(
        _ffn_kernel,
        grid=(n // FFN_TILE,),
        in_specs=[
            pl.BlockSpec((FFN_TILE, D_MODEL), lambda i: (i, 0)),
            pl.BlockSpec((1, 1, 6 * D_MODEL), m_index),
            _const_spec(n2g.shape),
            _const_spec(fgain.shape),
            _const_spec(w1.shape),
            _const_spec(w2.shape),
        ],
        out_specs=pl.BlockSpec((FFN_TILE, D_MODEL), lambda i: (i, 0)),
        out_shape=jax.ShapeDtypeStruct((n, D_MODEL), F32),
        compiler_params=pltpu.CompilerParams(
            dimension_semantics=("arbitrary",), vmem_limit_bytes=VMEM_LIMIT),
        name="ffn",
    )(x1, m3, n2g, fgain, w1, w2)


def _grid_pos_tables(n_tok):
    quarter = D_MODEL // 4
    omega = (1.0 / (np.float32(POS_BASE) ** (np.arange(quarter, dtype=np.float32) / np.float32(quarter)))
             ).astype(np.float32)

    def emb(n):
        ang = np.arange(n).reshape(-1, 1).astype(np.float32) * omega
        return np.concatenate([np.sin(ang), np.cos(ang)], axis=-1)

    rows = np.repeat(emb(n_tok // GRID_W)[:, None, :], 8, axis=1)
    return jnp.asarray(rows, dtype=F32), jnp.asarray(emb(GRID_W), dtype=F32)


def kernel(x_prompt, x_sample, c, state_hgrn, state_rglru, c_ctx, w_ada, b_ada, norm1_gain, norm2_gain,
           w_in, hg_lb_logits, hg_norm_gain, conv_w, conv_b, lru_wa, lru_ba, lru_wx, lru_bx, lru_lambda,
           w_out, w_ff1, w_ff2, final_gain):
    bp, tp, _ = x_prompt.shape
    bs_, ts, _ = x_sample.shape

    cond8 = jnp.concatenate([c_ctx[None, :], c, jnp.zeros((8 - 1 - bs_, D_MODEL), F32)], axis=0)
    m3 = _modulation(cond8, w_ada[0], b_ada).reshape(8, 1, 6 * D_MODEL)

    masks_np, tri_np = _level_tables()
    grp = [slice(g * LRU_GROUP, (g + 1) * LRU_GROUP) for g in range(LRU_WIDTH // LRU_GROUP)]
    bias_g = jnp.stack([jnp.concatenate([lru_ba[0, 0, s], lru_bx[0, 0, s], lru_ba[0, 1, s], lru_bx[0, 1, s]])
                        for s in grp])
    consts = [
        norm1_gain,
        w_in[0].astype(BF16),
        hg_lb_logits.reshape(2, N_DIR * HG_HEADS, HG_DK),
        hg_norm_gain[0].reshape(1, HG_WIDTH),
        jnp.asarray(masks_np),
        jnp.asarray(tri_np, dtype=BF16),
        conv_w[0],
        conv_b,
        jnp.stack([lru_wa[0], lru_wx[0]], axis=1).reshape(
            N_DIR * 2 * LRU_BLOCKS, LRU_BLOCK, LRU_BLOCK),
        bias_g,
        lru_lambda[0],
        w_out[0].astype(BF16),
    ]
    w1 = w_ff1[0].astype(BF16)
    w2 = w_ff2[0].astype(BF16)
    fgain = final_gain.reshape(1, D_MODEL)

    x1p, hs, ls = _mixer(x_prompt, m3, 0, 0, None, consts, None, True)
    y_prompt = _ffn(x1p.reshape(bp * tp, D_MODEL), m3, 0, None, norm2_gain, fgain, w1, w2)

    x1s = _mixer(x_sample, m3, 1, 1, _grid_pos_tables(ts), consts,
                 (state_hgrn.reshape(bs_, N_DIR * HG_HEADS, HG_DK, HG_DK),
                  state_rglru.reshape(bs_, N_DIR, LRU_WIDTH)), False)[0]
    y_sample = _ffn(x1s.reshape(bs_ * ts, D_MODEL), m3, 1, ts // FFN_TILE, norm2_gain, fgain, w1, w2)

    return (y_prompt.reshape(bp, tp, D_MODEL),
            y_sample.reshape(bs_, ts, D_MODEL),
            hs.reshape(bp, 1, N_DIR, HG_HEADS, HG_DK, HG_DK),
            ls.reshape(bp, 1, N_DIR, LRU_WIDTH))
```

```python
import functools

import numpy as np
import jax
import jax.numpy as jnp
from jax import lax
from jax.experimental import pallas as pl
from jax.experimental.pallas import tpu as pltpu

F32 = jnp.float32
BF16 = jnp.bfloat16

D_MODEL = 1024
N_DIR = 2
HG_HEADS = 4
HG_DK = 128
HG_WIDTH = 512
LRU_WIDTH = 512
LRU_BLOCKS = 8
LRU_BLOCK = 64
LRU_C = 8.0
D_FF = 4096
IN_COLS = 4096
EPS = 1e-6
LOG2E = 1.4426950408889634
GRID_W = 64
POS_BASE = 10000.0

COL_Q = 0
COL_F = 1024
COL_V = 2048
COL_GATE = 2560
COL_LX = 3072
COL_LG = 3584

CHUNK = 128
LEVELS = (64, 32, 16, 8, 4)
LV_4 = LEVELS.index(4)
LV_21 = len(LEVELS)
LV_DIAG = LV_21 + 1
LV_STATE = LV_DIAG + 1
N_MASKS = LV_DIAG + 1
ROW_TILE = 256
LRU_SEGS = 8
CONV_TILE = 32
SCAN_UNROLL = 4
FIX_TILE = 64
STAGE_SLOTS = 1
MAX_INLINE_TRIPS = 2
LRU_GROUP = 256
LRS_SEG_COL = 2 * N_DIR * LRU_WIDTH
FFN_TILE = 512
FF_CHUNK = 1024
MOD_TILE = 1024
VMEM_LIMIT = 58 * 1024 * 1024
VMEM_HEADROOM = 6 * 1024 * 1024


def _sig(x):
    return jax.nn.sigmoid(x)


def _nt_dot(a, b):
    return lax.dot_general(a, b, (((1,), (1,)), ((), ())), preferred_element_type=F32)


def _staged_transpose(xb, slot_ref):
    slot_ref[...] = xb.T
    return slot_ref[...]


def _gram(x, slot_ref):
    xb = x.astype(BF16)
    return jnp.dot(xb, _staged_transpose(xb, slot_ref), preferred_element_type=F32)


def _level_tables():
    t = np.arange(CHUNK)[:, None]
    s = np.arange(CHUNK)[None, :]
    masks = np.zeros((N_DIR, N_MASKS, CHUNK, CHUNK), np.float32)
    for li, h in enumerate(LEVELS):
        same = (t // (2 * h)) == (s // (2 * h))
        t_hi = (t // h) % 2 == 1
        s_hi = (s // h) % 2 == 1
        masks[0, li] = same & t_hi & ~s_hi
        masks[1, li] = same & ~t_hi & s_hi
    same4 = (t // 4) == (s // 4)
    masks[0, LV_21] = same4 & (s < t)
    masks[1, LV_21] = same4 & (s > t)
    masks[:, LV_DIAG] = (t == s)
    tri = np.stack([(s <= t), (s >= t)]).astype(np.float32)
    return masks, tri


def _mod_kernel(c_ref, w_ref, b_ref, o_ref):
    c = c_ref[...]
    a = (c * _sig(c)).astype(BF16)
    o_ref[...] = jnp.dot(a, w_ref[...].astype(BF16), preferred_element_type=F32) + b_ref[...]


def _modulation(cond8, w_ada, b_ada):
    n = w_ada.shape[1]
    return pl.pallas_call(
        _mod_kernel,
        grid=(n // MOD_TILE,),
        in_specs=[
            pl.BlockSpec((8, D_MODEL), lambda j: (0, 0)),
            pl.BlockSpec((D_MODEL, MOD_TILE), lambda j: (0, j)),
            pl.BlockSpec((1, MOD_TILE), lambda j: (0, j)),
        ],
        out_specs=pl.BlockSpec((8, MOD_TILE), lambda j: (0, j)),
        out_shape=jax.ShapeDtypeStruct((8, n), F32),
        compiler_params=pltpu.CompilerParams(dimension_semantics=("arbitrary",)),
        name="adaln_modulation",
    )(cond8, w_ada, b_ada)


def _mixer_kernel(*refs, T, has_pos, has_state, emit_state, fuse_scan):
    it = iter(refs)
    x_ref = next(it)
    m_ref = next(it)
    if has_pos:
        posr_ref = next(it)
        posc_ref = next(it)
    n1g_ref = next(it)
    win_ref = next(it)
    lbl_ref = next(it)
    hgg_ref = next(it)
    masks_ref = next(it)
    tri_ref = next(it)
    convw_ref = next(it)
    convb_ref = next(it)
    gw_ref = next(it)
    bg_ref = next(it)
    lam_ref = next(it)
    wout_ref = next(it)
    if has_state:
        hs0_ref = next(it)
        ls0_ref = next(it)
    x1_ref = next(it)
    if emit_state:
        hs_ref = next(it)
        ls_ref = next(it)
    proj = next(it)
    lxp = next(it)
    mixin = next(it)
    st = next(it)
    ksc = next(it)
    xci = next(it)
    vts = next(it)
    rowc = next(it)
    tsl = next(it)
    wg = next(it)
    if fuse_scan:
        lrs = next(it)

    L = CHUNK

    @pl.when(pl.program_id(0) == 0)
    def _build_gate_weights():
        per_group = LRU_GROUP // LRU_BLOCK
        wg[...] = jnp.zeros(wg.shape, BF16)
        for g in range(LRU_WIDTH // LRU_GROUP):
            for p in range(2 * N_DIR):
                for n in range(per_group):
                    r = n * LRU_BLOCK
                    col = p * LRU_GROUP + r
                    wg[g, r:r + LRU_BLOCK, col:col + LRU_BLOCK] = gw_ref[
                        p * LRU_BLOCKS + g * per_group + n].astype(BF16)

    mrow = m_ref[0]
    sh1 = mrow[:, 0:D_MODEL]
    sc1 = mrow[:, D_MODEL:2 * D_MODEL]
    g1 = mrow[:, 2 * D_MODEL:3 * D_MODEL]
    gain1 = n1g_ref[...] * (1.0 + sc1)

    zrows = jnp.zeros((8, LRU_WIDTH), F32)
    lxp[0:8, :] = zrows
    lxp[T + 8:T + 16, :] = zrows

    def load_x(i):
        xt = x_ref[0, pl.ds(pl.multiple_of(i * ROW_TILE, ROW_TILE), ROW_TILE), :]
        if has_pos:
            per_tile = ROW_TILE // GRID_W
            tiles = []
            for s in range(per_tile):
                row_emb = jnp.concatenate([posr_ref[i * per_tile + s]] * (GRID_W // 8), axis=0)
                tiles.append(jnp.concatenate([row_emb, posc_ref[...]], axis=1))
            xt = xt + jnp.concatenate(tiles, axis=0)
        return xt

    def proj_body(i, carry):
        r0 = pl.multiple_of(i * ROW_TILE, ROW_TILE)
        xt = load_x(i)
        ms = jnp.mean(xt * xt, axis=-1, keepdims=True)
        hb = (xt * lax.rsqrt(ms + EPS) * gain1 + sh1).astype(BF16)
        n_col = IN_COLS // 512
        order = [COL_LX // 512] + [c for c in range(n_col) if c * 512 < COL_GATE] + [
            c for c in range(n_col) if c * 512 >= COL_GATE and c * 512 != COL_LX]
        for c in order:
            res = jnp.dot(hb, win_ref[:, c * 512:(c + 1) * 512], preferred_element_type=F32)
            if c * 512 == COL_LX:
                lxp[pl.ds(pl.multiple_of(r0 + 8, 8), ROW_TILE), :] = res
            else:
                proj[pl.ds(r0, ROW_TILE), c * 512:(c + 1) * 512] = res
        return carry

    l0 = lbl_ref[0]
    l1 = lbl_ref[1]
    lmx = jnp.maximum(l0, l1)
    e0 = jnp.exp(l0 - lmx)
    e1 = jnp.exp(l1 - lmx)
    lb_all = e0 / (e0 + e1)

    def hg_prep(c, carry):
        rows = pl.ds(pl.multiple_of(c * L, L), L)
        for hd in range(HG_HEADS):
            vts[c, hd] = proj[rows, COL_V + hd * HG_DK:COL_V + (hd + 1) * HG_DK].T.astype(BF16)
        for d in range(N_DIR):
            parts = []
            for hd in range(HG_HEADS):
                idx = d * HG_HEADS + hd
                cq = COL_Q + idx * HG_DK
                cf = COL_F + idx * HG_DK
                hq = proj[rows, cq:cq + HG_DK]
                fz = proj[rows, cf:cf + HG_DK]
                proj[rows, cq:cq + HG_DK] = hq * _sig(hq)
                sg = _sig(fz)
                lb = lb_all[idx:idx + 1, :]
                oml = 1.0 - lb
                ksc[rows, idx * HG_DK:(idx + 1) * HG_DK] = oml * (1.0 - sg)
                logf = jnp.log(lb + oml * sg)
                p1 = logf.astype(BF16)
                p2 = (logf - p1.astype(F32)).astype(BF16)
                parts += [p1, p2]
            bb = jnp.dot(tri_ref[d], jnp.concatenate(parts, axis=1), preferred_element_type=F32)
            for hd in range(HG_HEADS):
                cf = COL_F + (d * HG_HEADS + hd) * HG_DK
                o2 = 2 * hd * HG_DK
                proj[rows, cf:cf + HG_DK] = (bb[:, o2:o2 + HG_DK] + bb[:, o2 + HG_DK:o2 + 2 * HG_DK]) * LOG2E
        return carry

    lax.fori_loop(0, T // ROW_TILE, proj_body, 0)

    S = T // LRU_SEGS
    cw = convw_ref[...]
    cb = convb_ref[...]

    def irows(i0, j, n):
        return pl.ds(pl.multiple_of(LRU_SEGS * i0, 8) + j, n, stride=LRU_SEGS)

    for j in range(LRU_SEGS):
        def conv_body(ti, carry, j=j):
            i0 = pl.multiple_of(ti * CONV_TILE, CONV_TILE)
            win = lxp[pl.ds(pl.multiple_of(j * S + i0, 8), CONV_TILE + 16), :]
            xc = cb
            for tap in range(4):
                xc = xc + win[6 + tap:6 + tap + CONV_TILE] * cw[tap:tap + 1]
            for l in range(LRU_WIDTH // 128):
                xci[l, irows(i0, j, CONV_TILE), :] = xc[:, l * 128:(l + 1) * 128]
            return carry

        lax.fori_loop(0, S // CONV_TILE, conv_body, 0, unroll=S // CONV_TILE <= MAX_INLINE_TRIPS)

    n_grp = LRU_WIDTH // LRU_GROUP
    slabs = LRU_GROUP // 128
    gates = lrs if fuse_scan else proj

    def xc_tile(rows, g):
        return jnp.concatenate([xci[g * slabs + l, rows, :] for l in range(slabs)], axis=1)

    def gate_body(i, carry):
        rows = pl.ds(pl.multiple_of(i * ROW_TILE, ROW_TILE), ROW_TILE)
        for g in range(n_grp):
            gates[rows, g * 1024:(g + 1) * 1024] = jnp.dot(
                xc_tile(rows, g).astype(BF16), wg[g], preferred_element_type=F32)
        return carry

    lam = lam_ref[...]
    nl = -lam
    c8 = -LRU_C * (jnp.maximum(nl, 0.0) + jnp.log1p(jnp.exp(-jnp.abs(nl))))
    rowi2 = lax.broadcasted_iota(jnp.int32, (8, LRU_GROUP), 0)

    for d in range(N_DIR):
        for g in range(n_grp):
            k3 = 3 * (d * n_grp + g)
            for r, row in enumerate((bg_ref[g:g + 1, d * 512:d * 512 + LRU_GROUP],
                                     bg_ref[g:g + 1, d * 512 + LRU_GROUP:d * 512 + 2 * LRU_GROUP],
                                     c8[d:d + 1, g * LRU_GROUP:(g + 1) * LRU_GROUP])):
                rowc[k3 + r] = jnp.broadcast_to(row, (LRU_SEGS, LRU_GROUP))

    def seg_out(d, rows, g, part):
        if fuse_scan:
            col = LRS_SEG_COL + ((d * n_grp + g) * 2 + part) * LRU_GROUP
            return lrs.at[rows, col:col + LRU_GROUP]
        cols = slice((g * 2 + part) * LRU_GROUP, (g * 2 + part + 1) * LRU_GROUP)
        return (ksc.at[rows, cols] if d == 0 else x1_ref.at[0, rows, cols])

    def lru_inputs(rows8, g, d):
        base = g * 1024 + d * 512
        k3 = 3 * (d * n_grp + g)
        ga = gates[rows8, base:base + LRU_GROUP] + rowc[k3]
        gx = gates[rows8, base + LRU_GROUP:base + 2 * LRU_GROUP] + rowc[k3 + 1]
        xc8 = xc_tile(rows8, g)
        log_a = rowc[k3 + 2] * _sig(ga)
        a = jnp.exp(log_a)
        mult = jnp.sqrt(jnp.tanh(-log_a) * (1.0 + a * a))
        return a, mult * (_sig(gx) * xc8)

    def scan8(a, u, d):
        for sft in (1, 2, 4):
            if d == 0:
                keep = rowi2 >= sft
                amt = sft
            else:
                keep = rowi2 < 8 - sft
                amt = 8 - sft
            ash = jnp.where(keep, pltpu.roll(a, amt, 0), 1.0)
            ush = jnp.where(keep, pltpu.roll(u, amt, 0), 0.0)
            u = a * ush + u
            a = a * ash
        return a, u

    def scan_steps(first, count, carry):
        hs, ds = [list(c) for c in carry[:2]], [list(c) for c in carry[2:]]
        for u in range(count):
            i_f = first + u
            for d, i in ((0, i_f), (1, S - 1 - i_f)):
                rows8 = pl.ds(pl.multiple_of(LRU_SEGS * i, 8), 8)
                for g in range(n_grp):
                    a, uu = lru_inputs(rows8, g, d)
                    hs[d][g] = a * hs[d][g] + uu
                    ds[d][g] = a * ds[d][g]
                    seg_out(d, rows8, g, 0)[...] = hs[d][g]
                    seg_out(d, rows8, g, 1)[...] = ds[d][g]
        return tuple(tuple(c) for c in hs + ds)

    zero8 = jnp.zeros((LRU_SEGS, LRU_GROUP), F32)
    one8 = jnp.ones((LRU_SEGS, LRU_GROUP), F32)
    scan_init = ((zero8,) * n_grp, (zero8,) * n_grp, (one8,) * n_grp, (one8,) * n_grp)
    if fuse_scan:
        lax.fori_loop(0, T // ROW_TILE, gate_body, 0)

    rowi = lax.broadcasted_iota(jnp.int32, (8, HG_DK), 0)
    r4 = rowi & 3
    is_r0 = r4 == 0
    is_r1 = r4 == 1
    is_r2 = r4 == 2
    hi4 = rowi >= 4

    for i in range(N_DIR * HG_HEADS):
        if has_state:
            st[i] = hs0_ref[0, i].T
        else:
            st[i] = jnp.zeros((HG_DK, HG_DK), F32)

    def hg_chunk(c, d):
        r0 = pl.multiple_of(c * L, L)
        rows = pl.ds(r0, L)
        for hd in range(HG_HEADS):
            idx = d * HG_HEADS + hd
            cq = COL_Q + idx * HG_DK
            cf = COL_F + idx * HG_DK
            ck = idx * HG_DK
            slot = idx % STAGE_SLOTS

            def ldq(lo, n, cq=cq):
                return proj[pl.ds(pl.multiple_of(r0 + lo, 8), n), cq:cq + HG_DK]

            def ldb(lo, n, cf=cf):
                return proj[pl.ds(pl.multiple_of(r0 + lo, 8), n), cf:cf + HG_DK]

            def ldk(lo, n, ck=ck):
                return ksc[pl.ds(pl.multiple_of(r0 + lo, 8), n), ck:ck + HG_DK]

            def bline(r, cf=cf):
                grp = proj[pl.ds(pl.multiple_of(r0 + 8 * (r // 8), 8), 8), cf:cf + HG_DK]
                return grp[r % 8:r % 8 + 1, :]

            def brow(r, n):
                return jnp.broadcast_to(bline(r), (n, HG_DK))

            acc_rows = [None] * (L // 8)

            def accumulate(li, p, row0):
                for i in range(p.shape[0] // 8):
                    g = row0 // 8 + i
                    term = masks_ref[d, li, 8 * g:8 * g + 8, :] * p[8 * i:8 * i + 8]
                    acc_rows[g] = term if acc_rows[g] is None else acc_rows[g] + term

            accumulate(LV_DIAG, jnp.dot(
                ldq(0, L).astype(BF16), _staged_transpose(ldk(0, L).astype(BF16), tsl.at[slot, LV_DIAG]),
                preferred_element_type=F32), 0)
            for li, h in enumerate(LEVELS[:LV_4]):
                pieces, q_pieces, q_starts = [], [], []
                for j in range(L // (2 * h)):
                    lo = j * 2 * h
                    mid = lo + h
                    if d == 0:
                        bm = brow(mid - 1, h)
                        kp = ldk(lo, h) * jnp.exp2(bm - ldb(lo, h))
                        qp = ldq(mid, h) * jnp.exp2(ldb(mid, h) - bm)
                        pieces += [kp, qp]
                        q_starts.append(mid)
                    else:
                        bm = brow(mid, h)
                        qp = ldq(lo, h) * jnp.exp2(ldb(lo, h) - bm)
                        kp = ldk(mid, h) * jnp.exp2(bm - ldb(mid, h))
                        pieces += [qp, kp]
                        q_starts.append(lo)
                    q_pieces.append(qp)
                xt = _staged_transpose(jnp.concatenate(pieces, axis=0).astype(BF16), tsl.at[slot,li])
                p = jnp.dot(jnp.concatenate(q_pieces, axis=0).astype(BF16), xt, preferred_element_type=F32)
                for j, row0 in enumerate(q_starts):
                    accumulate(li, p[j * h:(j + 1) * h], row0)
            x4, xq21, xk21 = [], [], []
            for g in range(L // 8):
                qg, kg, bg = ldq(8 * g, 8), ldk(8 * g, 8), ldb(8 * g, 8)
                fg = 1.0 - kg
                qfg = qg * fg
                mid = 8 * g + (3 if d == 0 else 4)
                e4 = jnp.exp2(-jnp.abs(bg - brow(mid, 8)))
                fnx = pltpu.roll(fg, 7, 0)
                fpv = pltpu.roll(fg, 1, 0)
                k_over_f = kg / fg
                if d == 0:
                    x4.append(jnp.where(hi4, qg, kg) * e4)
                    xq21.append(jnp.where(is_r0, 0.0, jnp.where(is_r1, qg, jnp.where(is_r2, qfg, qfg * fpv))))
                    xk21.append(jnp.where(is_r0, kg * fnx, jnp.where(is_r1, kg, jnp.where(is_r2, k_over_f, 0.0))))
                else:
                    x4.append(jnp.where(hi4, kg, qg) * e4)
                    xq21.append(jnp.where(is_r0, qfg * fnx, jnp.where(is_r1, qfg, jnp.where(is_r2, qg, 0.0))))
                    xk21.append(jnp.where(is_r0, 0.0, jnp.where(is_r1, k_over_f, jnp.where(is_r2, kg, kg * fpv))))
            accumulate(LV_4, _gram(jnp.concatenate(x4, axis=0), tsl.at[slot, LV_4]), 0)
            accumulate(LV_21, jnp.dot(
                jnp.concatenate(xq21, axis=0).astype(BF16),
                _staged_transpose(jnp.concatenate(xk21, axis=0).astype(BF16), tsl.at[slot, LV_21]),
                preferred_element_type=F32), 0)
            acc = jnp.concatenate(acc_rows, axis=0)

            vt = vts[c, hd]
            st_t = st[idx]
            b = ldb(0, L)
            qt = (ldq(0, L) * jnp.exp2(b)).astype(BF16)
            vb = proj[rows, COL_V + hd * HG_DK:COL_V + (hd + 1) * HG_DK].astype(BF16)
            o = jnp.dot(jnp.concatenate([acc.astype(BF16), qt], axis=1),
                        jnp.concatenate([vb, _staged_transpose(st_t.astype(BF16), tsl.at[slot, LV_STATE])], axis=0),
                        preferred_element_type=F32)
            btot = bline(L - 1 if d == 0 else 0)
            kt = (ldk(0, L) * jnp.exp2(btot - b)).astype(BF16)
            st[idx] = st_t * jnp.exp2(btot) + jnp.dot(vt, kt, preferred_element_type=F32)
            x1_ref[0, rows, d * HG_WIDTH + hd * HG_DK:d * HG_WIDTH + (hd + 1) * HG_DK] = o

    n_chunks = T // L

    def hg_both(c, carry):
        hg_chunk(c, 0)
        hg_chunk(n_chunks - 1 - c, 1)
        if fuse_scan:
            steps = S // n_chunks
            carry = scan_steps(c * steps, steps, carry)
        return carry

    lax.fori_loop(0, n_chunks, hg_prep, 0, unroll=n_chunks <= MAX_INLINE_TRIPS)
    scan_state = lax.fori_loop(0, n_chunks, hg_both, scan_init if fuse_scan else 0)

    if emit_state:
        for i in range(N_DIR * HG_HEADS):
            hs_ref[0, i] = st[i].T

    def hg_fin(i, carry):
        rows = pl.ds(pl.multiple_of(i * L, L), L)
        for hd in range(HG_HEADS):
            cs = slice(hd * HG_DK, (hd + 1) * HG_DK)
            o = x1_ref[0, rows, cs] + x1_ref[0, rows, HG_WIDTH + hd * HG_DK:HG_WIDTH + (hd + 1) * HG_DK]
            ms = jnp.mean(o * o, axis=-1, keepdims=True)
            y = o * lax.rsqrt(ms + EPS) * hgg_ref[:, cs]
            gz = proj[rows, COL_GATE + hd * HG_DK:COL_GATE + (hd + 1) * HG_DK]
            mixin[rows, cs] = (y * (gz * _sig(gz))).astype(BF16)
        return carry

    lax.fori_loop(0, n_chunks, hg_fin, 0, unroll=n_chunks <= MAX_INLINE_TRIPS)

    if not fuse_scan:
        lax.fori_loop(0, T // ROW_TILE, gate_body, 0)
        scan_state = lax.fori_loop(
            0, S // SCAN_UNROLL, lambda n, carry: scan_steps(n * SCAN_UNROLL, SCAN_UNROLL, carry), scan_init)
    h_end, d_end = scan_state[:2], scan_state[2:]

    carry_in = [[None] * n_grp for _ in range(N_DIR)]
    for d in range(N_DIR):
        for g in range(n_grp):
            if has_state:
                h0 = ls0_ref[0, d:d + 1, g * LRU_GROUP:(g + 1) * LRU_GROUP]
            else:
                h0 = jnp.zeros((1, LRU_GROUP), F32)
            dd, hh = scan8(d_end[d][g], h_end[d][g], d)
            seg_end = hh + dd * h0
            if d == 0:
                carry_in[d][g] = jnp.where(rowi2 >= 1, pltpu.roll(seg_end, 1, 0), h0)
                last = seg_end[LRU_SEGS - 1:LRU_SEGS, :]
            else:
                carry_in[d][g] = jnp.where(rowi2 < LRU_SEGS - 1, pltpu.roll(seg_end, LRU_SEGS - 1, 0), h0)
                last = seg_end[0:1, :]
            if emit_state:
                ls_ref[0, d:d + 1, g * LRU_GROUP:(g + 1) * LRU_GROUP] = last

    def fix_body(i, carry):
        rows = pl.ds(pl.multiple_of(i * FIX_TILE, FIX_TILE), FIX_TILE)
        for g in range(n_grp):
            tot = None
            for d in range(N_DIR):
                cin = jnp.concatenate([carry_in[d][g]] * (FIX_TILE // LRU_SEGS), axis=0)
                h = seg_out(d, rows, g, 0)[...] + seg_out(d, rows, g, 1)[...] * cin
                tot = h if tot is None else tot + h
            for l in range(slabs):
                xci[g * slabs + l, rows, :] = tot[:, l * 128:(l + 1) * 128]
        return carry

    lax.fori_loop(0, T // FIX_TILE, fix_body, 0)

    for j in range(LRU_SEGS):
        def lru_fin(ti, carry, j=j):
            i0 = pl.multiple_of(ti * CONV_TILE, CONV_TILE)
            rows = pl.ds(pl.multiple_of(j * S + i0, CONV_TILE), CONV_TILE)
            hsum = jnp.concatenate([xci[l, irows(i0, j, CONV_TILE), :] for l in range(LRU_WIDTH // 128)], axis=1)
            lg = proj[rows, COL_LG:COL_LG + LRU_WIDTH]
            gl = lg * (0.5 * (1.0 + jnp.tanh(0.7978845608028654 * (lg + 0.044715 * (lg * lg * lg)))))
            mixin[rows, HG_WIDTH:HG_WIDTH + LRU_WIDTH] = (hsum * gl).astype(BF16)
            return carry

        lax.fori_loop(0, S // CONV_TILE, lru_fin, 0, unroll=S // CONV_TILE <= MAX_INLINE_TRIPS)

    def out_body(i, carry):
        rows = pl.ds(pl.multiple_of(i * ROW_TILE, ROW_TILE), ROW_TILE)
        mix = jnp.dot(mixin[rows, :], wout_ref[...], preferred_element_type=F32)
        x1_ref[0, rows, :] = load_x(i) + g1 * mix
        return carry

    lax.fori_loop(0, T // ROW_TILE, out_body, 0)


def _const_spec(shape):
    nd = len(shape)
    return pl.BlockSpec(shape, lambda b, _n=nd: (0,) * _n, pipeline_mode=pl.Buffered(1))


def _nbytes(shape, dtype):
    return int(np.prod(shape)) * jnp.dtype(dtype).itemsize


def _mixer(x, m3, m_off, m_step, pos, consts, states, emit_state):
    B, T, _ = x.shape
    has_pos = pos is not None
    has_state = states is not None
    scratch_shapes = [
        ((T, IN_COLS), F32),
        ((T + 16, LRU_WIDTH), F32),
        ((T, D_MODEL), BF16),
        ((8, HG_DK, HG_DK), F32),
        ((T, N_DIR * HG_HEADS * HG_DK), F32),
        ((LRU_WIDTH // 128, T, 128), F32),
        ((T // CHUNK, HG_HEADS, HG_DK, CHUNK), BF16),
        ((3 * N_DIR * (LRU_WIDTH // LRU_GROUP), LRU_SEGS, LRU_GROUP), F32),
        ((STAGE_SLOTS, LV_STATE + 1, HG_DK, CHUNK), BF16),
        ((LRU_WIDTH // LRU_GROUP, LRU_GROUP, 2 * N_DIR * LRU_GROUP), BF16),
    ]
    resident = (sum(_nbytes(s, d) for s, d in scratch_shapes)
                + sum(_nbytes(c.shape, c.dtype) for c in consts)
                + (sum(_nbytes(p.shape, p.dtype) for p in pos) if has_pos else 0))
    io_block = _nbytes((T, D_MODEL), F32)
    budget = VMEM_LIMIT - VMEM_HEADROOM
    lrs_shape = ((T, 2 * LRS_SEG_COL), F32)
    fuse_scan = resident + _nbytes(*lrs_shape) + 4 * io_block <= budget
    if fuse_scan:
        scratch_shapes.append(lrs_shape)
        resident += _nbytes(*lrs_shape)
    io_bufs = 2 if resident + 4 * io_block <= budget else 1
    io_mode = pl.Buffered(io_bufs)
    in_specs = [
        pl.BlockSpec((1, T, D_MODEL), lambda b: (b, 0, 0), pipeline_mode=io_mode),
        pl.BlockSpec((1, 1, 6 * D_MODEL), lambda b: (m_off + m_step * b, 0, 0)),
    ]
    args = [x, m3]
    if has_pos:
        in_specs += [_const_spec(p.shape) for p in pos]
        args += list(pos)
    in_specs += [_const_spec(c.shape) for c in consts]
    args += list(consts)
    if has_state:
        hs0, ls0 = states
        in_specs += [
            pl.BlockSpec((1, 8, HG_DK, HG_DK), lambda b: (b, 0, 0, 0)),
            pl.BlockSpec((1, N_DIR, LRU_WIDTH), lambda b: (b, 0, 0)),
        ]
        args += [hs0, ls0]
    out_shape = [jax.ShapeDtypeStruct((B, T, D_MODEL), F32)]
    out_specs = [pl.BlockSpec((1, T, D_MODEL), lambda b: (b, 0, 0), pipeline_mode=io_mode)]
    if emit_state:
        out_shape += [jax.ShapeDtypeStruct((B, 8, HG_DK, HG_DK), F32),
                      jax.ShapeDtypeStruct((B, N_DIR, LRU_WIDTH), F32)]
        out_specs += [pl.BlockSpec((1, 8, HG_DK, HG_DK), lambda b: (b, 0, 0, 0)),
                      pl.BlockSpec((1, N_DIR, LRU_WIDTH), lambda b: (b, 0, 0))]
    scratch = [pltpu.VMEM(s, d) for s, d in scratch_shapes]
    return pl.pallas_call(
        functools.partial(_mixer_kernel, T=T, has_pos=has_pos, has_state=has_state, emit_state=emit_state,
                          fuse_scan=fuse_scan),
        grid=(B,),
        in_specs=in_specs,
        out_specs=out_specs,
        out_shape=out_shape,
        scratch_shapes=scratch,
        compiler_params=pltpu.CompilerParams(
            dimension_semantics=("arbitrary",), vmem_limit_bytes=VMEM_LIMIT),
        name=f"mixer_t{T}",
    )(*args)


def _ffn_kernel(x_ref, m_ref, n2g_ref, fg_ref, w1_ref, w2_ref, y_ref):
    mrow = m_ref[0]
    sh2 = mrow[:, 3 * D_MODEL:4 * D_MODEL]
    sc2 = mrow[:, 4 * D_MODEL:5 * D_MODEL]
    g2 = mrow[:, 5 * D_MODEL:6 * D_MODEL]
    x = x_ref[...]
    ms = jnp.mean(x * x, axis=-1, keepdims=True)
    hb = (x * lax.rsqrt(ms + EPS) * (n2g_ref[...] * (1.0 + sc2)) + sh2).astype(BF16)
    ff = jnp.zeros(x.shape, F32)
    for c in range(D_FF // FF_CHUNK):
        a = jnp.dot(hb, w1_ref[:, c * FF_CHUNK:(c + 1) * FF_CHUNK], preferred_element_type=F32)
        a = jnp.maximum(a, 0.0)
        ff = ff + jnp.dot((a * a).astype(BF16), w2_ref[c * FF_CHUNK:(c + 1) * FF_CHUNK, :],
                          preferred_element_type=F32)
    x2 = x + g2 * ff
    ms2 = jnp.mean(x2 * x2, axis=-1, keepdims=True)
    y_ref[...] = x2 * lax.rsqrt(ms2 + EPS) * fg_ref[...]


def _ffn(x1, m3, m_off, tiles_per_cond, n2g, fgain, w1, w2):
    n = x1.shape[0]

    def m_index(i):
        if tiles_per_cond is None:
            return (m_off, 0, 0)
        return (m_off + i // tiles_per_cond, 0, 0)

    return pl.pallas_call(
        _ffn_kernel,
        grid=(n // FFN_TILE,),
        in_specs=[
            pl.BlockSpec((FFN_TILE, D_MODEL), lambda i: (i, 0)),
            pl.BlockSpec((1, 1, 6 * D_MODEL), m_index),
            _const_spec(n2g.shape),
            _const_spec(fgain.shape),
            _const_spec(w1.shape),
            _const_spec(w2.shape),
        ],
        out_specs=pl.BlockSpec((FFN_TILE, D_MODEL), lambda i: (i, 0)),
        out_shape=jax.ShapeDtypeStruct((n, D_MODEL), F32),
        compiler_params=pltpu.CompilerParams(
            dimension_semantics=("arbitrary",), vmem_limit_bytes=VMEM_LIMIT),
        name="ffn",
    )(x1, m3, n2g, fgain, w1, w2)


def _grid_pos_tables(n_tok):
    quarter = D_MODEL // 4
    omega = (1.0 / (np.float32(POS_BASE) ** (np.arange(quarter, dtype=np.float32) / np.float32(quarter)))
             ).astype(np.float32)

    def emb(n):
        ang = np.arange(n).reshape(-1, 1).astype(np.float32) * omega
        return np.concatenate([np.sin(ang), np.cos(ang)], axis=-1)

    rows = np.repeat(emb(n_tok // GRID_W)[:, None, :], 8, axis=1)
    return jnp.asarray(rows, dtype=F32), jnp.asarray(emb(GRID_W), dtype=F32)


def kernel(x_prompt, x_sample, c, state_hgrn, state_rglru, c_ctx, w_ada, b_ada, norm1_gain, norm2_gain,
           w_in, hg_lb_logits, hg_norm_gain, conv_w, conv_b, lru_wa, lru_ba, lru_wx, lru_bx, lru_lambda,
           w_out, w_ff1, w_ff2, final_gain):
    bp, tp, _ = x_prompt.shape
    bs_, ts, _ = x_sample.shape

    cond8 = jnp.concatenate([c_ctx[None, :], c, jnp.zeros((8 - 1 - bs_, D_MODEL), F32)], axis=0)
    m3 = _modulation(cond8, w_ada[0], b_ada).reshape(8, 1, 6 * D_MODEL)

    masks_np, tri_np = _level_tables()
    grp = [slice(g * LRU_GROUP, (g + 1) * LRU_GROUP) for g in range(LRU_WIDTH // LRU_GROUP)]
    bias_g = jnp.stack([jnp.concatenate([lru_ba[0, 0, s], lru_bx[0, 0, s], lru_ba[0, 1, s], lru_bx[0, 1, s]])
                        for s in grp])
    consts = [
        norm1_gain,
        w_in[0].astype(BF16),
        hg_lb_logits.reshape(2, N_DIR * HG_HEADS, HG_DK),
        hg_norm_gain[0].reshape(1, HG_WIDTH),
        jnp.asarray(masks_np),
        jnp.asarray(tri_np, dtype=BF16),
        conv_w[0],
        conv_b,
        jnp.stack([lru_wa[0], lru_wx[0]], axis=1).reshape(
            N_DIR * 2 * LRU_BLOCKS, LRU_BLOCK, LRU_BLOCK),
        bias_g,
        lru_lambda[0],
        w_out[0].astype(BF16),
    ]
    w1 = w_ff1[0].astype(BF16)
    w2 = w_ff2[0].astype(BF16)
    fgain = final_gain.reshape(1, D_MODEL)

    x1p, hs, ls = _mixer(x_prompt, m3, 0, 0, None, consts, None, True)
    y_prompt = _ffn(x1p.reshape(bp * tp, D_MODEL), m3, 0, None, norm2_gain, fgain, w1, w2)

    x1s = _mixer(x_sample, m3, 1, 1, _grid_pos_tables(ts), consts,
                 (state_hgrn.reshape(bs_, N_DIR * HG_HEADS, HG_DK, HG_DK),
                  state_rglru.reshape(bs_, N_DIR, LRU_WIDTH)), False)[0]
    y_sample = _ffn(x1s.reshape(bs_ * ts, D_MODEL), m3, 1, ts // FFN_TILE, norm2_gain, fgain, w1, w2)

    return (y_prompt.reshape(bp, tp, D_MODEL),
            y_sample.reshape(bs_, ts, D_MODEL),
            hs.reshape(bp, 1, N_DIR, HG_HEADS, HG_DK, HG_DK),
            ls.reshape(bp, 1, N_DIR, LRU_WIDTH))
```

```python
import functools

import numpy as np
import jax
import jax.numpy as jnp
from jax import lax
from jax.experimental import pallas as pl
from jax.experimental.pallas import tpu as pltpu

F32 = jnp.float32
BF16 = jnp.bfloat16

D_MODEL = 1024
N_DIR = 2
HG_HEADS = 4
HG_DK = 128
HG_WIDTH = 512
LRU_WIDTH = 512
LRU_BLOCKS = 8
LRU_BLOCK = 64
LRU_C = 8.0
D_FF = 4096
IN_COLS = 4096
EPS = 1e-6
LOG2E = 1.4426950408889634
GRID_W = 64
POS_BASE = 10000.0

COL_Q = 0
COL_F = 1024
COL_V = 2048
COL_GATE = 2560
COL_LX = 3072
COL_LG = 3584
PROJ_LG = COL_LX
PROJ_COLS = IN_COLS - LRU_WIDTH

CHUNK = 128
LEVELS = (64, 32, 16, 8, 4)
LV_4 = LEVELS.index(4)
LV_21 = len(LEVELS)
LV_DIAG = LV_21 + 1
LV_STATE = LV_DIAG + 1
N_MASKS = LV_DIAG + 1
ROW_TILE = 256
LRU_SEGS = 8
CONV_TILE = 32
SCAN_UNROLL = 4
FIX_TILE = 64
STAGE_SLOTS = 1
MAX_INLINE_TRIPS = 2
LRU_GROUP = 256
LRS_SEG_COL = 2 * N_DIR * LRU_WIDTH
FFN_TILE = 512
FF_CHUNK = 1024
MOD_TILE = 2048
VMEM_LIMIT = 58 * 1024 * 1024
VMEM_HEADROOM = 6 * 1024 * 1024


def _sig(x):
    return jax.nn.sigmoid(x)


def _nt_dot(a, b):
    return lax.dot_general(a, b, (((1,), (1,)), ((), ())), preferred_element_type=F32)


def _staged_transpose(xb, slot_ref):
    slot_ref[...] = xb.T
    return slot_ref[...]


def _gram(x, slot_ref):
    xb = x.astype(BF16)
    return jnp.dot(xb, _staged_transpose(xb, slot_ref), preferred_element_type=F32)


def _level_tables():
    t = np.arange(CHUNK)[:, None]
    s = np.arange(CHUNK)[None, :]
    masks = np.zeros((N_DIR, N_MASKS, CHUNK, CHUNK), np.float32)
    for li, h in enumerate(LEVELS):
        same = (t // (2 * h)) == (s // (2 * h))
        t_hi = (t // h) % 2 == 1
        s_hi = (s // h) % 2 == 1
        masks[0, li] = same & t_hi & ~s_hi
        masks[1, li] = same & ~t_hi & s_hi
    same4 = (t // 4) == (s // 4)
    masks[0, LV_21] = same4 & (s < t)
    masks[1, LV_21] = same4 & (s > t)
    masks[:, LV_DIAG] = (t == s)
    tri = np.stack([(s <= t), (s >= t)]).astype(np.float32)
    return masks, tri


def _mod_kernel(c_ref, w_ref, b_ref, o_ref):
    c = c_ref[...]
    a = (c * _sig(c)).astype(BF16)
    o_ref[...] = jnp.dot(a, w_ref[...].astype(BF16), preferred_element_type=F32) + b_ref[...]


def _modulation(cond8, w_ada, b_ada):
    n = w_ada.shape[1]
    return pl.pallas_call(
        _mod_kernel,
        grid=(n // MOD_TILE,),
        in_specs=[
            pl.BlockSpec((8, D_MODEL), lambda j: (0, 0)),
            pl.BlockSpec((D_MODEL, MOD_TILE), lambda j: (0, j)),
            pl.BlockSpec((1, MOD_TILE), lambda j: (0, j)),
        ],
        out_specs=pl.BlockSpec((8, MOD_TILE), lambda j: (0, j)),
        out_shape=jax.ShapeDtypeStruct((8, n), F32),
        compiler_params=pltpu.CompilerParams(dimension_semantics=("arbitrary",)),
        name="adaln_modulation",
    )(cond8, w_ada, b_ada)


def _mixer_kernel(*refs, T, has_pos, has_state, emit_state, fuse_scan):
    it = iter(refs)
    x_ref = next(it)
    m_ref = next(it)
    if has_pos:
        posr_ref = next(it)
        posc_ref = next(it)
    n1g_ref = next(it)
    win_ref = next(it)
    lbl_ref = next(it)
    hgg_ref = next(it)
    masks_ref = next(it)
    tri_ref = next(it)
    convw_ref = next(it)
    convb_ref = next(it)
    gw_ref = next(it)
    bg_ref = next(it)
    lam_ref = next(it)
    wout_ref = next(it)
    if has_state:
        hs0_ref = next(it)
        ls0_ref = next(it)
    x1_ref = next(it)
    if emit_state:
        hs_ref = next(it)
        ls_ref = next(it)
    proj = next(it)
    lxp = next(it)
    mixin = next(it)
    st = next(it)
    ksc = next(it)
    xci = next(it)
    vts = next(it)
    rowc = next(it)
    tsl = next(it)
    wg = next(it)
    if fuse_scan:
        lrs = next(it)

    L = CHUNK

    @pl.when(pl.program_id(0) == 0)
    def _build_gate_weights():
        per_group = LRU_GROUP // LRU_BLOCK
        wg[...] = jnp.zeros(wg.shape, BF16)
        for g in range(LRU_WIDTH // LRU_GROUP):
            for p in range(2 * N_DIR):
                for n in range(per_group):
                    r = n * LRU_BLOCK
                    col = p * LRU_GROUP + r
                    wg[g, r:r + LRU_BLOCK, col:col + LRU_BLOCK] = gw_ref[
                        p * LRU_BLOCKS + g * per_group + n].astype(BF16)

    mrow = m_ref[0]
    sh1 = mrow[:, 0:D_MODEL]
    sc1 = mrow[:, D_MODEL:2 * D_MODEL]
    g1 = mrow[:, 2 * D_MODEL:3 * D_MODEL]
    gain1 = n1g_ref[...] * (1.0 + sc1)

    zrows = jnp.zeros((8, LRU_WIDTH), F32)
    lxp[0:8, :] = zrows
    lxp[T + 8:T + 16, :] = zrows

    def load_x(i):
        xt = x_ref[0, pl.ds(pl.multiple_of(i * ROW_TILE, ROW_TILE), ROW_TILE), :]
        if has_pos:
            per_tile = ROW_TILE // GRID_W
            tiles = []
            for s in range(per_tile):
                row_emb = jnp.concatenate([posr_ref[i * per_tile + s]] * (GRID_W // 8), axis=0)
                tiles.append(jnp.concatenate([row_emb, posc_ref[...]], axis=1))
            xt = xt + jnp.concatenate(tiles, axis=0)
        return xt

    def proj_body(i, carry):
        r0 = pl.multiple_of(i * ROW_TILE, ROW_TILE)
        xt = load_x(i)
        ms = jnp.mean(xt * xt, axis=-1, keepdims=True)
        hb = (xt * lax.rsqrt(ms + EPS) * gain1 + sh1).astype(BF16)
        n_col = IN_COLS // 512
        order = [COL_LX // 512] + [c for c in range(n_col) if c * 512 < COL_GATE] + [
            c for c in range(n_col) if c * 512 >= COL_GATE and c * 512 != COL_LX]
        for c in order:
            res = jnp.dot(hb, win_ref[:, c * 512:(c + 1) * 512], preferred_element_type=F32)
            if c * 512 == COL_LX:
                lxp[pl.ds(pl.multiple_of(r0 + 8, 8), ROW_TILE), :] = res
            elif c * 512 == COL_LG:
                proj[pl.ds(r0, ROW_TILE), PROJ_LG:PROJ_LG + LRU_WIDTH] = res
            else:
                proj[pl.ds(r0, ROW_TILE), c * 512:(c + 1) * 512] = res
        return carry

    l0 = lbl_ref[0]
    l1 = lbl_ref[1]
    lmx = jnp.maximum(l0, l1)
    e0 = jnp.exp(l0 - lmx)
    e1 = jnp.exp(l1 - lmx)
    lb_all = e0 / (e0 + e1)

    def hg_prep(c, carry):
        rows = pl.ds(pl.multiple_of(c * L, L), L)
        for hd in range(HG_HEADS):
            vts[c, hd] = proj[rows, COL_V + hd * HG_DK:COL_V + (hd + 1) * HG_DK].T.astype(BF16)
        for d in range(N_DIR):
            parts = []
            for hd in range(HG_HEADS):
                idx = d * HG_HEADS + hd
                cq = COL_Q + idx * HG_DK
                cf = COL_F + idx * HG_DK
                hq = proj[rows, cq:cq + HG_DK]
                fz = proj[rows, cf:cf + HG_DK]
                proj[rows, cq:cq + HG_DK] = hq * _sig(hq)
                sg = _sig(fz)
                lb = lb_all[idx:idx + 1, :]
                oml = 1.0 - lb
                ksc[rows, idx * HG_DK:(idx + 1) * HG_DK] = oml * (1.0 - sg)
                logf = jnp.log(lb + oml * sg)
                p1 = logf.astype(BF16)
                p2 = (logf - p1.astype(F32)).astype(BF16)
                parts += [p1, p2]
            bb = jnp.dot(tri_ref[d], jnp.concatenate(parts, axis=1), preferred_element_type=F32)
            for hd in range(HG_HEADS):
                cf = COL_F + (d * HG_HEADS + hd) * HG_DK
                o2 = 2 * hd * HG_DK
                proj[rows, cf:cf + HG_DK] = (bb[:, o2:o2 + HG_DK] + bb[:, o2 + HG_DK:o2 + 2 * HG_DK]) * LOG2E
        return carry

    lax.fori_loop(0, T // ROW_TILE, proj_body, 0)

    S = T // LRU_SEGS
    cw = convw_ref[...]
    cb = convb_ref[...]

    def irows(i0, j, n):
        return pl.ds(pl.multiple_of(LRU_SEGS * i0, 8) + j, n, stride=LRU_SEGS)

    for j in range(LRU_SEGS):
        def conv_body(ti, carry, j=j):
            i0 = pl.multiple_of(ti * CONV_TILE, CONV_TILE)
            win = lxp[pl.ds(pl.multiple_of(j * S + i0, 8), CONV_TILE + 16), :]
            xc = cb
            for tap in range(4):
                xc = xc + win[6 + tap:6 + tap + CONV_TILE] * cw[tap:tap + 1]
            for l in range(LRU_WIDTH // 128):
                xci[l, irows(i0, j, CONV_TILE), :] = xc[:, l * 128:(l + 1) * 128]
            return carry

        lax.fori_loop(0, S // CONV_TILE, conv_body, 0, unroll=S // CONV_TILE <= MAX_INLINE_TRIPS)

    n_grp = LRU_WIDTH // LRU_GROUP
    slabs = LRU_GROUP // 128
    gates = lrs if fuse_scan else proj

    def xc_tile(rows, g):
        return jnp.concatenate([xci[g * slabs + l, rows, :] for l in range(slabs)], axis=1)

    def gate_body(i, carry):
        rows = pl.ds(pl.multiple_of(i * ROW_TILE, ROW_TILE), ROW_TILE)
        for g in range(n_grp):
            gates[rows, g * 1024:(g + 1) * 1024] = jnp.dot(
                xc_tile(rows, g).astype(BF16), wg[g], preferred_element_type=F32)
        return carry

    lam = lam_ref[...]
    nl = -lam
    c8 = -LRU_C * (jnp.maximum(nl, 0.0) + jnp.log1p(jnp.exp(-jnp.abs(nl))))
    rowi2 = lax.broadcasted_iota(jnp.int32, (8, LRU_GROUP), 0)

    for d in range(N_DIR):
        for g in range(n_grp):
            k3 = 3 * (d * n_grp + g)
            for r, row in enumerate((bg_ref[g:g + 1, d * 512:d * 512 + LRU_GROUP],
                                     bg_ref[g:g + 1, d * 512 + LRU_GROUP:d * 512 + 2 * LRU_GROUP],
                                     c8[d:d + 1, g * LRU_GROUP:(g + 1) * LRU_GROUP])):
                rowc[k3 + r] = jnp.broadcast_to(row, (LRU_SEGS, LRU_GROUP))

    def seg_out(d, rows, g, part):
        if fuse_scan:
            col = LRS_SEG_COL + ((d * n_grp + g) * 2 + part) * LRU_GROUP
            return lrs.at[rows, col:col + LRU_GROUP]
        cols = slice((g * 2 + part) * LRU_GROUP, (g * 2 + part + 1) * LRU_GROUP)
        return (ksc.at[rows, cols] if d == 0 else x1_ref.at[0, rows, cols])

    def lru_inputs(rows8, g, d):
        base = g * 1024 + d * 512
        k3 = 3 * (d * n_grp + g)
        ga = gates[rows8, base:base + LRU_GROUP] + rowc[k3]
        gx = gates[rows8, base + LRU_GROUP:base + 2 * LRU_GROUP] + rowc[k3 + 1]
        xc8 = xc_tile(rows8, g)
        log_a = rowc[k3 + 2] * _sig(ga)
        a = jnp.exp(log_a)
        z = jnp.tanh(-log_a) * (1.0 + a * a)
        mult = jnp.where(z > 0.0, z * lax.rsqrt(z), 0.0)
        return a, mult * (_sig(gx) * xc8)

    def scan8(a, u, d):
        for sft in (1, 2, 4):
            if d == 0:
                keep = rowi2 >= sft
                amt = sft
            else:
                keep = rowi2 < 8 - sft
                amt = 8 - sft
            ash = jnp.where(keep, pltpu.roll(a, amt, 0), 1.0)
            ush = jnp.where(keep, pltpu.roll(u, amt, 0), 0.0)
            u = a * ush + u
            a = a * ash
        return a, u

    def scan_steps(first, count, carry):
        hs, ds = [list(c) for c in carry[:2]], [list(c) for c in carry[2:]]
        for u in range(count):
            i_f = first + u
            for d, i in ((0, i_f), (1, S - 1 - i_f)):
                rows8 = pl.ds(pl.multiple_of(LRU_SEGS * i, 8), 8)
                for g in range(n_grp):
                    a, uu = lru_inputs(rows8, g, d)
                    hs[d][g] = a * hs[d][g] + uu
                    ds[d][g] = a * ds[d][g]
                    seg_out(d, rows8, g, 0)[...] = hs[d][g]
                    seg_out(d, rows8, g, 1)[...] = ds[d][g]
        return tuple(tuple(c) for c in hs + ds)

    zero8 = jnp.zeros((LRU_SEGS, LRU_GROUP), F32)
    one8 = jnp.ones((LRU_SEGS, LRU_GROUP), F32)
    scan_init = ((zero8,) * n_grp, (zero8,) * n_grp, (one8,) * n_grp, (one8,) * n_grp)
    if fuse_scan:
        lax.fori_loop(0, T // ROW_TILE, gate_body, 0)

    rowi = lax.broadcasted_iota(jnp.int32, (8, HG_DK), 0)
    r4 = rowi & 3
    is_r0 = r4 == 0
    is_r1 = r4 == 1
    is_r2 = r4 == 2
    hi4 = rowi >= 4

    for i in range(N_DIR * HG_HEADS):
        if has_state:
            st[i] = hs0_ref[0, i].T
        else:
            st[i] = jnp.zeros((HG_DK, HG_DK), F32)

    def hg_chunk(c, d):
        r0 = pl.multiple_of(c * L, L)
        rows = pl.ds(r0, L)
        for hd in range(HG_HEADS):
            idx = d * HG_HEADS + hd
            cq = COL_Q + idx * HG_DK
            cf = COL_F + idx * HG_DK
            ck = idx * HG_DK
            slot = idx % STAGE_SLOTS

            def ldq(lo, n, cq=cq):
                return proj[pl.ds(pl.multiple_of(r0 + lo, 8), n), cq:cq + HG_DK]

            def ldb(lo, n, cf=cf):
                return proj[pl.ds(pl.multiple_of(r0 + lo, 8), n), cf:cf + HG_DK]

            def ldk(lo, n, ck=ck):
                return ksc[pl.ds(pl.multiple_of(r0 + lo, 8), n), ck:ck + HG_DK]

            def bline(r, cf=cf):
                grp = proj[pl.ds(pl.multiple_of(r0 + 8 * (r // 8), 8), 8), cf:cf + HG_DK]
                return grp[r % 8:r % 8 + 1, :]

            def brow(r, n):
                return jnp.broadcast_to(bline(r), (n, HG_DK))

            acc_rows = [None] * (L // 8)

            def accumulate(li, p, row0):
                for i in range(p.shape[0] // 8):
                    g = row0 // 8 + i
                    term = masks_ref[d, li, 8 * g:8 * g + 8, :] * p[8 * i:8 * i + 8]
                    acc_rows[g] = term if acc_rows[g] is None else acc_rows[g] + term

            accumulate(LV_DIAG, jnp.dot(
                ldq(0, L).astype(BF16), _staged_transpose(ldk(0, L).astype(BF16), tsl.at[slot, LV_DIAG]),
                preferred_element_type=F32), 0)
            for li, h in enumerate(LEVELS[:LV_4]):
                pieces, q_pieces, q_starts = [], [], []
                for j in range(L // (2 * h)):
                    lo = j * 2 * h
                    mid = lo + h
                    if d == 0:
                        bm = brow(mid - 1, h)
                        kp = ldk(lo, h) * jnp.exp2(bm - ldb(lo, h))
                        qp = ldq(mid, h) * jnp.exp2(ldb(mid, h) - bm)
                        pieces += [kp, qp]
                        q_starts.append(mid)
                    else:
                        bm = brow(mid, h)
                        qp = ldq(lo, h) * jnp.exp2(ldb(lo, h) - bm)
                        kp = ldk(mid, h) * jnp.exp2(bm - ldb(mid, h))
                        pieces += [qp, kp]
                        q_starts.append(lo)
                    q_pieces.append(qp)
                xt = _staged_transpose(jnp.concatenate(pieces, axis=0).astype(BF16), tsl.at[slot,li])
                p = jnp.dot(jnp.concatenate(q_pieces, axis=0).astype(BF16), xt, preferred_element_type=F32)
                for j, row0 in enumerate(q_starts):
                    accumulate(li, p[j * h:(j + 1) * h], row0)
            x4, xq21, xk21 = [], [], []
            for g in range(L // 8):
                qg, kg, bg = ldq(8 * g, 8), ldk(8 * g, 8), ldb(8 * g, 8)
                fg = 1.0 - kg
                qfg = qg * fg
                mid = 8 * g + (3 if d == 0 else 4)
                e4 = jnp.exp2(-jnp.abs(bg - brow(mid, 8)))
                fnx = pltpu.roll(fg, 7, 0)
                fpv = pltpu.roll(fg, 1, 0)
                k_over_f = kg / fg
                if d == 0:
                    x4.append(jnp.where(hi4, qg, kg) * e4)
                    xq21.append(jnp.where(is_r0, 0.0, jnp.where(is_r1, qg, jnp.where(is_r2, qfg, qfg * fpv))))
                    xk21.append(jnp.where(is_r0, kg * fnx, jnp.where(is_r1, kg, jnp.where(is_r2, k_over_f, 0.0))))
                else:
                    x4.append(jnp.where(hi4, kg, qg) * e4)
                    xq21.append(jnp.where(is_r0, qfg * fnx, jnp.where(is_r1, qfg, jnp.where(is_r2, qg, 0.0))))
                    xk21.append(jnp.where(is_r0, 0.0, jnp.where(is_r1, k_over_f, jnp.where(is_r2, kg, kg * fpv))))
            accumulate(LV_4, _gram(jnp.concatenate(x4, axis=0), tsl.at[slot, LV_4]), 0)
            accumulate(LV_21, jnp.dot(
                jnp.concatenate(xq21, axis=0).astype(BF16),
                _staged_transpose(jnp.concatenate(xk21, axis=0).astype(BF16), tsl.at[slot, LV_21]),
                preferred_element_type=F32), 0)
            acc = jnp.concatenate(acc_rows, axis=0)

            vt = vts[c, hd]
            st_t = st[idx]
            b = ldb(0, L)
            qt = (ldq(0, L) * jnp.exp2(b)).astype(BF16)
            vb = proj[rows, COL_V + hd * HG_DK:COL_V + (hd + 1) * HG_DK].astype(BF16)
            o = jnp.dot(jnp.concatenate([acc.astype(BF16), qt], axis=1),
                        jnp.concatenate([vb, _staged_transpose(st_t.astype(BF16), tsl.at[slot, LV_STATE])], axis=0),
                        preferred_element_type=F32)
            btot = bline(L - 1 if d == 0 else 0)
            kt = (ldk(0, L) * jnp.exp2(btot - b)).astype(BF16)
            st[idx] = st_t * jnp.exp2(btot) + jnp.dot(vt, kt, preferred_element_type=F32)
            x1_ref[0, rows, d * HG_WIDTH + hd * HG_DK:d * HG_WIDTH + (hd + 1) * HG_DK] = o

    n_chunks = T // L

    def hg_both(c, carry):
        hg_chunk(c, 0)
        hg_chunk(n_chunks - 1 - c, 1)
        if fuse_scan:
            steps = S // n_chunks
            carry = scan_steps(c * steps, steps, carry)
        return carry

    lax.fori_loop(0, n_chunks, hg_prep, 0, unroll=n_chunks <= MAX_INLINE_TRIPS)
    scan_state = lax.fori_loop(0, n_chunks, hg_both, scan_init if fuse_scan else 0)

    if emit_state:
        for i in range(N_DIR * HG_HEADS):
            hs_ref[0, i] = st[i].T

    def hg_fin(i, carry):
        rows = pl.ds(pl.multiple_of(i * L, L), L)
        for hd in range(HG_HEADS):
            cs = slice(hd * HG_DK, (hd + 1) * HG_DK)
            o = x1_ref[0, rows, cs] + x1_ref[0, rows, HG_WIDTH + hd * HG_DK:HG_WIDTH + (hd + 1) * HG_DK]
            ms = jnp.mean(o * o, axis=-1, keepdims=True)
            y = o * lax.rsqrt(ms + EPS) * hgg_ref[:, cs]
            gz = proj[rows, COL_GATE + hd * HG_DK:COL_GATE + (hd + 1) * HG_DK]
            mixin[rows, cs] = (y * (gz * _sig(gz))).astype(BF16)
        return carry

    lax.fori_loop(0, n_chunks, hg_fin, 0, unroll=n_chunks <= MAX_INLINE_TRIPS)

    if not fuse_scan:
        lax.fori_loop(0, T // ROW_TILE, gate_body, 0)
        scan_state = lax.fori_loop(
            0, S // SCAN_UNROLL, lambda n, carry: scan_steps(n * SCAN_UNROLL, SCAN_UNROLL, carry), scan_init)
    h_end, d_end = scan_state[:2], scan_state[2:]

    carry_in = [[None] * n_grp for _ in range(N_DIR)]
    for d in range(N_DIR):
        for g in range(n_grp):
            if has_state:
                h0 = ls0_ref[0, d:d + 1, g * LRU_GROUP:(g + 1) * LRU_GROUP]
            else:
                h0 = jnp.zeros((1, LRU_GROUP), F32)
            dd, hh = scan8(d_end[d][g], h_end[d][g], d)
            seg_end = hh + dd * h0
            if d == 0:
                carry_in[d][g] = jnp.where(rowi2 >= 1, pltpu.roll(seg_end, 1, 0), h0)
                last = seg_end[LRU_SEGS - 1:LRU_SEGS, :]
            else:
                carry_in[d][g] = jnp.where(rowi2 < LRU_SEGS - 1, pltpu.roll(seg_end, LRU_SEGS - 1, 0), h0)
                last = seg_end[0:1, :]
            if emit_state:
                ls_ref[0, d:d + 1, g * LRU_GROUP:(g + 1) * LRU_GROUP] = last

    def fix_body(i, carry):
        rows = pl.ds(pl.multiple_of(i * FIX_TILE, FIX_TILE), FIX_TILE)
        for g in range(n_grp):
            tot = None
            for d in range(N_DIR):
                cin = jnp.concatenate([carry_in[d][g]] * (FIX_TILE // LRU_SEGS), axis=0)
                h = seg_out(d, rows, g, 0)[...] + seg_out(d, rows, g, 1)[...] * cin
                tot = h if tot is None else tot + h
            for l in range(slabs):
                xci[g * slabs + l, rows, :] = tot[:, l * 128:(l + 1) * 128]
        return carry

    lax.fori_loop(0, T // FIX_TILE, fix_body, 0)

    for j in range(LRU_SEGS):
        def lru_fin(ti, carry, j=j):
            i0 = pl.multiple_of(ti * CONV_TILE, CONV_TILE)
            rows = pl.ds(pl.multiple_of(j * S + i0, CONV_TILE), CONV_TILE)
            hsum = jnp.concatenate([xci[l, irows(i0, j, CONV_TILE), :] for l in range(LRU_WIDTH // 128)], axis=1)
            lg = proj[rows, PROJ_LG:PROJ_LG + LRU_WIDTH]
            gl = lg * (0.5 * (1.0 + jnp.tanh(0.7978845608028654 * (lg + 0.044715 * (lg * lg * lg)))))
            mixin[rows, HG_WIDTH:HG_WIDTH + LRU_WIDTH] = (hsum * gl).astype(BF16)
            return carry

        lax.fori_loop(0, S // CONV_TILE, lru_fin, 0, unroll=S // CONV_TILE <= MAX_INLINE_TRIPS)

    def out_body(i, carry):
        rows = pl.ds(pl.multiple_of(i * ROW_TILE, ROW_TILE), ROW_TILE)
        mix = jnp.dot(mixin[rows, :], wout_ref[...], preferred_element_type=F32)
        x1_ref[0, rows, :] = load_x(i) + g1 * mix
        return carry

    lax.fori_loop(0, T // ROW_TILE, out_body, 0)


def _const_spec(shape):
    nd = len(shape)
    return pl.BlockSpec(shape, lambda b, _n=nd: (0,) * _n, pipeline_mode=pl.Buffered(1))


def _nbytes(shape, dtype):
    return int(np.prod(shape)) * jnp.dtype(dtype).itemsize


def _mixer(x, m3, m_off, m_step, pos, consts, states, emit_state):
    B, T, _ = x.shape
    has_pos = pos is not None
    has_state = states is not None
    scratch_shapes = [
        ((T, PROJ_COLS), F32),
        ((T + 16, LRU_WIDTH), F32),
        ((T, D_MODEL), BF16),
        ((8, HG_DK, HG_DK), F32),
        ((T, N_DIR * HG_HEADS * HG_DK), F32),
        ((LRU_WIDTH // 128, T, 128), F32),
        ((T // CHUNK, HG_HEADS, HG_DK, CHUNK), BF16),
        ((3 * N_DIR * (LRU_WIDTH // LRU_GROUP), LRU_SEGS, LRU_GROUP), F32),
        ((STAGE_SLOTS, LV_STATE + 1, HG_DK, CHUNK), BF16),
        ((LRU_WIDTH // LRU_GROUP, LRU_GROUP, 2 * N_DIR * LRU_GROUP), BF16),
    ]
    resident = (sum(_nbytes(s, d) for s, d in scratch_shapes)
                + sum(_nbytes(c.shape, c.dtype) for c in consts)
                + (sum(_nbytes(p.shape, p.dtype) for p in pos) if has_pos else 0))
    io_block = _nbytes((T, D_MODEL), F32)
    budget = VMEM_LIMIT - VMEM_HEADROOM
    lrs_shape = ((T, 2 * LRS_SEG_COL), F32)
    fuse_scan = resident + _nbytes(*lrs_shape) + 4 * io_block <= budget
    if fuse_scan:
        scratch_shapes.append(lrs_shape)
        resident += _nbytes(*lrs_shape)
    in_bufs = 2 if resident + 3 * io_block <= budget else 1
    out_bufs = 2 if resident + 4 * io_block <= budget else 1
    io_mode = pl.Buffered(out_bufs)
    in_specs = [
        pl.BlockSpec((1, T, D_MODEL), lambda b: (b, 0, 0), pipeline_mode=pl.Buffered(in_bufs)),
        pl.BlockSpec((1, 1, 6 * D_MODEL), lambda b: (m_off + m_step * b, 0, 0)),
    ]
    args = [x, m3]
    if has_pos:
        in_specs += [_const_spec(p.shape) for p in pos]
        args += list(pos)
    in_specs += [_const_spec(c.shape) for c in consts]
    args += list(consts)
    if has_state:
        hs0, ls0 = states
        in_specs += [
            pl.BlockSpec((1, 8, HG_DK, HG_DK), lambda b: (b, 0, 0, 0)),
            pl.BlockSpec((1, N_DIR, LRU_WIDTH), lambda b: (b, 0, 0)),
        ]
        args += [hs0, ls0]
    out_shape = [jax.ShapeDtypeStruct((B, T, D_MODEL), F32)]
    out_specs = [pl.BlockSpec((1, T, D_MODEL), lambda b: (b, 0, 0), pipeline_mode=io_mode)]
    if emit_state:
        out_shape += [jax.ShapeDtypeStruct((B, 8, HG_DK, HG_DK), F32),
                      jax.ShapeDtypeStruct((B, N_DIR, LRU_WIDTH), F32)]
        out_specs += [pl.BlockSpec((1, 8, HG_DK, HG_DK), lambda b: (b, 0, 0, 0)),
                      pl.BlockSpec((1, N_DIR, LRU_WIDTH), lambda b: (b, 0, 0))]
    scratch = [pltpu.VMEM(s, d) for s, d in scratch_shapes]
    return pl.pallas_call(
        functools.partial(_mixer_kernel, T=T, has_pos=has_pos, has_state=has_state, emit_state=emit_state,
                          fuse_scan=fuse_scan),
        grid=(B,),
        in_specs=in_specs,
        out_specs=out_specs,
        out_shape=out_shape,
        scratch_shapes=scratch,
        compiler_params=pltpu.CompilerParams(
            dimension_semantics=("arbitrary",), vmem_limit_bytes=VMEM_LIMIT),
        name=f"mixer_t{T}",
    )(*args)


def _ffn_kernel(x_ref, m_ref, n2g_ref, fg_ref, w1_ref, w2_ref, y_ref):
    mrow = m_ref[0]
    sh2 = mrow[:, 3 * D_MODEL:4 * D_MODEL]
    sc2 = mrow[:, 4 * D_MODEL:5 * D_MODEL]
    g2 = mrow[:, 5 * D_MODEL:6 * D_MODEL]
    x = x_ref[...]
    ms = jnp.mean(x * x, axis=-1, keepdims=True)
    hb = (x * lax.rsqrt(ms + EPS) * (n2g_ref[...] * (1.0 + sc2)) + sh2).astype(BF16)
    ff = jnp.zeros(x.shape, F32)
    for c in range(D_FF // FF_CHUNK):
        a = jnp.dot(hb, w1_ref[:, c * FF_CHUNK:(c + 1) * FF_CHUNK], preferred_element_type=F32)
        a = jnp.maximum(a, 0.0)
        ff = ff + jnp.dot((a * a).astype(BF16), w2_ref[c * FF_CHUNK:(c + 1) * FF_CHUNK, :],
                          preferred_element_type=F32)
    x2 = x + g2 * ff
    ms2 = jnp.mean(x2 * x2, axis=-1, keepdims=True)
    y_ref[...] = x2 * lax.rsqrt(ms2 + EPS) * fg_ref[...]


def _ffn(x1, m3, m_off, tiles_per_cond, n2g, fgain, w1, w2):
    n = x1.shape[0]

    def m_index(i):
        if tiles_per_cond is None:
            return (m_off, 0, 0)
        return (m_off + i // tiles_per_cond, 0, 0)

    return pl.pallas_call(
        _ffn_kernel,
        grid=(n // FFN_TILE,),
        in_specs=[
            pl.BlockSpec((FFN_TILE, D_MODEL), lambda i: (i, 0)),
            pl.BlockSpec((1, 1, 6 * D_MODEL), m_index),
            _const_spec(n2g.shape),
            _const_spec(fgain.shape),
            _const_spec(w1.shape),
            _const_spec(w2.shape),
        ],
        out_specs=pl.BlockSpec((FFN_TILE, D_MODEL), lambda i: (i, 0)),
        out_shape=jax.ShapeDtypeStruct((n, D_MODEL), F32),
        compiler_params=pltpu.CompilerParams(
            dimension_semantics=("arbitrary",), vmem_limit_bytes=VMEM_LIMIT),
        name="ffn",
    )(x1, m3, n2g, fgain, w1, w2)


def _grid_pos_tables(n_tok):
    quarter = D_MODEL // 4
    omega = (1.0 / (np.float32(POS_BASE) ** (np.arange(quarter, dtype=np.float32) / np.float32(quarter)))
             ).astype(np.float32)

    def emb(n):
        ang = np.arange(n).reshape(-1, 1).astype(np.float32) * omega
        return np.concatenate([np.sin(ang), np.cos(ang)], axis=-1)

    rows = np.repeat(emb(n_tok // GRID_W)[:, None, :], 8, axis=1)
    return jnp.asarray(rows, dtype=F32), jnp.asarray(emb(GRID_W), dtype=F32)


def kernel(x_prompt, x_sample, c, state_hgrn, state_rglru, c_ctx, w_ada, b_ada, norm1_gain, norm2_gain,
           w_in, hg_lb_logits, hg_norm_gain, conv_w, conv_b, lru_wa, lru_ba, lru_wx, lru_bx, lru_lambda,
           w_out, w_ff1, w_ff2, final_gain):
    bp, tp, _ = x_prompt.shape
    bs_, ts, _ = x_sample.shape

    cond8 = jnp.concatenate([c_ctx[None, :], c, jnp.zeros((8 - 1 - bs_, D_MODEL), F32)], axis=0)
    m3 = _modulation(cond8, w_ada[0], b_ada).reshape(8, 1, 6 * D_MODEL)

    masks_np, tri_np = _level_tables()
    grp = [slice(g * LRU_GROUP, (g + 1) * LRU_GROUP) for g in range(LRU_WIDTH // LRU_GROUP)]
    bias_g = jnp.stack([jnp.concatenate([lru_ba[0, 0, s], lru_bx[0, 0, s], lru_ba[0, 1, s], lru_bx[0, 1, s]])
                        for s in grp])
    consts = [
        norm1_gain,
        w_in[0].astype(BF16),
        hg_lb_logits.reshape(2, N_DIR * HG_HEADS, HG_DK),
        hg_norm_gain[0].reshape(1, HG_WIDTH),
        jnp.asarray(masks_np),
        jnp.asarray(tri_np, dtype=BF16),
        conv_w[0],
        conv_b,
        jnp.stack([lru_wa[0], lru_wx[0]], axis=1).reshape(
            N_DIR * 2 * LRU_BLOCKS, LRU_BLOCK, LRU_BLOCK),
        bias_g,
        lru_lambda[0],
        w_out[0].astype(BF16),
    ]
    w1 = w_ff1[0].astype(BF16)
    w2 = w_ff2[0].astype(BF16)
    fgain = final_gain.reshape(1, D_MODEL)

    x1p, hs, ls = _mixer(x_prompt, m3, 0, 0, None, consts, None, True)
    y_prompt = _ffn(x1p.reshape(bp * tp, D_MODEL), m3, 0, None, norm2_gain, fgain, w1, w2)

    x1s = _mixer(x_sample, m3, 1, 1, _grid_pos_tables(ts), consts,
                 (state_hgrn.reshape(bs_, N_DIR * HG_HEADS, HG_DK, HG_DK),
                  state_rglru.reshape(bs_, N_DIR, LRU_WIDTH)), False)[0]
    y_sample = _ffn(x1s.reshape(bs_ * ts, D_MODEL), m3, 1, ts // FFN_TILE, norm2_gain, fgain, w1, w2)

    return (y_prompt.reshape(bp, tp, D_MODEL),
            y_sample.reshape(bs_, ts, D_MODEL),
            hs.reshape(bp, 1, N_DIR, HG_HEADS, HG_DK, HG_DK),
            ls.reshape(bp, 1, N_DIR, LRU_WIDTH))
```

```python
import functools

import numpy as np
import jax
import jax.numpy as jnp
from jax import lax
from jax.experimental import pallas as pl
from jax.experimental.pallas import tpu as pltpu

F32 = jnp.float32
BF16 = jnp.bfloat16

D_MODEL = 1024
N_DIR = 2
HG_HEADS = 4
HG_DK = 128
HG_WIDTH = 512
LRU_WIDTH = 512
LRU_BLOCKS = 8
LRU_BLOCK = 64
LRU_C = 8.0
D_FF = 4096
IN_COLS = 4096
EPS = 1e-6
LOG2E = 1.4426950408889634
GRID_W = 64
POS_BASE = 10000.0

COL_Q = 0
COL_F = 1024
COL_V = 2048
COL_GATE = 2560
COL_LX = 3072
COL_LG = 3584
PROJ_LG = COL_LX
PROJ_COLS = IN_COLS - LRU_WIDTH

CHUNK = 128
LEVELS = (64, 32, 16, 8, 4)
LV_4 = LEVELS.index(4)
LV_21 = len(LEVELS)
LV_DIAG = LV_21 + 1
LV_STATE = LV_DIAG + 1
N_MASKS = LV_DIAG + 1
ROW_TILE = 256
LRU_SEGS = 8
CONV_TILE = 32
SCAN_UNROLL = 4
FIX_TILE = 64
STAGE_SLOTS = 1
MAX_INLINE_TRIPS = 2
LRU_GROUP = 256
LRS_SEG_COL = 2 * N_DIR * LRU_WIDTH
FFN_TILE = 512
FF_CHUNK = 1024
MOD_TILE = 2048
VMEM_LIMIT = 58 * 1024 * 1024
VMEM_HEADROOM = 6 * 1024 * 1024


def _sig(x):
    return jax.nn.sigmoid(x)


def _nt_dot(a, b):
    return lax.dot_general(a, b, (((1,), (1,)), ((), ())), preferred_element_type=F32)


def _staged_transpose(xb, slot_ref):
    slot_ref[...] = xb.T
    return slot_ref[...]


def _gram(x, slot_ref):
    xb = x.astype(BF16)
    return jnp.dot(xb, _staged_transpose(xb, slot_ref), preferred_element_type=F32)


def _level_tables():
    t = np.arange(CHUNK)[:, None]
    s = np.arange(CHUNK)[None, :]
    masks = np.zeros((N_DIR, N_MASKS, CHUNK, CHUNK), np.float32)
    for li, h in enumerate(LEVELS):
        same = (t // (2 * h)) == (s // (2 * h))
        t_hi = (t // h) % 2 == 1
        s_hi = (s // h) % 2 == 1
        masks[0, li] = same & t_hi & ~s_hi
        masks[1, li] = same & ~t_hi & s_hi
    same4 = (t // 4) == (s // 4)
    masks[0, LV_21] = same4 & (s < t)
    masks[1, LV_21] = same4 & (s > t)
    masks[:, LV_DIAG] = (t == s)
    tri = np.stack([(s <= t), (s >= t)]).astype(np.float32)
    return masks, tri


def _mod_kernel(c_ref, w_ref, b_ref, o_ref):
    c = c_ref[...]
    a = (c * _sig(c)).astype(BF16)
    o_ref[...] = jnp.dot(a, w_ref[...].astype(BF16), preferred_element_type=F32) + b_ref[...]


def _modulation(cond8, w_ada, b_ada):
    n = w_ada.shape[1]
    return pl.pallas_call(
        _mod_kernel,
        grid=(n // MOD_TILE,),
        in_specs=[
            pl.BlockSpec((8, D_MODEL), lambda j: (0, 0)),
            pl.BlockSpec((D_MODEL, MOD_TILE), lambda j: (0, j)),
            pl.BlockSpec((1, MOD_TILE), lambda j: (0, j)),
        ],
        out_specs=pl.BlockSpec((8, MOD_TILE), lambda j: (0, j)),
        out_shape=jax.ShapeDtypeStruct((8, n), F32),
        compiler_params=pltpu.CompilerParams(dimension_semantics=("arbitrary",)),
        name="adaln_modulation",
    )(cond8, w_ada, b_ada)


def _mixer_kernel(*refs, T, has_pos, has_state, emit_state, fuse_scan, fuse_ffn):
    it = iter(refs)
    x_ref = next(it)
    m_ref = next(it)
    if has_pos:
        posr_ref = next(it)
        posc_ref = next(it)
    n1g_ref = next(it)
    win_ref = next(it)
    lbl_ref = next(it)
    hgg_ref = next(it)
    masks_ref = next(it)
    tri_ref = next(it)
    convw_ref = next(it)
    convb_ref = next(it)
    gw_ref = next(it)
    bg_ref = next(it)
    lam_ref = next(it)
    wout_ref = next(it)
    if fuse_ffn:
        n2g_ref = next(it)
        fg_ref = next(it)
        w1_ref = next(it)
        w2_ref = next(it)
    if has_state:
        hs0_ref = next(it)
        ls0_ref = next(it)
    y_ref = next(it)
    if emit_state:
        hs_ref = next(it)
        ls_ref = next(it)
    x1_ref = next(it) if fuse_ffn else y_ref
    proj = next(it)
    lxp = next(it)
    mixin = next(it)
    st = next(it)
    ksc = next(it)
    xci = next(it)
    vts = next(it)
    rowc = next(it)
    tsl = next(it)
    wg = next(it)
    if fuse_scan:
        lrs = next(it)

    L = CHUNK

    @pl.when(pl.program_id(0) == 0)
    def _build_gate_weights():
        per_group = LRU_GROUP // LRU_BLOCK
        wg[...] = jnp.zeros(wg.shape, BF16)
        for g in range(LRU_WIDTH // LRU_GROUP):
            for p in range(2 * N_DIR):
                for n in range(per_group):
                    r = n * LRU_BLOCK
                    col = p * LRU_GROUP + r
                    wg[g, r:r + LRU_BLOCK, col:col + LRU_BLOCK] = gw_ref[
                        p * LRU_BLOCKS + g * per_group + n].astype(BF16)

    mrow = m_ref[0]
    sh1 = mrow[:, 0:D_MODEL]
    sc1 = mrow[:, D_MODEL:2 * D_MODEL]
    g1 = mrow[:, 2 * D_MODEL:3 * D_MODEL]
    gain1 = n1g_ref[...] * (1.0 + sc1)

    zrows = jnp.zeros((8, LRU_WIDTH), F32)
    lxp[0:8, :] = zrows
    lxp[T + 8:T + 16, :] = zrows

    def load_x(i):
        xt = x_ref[0, pl.ds(pl.multiple_of(i * ROW_TILE, ROW_TILE), ROW_TILE), :]
        if has_pos:
            per_tile = ROW_TILE // GRID_W
            tiles = []
            for s in range(per_tile):
                row_emb = jnp.concatenate([posr_ref[i * per_tile + s]] * (GRID_W // 8), axis=0)
                tiles.append(jnp.concatenate([row_emb, posc_ref[...]], axis=1))
            xt = xt + jnp.concatenate(tiles, axis=0)
        return xt

    def proj_body(i, carry):
        r0 = pl.multiple_of(i * ROW_TILE, ROW_TILE)
        xt = load_x(i)
        ms = jnp.mean(xt * xt, axis=-1, keepdims=True)
        hb = (xt * lax.rsqrt(ms + EPS) * gain1 + sh1).astype(BF16)
        n_col = IN_COLS // 512
        order = [COL_LX // 512] + [c for c in range(n_col) if c * 512 < COL_GATE] + [
            c for c in range(n_col) if c * 512 >= COL_GATE and c * 512 != COL_LX]
        for c in order:
            res = jnp.dot(hb, win_ref[:, c * 512:(c + 1) * 512], preferred_element_type=F32)
            if c * 512 == COL_LX:
                lxp[pl.ds(pl.multiple_of(r0 + 8, 8), ROW_TILE), :] = res
            elif c * 512 == COL_LG:
                proj[pl.ds(r0, ROW_TILE), PROJ_LG:PROJ_LG + LRU_WIDTH] = res
            else:
                proj[pl.ds(r0, ROW_TILE), c * 512:(c + 1) * 512] = res
        return carry

    l0 = lbl_ref[0]
    l1 = lbl_ref[1]
    lmx = jnp.maximum(l0, l1)
    e0 = jnp.exp(l0 - lmx)
    e1 = jnp.exp(l1 - lmx)
    lb_all = e0 / (e0 + e1)

    def hg_prep(c, carry):
        rows = pl.ds(pl.multiple_of(c * L, L), L)
        for hd in range(HG_HEADS):
            vts[c, hd] = proj[rows, COL_V + hd * HG_DK:COL_V + (hd + 1) * HG_DK].T.astype(BF16)
        for d in range(N_DIR):
            parts = []
            for hd in range(HG_HEADS):
                idx = d * HG_HEADS + hd
                cq = COL_Q + idx * HG_DK
                cf = COL_F + idx * HG_DK
                hq = proj[rows, cq:cq + HG_DK]
                fz = proj[rows, cf:cf + HG_DK]
                proj[rows, cq:cq + HG_DK] = hq * _sig(hq)
                sg = _sig(fz)
                lb = lb_all[idx:idx + 1, :]
                oml = 1.0 - lb
                ksc[rows, idx * HG_DK:(idx + 1) * HG_DK] = oml * (1.0 - sg)
                logf = jnp.log(lb + oml * sg)
                p1 = logf.astype(BF16)
                p2 = (logf - p1.astype(F32)).astype(BF16)
                parts += [p1, p2]
            bb = jnp.dot(tri_ref[d], jnp.concatenate(parts, axis=1), preferred_element_type=F32)
            for hd in range(HG_HEADS):
                cf = COL_F + (d * HG_HEADS + hd) * HG_DK
                o2 = 2 * hd * HG_DK
                proj[rows, cf:cf + HG_DK] = (bb[:, o2:o2 + HG_DK] + bb[:, o2 + HG_DK:o2 + 2 * HG_DK]) * LOG2E
        return carry

    lax.fori_loop(0, T // ROW_TILE, proj_body, 0)

    S = T // LRU_SEGS
    cw = convw_ref[...]
    cb = convb_ref[...]

    def irows(i0, j, n):
        return pl.ds(pl.multiple_of(LRU_SEGS * i0, 8) + j, n, stride=LRU_SEGS)

    for j in range(LRU_SEGS):
        def conv_body(ti, carry, j=j):
            i0 = pl.multiple_of(ti * CONV_TILE, CONV_TILE)
            win = lxp[pl.ds(pl.multiple_of(j * S + i0, 8), CONV_TILE + 16), :]
            xc = cb
            for tap in range(4):
                xc = xc + win[6 + tap:6 + tap + CONV_TILE] * cw[tap:tap + 1]
            for l in range(LRU_WIDTH // 128):
                xci[l, irows(i0, j, CONV_TILE), :] = xc[:, l * 128:(l + 1) * 128]
            return carry

        lax.fori_loop(0, S // CONV_TILE, conv_body, 0, unroll=S // CONV_TILE <= MAX_INLINE_TRIPS)

    n_grp = LRU_WIDTH // LRU_GROUP
    slabs = LRU_GROUP // 128
    gates = lrs if fuse_scan else proj

    def xc_tile(rows, g):
        return jnp.concatenate([xci[g * slabs + l, rows, :] for l in range(slabs)], axis=1)

    def gate_body(i, carry):
        rows = pl.ds(pl.multiple_of(i * ROW_TILE, ROW_TILE), ROW_TILE)
        for g in range(n_grp):
            gates[rows, g * 1024:(g + 1) * 1024] = jnp.dot(
                xc_tile(rows, g).astype(BF16), wg[g], preferred_element_type=F32)
        return carry

    lam = lam_ref[...]
    nl = -lam
    c8 = -LRU_C * (jnp.maximum(nl, 0.0) + jnp.log1p(jnp.exp(-jnp.abs(nl))))
    rowi2 = lax.broadcasted_iota(jnp.int32, (8, LRU_GROUP), 0)

    for d in range(N_DIR):
        for g in range(n_grp):
            k3 = 3 * (d * n_grp + g)
            for r, row in enumerate((bg_ref[g:g + 1, d * 512:d * 512 + LRU_GROUP],
                                     bg_ref[g:g + 1, d * 512 + LRU_GROUP:d * 512 + 2 * LRU_GROUP],
                                     c8[d:d + 1, g * LRU_GROUP:(g + 1) * LRU_GROUP])):
                rowc[k3 + r] = jnp.broadcast_to(row, (LRU_SEGS, LRU_GROUP))

    def seg_out(d, rows, g, part):
        if fuse_scan:
            col = LRS_SEG_COL + ((d * n_grp + g) * 2 + part) * LRU_GROUP
            return lrs.at[rows, col:col + LRU_GROUP]
        cols = slice((g * 2 + part) * LRU_GROUP, (g * 2 + part + 1) * LRU_GROUP)
        return (ksc.at[rows, cols] if d == 0 else x1_ref.at[0, rows, cols])

    def lru_inputs(rows8, g, d):
        base = g * 1024 + d * 512
        k3 = 3 * (d * n_grp + g)
        ga = gates[rows8, base:base + LRU_GROUP] + rowc[k3]
        gx = gates[rows8, base + LRU_GROUP:base + 2 * LRU_GROUP] + rowc[k3 + 1]
        xc8 = xc_tile(rows8, g)
        log_a = rowc[k3 + 2] * _sig(ga)
        a = jnp.exp(log_a)
        z = jnp.tanh(-log_a) * (1.0 + a * a)
        mult = jnp.where(z > 0.0, z * lax.rsqrt(z), 0.0)
        return a, mult * (_sig(gx) * xc8)

    def scan8(a, u, d):
        for sft in (1, 2, 4):
            if d == 0:
                keep = rowi2 >= sft
                amt = sft
            else:
                keep = rowi2 < 8 - sft
                amt = 8 - sft
            ash = jnp.where(keep, pltpu.roll(a, amt, 0), 1.0)
            ush = jnp.where(keep, pltpu.roll(u, amt, 0), 0.0)
            u = a * ush + u
            a = a * ash
        return a, u

    def scan_steps(first, count, carry):
        hs, ds = [list(c) for c in carry[:2]], [list(c) for c in carry[2:]]
        for u in range(count):
            i_f = first + u
            for d, i in ((0, i_f), (1, S - 1 - i_f)):
                rows8 = pl.ds(pl.multiple_of(LRU_SEGS * i, 8), 8)
                for g in range(n_grp):
                    a, uu = lru_inputs(rows8, g, d)
                    hs[d][g] = a * hs[d][g] + uu
                    ds[d][g] = a * ds[d][g]
                    seg_out(d, rows8, g, 0)[...] = hs[d][g]
                    seg_out(d, rows8, g, 1)[...] = ds[d][g]
        return tuple(tuple(c) for c in hs + ds)

    zero8 = jnp.zeros((LRU_SEGS, LRU_GROUP), F32)
    one8 = jnp.ones((LRU_SEGS, LRU_GROUP), F32)
    scan_init = ((zero8,) * n_grp, (zero8,) * n_grp, (one8,) * n_grp, (one8,) * n_grp)
    if fuse_scan:
        lax.fori_loop(0, T // ROW_TILE, gate_body, 0)

    rowi = lax.broadcasted_iota(jnp.int32, (8, HG_DK), 0)
    r4 = rowi & 3
    is_r0 = r4 == 0
    is_r1 = r4 == 1
    is_r2 = r4 == 2
    hi4 = rowi >= 4

    for i in range(N_DIR * HG_HEADS):
        if has_state:
            st[i] = hs0_ref[0, i].T
        else:
            st[i] = jnp.zeros((HG_DK, HG_DK), F32)

    def hg_chunk(c, d):
        r0 = pl.multiple_of(c * L, L)
        rows = pl.ds(r0, L)
        for hd in range(HG_HEADS):
            idx = d * HG_HEADS + hd
            cq = COL_Q + idx * HG_DK
            cf = COL_F + idx * HG_DK
            ck = idx * HG_DK
            slot = idx % STAGE_SLOTS

            def ldq(lo, n, cq=cq):
                return proj[pl.ds(pl.multiple_of(r0 + lo, 8), n), cq:cq + HG_DK]

            def ldb(lo, n, cf=cf):
                return proj[pl.ds(pl.multiple_of(r0 + lo, 8), n), cf:cf + HG_DK]

            def ldk(lo, n, ck=ck):
                return ksc[pl.ds(pl.multiple_of(r0 + lo, 8), n), ck:ck + HG_DK]

            def bline(r, cf=cf):
                grp = proj[pl.ds(pl.multiple_of(r0 + 8 * (r // 8), 8), 8), cf:cf + HG_DK]
                return grp[r % 8:r % 8 + 1, :]

            def brow(r, n):
                return jnp.broadcast_to(bline(r), (n, HG_DK))

            acc_rows = [None] * (L // 8)

            def accumulate(li, p, row0):
                for i in range(p.shape[0] // 8):
                    g = row0 // 8 + i
                    term = masks_ref[d, li, 8 * g:8 * g + 8, :] * p[8 * i:8 * i + 8]
                    acc_rows[g] = term if acc_rows[g] is None else acc_rows[g] + term

            accumulate(LV_DIAG, jnp.dot(
                ldq(0, L).astype(BF16), _staged_transpose(ldk(0, L).astype(BF16), tsl.at[slot, LV_DIAG]),
                preferred_element_type=F32), 0)
            for li, h in enumerate(LEVELS[:LV_4]):
                pieces, q_pieces, q_starts = [], [], []
                for j in range(L // (2 * h)):
                    lo = j * 2 * h
                    mid = lo + h
                    if d == 0:
                        bm = brow(mid - 1, h)
                        kp = ldk(lo, h) * jnp.exp2(bm - ldb(lo, h))
                        qp = ldq(mid, h) * jnp.exp2(ldb(mid, h) - bm)
                        pieces += [kp, qp]
                        q_starts.append(mid)
                    else:
                        bm = brow(mid, h)
                        qp = ldq(lo, h) * jnp.exp2(ldb(lo, h) - bm)
                        kp = ldk(mid, h) * jnp.exp2(bm - ldb(mid, h))
                        pieces += [qp, kp]
                        q_starts.append(lo)
                    q_pieces.append(qp)
                xt = _staged_transpose(jnp.concatenate(pieces, axis=0).astype(BF16), tsl.at[slot,li])
                p = jnp.dot(jnp.concatenate(q_pieces, axis=0).astype(BF16), xt, preferred_element_type=F32)
                for j, row0 in enumerate(q_starts):
                    accumulate(li, p[j * h:(j + 1) * h], row0)
            x4, xq21, xk21 = [], [], []
            for g in range(L // 8):
                qg, kg, bg = ldq(8 * g, 8), ldk(8 * g, 8), ldb(8 * g, 8)
                fg = 1.0 - kg
                qfg = qg * fg
                mid = 8 * g + (3 if d == 0 else 4)
                e4 = jnp.exp2(-jnp.abs(bg - brow(mid, 8)))
                fnx = pltpu.roll(fg, 7, 0)
                fpv = pltpu.roll(fg, 1, 0)
                k_over_f = kg / fg
                if d == 0:
                    x4.append(jnp.where(hi4, qg, kg) * e4)
                    xq21.append(jnp.where(is_r0, 0.0, jnp.where(is_r1, qg, jnp.where(is_r2, qfg, qfg * fpv))))
                    xk21.append(jnp.where(is_r0, kg * fnx, jnp.where(is_r1, kg, jnp.where(is_r2, k_over_f, 0.0))))
                else:
                    x4.append(jnp.where(hi4, kg, qg) * e4)
                    xq21.append(jnp.where(is_r0, qfg * fnx, jnp.where(is_r1, qfg, jnp.where(is_r2, qg, 0.0))))
                    xk21.append(jnp.where(is_r0, 0.0, jnp.where(is_r1, k_over_f, jnp.where(is_r2, kg, kg * fpv))))
            accumulate(LV_4, _gram(jnp.concatenate(x4, axis=0), tsl.at[slot, LV_4]), 0)
            accumulate(LV_21, jnp.dot(
                jnp.concatenate(xq21, axis=0).astype(BF16),
                _staged_transpose(jnp.concatenate(xk21, axis=0).astype(BF16), tsl.at[slot, LV_21]),
                preferred_element_type=F32), 0)
            acc = jnp.concatenate(acc_rows, axis=0)

            vt = vts[c, hd]
            st_t = st[idx]
            b = ldb(0, L)
            qt = (ldq(0, L) * jnp.exp2(b)).astype(BF16)
            vb = proj[rows, COL_V + hd * HG_DK:COL_V + (hd + 1) * HG_DK].astype(BF16)
            o = jnp.dot(jnp.concatenate([acc.astype(BF16), qt], axis=1),
                        jnp.concatenate([vb, _staged_transpose(st_t.astype(BF16), tsl.at[slot, LV_STATE])], axis=0),
                        preferred_element_type=F32)
            btot = bline(L - 1 if d == 0 else 0)
            kt = (ldk(0, L) * jnp.exp2(btot - b)).astype(BF16)
            st[idx] = st_t * jnp.exp2(btot) + jnp.dot(vt, kt, preferred_element_type=F32)
            x1_ref[0, rows, d * HG_WIDTH + hd * HG_DK:d * HG_WIDTH + (hd + 1) * HG_DK] = o

    n_chunks = T // L

    def hg_both(c, carry):
        hg_chunk(c, 0)
        hg_chunk(n_chunks - 1 - c, 1)
        if fuse_scan:
            steps = S // n_chunks
            carry = scan_steps(c * steps, steps, carry)
        return carry

    lax.fori_loop(0, n_chunks, hg_prep, 0, unroll=n_chunks <= MAX_INLINE_TRIPS)
    scan_state = lax.fori_loop(0, n_chunks, hg_both, scan_init if fuse_scan else 0)

    if emit_state:
        for i in range(N_DIR * HG_HEADS):
            hs_ref[0, i] = st[i].T

    def hg_fin(i, carry):
        rows = pl.ds(pl.multiple_of(i * L, L), L)
        for hd in range(HG_HEADS):
            cs = slice(hd * HG_DK, (hd + 1) * HG_DK)
            o = x1_ref[0, rows, cs] + x1_ref[0, rows, HG_WIDTH + hd * HG_DK:HG_WIDTH + (hd + 1) * HG_DK]
            ms = jnp.mean(o * o, axis=-1, keepdims=True)
            y = o * lax.rsqrt(ms + EPS) * hgg_ref[:, cs]
            gz = proj[rows, COL_GATE + hd * HG_DK:COL_GATE + (hd + 1) * HG_DK]
            mixin[rows, cs] = (y * (gz * _sig(gz))).astype(BF16)
        return carry

    lax.fori_loop(0, n_chunks, hg_fin, 0, unroll=n_chunks <= MAX_INLINE_TRIPS)

    if not fuse_scan:
        lax.fori_loop(0, T // ROW_TILE, gate_body, 0)
        scan_state = lax.fori_loop(
            0, S // SCAN_UNROLL, lambda n, carry: scan_steps(n * SCAN_UNROLL, SCAN_UNROLL, carry), scan_init)
    h_end, d_end = scan_state[:2], scan_state[2:]

    carry_in = [[None] * n_grp for _ in range(N_DIR)]
    for d in range(N_DIR):
        for g in range(n_grp):
            if has_state:
                h0 = ls0_ref[0, d:d + 1, g * LRU_GROUP:(g + 1) * LRU_GROUP]
            else:
                h0 = jnp.zeros((1, LRU_GROUP), F32)
            dd, hh = scan8(d_end[d][g], h_end[d][g], d)
            seg_end = hh + dd * h0
            if d == 0:
                carry_in[d][g] = jnp.where(rowi2 >= 1, pltpu.roll(seg_end, 1, 0), h0)
                last = seg_end[LRU_SEGS - 1:LRU_SEGS, :]
            else:
                carry_in[d][g] = jnp.where(rowi2 < LRU_SEGS - 1, pltpu.roll(seg_end, LRU_SEGS - 1, 0), h0)
                last = seg_end[0:1, :]
            if emit_state:
                ls_ref[0, d:d + 1, g * LRU_GROUP:(g + 1) * LRU_GROUP] = last

    def fix_body(i, carry):
        rows = pl.ds(pl.multiple_of(i * FIX_TILE, FIX_TILE), FIX_TILE)
        for g in range(n_grp):
            tot = None
            for d in range(N_DIR):
                cin = jnp.concatenate([carry_in[d][g]] * (FIX_TILE // LRU_SEGS), axis=0)
                h = seg_out(d, rows, g, 0)[...] + seg_out(d, rows, g, 1)[...] * cin
                tot = h if tot is None else tot + h
            for l in range(slabs):
                xci[g * slabs + l, rows, :] = tot[:, l * 128:(l + 1) * 128]
        return carry

    lax.fori_loop(0, T // FIX_TILE, fix_body, 0)

    for j in range(LRU_SEGS):
        def lru_fin(ti, carry, j=j):
            i0 = pl.multiple_of(ti * CONV_TILE, CONV_TILE)
            rows = pl.ds(pl.multiple_of(j * S + i0, CONV_TILE), CONV_TILE)
            hsum = jnp.concatenate([xci[l, irows(i0, j, CONV_TILE), :] for l in range(LRU_WIDTH // 128)], axis=1)
            lg = proj[rows, PROJ_LG:PROJ_LG + LRU_WIDTH]
            gl = lg * (0.5 * (1.0 + jnp.tanh(0.7978845608028654 * (lg + 0.044715 * (lg * lg * lg)))))
            mixin[rows, HG_WIDTH:HG_WIDTH + LRU_WIDTH] = (hsum * gl).astype(BF16)
            return carry

        lax.fori_loop(0, S // CONV_TILE, lru_fin, 0, unroll=S // CONV_TILE <= MAX_INLINE_TRIPS)

    def out_body(i, carry):
        rows = pl.ds(pl.multiple_of(i * ROW_TILE, ROW_TILE), ROW_TILE)
        mix = jnp.dot(mixin[rows, :], wout_ref[...], preferred_element_type=F32)
        x1_ref[0, rows, :] = load_x(i) + g1 * mix
        return carry

    lax.fori_loop(0, T // ROW_TILE, out_body, 0)

    if fuse_ffn:
        def ffn_body(i, carry):
            rows = pl.ds(pl.multiple_of(i * ROW_TILE, ROW_TILE), ROW_TILE)
            y_ref[0, rows, :] = _ffn_rows(x1_ref[0, rows, :], mrow, n2g_ref[...], fg_ref[...], w1_ref, w2_ref)
            return carry

        lax.fori_loop(0, T // ROW_TILE, ffn_body, 0)


def _const_spec(shape):
    nd = len(shape)
    return pl.BlockSpec(shape, lambda b, _n=nd: (0,) * _n, pipeline_mode=pl.Buffered(1))


def _nbytes(shape, dtype):
    return int(np.prod(shape)) * jnp.dtype(dtype).itemsize


def _mixer(x, m3, m_off, m_step, pos, consts, ffn_consts, states, emit_state):
    B, T, _ = x.shape
    has_pos = pos is not None
    has_state = states is not None
    scratch_shapes = [
        ((T, PROJ_COLS), F32),
        ((T + 16, LRU_WIDTH), F32),
        ((T, D_MODEL), BF16),
        ((8, HG_DK, HG_DK), F32),
        ((T, N_DIR * HG_HEADS * HG_DK), F32),
        ((LRU_WIDTH // 128, T, 128), F32),
        ((T // CHUNK, HG_HEADS, HG_DK, CHUNK), BF16),
        ((3 * N_DIR * (LRU_WIDTH // LRU_GROUP), LRU_SEGS, LRU_GROUP), F32),
        ((STAGE_SLOTS, LV_STATE + 1, HG_DK, CHUNK), BF16),
        ((LRU_WIDTH // LRU_GROUP, LRU_GROUP, 2 * N_DIR * LRU_GROUP), BF16),
    ]
    resident = (sum(_nbytes(s, d) for s, d in scratch_shapes)
                + sum(_nbytes(c.shape, c.dtype) for c in consts)
                + (sum(_nbytes(p.shape, p.dtype) for p in pos) if has_pos else 0))
    io_block = _nbytes((T, D_MODEL), F32)
    budget = VMEM_LIMIT - VMEM_HEADROOM
    lrs_shape = ((T, 2 * LRS_SEG_COL), F32)
    fuse_scan = resident + _nbytes(*lrs_shape) + 4 * io_block <= budget
    if fuse_scan:
        scratch_shapes.append(lrs_shape)
        resident += _nbytes(*lrs_shape)
    ffn_bytes = sum(_nbytes(c.shape, c.dtype) for c in ffn_consts) + io_block
    fuse_ffn = resident + ffn_bytes + 4 * io_block <= budget
    if fuse_ffn:
        scratch_shapes.insert(0, ((1, T, D_MODEL), F32))
        resident += ffn_bytes
    in_bufs = 2 if resident + 3 * io_block <= budget else 1
    out_bufs = 2 if resident + 4 * io_block <= budget else 1
    io_mode = pl.Buffered(out_bufs)
    in_specs = [
        pl.BlockSpec((1, T, D_MODEL), lambda b: (b, 0, 0), pipeline_mode=pl.Buffered(in_bufs)),
        pl.BlockSpec((1, 1, 6 * D_MODEL), lambda b: (m_off + m_step * b, 0, 0)),
    ]
    args = [x, m3]
    if has_pos:
        in_specs += [_const_spec(p.shape) for p in pos]
        args += list(pos)
    in_specs += [_const_spec(c.shape) for c in consts]
    args += list(consts)
    if fuse_ffn:
        in_specs += [_const_spec(c.shape) for c in ffn_consts]
        args += list(ffn_consts)
    if has_state:
        hs0, ls0 = states
        in_specs += [
            pl.BlockSpec((1, 8, HG_DK, HG_DK), lambda b: (b, 0, 0, 0)),
            pl.BlockSpec((1, N_DIR, LRU_WIDTH), lambda b: (b, 0, 0)),
        ]
        args += [hs0, ls0]
    out_shape = [jax.ShapeDtypeStruct((B, T, D_MODEL), F32)]
    out_specs = [pl.BlockSpec((1, T, D_MODEL), lambda b: (b, 0, 0), pipeline_mode=io_mode)]
    if emit_state:
        out_shape += [jax.ShapeDtypeStruct((B, 8, HG_DK, HG_DK), F32),
                      jax.ShapeDtypeStruct((B, N_DIR, LRU_WIDTH), F32)]
        out_specs += [pl.BlockSpec((1, 8, HG_DK, HG_DK), lambda b: (b, 0, 0, 0)),
                      pl.BlockSpec((1, N_DIR, LRU_WIDTH), lambda b: (b, 0, 0))]
    scratch = [pltpu.VMEM(s, d) for s, d in scratch_shapes]
    outs = pl.pallas_call(
        functools.partial(_mixer_kernel, T=T, has_pos=has_pos, has_state=has_state, emit_state=emit_state,
                          fuse_scan=fuse_scan, fuse_ffn=fuse_ffn),
        grid=(B,),
        in_specs=in_specs,
        out_specs=out_specs,
        out_shape=out_shape,
        scratch_shapes=scratch,
        compiler_params=pltpu.CompilerParams(
            dimension_semantics=("arbitrary",), vmem_limit_bytes=VMEM_LIMIT),
        name=f"mixer_t{T}",
    )(*args)
    return (outs[0], fuse_ffn) + tuple(outs[1:])


def _ffn_rows(x, mrow, n2g, fgain, w1_ref, w2_ref):
    sh2 = mrow[:, 3 * D_MODEL:4 * D_MODEL]
    sc2 = mrow[:, 4 * D_MODEL:5 * D_MODEL]
    g2 = mrow[:, 5 * D_MODEL:6 * D_MODEL]
    ms = jnp.mean(x * x, axis=-1, keepdims=True)
    hb = (x * lax.rsqrt(ms + EPS) * (n2g * (1.0 + sc2)) + sh2).astype(BF16)
    ff = jnp.zeros(x.shape, F32)
    for c in range(D_FF // FF_CHUNK):
        a = jnp.dot(hb, w1_ref[:, c * FF_CHUNK:(c + 1) * FF_CHUNK], preferred_element_type=F32)
        a = jnp.maximum(a, 0.0)
        ff = ff + jnp.dot((a * a).astype(BF16), w2_ref[c * FF_CHUNK:(c + 1) * FF_CHUNK, :],
                          preferred_element_type=F32)
    x2 = x + g2 * ff
    ms2 = jnp.mean(x2 * x2, axis=-1, keepdims=True)
    return x2 * lax.rsqrt(ms2 + EPS) * fgain


def _ffn_kernel(x_ref, m_ref, n2g_ref, fg_ref, w1_ref, w2_ref, y_ref):
    y_ref[...] = _ffn_rows(x_ref[...], m_ref[0], n2g_ref[...], fg_ref[...], w1_ref, w2_ref)


def _ffn(x1, m3, m_off, tiles_per_cond, n2g, fgain, w1, w2):
    n = x1.shape[0]

    def m_index(i):
        if tiles_per_cond is None:
            return (m_off, 0, 0)
        return (m_off + i // tiles_per_cond, 0, 0)

    return pl.pallas_call(
        _ffn_kernel,
        grid=(n // FFN_TILE,),
        in_specs=[
            pl.BlockSpec((FFN_TILE, D_MODEL), lambda i: (i, 0)),
            pl.BlockSpec((1, 1, 6 * D_MODEL), m_index),
            _const_spec(n2g.shape),
            _const_spec(fgain.shape),
            _const_spec(w1.shape),
            _const_spec(w2.shape),
        ],
        out_specs=pl.BlockSpec((FFN_TILE, D_MODEL), lambda i: (i, 0)),
        out_shape=jax.ShapeDtypeStruct((n, D_MODEL), F32),
        compiler_params=pltpu.CompilerParams(
            dimension_semantics=("arbitrary",), vmem_limit_bytes=VMEM_LIMIT),
        name="ffn",
    )(x1, m3, n2g, fgain, w1, w2)


def _grid_pos_tables(n_tok):
    quarter = D_MODEL // 4
    omega = (1.0 / (np.float32(POS_BASE) ** (np.arange(quarter, dtype=np.float32) / np.float32(quarter)))
             ).astype(np.float32)

    def emb(n):
        ang = np.arange(n).reshape(-1, 1).astype(np.float32) * omega
        return np.concatenate([np.sin(ang), np.cos(ang)], axis=-1)

    rows = np.repeat(emb(n_tok // GRID_W)[:, None, :], 8, axis=1)
    return jnp.asarray(rows, dtype=F32), jnp.asarray(emb(GRID_W), dtype=F32)


def kernel(x_prompt, x_sample, c, state_hgrn, state_rglru, c_ctx, w_ada, b_ada, norm1_gain, norm2_gain,
           w_in, hg_lb_logits, hg_norm_gain, conv_w, conv_b, lru_wa, lru_ba, lru_wx, lru_bx, lru_lambda,
           w_out, w_ff1, w_ff2, final_gain):
    bp, tp, _ = x_prompt.shape
    bs_, ts, _ = x_sample.shape

    cond8 = jnp.concatenate([c_ctx[None, :], c, jnp.zeros((8 - 1 - bs_, D_MODEL), F32)], axis=0)
    m3 = _modulation(cond8, w_ada[0], b_ada).reshape(8, 1, 6 * D_MODEL)

    masks_np, tri_np = _level_tables()
    grp = [slice(g * LRU_GROUP, (g + 1) * LRU_GROUP) for g in range(LRU_WIDTH // LRU_GROUP)]
    bias_g = jnp.stack([jnp.concatenate([lru_ba[0, 0, s], lru_bx[0, 0, s], lru_ba[0, 1, s], lru_bx[0, 1, s]])
                        for s in grp])
    consts = [
        norm1_gain,
        w_in[0].astype(BF16),
        hg_lb_logits.reshape(2, N_DIR * HG_HEADS, HG_DK),
        hg_norm_gain[0].reshape(1, HG_WIDTH),
        jnp.asarray(masks_np),
        jnp.asarray(tri_np, dtype=BF16),
        conv_w[0],
        conv_b,
        jnp.stack([lru_wa[0], lru_wx[0]], axis=1).reshape(
            N_DIR * 2 * LRU_BLOCKS, LRU_BLOCK, LRU_BLOCK),
        bias_g,
        lru_lambda[0],
        w_out[0].astype(BF16),
    ]
    w1 = w_ff1[0].astype(BF16)
    w2 = w_ff2[0].astype(BF16)
    fgain = final_gain.reshape(1, D_MODEL)
    ffn_consts = [norm2_gain, fgain, w1, w2]

    y_prompt, done, hs, ls = _mixer(x_prompt, m3, 0, 0, None, consts, ffn_consts, None, True)
    if not done:
        y_prompt = _ffn(y_prompt.reshape(bp * tp, D_MODEL), m3, 0, None, *ffn_consts)

    y_sample, done = _mixer(x_sample, m3, 1, 1, _grid_pos_tables(ts), consts, ffn_consts,
                            (state_hgrn.reshape(bs_, N_DIR * HG_HEADS, HG_DK, HG_DK),
                             state_rglru.reshape(bs_, N_DIR, LRU_WIDTH)), False)
    if not done:
        y_sample = _ffn(y_sample.reshape(bs_ * ts, D_MODEL), m3, 1, ts // FFN_TILE, *ffn_consts)

    return (y_prompt.reshape(bp, tp, D_MODEL),
            y_sample.reshape(bs_, ts, D_MODEL),
            hs.reshape(bp, 1, N_DIR, HG_HEADS, HG_DK, HG_DK),
            ls.reshape(bp, 1, N_DIR, LRU_WIDTH))
```

```python
import functools

import numpy as np
import jax
import jax.numpy as jnp
from jax import lax
from jax.experimental import pallas as pl
from jax.experimental.pallas import tpu as pltpu

F32 = jnp.float32
BF16 = jnp.bfloat16

D_MODEL = 1024
N_DIR = 2
HG_HEADS = 4
HG_DK = 128
HG_WIDTH = 512
LRU_WIDTH = 512
LRU_BLOCKS = 8
LRU_BLOCK = 64
LRU_C = 8.0
D_FF = 4096
IN_COLS = 4096
EPS = 1e-6
LOG2E = 1.4426950408889634
GRID_W = 64
POS_BASE = 10000.0

COL_Q = 0
COL_F = 1024
COL_V = 2048
COL_GATE = 2560
COL_LX = 3072
COL_LG = 3584
PROJ_LG = COL_LX
PROJ_COLS = IN_COLS - LRU_WIDTH

CHUNK = 128
LEVELS = (64, 32, 16, 8, 4)
LV_4 = LEVELS.index(4)
LV_21 = len(LEVELS)
LV_DIAG = LV_21 + 1
LV_STATE = LV_DIAG + 1
N_MASKS = LV_DIAG + 1
ROW_TILE = 256
LRU_SEGS = 8
CONV_TILE = 32
SCAN_UNROLL = 4
FIX_TILE = 64
STAGE_SLOTS = 1
MAX_INLINE_TRIPS = 2
LRU_GROUP = 256
GATE_D = 2 * LRU_GROUP
GATE_G = N_DIR * GATE_D
LRS_SEG_COL = 2 * N_DIR * LRU_WIDTH
PROJ_STEP = 512
FFN_TILE = 512
FF_CHUNK = 1024
MOD_TILE = 2048
VMEM_LIMIT = 58 * 1024 * 1024
VMEM_HEADROOM = 6 * 1024 * 1024


def _sig(x):
    return jax.nn.sigmoid(x)


def _nt_dot(a, b):
    return lax.dot_general(a, b, (((1,), (1,)), ((), ())), preferred_element_type=F32)


def _staged_transpose(xb, slot_ref):
    slot_ref[...] = xb.T
    return slot_ref[...]


def _gram(x, slot_ref):
    xb = x.astype(BF16)
    return jnp.dot(xb, _staged_transpose(xb, slot_ref), preferred_element_type=F32)


def _level_tables():
    t = np.arange(CHUNK)[:, None]
    s = np.arange(CHUNK)[None, :]
    masks = np.zeros((N_DIR, N_MASKS, CHUNK, CHUNK), np.float32)
    for li, h in enumerate(LEVELS):
        same = (t // (2 * h)) == (s // (2 * h))
        t_hi = (t // h) % 2 == 1
        s_hi = (s // h) % 2 == 1
        masks[0, li] = same & t_hi & ~s_hi
        masks[1, li] = same & ~t_hi & s_hi
    same4 = (t // 4) == (s // 4)
    masks[0, LV_21] = same4 & (s < t)
    masks[1, LV_21] = same4 & (s > t)
    masks[:, LV_DIAG] = (t == s)
    tri = np.stack([(s <= t), (s >= t)]).astype(np.float32)
    return masks, tri


def _mod_kernel(c_ref, w_ref, b_ref, o_ref):
    c = c_ref[...]
    a = (c * _sig(c)).astype(BF16)
    o_ref[...] = jnp.dot(a, w_ref[...].astype(BF16), preferred_element_type=F32) + b_ref[...]


def _modulation(cond8, w_ada, b_ada):
    n = w_ada.shape[1]
    return pl.pallas_call(
        _mod_kernel,
        grid=(n // MOD_TILE,),
        in_specs=[
            pl.BlockSpec((8, D_MODEL), lambda j: (0, 0)),
            pl.BlockSpec((D_MODEL, MOD_TILE), lambda j: (0, j)),
            pl.BlockSpec((1, MOD_TILE), lambda j: (0, j)),
        ],
        out_specs=pl.BlockSpec((8, MOD_TILE), lambda j: (0, j)),
        out_shape=jax.ShapeDtypeStruct((8, n), F32),
        compiler_params=pltpu.CompilerParams(dimension_semantics=("arbitrary",)),
        name="adaln_modulation",
    )(cond8, w_ada, b_ada)


def _mixer_kernel(*refs, T, has_pos, has_state, emit_state, fuse_scan, fuse_ffn):
    it = iter(refs)
    x_ref = next(it)
    m_ref = next(it)
    if has_pos:
        posr_ref = next(it)
        posc_ref = next(it)
    n1g_ref = next(it)
    win_ref = next(it)
    lbl_ref = next(it)
    hgg_ref = next(it)
    masks_ref = next(it)
    tri_ref = next(it)
    convw_ref = next(it)
    convb_ref = next(it)
    gw_ref = next(it)
    bg_ref = next(it)
    lam_ref = next(it)
    wout_ref = next(it)
    if fuse_ffn:
        n2g_ref = next(it)
        fg_ref = next(it)
        w1_ref = next(it)
        w2_ref = next(it)
    if has_state:
        hs0_ref = next(it)
        ls0_ref = next(it)
    y_ref = next(it)
    if emit_state:
        hs_ref = next(it)
        ls_ref = next(it)
    x1_ref = next(it) if fuse_ffn else y_ref
    proj = next(it)
    lxp = next(it)
    mixin = next(it)
    st = next(it)
    ksc = next(it)
    xci = next(it)
    vts = next(it)
    rowc = next(it)
    tsl = next(it)
    wg = next(it)
    if fuse_scan:
        lrs = next(it)

    L = CHUNK

    @pl.when(pl.program_id(0) == 0)
    def _build_gate_weights():
        per_group = LRU_GROUP // LRU_BLOCK
        wg[...] = jnp.zeros(wg.shape, BF16)
        for g in range(LRU_WIDTH // LRU_GROUP):
            for p in range(2 * N_DIR):
                for n in range(per_group):
                    r = n * LRU_BLOCK
                    col = p * LRU_GROUP + r
                    wg[g, r:r + LRU_BLOCK, col:col + LRU_BLOCK] = gw_ref[
                        p * LRU_BLOCKS + g * per_group + n].astype(BF16)

    mrow = m_ref[0]
    sh1 = mrow[:, 0:D_MODEL]
    sc1 = mrow[:, D_MODEL:2 * D_MODEL]
    g1 = mrow[:, 2 * D_MODEL:3 * D_MODEL]
    gain1 = n1g_ref[...] * (1.0 + sc1)

    zrows = jnp.zeros((8, LRU_WIDTH), F32)
    lxp[0:8, :] = zrows
    lxp[T + 8:T + 16, :] = zrows

    def load_x(i):
        xt = x_ref[0, pl.ds(pl.multiple_of(i * ROW_TILE, ROW_TILE), ROW_TILE), :]
        if has_pos:
            per_tile = ROW_TILE // GRID_W
            tiles = []
            for s in range(per_tile):
                row_emb = jnp.concatenate([posr_ref[i * per_tile + s]] * (GRID_W // 8), axis=0)
                tiles.append(jnp.concatenate([row_emb, posc_ref[...]], axis=1))
            xt = xt + jnp.concatenate(tiles, axis=0)
        return xt

    def proj_body(i, carry):
        r0 = pl.multiple_of(i * ROW_TILE, ROW_TILE)
        xt = load_x(i)
        ms = jnp.mean(xt * xt, axis=-1, keepdims=True)
        hb = (xt * lax.rsqrt(ms + EPS) * gain1 + sh1).astype(BF16)
        starts = range(0, IN_COLS, PROJ_STEP)
        order = [COL_LX] + [c0 for c0 in starts if c0 < COL_GATE] + [
            c0 for c0 in starts if c0 >= COL_GATE and c0 != COL_LX]
        for c0 in order:
            res = jnp.dot(hb, win_ref[:, c0:c0 + PROJ_STEP], preferred_element_type=F32)
            if c0 == COL_LX:
                lxp[pl.ds(pl.multiple_of(r0 + 8, 8), ROW_TILE), :] = res
            elif c0 == COL_LG:
                proj[pl.ds(r0, ROW_TILE), PROJ_LG:PROJ_LG + LRU_WIDTH] = res
            else:
                proj[pl.ds(r0, ROW_TILE), c0:c0 + PROJ_STEP] = res
        return carry

    l0 = lbl_ref[0]
    l1 = lbl_ref[1]
    lmx = jnp.maximum(l0, l1)
    e0 = jnp.exp(l0 - lmx)
    e1 = jnp.exp(l1 - lmx)
    lb_all = e0 / (e0 + e1)

    def hg_prep(c, carry):
        rows = pl.ds(pl.multiple_of(c * L, L), L)
        for hd in range(HG_HEADS):
            vts[c, hd] = proj[rows, COL_V + hd * HG_DK:COL_V + (hd + 1) * HG_DK].T.astype(BF16)
        for d in range(N_DIR):
            for hd in range(HG_HEADS):
                idx = d * HG_HEADS + hd
                cq = COL_Q + idx * HG_DK
                cf = COL_F + idx * HG_DK
                hq = proj[rows, cq:cq + HG_DK]
                fz = proj[rows, cf:cf + HG_DK]
                proj[rows, cq:cq + HG_DK] = hq * _sig(hq)
                sg = _sig(fz)
                lb = lb_all[idx:idx + 1, :]
                oml = 1.0 - lb
                ksc[rows, idx * HG_DK:(idx + 1) * HG_DK] = oml * (1.0 - sg)
                logf = jnp.log(lb + oml * sg)
                p1 = logf.astype(BF16)
                p2 = (logf - p1.astype(F32)).astype(BF16)
                bb = jnp.dot(tri_ref[d], jnp.concatenate([p1, p2], axis=1), preferred_element_type=F32)
                proj[rows, cf:cf + HG_DK] = (bb[:, 0:HG_DK] + bb[:, HG_DK:2 * HG_DK]) * LOG2E
        return carry

    lax.fori_loop(0, T // ROW_TILE, proj_body, 0)

    S = T // LRU_SEGS
    cw = convw_ref[...]
    cb = convb_ref[...]

    def irows(i0, j, n):
        return pl.ds(pl.multiple_of(LRU_SEGS * i0, 8) + j, n, stride=LRU_SEGS)

    for j in range(LRU_SEGS):
        def conv_body(ti, carry, j=j):
            i0 = pl.multiple_of(ti * CONV_TILE, CONV_TILE)
            win = lxp[pl.ds(pl.multiple_of(j * S + i0, 8), CONV_TILE + 16), :]
            xc = cb
            for tap in range(4):
                xc = xc + win[6 + tap:6 + tap + CONV_TILE] * cw[tap:tap + 1]
            for l in range(LRU_WIDTH // 128):
                xci[l, irows(i0, j, CONV_TILE), :] = xc[:, l * 128:(l + 1) * 128]
            return carry

        lax.fori_loop(0, S // CONV_TILE, conv_body, 0, unroll=S // CONV_TILE <= MAX_INLINE_TRIPS)

    n_grp = LRU_WIDTH // LRU_GROUP
    slabs = LRU_GROUP // 128
    gates = lrs if fuse_scan else proj

    def xc_tile(rows, g):
        return jnp.concatenate([xci[g * slabs + l, rows, :] for l in range(slabs)], axis=1)

    def gate_body(i, carry):
        rows = pl.ds(pl.multiple_of(i * ROW_TILE, ROW_TILE), ROW_TILE)
        for g in range(n_grp):
            gates[rows, g * GATE_G:(g + 1) * GATE_G] = jnp.dot(
                xc_tile(rows, g).astype(BF16), wg[g], preferred_element_type=F32)
        return carry

    lam = lam_ref[...]
    nl = -lam
    c8 = -LRU_C * (jnp.maximum(nl, 0.0) + jnp.log1p(jnp.exp(-jnp.abs(nl))))
    rowi2 = lax.broadcasted_iota(jnp.int32, (8, LRU_GROUP), 0)

    for d in range(N_DIR):
        for g in range(n_grp):
            k3 = 3 * (d * n_grp + g)
            for r, row in enumerate((bg_ref[g:g + 1, d * GATE_D:d * GATE_D + LRU_GROUP],
                                     bg_ref[g:g + 1, d * GATE_D + LRU_GROUP:d * GATE_D + 2 * LRU_GROUP],
                                     c8[d:d + 1, g * LRU_GROUP:(g + 1) * LRU_GROUP])):
                rowc[k3 + r] = jnp.broadcast_to(row, (LRU_SEGS, LRU_GROUP))

    def seg_out(d, rows, g, part):
        if fuse_scan:
            col = LRS_SEG_COL + ((d * n_grp + g) * 2 + part) * LRU_GROUP
            return lrs.at[rows, col:col + LRU_GROUP]
        cols = slice((g * 2 + part) * LRU_GROUP, (g * 2 + part + 1) * LRU_GROUP)
        return (ksc.at[rows, cols] if d == 0 else x1_ref.at[0, rows, cols])

    def lru_inputs(rows8, g, d):
        base = g * GATE_G + d * GATE_D
        k3 = 3 * (d * n_grp + g)
        ga = gates[rows8, base:base + LRU_GROUP] + rowc[k3]
        gx = gates[rows8, base + LRU_GROUP:base + 2 * LRU_GROUP] + rowc[k3 + 1]
        xc8 = xc_tile(rows8, g)
        log_a = rowc[k3 + 2] * _sig(ga)
        a = jnp.exp(log_a)
        z = jnp.tanh(-log_a) * (1.0 + a * a)
        mult = jnp.where(z > 0.0, z * lax.rsqrt(z), 0.0)
        return a, mult * (_sig(gx) * xc8)

    def scan8(a, u, d):
        for sft in (1, 2, 4):
            if d == 0:
                keep = rowi2 >= sft
                amt = sft
            else:
                keep = rowi2 < 8 - sft
                amt = 8 - sft
            ash = jnp.where(keep, pltpu.roll(a, amt, 0), 1.0)
            ush = jnp.where(keep, pltpu.roll(u, amt, 0), 0.0)
            u = a * ush + u
            a = a * ash
        return a, u

    def scan_steps(first, count, carry):
        hs, ds = [list(c) for c in carry[:2]], [list(c) for c in carry[2:]]
        for u in range(count):
            i_f = first + u
            for d, i in ((0, i_f), (1, S - 1 - i_f)):
                rows8 = pl.ds(pl.multiple_of(LRU_SEGS * i, 8), 8)
                for g in range(n_grp):
                    a, uu = lru_inputs(rows8, g, d)
                    hs[d][g] = a * hs[d][g] + uu
                    ds[d][g] = a * ds[d][g]
                    seg_out(d, rows8, g, 0)[...] = hs[d][g]
                    seg_out(d, rows8, g, 1)[...] = ds[d][g]
        return tuple(tuple(c) for c in hs + ds)

    zero8 = jnp.zeros((LRU_SEGS, LRU_GROUP), F32)
    one8 = jnp.ones((LRU_SEGS, LRU_GROUP), F32)
    scan_init = ((zero8,) * n_grp, (zero8,) * n_grp, (one8,) * n_grp, (one8,) * n_grp)
    if fuse_scan:
        lax.fori_loop(0, T // ROW_TILE, gate_body, 0)

    rowi = lax.broadcasted_iota(jnp.int32, (8, HG_DK), 0)
    r4 = rowi & 3
    is_r0 = r4 == 0
    is_r1 = r4 == 1
    is_r2 = r4 == 2
    hi4 = rowi >= 4

    for i in range(N_DIR * HG_HEADS):
        if has_state:
            st[i] = hs0_ref[0, i].T
        else:
            st[i] = jnp.zeros((HG_DK, HG_DK), F32)

    def hg_chunk(c, d):
        r0 = pl.multiple_of(c * L, L)
        rows = pl.ds(r0, L)
        for hd in range(HG_HEADS):
            idx = d * HG_HEADS + hd
            cq = COL_Q + idx * HG_DK
            cf = COL_F + idx * HG_DK
            ck = idx * HG_DK
            slot = idx % STAGE_SLOTS

            def ldq(lo, n, cq=cq):
                return proj[pl.ds(pl.multiple_of(r0 + lo, 8), n), cq:cq + HG_DK]

            def ldb(lo, n, cf=cf):
                return proj[pl.ds(pl.multiple_of(r0 + lo, 8), n), cf:cf + HG_DK]

            def ldk(lo, n, ck=ck):
                return ksc[pl.ds(pl.multiple_of(r0 + lo, 8), n), ck:ck + HG_DK]

            def bline(r, cf=cf):
                grp = proj[pl.ds(pl.multiple_of(r0 + 8 * (r // 8), 8), 8), cf:cf + HG_DK]
                return grp[r % 8:r % 8 + 1, :]

            def brow(r, n):
                return jnp.broadcast_to(bline(r), (n, HG_DK))

            acc_rows = [None] * (L // 8)

            def accumulate(li, p, row0):
                for i in range(p.shape[0] // 8):
                    g = row0 // 8 + i
                    term = masks_ref[d, li, 8 * g:8 * g + 8, :] * p[8 * i:8 * i + 8]
                    acc_rows[g] = term if acc_rows[g] is None else acc_rows[g] + term

            accumulate(LV_DIAG, jnp.dot(
                ldq(0, L).astype(BF16), _staged_transpose(ldk(0, L).astype(BF16), tsl.at[slot, LV_DIAG]),
                preferred_element_type=F32), 0)
            for li, h in enumerate(LEVELS[:LV_4]):
                pieces, q_pieces, q_starts = [], [], []
                for j in range(L // (2 * h)):
                    lo = j * 2 * h
                    mid = lo + h
                    if d == 0:
                        bm = brow(mid - 1, h)
                        kp = ldk(lo, h) * jnp.exp2(bm - ldb(lo, h))
                        qp = ldq(mid, h) * jnp.exp2(ldb(mid, h) - bm)
                        pieces += [kp, qp]
                        q_starts.append(mid)
                    else:
                        bm = brow(mid, h)
                        qp = ldq(lo, h) * jnp.exp2(ldb(lo, h) - bm)
                        kp = ldk(mid, h) * jnp.exp2(bm - ldb(mid, h))
                        pieces += [qp, kp]
                        q_starts.append(lo)
                    q_pieces.append(qp)
                xt = _staged_transpose(jnp.concatenate(pieces, axis=0).astype(BF16), tsl.at[slot,li])
                p = jnp.dot(jnp.concatenate(q_pieces, axis=0).astype(BF16), xt, preferred_element_type=F32)
                for j, row0 in enumerate(q_starts):
                    accumulate(li, p[j * h:(j + 1) * h], row0)
            x4, xq21, xk21 = [], [], []
            for g in range(L // 8):
                qg, kg, bg = ldq(8 * g, 8), ldk(8 * g, 8), ldb(8 * g, 8)
                fg = 1.0 - kg
                qfg = qg * fg
                mid = 8 * g + (3 if d == 0 else 4)
                e4 = jnp.exp2(-jnp.abs(bg - brow(mid, 8)))
                fnx = pltpu.roll(fg, 7, 0)
                fpv = pltpu.roll(fg, 1, 0)
                k_over_f = kg / fg
                if d == 0:
                    x4.append(jnp.where(hi4, qg, kg) * e4)
                    xq21.append(jnp.where(is_r0, 0.0, jnp.where(is_r1, qg, jnp.where(is_r2, qfg, qfg * fpv))))
                    xk21.append(jnp.where(is_r0, kg * fnx, jnp.where(is_r1, kg, jnp.where(is_r2, k_over_f, 0.0))))
                else:
                    x4.append(jnp.where(hi4, kg, qg) * e4)
                    xq21.append(jnp.where(is_r0, qfg * fnx, jnp.where(is_r1, qfg, jnp.where(is_r2, qg, 0.0))))
                    xk21.append(jnp.where(is_r0, 0.0, jnp.where(is_r1, k_over_f, jnp.where(is_r2, kg, kg * fpv))))
            accumulate(LV_4, _gram(jnp.concatenate(x4, axis=0), tsl.at[slot, LV_4]), 0)
            accumulate(LV_21, jnp.dot(
                jnp.concatenate(xq21, axis=0).astype(BF16),
                _staged_transpose(jnp.concatenate(xk21, axis=0).astype(BF16), tsl.at[slot, LV_21]),
                preferred_element_type=F32), 0)
            acc = jnp.concatenate(acc_rows, axis=0)

            vt = vts[c, hd]
            st_t = st[idx]
            b = ldb(0, L)
            qt = (ldq(0, L) * jnp.exp2(b)).astype(BF16)
            vb = proj[rows, COL_V + hd * HG_DK:COL_V + (hd + 1) * HG_DK].astype(BF16)
            o = jnp.dot(jnp.concatenate([acc.astype(BF16), qt], axis=1),
                        jnp.concatenate([vb, _staged_transpose(st_t.astype(BF16), tsl.at[slot, LV_STATE])], axis=0),
                        preferred_element_type=F32)
            btot = bline(L - 1 if d == 0 else 0)
            kt = (ldk(0, L) * jnp.exp2(btot - b)).astype(BF16)
            st[idx] = st_t * jnp.exp2(btot) + jnp.dot(vt, kt, preferred_element_type=F32)
            x1_ref[0, rows, d * HG_WIDTH + hd * HG_DK:d * HG_WIDTH + (hd + 1) * HG_DK] = o

    n_chunks = T // L

    def hg_both(c, carry):
        hg_chunk(c, 0)
        hg_chunk(n_chunks - 1 - c, 1)
        if fuse_scan:
            steps = S // n_chunks
            carry = scan_steps(c * steps, steps, carry)
        return carry

    lax.fori_loop(0, n_chunks, hg_prep, 0, unroll=n_chunks <= MAX_INLINE_TRIPS)
    scan_state = lax.fori_loop(0, n_chunks, hg_both, scan_init if fuse_scan else 0)

    if emit_state:
        for i in range(N_DIR * HG_HEADS):
            hs_ref[0, i] = st[i].T

    def hg_fin(i, carry):
        rows = pl.ds(pl.multiple_of(i * L, L), L)
        for hd in range(HG_HEADS):
            cs = slice(hd * HG_DK, (hd + 1) * HG_DK)
            o = x1_ref[0, rows, cs] + x1_ref[0, rows, HG_WIDTH + hd * HG_DK:HG_WIDTH + (hd + 1) * HG_DK]
            ms = jnp.mean(o * o, axis=-1, keepdims=True)
            y = o * lax.rsqrt(ms + EPS) * hgg_ref[:, cs]
            gz = proj[rows, COL_GATE + hd * HG_DK:COL_GATE + (hd + 1) * HG_DK]
            mixin[rows, cs] = (y * (gz * _sig(gz))).astype(BF16)
        return carry

    lax.fori_loop(0, n_chunks, hg_fin, 0, unroll=n_chunks <= MAX_INLINE_TRIPS)

    if not fuse_scan:
        lax.fori_loop(0, T // ROW_TILE, gate_body, 0)
        scan_state = lax.fori_loop(
            0, S // SCAN_UNROLL, lambda n, carry: scan_steps(n * SCAN_UNROLL, SCAN_UNROLL, carry), scan_init)
    h_end, d_end = scan_state[:2], scan_state[2:]

    carry_in = [[None] * n_grp for _ in range(N_DIR)]
    for d in range(N_DIR):
        for g in range(n_grp):
            if has_state:
                h0 = ls0_ref[0, d:d + 1, g * LRU_GROUP:(g + 1) * LRU_GROUP]
            else:
                h0 = jnp.zeros((1, LRU_GROUP), F32)
            dd, hh = scan8(d_end[d][g], h_end[d][g], d)
            seg_end = hh + dd * h0
            if d == 0:
                carry_in[d][g] = jnp.where(rowi2 >= 1, pltpu.roll(seg_end, 1, 0), h0)
                last = seg_end[LRU_SEGS - 1:LRU_SEGS, :]
            else:
                carry_in[d][g] = jnp.where(rowi2 < LRU_SEGS - 1, pltpu.roll(seg_end, LRU_SEGS - 1, 0), h0)
                last = seg_end[0:1, :]
            if emit_state:
                ls_ref[0, d:d + 1, g * LRU_GROUP:(g + 1) * LRU_GROUP] = last

    def fix_body(i, carry):
        rows = pl.ds(pl.multiple_of(i * FIX_TILE, FIX_TILE), FIX_TILE)
        for g in range(n_grp):
            tot = None
            for d in range(N_DIR):
                cin = jnp.concatenate([carry_in[d][g]] * (FIX_TILE // LRU_SEGS), axis=0)
                h = seg_out(d, rows, g, 0)[...] + seg_out(d, rows, g, 1)[...] * cin
                tot = h if tot is None else tot + h
            for l in range(slabs):
                xci[g * slabs + l, rows, :] = tot[:, l * 128:(l + 1) * 128]
        return carry

    lax.fori_loop(0, T // FIX_TILE, fix_body, 0)

    for j in range(LRU_SEGS):
        def lru_fin(ti, carry, j=j):
            i0 = pl.multiple_of(ti * CONV_TILE, CONV_TILE)
            rows = pl.ds(pl.multiple_of(j * S + i0, CONV_TILE), CONV_TILE)
            hsum = jnp.concatenate([xci[l, irows(i0, j, CONV_TILE), :] for l in range(LRU_WIDTH // 128)], axis=1)
            lg = proj[rows, PROJ_LG:PROJ_LG + LRU_WIDTH]
            gl = lg * (0.5 * (1.0 + jnp.tanh(0.7978845608028654 * (lg + 0.044715 * (lg * lg * lg)))))
            mixin[rows, HG_WIDTH:HG_WIDTH + LRU_WIDTH] = (hsum * gl).astype(BF16)
            return carry

        lax.fori_loop(0, S // CONV_TILE, lru_fin, 0, unroll=S // CONV_TILE <= MAX_INLINE_TRIPS)

    def out_body(i, carry):
        rows = pl.ds(pl.multiple_of(i * ROW_TILE, ROW_TILE), ROW_TILE)
        mix = jnp.dot(mixin[rows, :], wout_ref[...], preferred_element_type=F32)
        x1_ref[0, rows, :] = load_x(i) + g1 * mix
        return carry

    lax.fori_loop(0, T // ROW_TILE, out_body, 0)

    if fuse_ffn:
        def ffn_body(i, carry):
            rows = pl.ds(pl.multiple_of(i * ROW_TILE, ROW_TILE), ROW_TILE)
            y_ref[0, rows, :] = _ffn_rows(x1_ref[0, rows, :], mrow, n2g_ref[...], fg_ref[...], w1_ref, w2_ref)
            return carry

        lax.fori_loop(0, T // ROW_TILE, ffn_body, 0)


def _const_spec(shape):
    nd = len(shape)
    return pl.BlockSpec(shape, lambda b, _n=nd: (0,) * _n, pipeline_mode=pl.Buffered(1))


def _nbytes(shape, dtype):
    return int(np.prod(shape)) * jnp.dtype(dtype).itemsize


def _mixer(x, m3, m_off, m_step, pos, consts, ffn_consts, states, emit_state):
    B, T, _ = x.shape
    has_pos = pos is not None
    has_state = states is not None
    scratch_shapes = [
        ((T, PROJ_COLS), F32),
        ((T + 16, LRU_WIDTH), F32),
        ((T, D_MODEL), BF16),
        ((8, HG_DK, HG_DK), F32),
        ((T, N_DIR * HG_HEADS * HG_DK), F32),
        ((LRU_WIDTH // 128, T, 128), F32),
        ((T // CHUNK, HG_HEADS, HG_DK, CHUNK), BF16),
        ((3 * N_DIR * (LRU_WIDTH // LRU_GROUP), LRU_SEGS, LRU_GROUP), F32),
        ((STAGE_SLOTS, LV_STATE + 1, HG_DK, CHUNK), BF16),
        ((LRU_WIDTH // LRU_GROUP, LRU_GROUP, 2 * N_DIR * LRU_GROUP), BF16),
    ]
    resident = (sum(_nbytes(s, d) for s, d in scratch_shapes)
                + sum(_nbytes(c.shape, c.dtype) for c in consts)
                + (sum(_nbytes(p.shape, p.dtype) for p in pos) if has_pos else 0))
    io_block = _nbytes((T, D_MODEL), F32)
    budget = VMEM_LIMIT - VMEM_HEADROOM
    lrs_shape = ((T, 2 * LRS_SEG_COL), F32)
    fuse_scan = resident + _nbytes(*lrs_shape) + 4 * io_block <= budget
    if fuse_scan:
        scratch_shapes.append(lrs_shape)
        resident += _nbytes(*lrs_shape)
    ffn_bytes = sum(_nbytes(c.shape, c.dtype) for c in ffn_consts) + io_block
    fuse_ffn = resident + ffn_bytes + 4 * io_block <= budget
    if fuse_ffn:
        scratch_shapes.insert(0, ((1, T, D_MODEL), F32))
        resident += ffn_bytes
    in_bufs = 2 if resident + 3 * io_block <= budget else 1
    out_bufs = 2 if resident + 4 * io_block <= budget else 1
    io_mode = pl.Buffered(out_bufs)
    in_specs = [
        pl.BlockSpec((1, T, D_MODEL), lambda b: (b, 0, 0), pipeline_mode=pl.Buffered(in_bufs)),
        pl.BlockSpec((1, 1, 6 * D_MODEL), lambda b: (m_off + m_step * b, 0, 0)),
    ]
    args = [x, m3]
    if has_pos:
        in_specs += [_const_spec(p.shape) for p in pos]
        args += list(pos)
    in_specs += [_const_spec(c.shape) for c in consts]
    args += list(consts)
    if fuse_ffn:
        in_specs += [_const_spec(c.shape) for c in ffn_consts]
        args += list(ffn_consts)
    if has_state:
        hs0, ls0 = states
        in_specs += [
            pl.BlockSpec((1, 8, HG_DK, HG_DK), lambda b: (b, 0, 0, 0)),
            pl.BlockSpec((1, N_DIR, LRU_WIDTH), lambda b: (b, 0, 0)),
        ]
        args += [hs0, ls0]
    out_shape = [jax.ShapeDtypeStruct((B, T, D_MODEL), F32)]
    out_specs = [pl.BlockSpec((1, T, D_MODEL), lambda b: (b, 0, 0), pipeline_mode=io_mode)]
    if emit_state:
        out_shape += [jax.ShapeDtypeStruct((B, 8, HG_DK, HG_DK), F32),
                      jax.ShapeDtypeStruct((B, N_DIR, LRU_WIDTH), F32)]
        out_specs += [pl.BlockSpec((1, 8, HG_DK, HG_DK), lambda b: (b, 0, 0, 0)),
                      pl.BlockSpec((1, N_DIR, LRU_WIDTH), lambda b: (b, 0, 0))]
    scratch = [pltpu.VMEM(s, d) for s, d in scratch_shapes]
    outs = pl.pallas_call(
        functools.partial(_mixer_kernel, T=T, has_pos=has_pos, has_state=has_state, emit_state=emit_state,
                          fuse_scan=fuse_scan, fuse_ffn=fuse_ffn),
        grid=(B,),
        in_specs=in_specs,
        out_specs=out_specs,
        out_shape=out_shape,
        scratch_shapes=scratch,
        compiler_params=pltpu.CompilerParams(
            dimension_semantics=("arbitrary",), vmem_limit_bytes=VMEM_LIMIT),
        name=f"mixer_t{T}",
    )(*args)
    return (outs[0], fuse_ffn) + tuple(outs[1:])


def _ffn_rows(x, mrow, n2g, fgain, w1_ref, w2_ref):
    sh2 = mrow[:, 3 * D_MODEL:4 * D_MODEL]
    sc2 = mrow[:, 4 * D_MODEL:5 * D_MODEL]
    g2 = mrow[:, 5 * D_MODEL:6 * D_MODEL]
    ms = jnp.mean(x * x, axis=-1, keepdims=True)
    hb = (x * lax.rsqrt(ms + EPS) * (n2g * (1.0 + sc2)) + sh2).astype(BF16)
    ff = jnp.zeros(x.shape, F32)
    for c in range(D_FF // FF_CHUNK):
        a = jnp.dot(hb, w1_ref[:, c * FF_CHUNK:(c + 1) * FF_CHUNK], preferred_element_type=F32)
        a = jnp.maximum(a, 0.0)
        ff = ff + jnp.dot((a * a).astype(BF16), w2_ref[c * FF_CHUNK:(c + 1) * FF_CHUNK, :],
                          preferred_element_type=F32)
    x2 = x + g2 * ff
    ms2 = jnp.mean(x2 * x2, axis=-1, keepdims=True)
    return x2 * lax.rsqrt(ms2 + EPS) * fgain


def _ffn_kernel(x_ref, m_ref, n2g_ref, fg_ref, w1_ref, w2_ref, y_ref):
    y_ref[...] = _ffn_rows(x_ref[...], m_ref[0], n2g_ref[...], fg_ref[...], w1_ref, w2_ref)


def _ffn(x1, m3, m_off, tiles_per_cond, n2g, fgain, w1, w2):
    n = x1.shape[0]

    def m_index(i):
        if tiles_per_cond is None:
            return (m_off, 0, 0)
        return (m_off + i // tiles_per_cond, 0, 0)

    return pl.pallas_call(
        _ffn_kernel,
        grid=(n // FFN_TILE,),
        in_specs=[
            pl.BlockSpec((FFN_TILE, D_MODEL), lambda i: (i, 0)),
            pl.BlockSpec((1, 1, 6 * D_MODEL), m_index),
            _const_spec(n2g.shape),
            _const_spec(fgain.shape),
            _const_spec(w1.shape),
            _const_spec(w2.shape),
        ],
        out_specs=pl.BlockSpec((FFN_TILE, D_MODEL), lambda i: (i, 0)),
        out_shape=jax.ShapeDtypeStruct((n, D_MODEL), F32),
        compiler_params=pltpu.CompilerParams(
            dimension_semantics=("arbitrary",), vmem_limit_bytes=VMEM_LIMIT),
        name="ffn",
    )(x1, m3, n2g, fgain, w1, w2)


def _grid_pos_tables(n_tok):
    quarter = D_MODEL // 4
    omega = (1.0 / (np.float32(POS_BASE) ** (np.arange(quarter, dtype=np.float32) / np.float32(quarter)))
             ).astype(np.float32)

    def emb(n):
        ang = np.arange(n).reshape(-1, 1).astype(np.float32) * omega
        return np.concatenate([np.sin(ang), np.cos(ang)], axis=-1)

    rows = np.repeat(emb(n_tok // GRID_W)[:, None, :], 8, axis=1)
    return jnp.asarray(rows, dtype=F32), jnp.asarray(emb(GRID_W), dtype=F32)


def kernel(x_prompt, x_sample, c, state_hgrn, state_rglru, c_ctx, w_ada, b_ada, norm1_gain, norm2_gain,
           w_in, hg_lb_logits, hg_norm_gain, conv_w, conv_b, lru_wa, lru_ba, lru_wx, lru_bx, lru_lambda,
           w_out, w_ff1, w_ff2, final_gain):
    bp, tp, _ = x_prompt.shape
    bs_, ts, _ = x_sample.shape

    cond8 = jnp.concatenate([c_ctx[None, :], c, jnp.zeros((8 - 1 - bs_, D_MODEL), F32)], axis=0)
    m3 = _modulation(cond8, w_ada[0], b_ada).reshape(8, 1, 6 * D_MODEL)

    masks_np, tri_np = _level_tables()
    grp = [slice(g * LRU_GROUP, (g + 1) * LRU_GROUP) for g in range(LRU_WIDTH // LRU_GROUP)]
    bias_g = jnp.stack([jnp.concatenate([lru_ba[0, 0, s], lru_bx[0, 0, s], lru_ba[0, 1, s], lru_bx[0, 1, s]])
                        for s in grp])
    consts = [
        norm1_gain,
        w_in[0].astype(BF16),
        hg_lb_logits.reshape(2, N_DIR * HG_HEADS, HG_DK),
        hg_norm_gain[0].reshape(1, HG_WIDTH),
        jnp.asarray(masks_np),
        jnp.asarray(tri_np, dtype=BF16),
        conv_w[0],
        conv_b,
        jnp.stack([lru_wa[0], lru_wx[0]], axis=1).reshape(
            N_DIR * 2 * LRU_BLOCKS, LRU_BLOCK, LRU_BLOCK),
        bias_g,
        lru_lambda[0],
        w_out[0].astype(BF16),
    ]
    w1 = w_ff1[0].astype(BF16)
    w2 = w_ff2[0].astype(BF16)
    fgain = final_gain.reshape(1, D_MODEL)
    ffn_consts = [norm2_gain, fgain, w1, w2]

    y_prompt, done, hs, ls = _mixer(x_prompt, m3, 0, 0, None, consts, ffn_consts, None, True)
    if not done:
        y_prompt = _ffn(y_prompt.reshape(bp * tp, D_MODEL), m3, 0, None, *ffn_consts)

    y_sample, done = _mixer(x_sample, m3, 1, 1, _grid_pos_tables(ts), consts, ffn_consts,
                            (state_hgrn.reshape(bs_, N_DIR * HG_HEADS, HG_DK, HG_DK),
                             state_rglru.reshape(bs_, N_DIR, LRU_WIDTH)), False)
    if not done:
        y_sample = _ffn(y_sample.reshape(bs_ * ts, D_MODEL), m3, 1, ts // FFN_TILE, *ffn_consts)

    return (y_prompt.reshape(bp, tp, D_MODEL),
            y_sample.reshape(bs_, ts, D_MODEL),
            hs.reshape(bp, 1, N_DIR, HG_HEADS, HG_DK, HG_DK),
            ls.reshape(bp, 1, N_DIR, LRU_WIDTH))
```

```python
import functools

import numpy as np
import jax
import jax.numpy as jnp
from jax import lax
from jax.experimental import pallas as pl
from jax.experimental.pallas import tpu as pltpu

F32 = jnp.float32
BF16 = jnp.bfloat16

D_MODEL = 1024
N_DIR = 2
HG_HEADS = 4
HG_DK = 128
HG_WIDTH = 512
LRU_WIDTH = 512
LRU_BLOCKS = 8
LRU_BLOCK = 64
LRU_C = 8.0
D_FF = 4096
IN_COLS = 4096
EPS = 1e-6
LOG2E = 1.4426950408889634
GRID_W = 64
POS_BASE = 10000.0

COL_Q = 0
COL_F = 1024
COL_V = 2048
COL_GATE = 2560
COL_LX = 3072
COL_LG = 3584
PROJ_LG = COL_LX
PROJ_COLS = IN_COLS - LRU_WIDTH

CHUNK = 128
LEVELS = (64, 32, 16, 8, 4)
LV_4 = LEVELS.index(4)
LV_21 = len(LEVELS)
LV_DIAG = LV_21 + 1
LV_STATE = LV_DIAG + 1
N_MASKS = LV_DIAG + 1
ROW_TILE = 256
LRU_SEGS = 8
CONV_TILE = 32
SCAN_UNROLL = 4
FIX_TILE = 64
STAGE_SLOTS = 1
MAX_INLINE_TRIPS = 2
LRU_GROUP = 256
GATE_D = 2 * LRU_GROUP
GATE_G = N_DIR * GATE_D
LRS_SEG_COL = 2 * N_DIR * LRU_WIDTH
PROJ_STEP = 512
FFN_TILE = 512
FF_CHUNK = 1024
MOD_TILE = 2048
MOD_ROWS = 8
VMEM_LIMIT = 58 * 1024 * 1024
VMEM_HEADROOM = 6 * 1024 * 1024


def _sig(x):
    return jax.nn.sigmoid(x)


def _nt_dot(a, b):
    return lax.dot_general(a, b, (((1,), (1,)), ((), ())), preferred_element_type=F32)


def _staged_transpose(xb, slot_ref):
    slot_ref[...] = xb.T
    return slot_ref[...]


def _gram(x, slot_ref):
    xb = x.astype(BF16)
    return jnp.dot(xb, _staged_transpose(xb, slot_ref), preferred_element_type=F32)


def _level_tables():
    t = np.arange(CHUNK)[:, None]
    s = np.arange(CHUNK)[None, :]
    masks = np.zeros((N_DIR, N_MASKS, CHUNK, CHUNK), np.float32)
    for li, h in enumerate(LEVELS):
        same = (t // (2 * h)) == (s // (2 * h))
        t_hi = (t // h) % 2 == 1
        s_hi = (s // h) % 2 == 1
        masks[0, li] = same & t_hi & ~s_hi
        masks[1, li] = same & ~t_hi & s_hi
    same4 = (t // 4) == (s // 4)
    masks[0, LV_21] = same4 & (s < t)
    masks[1, LV_21] = same4 & (s > t)
    masks[:, LV_DIAG] = (t == s)
    tri = np.stack([(s <= t), (s >= t)]).astype(np.float32)
    return masks, tri


def _mod_kernel(cctx_ref, c_ref, w_ref, b_ref, o_ref, cond):
    n = c_ref.shape[0]
    cond[...] = jnp.zeros(cond.shape, F32)
    cond[0:1, :] = cctx_ref[...]
    cond[1:1 + n, :] = c_ref[...]
    c = cond[...]
    a = (c * _sig(c)).astype(BF16)
    m = jnp.dot(a, w_ref[...].astype(BF16), preferred_element_type=F32) + b_ref[...]
    for r in range(MOD_ROWS):
        o_ref[r] = m[r:r + 1]


def _modulation(c_ctx, c, w_ada, b_ada):
    n = w_ada.shape[1]
    assert 1 + c.shape[0] <= MOD_ROWS
    return pl.pallas_call(
        _mod_kernel,
        grid=(n // MOD_TILE,),
        in_specs=[
            pl.BlockSpec((1, D_MODEL), lambda j: (0, 0)),
            pl.BlockSpec(c.shape, lambda j: (0, 0)),
            pl.BlockSpec((D_MODEL, MOD_TILE), lambda j: (0, j)),
            pl.BlockSpec((1, MOD_TILE), lambda j: (0, j)),
        ],
        out_specs=pl.BlockSpec((MOD_ROWS, 1, MOD_TILE), lambda j: (0, 0, j)),
        out_shape=jax.ShapeDtypeStruct((MOD_ROWS, 1, n), F32),
        scratch_shapes=[pltpu.VMEM((MOD_ROWS, D_MODEL), F32)],
        compiler_params=pltpu.CompilerParams(dimension_semantics=("arbitrary",)),
        name="adaln_modulation",
    )(c_ctx.reshape(1, D_MODEL), c, w_ada, b_ada)


def _mixer_kernel(*refs, T, has_pos, has_state, emit_state, fuse_scan, fuse_ffn):
    it = iter(refs)
    x_ref = next(it)
    m_ref = next(it)
    if has_pos:
        posr_ref = next(it)
        posc_ref = next(it)
    n1g_ref = next(it)
    win_ref = next(it)
    lbl_ref = next(it)
    hgg_ref = next(it)
    masks_ref = next(it)
    tri_ref = next(it)
    convw_ref = next(it)
    convb_ref = next(it)
    wa_ref = next(it)
    wx_ref = next(it)
    ba_ref = next(it)
    bx_ref = next(it)
    lam_ref = next(it)
    wout_ref = next(it)
    if fuse_ffn:
        n2g_ref = next(it)
        fg_ref = next(it)
        w1_ref = next(it)
        w2_ref = next(it)
    if has_state:
        hs0_ref = next(it)
        ls0_ref = next(it)
    y_ref = next(it)
    if emit_state:
        hs_ref = next(it)
        ls_ref = next(it)
    x1_ref = next(it) if fuse_ffn else y_ref
    proj = next(it)
    lxp = next(it)
    mixin = next(it)
    st = next(it)
    ksc = next(it)
    xci = next(it)
    vts = next(it)
    rowc = next(it)
    tsl = next(it)
    wg = next(it)
    if fuse_scan:
        lrs = next(it)

    L = CHUNK

    @pl.when(pl.program_id(0) == 0)
    def _build_gate_weights():
        per_group = LRU_GROUP // LRU_BLOCK
        wg[...] = jnp.zeros(wg.shape, BF16)
        for g in range(LRU_WIDTH // LRU_GROUP):
            for p in range(2 * N_DIR):
                src = wa_ref if p % 2 == 0 else wx_ref
                for n in range(per_group):
                    r = n * LRU_BLOCK
                    col = p * LRU_GROUP + r
                    wg[g, r:r + LRU_BLOCK, col:col + LRU_BLOCK] = src[
                        (p // 2) * LRU_BLOCKS + g * per_group + n].astype(BF16)

    mrow = m_ref[0]
    sh1 = mrow[:, 0:D_MODEL]
    sc1 = mrow[:, D_MODEL:2 * D_MODEL]
    g1 = mrow[:, 2 * D_MODEL:3 * D_MODEL]
    gain1 = n1g_ref[...] * (1.0 + sc1)

    zrows = jnp.zeros((8, LRU_WIDTH), F32)
    lxp[0:8, :] = zrows
    lxp[T + 8:T + 16, :] = zrows

    def load_x(i):
        xt = x_ref[0, pl.ds(pl.multiple_of(i * ROW_TILE, ROW_TILE), ROW_TILE), :]
        if has_pos:
            per_tile = ROW_TILE // GRID_W
            tiles = []
            for s in range(per_tile):
                row_emb = jnp.concatenate([posr_ref[i * per_tile + s]] * (GRID_W // 8), axis=0)
                tiles.append(jnp.concatenate([row_emb, posc_ref[...]], axis=1))
            xt = xt + jnp.concatenate(tiles, axis=0)
        return xt

    def proj_body(i, carry):
        r0 = pl.multiple_of(i * ROW_TILE, ROW_TILE)
        xt = load_x(i)
        ms = jnp.mean(xt * xt, axis=-1, keepdims=True)
        hb = (xt * lax.rsqrt(ms + EPS) * gain1 + sh1).astype(BF16)
        starts = range(0, IN_COLS, PROJ_STEP)
        order = [COL_LX] + [c0 for c0 in starts if c0 < COL_GATE] + [
            c0 for c0 in starts if c0 >= COL_GATE and c0 != COL_LX]
        for c0 in order:
            res = jnp.dot(hb, win_ref[:, c0:c0 + PROJ_STEP], preferred_element_type=F32)
            if c0 == COL_LX:
                lxp[pl.ds(pl.multiple_of(r0 + 8, 8), ROW_TILE), :] = res
            elif c0 == COL_LG:
                proj[pl.ds(r0, ROW_TILE), PROJ_LG:PROJ_LG + LRU_WIDTH] = res
            else:
                proj[pl.ds(r0, ROW_TILE), c0:c0 + PROJ_STEP] = res
        return carry

    l0 = lbl_ref[0]
    l1 = lbl_ref[1]
    lmx = jnp.maximum(l0, l1)
    e0 = jnp.exp(l0 - lmx)
    e1 = jnp.exp(l1 - lmx)
    lb_all = e0 / (e0 + e1)

    def hg_prep(c, carry):
        rows = pl.ds(pl.multiple_of(c * L, L), L)
        for hd in range(HG_HEADS):
            vts[c, hd] = proj[rows, COL_V + hd * HG_DK:COL_V + (hd + 1) * HG_DK].T.astype(BF16)
        for d in range(N_DIR):
            for hd in range(HG_HEADS):
                idx = d * HG_HEADS + hd
                cq = COL_Q + idx * HG_DK
                cf = COL_F + idx * HG_DK
                hq = proj[rows, cq:cq + HG_DK]
                fz = proj[rows, cf:cf + HG_DK]
                proj[rows, cq:cq + HG_DK] = hq * _sig(hq)
                sg = _sig(fz)
                lb = lb_all[idx:idx + 1, :]
                oml = 1.0 - lb
                ksc[rows, idx * HG_DK:(idx + 1) * HG_DK] = oml * (1.0 - sg)
                logf = jnp.log(lb + oml * sg)
                p1 = logf.astype(BF16)
                p2 = (logf - p1.astype(F32)).astype(BF16)
                bb = jnp.dot(tri_ref[d], jnp.concatenate([p1, p2], axis=1), preferred_element_type=F32)
                proj[rows, cf:cf + HG_DK] = (bb[:, 0:HG_DK] + bb[:, HG_DK:2 * HG_DK]) * LOG2E
        return carry

    lax.fori_loop(0, T // ROW_TILE, proj_body, 0)

    S = T // LRU_SEGS
    cw = convw_ref[...]
    cb = convb_ref[...]

    def irows(i0, j, n):
        return pl.ds(pl.multiple_of(LRU_SEGS * i0, 8) + j, n, stride=LRU_SEGS)

    for j in range(LRU_SEGS):
        def conv_body(ti, carry, j=j):
            i0 = pl.multiple_of(ti * CONV_TILE, CONV_TILE)
            win = lxp[pl.ds(pl.multiple_of(j * S + i0, 8), CONV_TILE + 16), :]
            xc = cb
            for tap in range(4):
                xc = xc + win[6 + tap:6 + tap + CONV_TILE] * cw[tap:tap + 1]
            for l in range(LRU_WIDTH // 128):
                xci[l, irows(i0, j, CONV_TILE), :] = xc[:, l * 128:(l + 1) * 128]
            return carry

        lax.fori_loop(0, S // CONV_TILE, conv_body, 0, unroll=S // CONV_TILE <= MAX_INLINE_TRIPS)

    n_grp = LRU_WIDTH // LRU_GROUP
    slabs = LRU_GROUP // 128
    gates = lrs if fuse_scan else proj

    def xc_tile(rows, g):
        return jnp.concatenate([xci[g * slabs + l, rows, :] for l in range(slabs)], axis=1)

    def gate_body(i, carry):
        rows = pl.ds(pl.multiple_of(i * ROW_TILE, ROW_TILE), ROW_TILE)
        for g in range(n_grp):
            gates[rows, g * GATE_G:(g + 1) * GATE_G] = jnp.dot(
                xc_tile(rows, g).astype(BF16), wg[g], preferred_element_type=F32)
        return carry

    lam = lam_ref[...]
    nl = -lam
    c8 = -LRU_C * (jnp.maximum(nl, 0.0) + jnp.log1p(jnp.exp(-jnp.abs(nl))))
    rowi2 = lax.broadcasted_iota(jnp.int32, (8, LRU_GROUP), 0)

    for d in range(N_DIR):
        for g in range(n_grp):
            k3 = 3 * (d * n_grp + g)
            chans = slice(g * LRU_GROUP, (g + 1) * LRU_GROUP)
            for r, row in enumerate((ba_ref[d:d + 1, chans], bx_ref[d:d + 1, chans], c8[d:d + 1, chans])):
                rowc[k3 + r] = jnp.broadcast_to(row, (LRU_SEGS, LRU_GROUP))

    def seg_out(d, rows, g, part):
        if fuse_scan:
            col = LRS_SEG_COL + ((d * n_grp + g) * 2 + part) * LRU_GROUP
            return lrs.at[rows, col:col + LRU_GROUP]
        cols = slice((g * 2 + part) * LRU_GROUP, (g * 2 + part + 1) * LRU_GROUP)
        return (ksc.at[rows, cols] if d == 0 else x1_ref.at[0, rows, cols])

    def lru_inputs(rows8, g, d):
        base = g * GATE_G + d * GATE_D
        k3 = 3 * (d * n_grp + g)
        ga = gates[rows8, base:base + LRU_GROUP] + rowc[k3]
        gx = gates[rows8, base + LRU_GROUP:base + 2 * LRU_GROUP] + rowc[k3 + 1]
        xc8 = xc_tile(rows8, g)
        log_a = rowc[k3 + 2] * _sig(ga)
        a = jnp.exp(log_a)
        z = jnp.tanh(-log_a) * (1.0 + a * a)
        mult = jnp.where(z > 0.0, z * lax.rsqrt(z), 0.0)
        return a, mult * (_sig(gx) * xc8)

    def scan8(a, u, d):
        for sft in (1, 2, 4):
            if d == 0:
                keep = rowi2 >= sft
                amt = sft
            else:
                keep = rowi2 < 8 - sft
                amt = 8 - sft
            ash = jnp.where(keep, pltpu.roll(a, amt, 0), 1.0)
            ush = jnp.where(keep, pltpu.roll(u, amt, 0), 0.0)
            u = a * ush + u
            a = a * ash
        return a, u

    def scan_steps(first, count, carry):
        hs, ds = [list(c) for c in carry[:2]], [list(c) for c in carry[2:]]
        for u in range(count):
            i_f = first + u
            for d, i in ((0, i_f), (1, S - 1 - i_f)):
                rows8 = pl.ds(pl.multiple_of(LRU_SEGS * i, 8), 8)
                for g in range(n_grp):
                    a, uu = lru_inputs(rows8, g, d)
                    hs[d][g] = a * hs[d][g] + uu
                    ds[d][g] = a * ds[d][g]
                    seg_out(d, rows8, g, 0)[...] = hs[d][g]
                    seg_out(d, rows8, g, 1)[...] = ds[d][g]
        return tuple(tuple(c) for c in hs + ds)

    zero8 = jnp.zeros((LRU_SEGS, LRU_GROUP), F32)
    one8 = jnp.ones((LRU_SEGS, LRU_GROUP), F32)
    scan_init = ((zero8,) * n_grp, (zero8,) * n_grp, (one8,) * n_grp, (one8,) * n_grp)
    if fuse_scan:
        lax.fori_loop(0, T // ROW_TILE, gate_body, 0)

    rowi = lax.broadcasted_iota(jnp.int32, (8, HG_DK), 0)
    r4 = rowi & 3
    is_r0 = r4 == 0
    is_r1 = r4 == 1
    is_r2 = r4 == 2
    hi4 = rowi >= 4

    for i in range(N_DIR * HG_HEADS):
        if has_state:
            st[i] = hs0_ref[0, i].T
        else:
            st[i] = jnp.zeros((HG_DK, HG_DK), F32)

    def hg_chunk(c, d):
        r0 = pl.multiple_of(c * L, L)
        rows = pl.ds(r0, L)
        for hd in range(HG_HEADS):
            idx = d * HG_HEADS + hd
            cq = COL_Q + idx * HG_DK
            cf = COL_F + idx * HG_DK
            ck = idx * HG_DK
            slot = idx % STAGE_SLOTS

            def ldq(lo, n, cq=cq):
                return proj[pl.ds(pl.multiple_of(r0 + lo, 8), n), cq:cq + HG_DK]

            def ldb(lo, n, cf=cf):
                return proj[pl.ds(pl.multiple_of(r0 + lo, 8), n), cf:cf + HG_DK]

            def ldk(lo, n, ck=ck):
                return ksc[pl.ds(pl.multiple_of(r0 + lo, 8), n), ck:ck + HG_DK]

            def bline(r, cf=cf):
                grp = proj[pl.ds(pl.multiple_of(r0 + 8 * (r // 8), 8), 8), cf:cf + HG_DK]
                return grp[r % 8:r % 8 + 1, :]

            def brow(r, n):
                return jnp.broadcast_to(bline(r), (n, HG_DK))

            acc_rows = [None] * (L // 8)

            def accumulate(li, p, row0):
                for i in range(p.shape[0] // 8):
                    g = row0 // 8 + i
                    term = masks_ref[d, li, 8 * g:8 * g + 8, :] * p[8 * i:8 * i + 8]
                    acc_rows[g] = term if acc_rows[g] is None else acc_rows[g] + term

            accumulate(LV_DIAG, jnp.dot(
                ldq(0, L).astype(BF16), _staged_transpose(ldk(0, L).astype(BF16), tsl.at[slot, LV_DIAG]),
                preferred_element_type=F32), 0)
            for li, h in enumerate(LEVELS[:LV_4]):
                pieces, q_pieces, q_starts = [], [], []
                for j in range(L // (2 * h)):
                    lo = j * 2 * h
                    mid = lo + h
                    if d == 0:
                        bm = brow(mid - 1, h)
                        kp = ldk(lo, h) * jnp.exp2(bm - ldb(lo, h))
                        qp = ldq(mid, h) * jnp.exp2(ldb(mid, h) - bm)
                        pieces += [kp, qp]
                        q_starts.append(mid)
                    else:
                        bm = brow(mid, h)
                        qp = ldq(lo, h) * jnp.exp2(ldb(lo, h) - bm)
                        kp = ldk(mid, h) * jnp.exp2(bm - ldb(mid, h))
                        pieces += [qp, kp]
                        q_starts.append(lo)
                    q_pieces.append(qp)
                xt = _staged_transpose(jnp.concatenate(pieces, axis=0).astype(BF16), tsl.at[slot,li])
                p = jnp.dot(jnp.concatenate(q_pieces, axis=0).astype(BF16), xt, preferred_element_type=F32)
                for j, row0 in enumerate(q_starts):
                    accumulate(li, p[j * h:(j + 1) * h], row0)
            x4, xq21, xk21 = [], [], []
            for g in range(L // 8):
                qg, kg, bg = ldq(8 * g, 8), ldk(8 * g, 8), ldb(8 * g, 8)
                fg = 1.0 - kg
                qfg = qg * fg
                mid = 8 * g + (3 if d == 0 else 4)
                e4 = jnp.exp2(-jnp.abs(bg - brow(mid, 8)))
                fnx = pltpu.roll(fg, 7, 0)
                fpv = pltpu.roll(fg, 1, 0)
                k_over_f = kg / fg
                if d == 0:
                    x4.append(jnp.where(hi4, qg, kg) * e4)
                    xq21.append(jnp.where(is_r0, 0.0, jnp.where(is_r1, qg, jnp.where(is_r2, qfg, qfg * fpv))))
                    xk21.append(jnp.where(is_r0, kg * fnx, jnp.where(is_r1, kg, jnp.where(is_r2, k_over_f, 0.0))))
                else:
                    x4.append(jnp.where(hi4, kg, qg) * e4)
                    xq21.append(jnp.where(is_r0, qfg * fnx, jnp.where(is_r1, qfg, jnp.where(is_r2, qg, 0.0))))
                    xk21.append(jnp.where(is_r0, 0.0, jnp.where(is_r1, k_over_f, jnp.where(is_r2, kg, kg * fpv))))
            accumulate(LV_4, _gram(jnp.concatenate(x4, axis=0), tsl.at[slot, LV_4]), 0)
            accumulate(LV_21, jnp.dot(
                jnp.concatenate(xq21, axis=0).astype(BF16),
                _staged_transpose(jnp.concatenate(xk21, axis=0).astype(BF16), tsl.at[slot, LV_21]),
                preferred_element_type=F32), 0)
            acc = jnp.concatenate(acc_rows, axis=0)

            vt = vts[c, hd]
            st_t = st[idx]
            b = ldb(0, L)
            qt = (ldq(0, L) * jnp.exp2(b)).astype(BF16)
            vb = proj[rows, COL_V + hd * HG_DK:COL_V + (hd + 1) * HG_DK].astype(BF16)
            o = jnp.dot(jnp.concatenate([acc.astype(BF16), qt], axis=1),
                        jnp.concatenate([vb, _staged_transpose(st_t.astype(BF16), tsl.at[slot, LV_STATE])], axis=0),
                        preferred_element_type=F32)
            btot = bline(L - 1 if d == 0 else 0)
            kt = (ldk(0, L) * jnp.exp2(btot - b)).astype(BF16)
            st[idx] = st_t * jnp.exp2(btot) + jnp.dot(vt, kt, preferred_element_type=F32)
            x1_ref[0, rows, d * HG_WIDTH + hd * HG_DK:d * HG_WIDTH + (hd + 1) * HG_DK] = o

    n_chunks = T // L

    def hg_both(c, carry):
        hg_chunk(c, 0)
        hg_chunk(n_chunks - 1 - c, 1)
        if fuse_scan:
            steps = S // n_chunks
            carry = scan_steps(c * steps, steps, carry)
        return carry

    lax.fori_loop(0, n_chunks, hg_prep, 0, unroll=n_chunks <= MAX_INLINE_TRIPS)
    scan_state = lax.fori_loop(0, n_chunks, hg_both, scan_init if fuse_scan else 0)

    if emit_state:
        for i in range(N_DIR * HG_HEADS):
            hs_ref[0, i] = st[i].T

    def hg_fin(i, carry):
        rows = pl.ds(pl.multiple_of(i * L, L), L)
        for hd in range(HG_HEADS):
            cs = slice(hd * HG_DK, (hd + 1) * HG_DK)
            o = x1_ref[0, rows, cs] + x1_ref[0, rows, HG_WIDTH + hd * HG_DK:HG_WIDTH + (hd + 1) * HG_DK]
            ms = jnp.mean(o * o, axis=-1, keepdims=True)
            y = o * lax.rsqrt(ms + EPS) * hgg_ref[:, cs]
            gz = proj[rows, COL_GATE + hd * HG_DK:COL_GATE + (hd + 1) * HG_DK]
            mixin[rows, cs] = (y * (gz * _sig(gz))).astype(BF16)
        return carry

    lax.fori_loop(0, n_chunks, hg_fin, 0, unroll=n_chunks <= MAX_INLINE_TRIPS)

    if not fuse_scan:
        lax.fori_loop(0, T // ROW_TILE, gate_body, 0)
        scan_state = lax.fori_loop(
            0, S // SCAN_UNROLL, lambda n, carry: scan_steps(n * SCAN_UNROLL, SCAN_UNROLL, carry), scan_init)
    h_end, d_end = scan_state[:2], scan_state[2:]

    carry_in = [[None] * n_grp for _ in range(N_DIR)]
    for d in range(N_DIR):
        for g in range(n_grp):
            if has_state:
                h0 = ls0_ref[0, d:d + 1, g * LRU_GROUP:(g + 1) * LRU_GROUP]
            else:
                h0 = jnp.zeros((1, LRU_GROUP), F32)
            dd, hh = scan8(d_end[d][g], h_end[d][g], d)
            seg_end = hh + dd * h0
            if d == 0:
                carry_in[d][g] = jnp.where(rowi2 >= 1, pltpu.roll(seg_end, 1, 0), h0)
                last = seg_end[LRU_SEGS - 1:LRU_SEGS, :]
            else:
                carry_in[d][g] = jnp.where(rowi2 < LRU_SEGS - 1, pltpu.roll(seg_end, LRU_SEGS - 1, 0), h0)
                last = seg_end[0:1, :]
            if emit_state:
                ls_ref[0, d:d + 1, g * LRU_GROUP:(g + 1) * LRU_GROUP] = last

    def fix_body(i, carry):
        rows = pl.ds(pl.multiple_of(i * FIX_TILE, FIX_TILE), FIX_TILE)
        for g in range(n_grp):
            tot = None
            for d in range(N_DIR):
                cin = jnp.concatenate([carry_in[d][g]] * (FIX_TILE // LRU_SEGS), axis=0)
                h = seg_out(d, rows, g, 0)[...] + seg_out(d, rows, g, 1)[...] * cin
                tot = h if tot is None else tot + h
            for l in range(slabs):
                xci[g * slabs + l, rows, :] = tot[:, l * 128:(l + 1) * 128]
        return carry

    lax.fori_loop(0, T // FIX_TILE, fix_body, 0)

    for j in range(LRU_SEGS):
        def lru_fin(ti, carry, j=j):
            i0 = pl.multiple_of(ti * CONV_TILE, CONV_TILE)
            rows = pl.ds(pl.multiple_of(j * S + i0, CONV_TILE), CONV_TILE)
            hsum = jnp.concatenate([xci[l, irows(i0, j, CONV_TILE), :] for l in range(LRU_WIDTH // 128)], axis=1)
            lg = proj[rows, PROJ_LG:PROJ_LG + LRU_WIDTH]
            gl = lg * (0.5 * (1.0 + jnp.tanh(0.7978845608028654 * (lg + 0.044715 * (lg * lg * lg)))))
            mixin[rows, HG_WIDTH:HG_WIDTH + LRU_WIDTH] = (hsum * gl).astype(BF16)
            return carry

        lax.fori_loop(0, S // CONV_TILE, lru_fin, 0, unroll=S // CONV_TILE <= MAX_INLINE_TRIPS)

    def out_body(i, carry):
        rows = pl.ds(pl.multiple_of(i * ROW_TILE, ROW_TILE), ROW_TILE)
        mix = jnp.dot(mixin[rows, :], wout_ref[...], preferred_element_type=F32)
        x1_ref[0, rows, :] = load_x(i) + g1 * mix
        return carry

    lax.fori_loop(0, T // ROW_TILE, out_body, 0)

    if fuse_ffn:
        def ffn_body(i, carry):
            rows = pl.ds(pl.multiple_of(i * ROW_TILE, ROW_TILE), ROW_TILE)
            y_ref[0, rows, :] = _ffn_rows(x1_ref[0, rows, :], mrow, n2g_ref[...], fg_ref[...], w1_ref, w2_ref)
            return carry

        lax.fori_loop(0, T // ROW_TILE, ffn_body, 0)


def _const_spec(shape):
    nd = len(shape)
    return pl.BlockSpec(shape, lambda b, _n=nd: (0,) * _n, pipeline_mode=pl.Buffered(1))


def _nbytes(shape, dtype):
    return int(np.prod(shape)) * jnp.dtype(dtype).itemsize


def _mixer(x, m3, m_off, m_step, pos, consts, ffn_consts, states, emit_state):
    B, T, _ = x.shape
    has_pos = pos is not None
    has_state = states is not None
    scratch_shapes = [
        ((T, PROJ_COLS), F32),
        ((T + 16, LRU_WIDTH), F32),
        ((T, D_MODEL), BF16),
        ((8, HG_DK, HG_DK), F32),
        ((T, N_DIR * HG_HEADS * HG_DK), F32),
        ((LRU_WIDTH // 128, T, 128), F32),
        ((T // CHUNK, HG_HEADS, HG_DK, CHUNK), BF16),
        ((3 * N_DIR * (LRU_WIDTH // LRU_GROUP), LRU_SEGS, LRU_GROUP), F32),
        ((STAGE_SLOTS, LV_STATE + 1, HG_DK, CHUNK), BF16),
        ((LRU_WIDTH // LRU_GROUP, LRU_GROUP, 2 * N_DIR * LRU_GROUP), BF16),
    ]
    resident = (sum(_nbytes(s, d) for s, d in scratch_shapes)
                + sum(_nbytes(c.shape, c.dtype) for c in consts)
                + (sum(_nbytes(p.shape, p.dtype) for p in pos) if has_pos else 0))
    io_block = _nbytes((T, D_MODEL), F32)
    budget = VMEM_LIMIT - VMEM_HEADROOM
    lrs_shape = ((T, 2 * LRS_SEG_COL), F32)
    fuse_scan = resident + _nbytes(*lrs_shape) + 4 * io_block <= budget
    if fuse_scan:
        scratch_shapes.append(lrs_shape)
        resident += _nbytes(*lrs_shape)
    ffn_bytes = sum(_nbytes(c.shape, c.dtype) for c in ffn_consts) + io_block
    fuse_ffn = resident + ffn_bytes + 4 * io_block <= budget
    if fuse_ffn:
        scratch_shapes.insert(0, ((1, T, D_MODEL), F32))
        resident += ffn_bytes
    in_bufs = 2 if resident + 3 * io_block <= budget else 1
    out_bufs = 2 if resident + 4 * io_block <= budget else 1
    io_mode = pl.Buffered(out_bufs)
    in_specs = [
        pl.BlockSpec((1, T, D_MODEL), lambda b: (b, 0, 0), pipeline_mode=pl.Buffered(in_bufs)),
        pl.BlockSpec((1, 1, 6 * D_MODEL), lambda b: (m_off + m_step * b, 0, 0)),
    ]
    args = [x, m3]
    if has_pos:
        in_specs += [_const_spec(p.shape) for p in pos]
        args += list(pos)
    in_specs += [_const_spec(c.shape) for c in consts]
    args += list(consts)
    if fuse_ffn:
        in_specs += [_const_spec(c.shape) for c in ffn_consts]
        args += list(ffn_consts)
    if has_state:
        hs0, ls0 = states
        in_specs += [
            pl.BlockSpec((1, 8, HG_DK, HG_DK), lambda b: (b, 0, 0, 0)),
            pl.BlockSpec((1, N_DIR, LRU_WIDTH), lambda b: (b, 0, 0)),
        ]
        args += [hs0, ls0]
    out_shape = [jax.ShapeDtypeStruct((B, T, D_MODEL), F32)]
    out_specs = [pl.BlockSpec((1, T, D_MODEL), lambda b: (b, 0, 0), pipeline_mode=io_mode)]
    if emit_state:
        out_shape += [jax.ShapeDtypeStruct((B, 8, HG_DK, HG_DK), F32),
                      jax.ShapeDtypeStruct((B, N_DIR, LRU_WIDTH), F32)]
        out_specs += [pl.BlockSpec((1, 8, HG_DK, HG_DK), lambda b: (b, 0, 0, 0)),
                      pl.BlockSpec((1, N_DIR, LRU_WIDTH), lambda b: (b, 0, 0))]
    scratch = [pltpu.VMEM(s, d) for s, d in scratch_shapes]
    outs = pl.pallas_call(
        functools.partial(_mixer_kernel, T=T, has_pos=has_pos, has_state=has_state, emit_state=emit_state,
                          fuse_scan=fuse_scan, fuse_ffn=fuse_ffn),
        grid=(B,),
        in_specs=in_specs,
        out_specs=out_specs,
        out_shape=out_shape,
        scratch_shapes=scratch,
        compiler_params=pltpu.CompilerParams(
            dimension_semantics=("arbitrary",), vmem_limit_bytes=VMEM_LIMIT),
        name=f"mixer_t{T}",
    )(*args)
    return (outs[0], fuse_ffn) + tuple(outs[1:])


def _ffn_rows(x, mrow, n2g, fgain, w1_ref, w2_ref):
    sh2 = mrow[:, 3 * D_MODEL:4 * D_MODEL]
    sc2 = mrow[:, 4 * D_MODEL:5 * D_MODEL]
    g2 = mrow[:, 5 * D_MODEL:6 * D_MODEL]
    ms = jnp.mean(x * x, axis=-1, keepdims=True)
    hb = (x * lax.rsqrt(ms + EPS) * (n2g * (1.0 + sc2)) + sh2).astype(BF16)
    ff = jnp.zeros(x.shape, F32)
    for c in range(D_FF // FF_CHUNK):
        a = jnp.dot(hb, w1_ref[:, c * FF_CHUNK:(c + 1) * FF_CHUNK], preferred_element_type=F32)
        a = jnp.maximum(a, 0.0)
        ff = ff + jnp.dot((a * a).astype(BF16), w2_ref[c * FF_CHUNK:(c + 1) * FF_CHUNK, :],
                          preferred_element_type=F32)
    x2 = x + g2 * ff
    ms2 = jnp.mean(x2 * x2, axis=-1, keepdims=True)
    return x2 * lax.rsqrt(ms2 + EPS) * fgain


def _ffn_kernel(x_ref, m_ref, n2g_ref, fg_ref, w1_ref, w2_ref, y_ref):
    y_ref[...] = _ffn_rows(x_ref[...], m_ref[0], n2g_ref[...], fg_ref[...], w1_ref, w2_ref)


def _ffn(x1, m3, m_off, tiles_per_cond, n2g, fgain, w1, w2):
    n = x1.shape[0]

    def m_index(i):
        if tiles_per_cond is None:
            return (m_off, 0, 0)
        return (m_off + i // tiles_per_cond, 0, 0)

    return pl.pallas_call(
        _ffn_kernel,
        grid=(n // FFN_TILE,),
        in_specs=[
            pl.BlockSpec((FFN_TILE, D_MODEL), lambda i: (i, 0)),
            pl.BlockSpec((1, 1, 6 * D_MODEL), m_index),
            _const_spec(n2g.shape),
            _const_spec(fgain.shape),
            _const_spec(w1.shape),
            _const_spec(w2.shape),
        ],
        out_specs=pl.BlockSpec((FFN_TILE, D_MODEL), lambda i: (i, 0)),
        out_shape=jax.ShapeDtypeStruct((n, D_MODEL), F32),
        compiler_params=pltpu.CompilerParams(
            dimension_semantics=("arbitrary",), vmem_limit_bytes=VMEM_LIMIT),
        name="ffn",
    )(x1, m3, n2g, fgain, w1, w2)


def _grid_pos_tables(n_tok):
    quarter = D_MODEL // 4
    omega = (1.0 / (np.float32(POS_BASE) ** (np.arange(quarter, dtype=np.float32) / np.float32(quarter)))
             ).astype(np.float32)

    def emb(n):
        ang = np.arange(n).reshape(-1, 1).astype(np.float32) * omega
        return np.concatenate([np.sin(ang), np.cos(ang)], axis=-1)

    rows = np.repeat(emb(n_tok // GRID_W)[:, None, :], 8, axis=1)
    return jnp.asarray(rows, dtype=F32), jnp.asarray(emb(GRID_W), dtype=F32)


def kernel(x_prompt, x_sample, c, state_hgrn, state_rglru, c_ctx, w_ada, b_ada, norm1_gain, norm2_gain,
           w_in, hg_lb_logits, hg_norm_gain, conv_w, conv_b, lru_wa, lru_ba, lru_wx, lru_bx, lru_lambda,
           w_out, w_ff1, w_ff2, final_gain):
    bp, tp, _ = x_prompt.shape
    bs_, ts, _ = x_sample.shape

    m3 = _modulation(c_ctx, c, w_ada[0], b_ada)

    masks_np, tri_np = _level_tables()
    consts = [
        norm1_gain,
        w_in[0].astype(BF16),
        hg_lb_logits.reshape(2, N_DIR * HG_HEADS, HG_DK),
        hg_norm_gain[0].reshape(1, HG_WIDTH),
        jnp.asarray(masks_np),
        jnp.asarray(tri_np, dtype=BF16),
        conv_w[0],
        conv_b,
        lru_wa[0].reshape(N_DIR * LRU_BLOCKS, LRU_BLOCK, LRU_BLOCK),
        lru_wx[0].reshape(N_DIR * LRU_BLOCKS, LRU_BLOCK, LRU_BLOCK),
        lru_ba[0],
        lru_bx[0],
        lru_lambda[0],
        w_out[0].astype(BF16),
    ]
    w1 = w_ff1[0].astype(BF16)
    w2 = w_ff2[0].astype(BF16)
    fgain = final_gain.reshape(1, D_MODEL)
    ffn_consts = [norm2_gain, fgain, w1, w2]

    y_prompt, done, hs, ls = _mixer(x_prompt, m3, 0, 0, None, consts, ffn_consts, None, True)
    if not done:
        y_prompt = _ffn(y_prompt.reshape(bp * tp, D_MODEL), m3, 0, None, *ffn_consts)

    y_sample, done = _mixer(x_sample, m3, 1, 1, _grid_pos_tables(ts), consts, ffn_consts,
                            (state_hgrn.reshape(bs_, N_DIR * HG_HEADS, HG_DK, HG_DK),
                             state_rglru.reshape(bs_, N_DIR, LRU_WIDTH)), False)
    if not done:
        y_sample = _ffn(y_sample.reshape(bs_ * ts, D_MODEL), m3, 1, ts // FFN_TILE, *ffn_consts)

    return (y_prompt.reshape(bp, tp, D_MODEL),
            y_sample.reshape(bs_, ts, D_MODEL),
            hs.reshape(bp, 1, N_DIR, HG_HEADS, HG_DK, HG_DK),
            ls.reshape(bp, 1, N_DIR, LRU_WIDTH))
```

```python
import functools

import numpy as np
import jax
import jax.numpy as jnp
from jax import lax
from jax.experimental import pallas as pl
from jax.experimental.pallas import tpu as pltpu

F32 = jnp.float32
BF16 = jnp.bfloat16

D_MODEL = 1024
N_DIR = 2
HG_HEADS = 4
HG_DK = 128
HG_WIDTH = 512
LRU_WIDTH = 512
LRU_BLOCKS = 8
LRU_BLOCK = 64
LRU_C = 8.0
D_FF = 4096
IN_COLS = 4096
EPS = 1e-6
LOG2E = 1.4426950408889634
GRID_W = 64
POS_BASE = 10000.0

COL_Q = 0
COL_F = 1024
COL_V = 2048
COL_GATE = 2560
COL_LX = 3072
COL_LG = 3584
PROJ_LG = COL_LX
PROJ_COLS = IN_COLS - LRU_WIDTH

CHUNK = 128
LEVELS = (64, 32, 16, 8, 4)
LV_4 = LEVELS.index(4)
LV_21 = len(LEVELS)
LV_DIAG = LV_21 + 1
LV_STATE = LV_DIAG + 1
N_MASKS = LV_DIAG + 1
ROW_TILE = 256
LRU_SEGS = 8
CONV_TILE = 32
SCAN_UNROLL = 4
FIX_TILE = 64
STAGE_SLOTS = 1
MAX_INLINE_TRIPS = 2
LRU_GROUP = 256
GATE_D = 2 * LRU_GROUP
GATE_G = N_DIR * GATE_D
LRS_SEG_COL = 2 * N_DIR * LRU_WIDTH
PROJ_STEP = 512
FFN_TILE = 512
FF_CHUNK = 1024
MOD_STEPS = 8
MOD_ROWS = 8
VMEM_LIMIT = 58 * 1024 * 1024
VMEM_HEADROOM = 6 * 1024 * 1024


def _sig(x):
    return jax.nn.sigmoid(x)


def _nt_dot(a, b):
    return lax.dot_general(a, b, (((1,), (1,)), ((), ())), preferred_element_type=F32)


def _staged_transpose(xb, slot_ref):
    slot_ref[...] = xb.T
    return slot_ref[...]


def _gram(x, slot_ref):
    xb = x.astype(BF16)
    return jnp.dot(xb, _staged_transpose(xb, slot_ref), preferred_element_type=F32)


def _level_tables():
    t = np.arange(CHUNK)[:, None]
    s = np.arange(CHUNK)[None, :]
    masks = np.zeros((N_DIR, N_MASKS, CHUNK, CHUNK), np.float32)
    for li, h in enumerate(LEVELS):
        same = (t // (2 * h)) == (s // (2 * h))
        t_hi = (t // h) % 2 == 1
        s_hi = (s // h) % 2 == 1
        masks[0, li] = same & t_hi & ~s_hi
        masks[1, li] = same & ~t_hi & s_hi
    same4 = (t // 4) == (s // 4)
    masks[0, LV_21] = same4 & (s < t)
    masks[1, LV_21] = same4 & (s > t)
    masks[:, LV_DIAG] = (t == s)
    tri = np.stack([(s <= t), (s >= t)]).astype(np.float32)
    return masks, tri


def _mod_kernel(cctx_ref, c_ref, w_ref, b_ref, *rest):
    n_w = (len(rest) - 2) // 2
    f32_refs, o_ref, bf16_refs, cond = rest[:n_w], rest[n_w], rest[n_w + 1:2 * n_w + 1], rest[-1]
    n = c_ref.shape[0]
    cond[...] = jnp.zeros(cond.shape, F32)
    cond[0:1, :] = cctx_ref[...]
    cond[1:1 + n, :] = c_ref[...]
    c = cond[...]
    a = (c * _sig(c)).astype(BF16)
    m = jnp.dot(a, w_ref[...].astype(BF16), preferred_element_type=F32) + b_ref[...]
    for r in range(MOD_ROWS):
        o_ref[r] = m[r:r + 1]
    for src, dst in zip(f32_refs, bf16_refs):
        dst[...] = src[...].astype(BF16)


def _modulation(c_ctx, c, w_ada, b_ada, weights):
    n = w_ada.shape[1]
    assert 1 + c.shape[0] <= MOD_ROWS
    tile = n // MOD_STEPS
    slab = lambda w: pl.BlockSpec((w.shape[0] // MOD_STEPS, w.shape[1]), lambda j: (j, 0))
    outs = pl.pallas_call(
        _mod_kernel,
        grid=(MOD_STEPS,),
        in_specs=[
            pl.BlockSpec((1, D_MODEL), lambda j: (0, 0)),
            pl.BlockSpec(c.shape, lambda j: (0, 0)),
            pl.BlockSpec((D_MODEL, tile), lambda j: (0, j)),
            pl.BlockSpec((1, tile), lambda j: (0, j)),
        ] + [slab(w) for w in weights],
        out_specs=[pl.BlockSpec((MOD_ROWS, 1, tile), lambda j: (0, 0, j))] + [slab(w) for w in weights],
        out_shape=[jax.ShapeDtypeStruct((MOD_ROWS, 1, n), F32)] + [
            jax.ShapeDtypeStruct(w.shape, BF16) for w in weights],
        scratch_shapes=[pltpu.VMEM((MOD_ROWS, D_MODEL), F32)],
        compiler_params=pltpu.CompilerParams(
            dimension_semantics=("arbitrary",), vmem_limit_bytes=VMEM_LIMIT),
        name="adaln_modulation",
    )(c_ctx.reshape(1, D_MODEL), c, w_ada, b_ada, *weights)
    return outs[0], outs[1:]


def _mixer_kernel(*refs, T, has_pos, has_state, emit_state, fuse_scan, fuse_ffn):
    it = iter(refs)
    x_ref = next(it)
    m_ref = next(it)
    if has_pos:
        posr_ref = next(it)
        posc_ref = next(it)
    n1g_ref = next(it)
    win_ref = next(it)
    lbl_ref = next(it)
    hgg_ref = next(it)
    masks_ref = next(it)
    tri_ref = next(it)
    convw_ref = next(it)
    convb_ref = next(it)
    wa_ref = next(it)
    wx_ref = next(it)
    ba_ref = next(it)
    bx_ref = next(it)
    lam_ref = next(it)
    wout_ref = next(it)
    if fuse_ffn:
        n2g_ref = next(it)
        fg_ref = next(it)
        w1_ref = next(it)
        w2_ref = next(it)
    if has_state:
        hs0_ref = next(it)
        ls0_ref = next(it)
    y_ref = next(it)
    if emit_state:
        hs_ref = next(it)
        ls_ref = next(it)
    x1_ref = next(it) if fuse_ffn else y_ref
    proj = next(it)
    lxp = next(it)
    mixin = next(it)
    st = next(it)
    ksc = next(it)
    xci = next(it)
    vts = next(it)
    rowc = next(it)
    tsl = next(it)
    wg = next(it)
    if fuse_scan:
        lrs = next(it)

    L = CHUNK

    @pl.when(pl.program_id(0) == 0)
    def _build_gate_weights():
        per_group = LRU_GROUP // LRU_BLOCK
        wg[...] = jnp.zeros(wg.shape, BF16)
        for g in range(LRU_WIDTH // LRU_GROUP):
            for p in range(2 * N_DIR):
                src = wa_ref if p % 2 == 0 else wx_ref
                for n in range(per_group):
                    r = n * LRU_BLOCK
                    col = p * LRU_GROUP + r
                    wg[g, r:r + LRU_BLOCK, col:col + LRU_BLOCK] = src[
                        (p // 2) * LRU_BLOCKS + g * per_group + n].astype(BF16)

    mrow = m_ref[0]
    sh1 = mrow[:, 0:D_MODEL]
    sc1 = mrow[:, D_MODEL:2 * D_MODEL]
    g1 = mrow[:, 2 * D_MODEL:3 * D_MODEL]
    gain1 = n1g_ref[...] * (1.0 + sc1)

    zrows = jnp.zeros((8, LRU_WIDTH), F32)
    lxp[0:8, :] = zrows
    lxp[T + 8:T + 16, :] = zrows

    def load_x(i):
        xt = x_ref[0, pl.ds(pl.multiple_of(i * ROW_TILE, ROW_TILE), ROW_TILE), :]
        if has_pos:
            per_tile = ROW_TILE // GRID_W
            tiles = []
            for s in range(per_tile):
                row_emb = jnp.concatenate([posr_ref[i * per_tile + s]] * (GRID_W // 8), axis=0)
                tiles.append(jnp.concatenate([row_emb, posc_ref[...]], axis=1))
            xt = xt + jnp.concatenate(tiles, axis=0)
        return xt

    def proj_body(i, carry):
        r0 = pl.multiple_of(i * ROW_TILE, ROW_TILE)
        xt = load_x(i)
        ms = jnp.mean(xt * xt, axis=-1, keepdims=True)
        hb = (xt * lax.rsqrt(ms + EPS) * gain1 + sh1).astype(BF16)
        starts = range(0, IN_COLS, PROJ_STEP)
        order = [COL_LX] + [c0 for c0 in starts if c0 < COL_GATE] + [
            c0 for c0 in starts if c0 >= COL_GATE and c0 != COL_LX]
        for c0 in order:
            res = jnp.dot(hb, win_ref[:, c0:c0 + PROJ_STEP], preferred_element_type=F32)
            if c0 == COL_LX:
                lxp[pl.ds(pl.multiple_of(r0 + 8, 8), ROW_TILE), :] = res
            elif c0 == COL_LG:
                proj[pl.ds(r0, ROW_TILE), PROJ_LG:PROJ_LG + LRU_WIDTH] = res
            else:
                proj[pl.ds(r0, ROW_TILE), c0:c0 + PROJ_STEP] = res
        return carry

    l0 = lbl_ref[0]
    l1 = lbl_ref[1]
    lmx = jnp.maximum(l0, l1)
    e0 = jnp.exp(l0 - lmx)
    e1 = jnp.exp(l1 - lmx)
    lb_all = e0 / (e0 + e1)

    def hg_prep(c, carry):
        rows = pl.ds(pl.multiple_of(c * L, L), L)
        for hd in range(HG_HEADS):
            vts[c, hd] = proj[rows, COL_V + hd * HG_DK:COL_V + (hd + 1) * HG_DK].T.astype(BF16)
        for d in range(N_DIR):
            for hd in range(HG_HEADS):
                idx = d * HG_HEADS + hd
                cq = COL_Q + idx * HG_DK
                cf = COL_F + idx * HG_DK
                hq = proj[rows, cq:cq + HG_DK]
                fz = proj[rows, cf:cf + HG_DK]
                proj[rows, cq:cq + HG_DK] = hq * _sig(hq)
                sg = _sig(fz)
                lb = lb_all[idx:idx + 1, :]
                oml = 1.0 - lb
                ksc[rows, idx * HG_DK:(idx + 1) * HG_DK] = oml * (1.0 - sg)
                logf = jnp.log(lb + oml * sg)
                p1 = logf.astype(BF16)
                p2 = (logf - p1.astype(F32)).astype(BF16)
                bb = jnp.dot(tri_ref[d], jnp.concatenate([p1, p2], axis=1), preferred_element_type=F32)
                proj[rows, cf:cf + HG_DK] = (bb[:, 0:HG_DK] + bb[:, HG_DK:2 * HG_DK]) * LOG2E
        return carry

    lax.fori_loop(0, T // ROW_TILE, proj_body, 0)

    S = T // LRU_SEGS
    cw = convw_ref[...]
    cb = convb_ref[...]

    def irows(i0, j, n):
        return pl.ds(pl.multiple_of(LRU_SEGS * i0, 8) + j, n, stride=LRU_SEGS)

    for j in range(LRU_SEGS):
        def conv_body(ti, carry, j=j):
            i0 = pl.multiple_of(ti * CONV_TILE, CONV_TILE)
            win = lxp[pl.ds(pl.multiple_of(j * S + i0, 8), CONV_TILE + 16), :]
            xc = cb
            for tap in range(4):
                xc = xc + win[6 + tap:6 + tap + CONV_TILE] * cw[tap:tap + 1]
            for l in range(LRU_WIDTH // 128):
                xci[l, irows(i0, j, CONV_TILE), :] = xc[:, l * 128:(l + 1) * 128]
            return carry

        lax.fori_loop(0, S // CONV_TILE, conv_body, 0, unroll=S // CONV_TILE <= MAX_INLINE_TRIPS)

    n_grp = LRU_WIDTH // LRU_GROUP
    slabs = LRU_GROUP // 128
    gates = lrs if fuse_scan else proj

    def xc_tile(rows, g):
        return jnp.concatenate([xci[g * slabs + l, rows, :] for l in range(slabs)], axis=1)

    def gate_body(i, carry):
        rows = pl.ds(pl.multiple_of(i * ROW_TILE, ROW_TILE), ROW_TILE)
        for g in range(n_grp):
            gates[rows, g * GATE_G:(g + 1) * GATE_G] = jnp.dot(
                xc_tile(rows, g).astype(BF16), wg[g], preferred_element_type=F32)
        return carry

    lam = lam_ref[...]
    nl = -lam
    c8 = -LRU_C * (jnp.maximum(nl, 0.0) + jnp.log1p(jnp.exp(-jnp.abs(nl))))
    rowi2 = lax.broadcasted_iota(jnp.int32, (8, LRU_GROUP), 0)

    for d in range(N_DIR):
        for g in range(n_grp):
            k3 = 3 * (d * n_grp + g)
            chans = slice(g * LRU_GROUP, (g + 1) * LRU_GROUP)
            for r, row in enumerate((ba_ref[d:d + 1, chans], bx_ref[d:d + 1, chans], c8[d:d + 1, chans])):
                rowc[k3 + r] = jnp.broadcast_to(row, (LRU_SEGS, LRU_GROUP))

    def seg_out(d, rows, g, part):
        if fuse_scan:
            col = LRS_SEG_COL + ((d * n_grp + g) * 2 + part) * LRU_GROUP
            return lrs.at[rows, col:col + LRU_GROUP]
        cols = slice((g * 2 + part) * LRU_GROUP, (g * 2 + part + 1) * LRU_GROUP)
        return (ksc.at[rows, cols] if d == 0 else x1_ref.at[0, rows, cols])

    def lru_inputs(rows8, g, d):
        base = g * GATE_G + d * GATE_D
        k3 = 3 * (d * n_grp + g)
        ga = gates[rows8, base:base + LRU_GROUP] + rowc[k3]
        gx = gates[rows8, base + LRU_GROUP:base + 2 * LRU_GROUP] + rowc[k3 + 1]
        xc8 = xc_tile(rows8, g)
        log_a = rowc[k3 + 2] * _sig(ga)
        a = jnp.exp(log_a)
        z = jnp.tanh(-log_a) * (1.0 + a * a)
        mult = jnp.where(z > 0.0, z * lax.rsqrt(z), 0.0)
        return a, mult * (_sig(gx) * xc8)

    def scan8(a, u, d):
        for sft in (1, 2, 4):
            if d == 0:
                keep = rowi2 >= sft
                amt = sft
            else:
                keep = rowi2 < 8 - sft
                amt = 8 - sft
            ash = jnp.where(keep, pltpu.roll(a, amt, 0), 1.0)
            ush = jnp.where(keep, pltpu.roll(u, amt, 0), 0.0)
            u = a * ush + u
            a = a * ash
        return a, u

    def scan_steps(first, count, carry):
        hs, ds = [list(c) for c in carry[:2]], [list(c) for c in carry[2:]]
        for u in range(count):
            i_f = first + u
            for d, i in ((0, i_f), (1, S - 1 - i_f)):
                rows8 = pl.ds(pl.multiple_of(LRU_SEGS * i, 8), 8)
                for g in range(n_grp):
                    a, uu = lru_inputs(rows8, g, d)
                    hs[d][g] = a * hs[d][g] + uu
                    ds[d][g] = a * ds[d][g]
                    seg_out(d, rows8, g, 0)[...] = hs[d][g]
                    seg_out(d, rows8, g, 1)[...] = ds[d][g]
        return tuple(tuple(c) for c in hs + ds)

    zero8 = jnp.zeros((LRU_SEGS, LRU_GROUP), F32)
    one8 = jnp.ones((LRU_SEGS, LRU_GROUP), F32)
    scan_init = ((zero8,) * n_grp, (zero8,) * n_grp, (one8,) * n_grp, (one8,) * n_grp)
    if fuse_scan:
        lax.fori_loop(0, T // ROW_TILE, gate_body, 0)

    rowi = lax.broadcasted_iota(jnp.int32, (8, HG_DK), 0)
    r4 = rowi & 3
    is_r0 = r4 == 0
    is_r1 = r4 == 1
    is_r2 = r4 == 2
    hi4 = rowi >= 4

    for i in range(N_DIR * HG_HEADS):
        if has_state:
            st[i] = hs0_ref[0, i].T
        else:
            st[i] = jnp.zeros((HG_DK, HG_DK), F32)

    def hg_chunk(c, d):
        r0 = pl.multiple_of(c * L, L)
        rows = pl.ds(r0, L)
        for hd in range(HG_HEADS):
            idx = d * HG_HEADS + hd
            cq = COL_Q + idx * HG_DK
            cf = COL_F + idx * HG_DK
            ck = idx * HG_DK
            slot = idx % STAGE_SLOTS

            def ldq(lo, n, cq=cq):
                return proj[pl.ds(pl.multiple_of(r0 + lo, 8), n), cq:cq + HG_DK]

            def ldb(lo, n, cf=cf):
                return proj[pl.ds(pl.multiple_of(r0 + lo, 8), n), cf:cf + HG_DK]

            def ldk(lo, n, ck=ck):
                return ksc[pl.ds(pl.multiple_of(r0 + lo, 8), n), ck:ck + HG_DK]

            def bline(r, cf=cf):
                grp = proj[pl.ds(pl.multiple_of(r0 + 8 * (r // 8), 8), 8), cf:cf + HG_DK]
                return grp[r % 8:r % 8 + 1, :]

            def brow(r, n):
                return jnp.broadcast_to(bline(r), (n, HG_DK))

            acc_rows = [None] * (L // 8)

            def accumulate(li, p, row0):
                for i in range(p.shape[0] // 8):
                    g = row0 // 8 + i
                    term = masks_ref[d, li, 8 * g:8 * g + 8, :] * p[8 * i:8 * i + 8]
                    acc_rows[g] = term if acc_rows[g] is None else acc_rows[g] + term

            accumulate(LV_DIAG, jnp.dot(
                ldq(0, L).astype(BF16), _staged_transpose(ldk(0, L).astype(BF16), tsl.at[slot, LV_DIAG]),
                preferred_element_type=F32), 0)
            for li, h in enumerate(LEVELS[:LV_4]):
                pieces, q_pieces, q_starts = [], [], []
                for j in range(L // (2 * h)):
                    lo = j * 2 * h
                    mid = lo + h
                    if d == 0:
                        bm = brow(mid - 1, h)
                        kp = ldk(lo, h) * jnp.exp2(bm - ldb(lo, h))
                        qp = ldq(mid, h) * jnp.exp2(ldb(mid, h) - bm)
                        pieces += [kp, qp]
                        q_starts.append(mid)
                    else:
                        bm = brow(mid, h)
                        qp = ldq(lo, h) * jnp.exp2(ldb(lo, h) - bm)
                        kp = ldk(mid, h) * jnp.exp2(bm - ldb(mid, h))
                        pieces += [qp, kp]
                        q_starts.append(lo)
                    q_pieces.append(qp)
                xt = _staged_transpose(jnp.concatenate(pieces, axis=0).astype(BF16), tsl.at[slot,li])
                p = jnp.dot(jnp.concatenate(q_pieces, axis=0).astype(BF16), xt, preferred_element_type=F32)
                for j, row0 in enumerate(q_starts):
                    accumulate(li, p[j * h:(j + 1) * h], row0)
            x4, xq21, xk21 = [], [], []
            for g in range(L // 8):
                qg, kg, bg = ldq(8 * g, 8), ldk(8 * g, 8), ldb(8 * g, 8)
                fg = 1.0 - kg
                qfg = qg * fg
                mid = 8 * g + (3 if d == 0 else 4)
                e4 = jnp.exp2(-jnp.abs(bg - brow(mid, 8)))
                fnx = pltpu.roll(fg, 7, 0)
                fpv = pltpu.roll(fg, 1, 0)
                k_over_f = kg / fg
                if d == 0:
                    x4.append(jnp.where(hi4, qg, kg) * e4)
                    xq21.append(jnp.where(is_r0, 0.0, jnp.where(is_r1, qg, jnp.where(is_r2, qfg, qfg * fpv))))
                    xk21.append(jnp.where(is_r0, kg * fnx, jnp.where(is_r1, kg, jnp.where(is_r2, k_over_f, 0.0))))
                else:
                    x4.append(jnp.where(hi4, kg, qg) * e4)
                    xq21.append(jnp.where(is_r0, qfg * fnx, jnp.where(is_r1, qfg, jnp.where(is_r2, qg, 0.0))))
                    xk21.append(jnp.where(is_r0, 0.0, jnp.where(is_r1, k_over_f, jnp.where(is_r2, kg, kg * fpv))))
            accumulate(LV_4, _gram(jnp.concatenate(x4, axis=0), tsl.at[slot, LV_4]), 0)
            accumulate(LV_21, jnp.dot(
                jnp.concatenate(xq21, axis=0).astype(BF16),
                _staged_transpose(jnp.concatenate(xk21, axis=0).astype(BF16), tsl.at[slot, LV_21]),
                preferred_element_type=F32), 0)
            acc = jnp.concatenate(acc_rows, axis=0)

            vt = vts[c, hd]
            st_t = st[idx]
            b = ldb(0, L)
            qt = (ldq(0, L) * jnp.exp2(b)).astype(BF16)
            vb = proj[rows, COL_V + hd * HG_DK:COL_V + (hd + 1) * HG_DK].astype(BF16)
            o = jnp.dot(jnp.concatenate([acc.astype(BF16), qt], axis=1),
                        jnp.concatenate([vb, _staged_transpose(st_t.astype(BF16), tsl.at[slot, LV_STATE])], axis=0),
                        preferred_element_type=F32)
            btot = bline(L - 1 if d == 0 else 0)
            kt = (ldk(0, L) * jnp.exp2(btot - b)).astype(BF16)
            st[idx] = st_t * jnp.exp2(btot) + jnp.dot(vt, kt, preferred_element_type=F32)
            x1_ref[0, rows, d * HG_WIDTH + hd * HG_DK:d * HG_WIDTH + (hd + 1) * HG_DK] = o

    n_chunks = T // L

    def hg_both(c, carry):
        hg_chunk(c, 0)
        hg_chunk(n_chunks - 1 - c, 1)
        if fuse_scan:
            steps = S // n_chunks
            carry = scan_steps(c * steps, steps, carry)
        return carry

    lax.fori_loop(0, n_chunks, hg_prep, 0, unroll=n_chunks <= MAX_INLINE_TRIPS)
    scan_state = lax.fori_loop(0, n_chunks, hg_both, scan_init if fuse_scan else 0)

    if emit_state:
        for i in range(N_DIR * HG_HEADS):
            hs_ref[0, i] = st[i].T

    def hg_fin(i, carry):
        rows = pl.ds(pl.multiple_of(i * L, L), L)
        for hd in range(HG_HEADS):
            cs = slice(hd * HG_DK, (hd + 1) * HG_DK)
            o = x1_ref[0, rows, cs] + x1_ref[0, rows, HG_WIDTH + hd * HG_DK:HG_WIDTH + (hd + 1) * HG_DK]
            ms = jnp.mean(o * o, axis=-1, keepdims=True)
            y = o * lax.rsqrt(ms + EPS) * hgg_ref[:, cs]
            gz = proj[rows, COL_GATE + hd * HG_DK:COL_GATE + (hd + 1) * HG_DK]
            mixin[rows, cs] = (y * (gz * _sig(gz))).astype(BF16)
        return carry

    lax.fori_loop(0, n_chunks, hg_fin, 0, unroll=n_chunks <= MAX_INLINE_TRIPS)

    if not fuse_scan:
        lax.fori_loop(0, T // ROW_TILE, gate_body, 0)
        scan_state = lax.fori_loop(
            0, S // SCAN_UNROLL, lambda n, carry: scan_steps(n * SCAN_UNROLL, SCAN_UNROLL, carry), scan_init)
    h_end, d_end = scan_state[:2], scan_state[2:]

    carry_in = [[None] * n_grp for _ in range(N_DIR)]
    for d in range(N_DIR):
        for g in range(n_grp):
            if has_state:
                h0 = ls0_ref[0, d:d + 1, g * LRU_GROUP:(g + 1) * LRU_GROUP]
            else:
                h0 = jnp.zeros((1, LRU_GROUP), F32)
            dd, hh = scan8(d_end[d][g], h_end[d][g], d)
            seg_end = hh + dd * h0
            if d == 0:
                carry_in[d][g] = jnp.where(rowi2 >= 1, pltpu.roll(seg_end, 1, 0), h0)
                last = seg_end[LRU_SEGS - 1:LRU_SEGS, :]
            else:
                carry_in[d][g] = jnp.where(rowi2 < LRU_SEGS - 1, pltpu.roll(seg_end, LRU_SEGS - 1, 0), h0)
                last = seg_end[0:1, :]
            if emit_state:
                ls_ref[0, d:d + 1, g * LRU_GROUP:(g + 1) * LRU_GROUP] = last

    def fix_body(i, carry):
        rows = pl.ds(pl.multiple_of(i * FIX_TILE, FIX_TILE), FIX_TILE)
        for g in range(n_grp):
            tot = None
            for d in range(N_DIR):
                cin = jnp.concatenate([carry_in[d][g]] * (FIX_TILE // LRU_SEGS), axis=0)
                h = seg_out(d, rows, g, 0)[...] + seg_out(d, rows, g, 1)[...] * cin
                tot = h if tot is None else tot + h
            for l in range(slabs):
                xci[g * slabs + l, rows, :] = tot[:, l * 128:(l + 1) * 128]
        return carry

    lax.fori_loop(0, T // FIX_TILE, fix_body, 0)

    for j in range(LRU_SEGS):
        def lru_fin(ti, carry, j=j):
            i0 = pl.multiple_of(ti * CONV_TILE, CONV_TILE)
            rows = pl.ds(pl.multiple_of(j * S + i0, CONV_TILE), CONV_TILE)
            hsum = jnp.concatenate([xci[l, irows(i0, j, CONV_TILE), :] for l in range(LRU_WIDTH // 128)], axis=1)
            lg = proj[rows, PROJ_LG:PROJ_LG + LRU_WIDTH]
            gl = lg * (0.5 * (1.0 + jnp.tanh(0.7978845608028654 * (lg + 0.044715 * (lg * lg * lg)))))
            mixin[rows, HG_WIDTH:HG_WIDTH + LRU_WIDTH] = (hsum * gl).astype(BF16)
            return carry

        lax.fori_loop(0, S // CONV_TILE, lru_fin, 0, unroll=S // CONV_TILE <= MAX_INLINE_TRIPS)

    def out_body(i, carry):
        rows = pl.ds(pl.multiple_of(i * ROW_TILE, ROW_TILE), ROW_TILE)
        mix = jnp.dot(mixin[rows, :], wout_ref[...], preferred_element_type=F32)
        x1_ref[0, rows, :] = load_x(i) + g1 * mix
        return carry

    lax.fori_loop(0, T // ROW_TILE, out_body, 0)

    if fuse_ffn:
        def ffn_body(i, carry):
            rows = pl.ds(pl.multiple_of(i * ROW_TILE, ROW_TILE), ROW_TILE)
            y_ref[0, rows, :] = _ffn_rows(x1_ref[0, rows, :], mrow, n2g_ref[...], fg_ref[...], w1_ref, w2_ref)
            return carry

        lax.fori_loop(0, T // ROW_TILE, ffn_body, 0)


def _const_spec(shape):
    nd = len(shape)
    return pl.BlockSpec(shape, lambda b, _n=nd: (0,) * _n, pipeline_mode=pl.Buffered(1))


def _nbytes(shape, dtype):
    return int(np.prod(shape)) * jnp.dtype(dtype).itemsize


def _mixer(x, m3, m_off, m_step, pos, consts, ffn_consts, states, emit_state):
    B, T, _ = x.shape
    has_pos = pos is not None
    has_state = states is not None
    scratch_shapes = [
        ((T, PROJ_COLS), F32),
        ((T + 16, LRU_WIDTH), F32),
        ((T, D_MODEL), BF16),
        ((8, HG_DK, HG_DK), F32),
        ((T, N_DIR * HG_HEADS * HG_DK), F32),
        ((LRU_WIDTH // 128, T, 128), F32),
        ((T // CHUNK, HG_HEADS, HG_DK, CHUNK), BF16),
        ((3 * N_DIR * (LRU_WIDTH // LRU_GROUP), LRU_SEGS, LRU_GROUP), F32),
        ((STAGE_SLOTS, LV_STATE + 1, HG_DK, CHUNK), BF16),
        ((LRU_WIDTH // LRU_GROUP, LRU_GROUP, 2 * N_DIR * LRU_GROUP), BF16),
    ]
    resident = (sum(_nbytes(s, d) for s, d in scratch_shapes)
                + sum(_nbytes(c.shape, c.dtype) for c in consts)
                + (sum(_nbytes(p.shape, p.dtype) for p in pos) if has_pos else 0))
    io_block = _nbytes((T, D_MODEL), F32)
    budget = VMEM_LIMIT - VMEM_HEADROOM
    lrs_shape = ((T, 2 * LRS_SEG_COL), F32)
    fuse_scan = resident + _nbytes(*lrs_shape) + 4 * io_block <= budget
    if fuse_scan:
        scratch_shapes.append(lrs_shape)
        resident += _nbytes(*lrs_shape)
    ffn_bytes = sum(_nbytes(c.shape, c.dtype) for c in ffn_consts) + io_block
    fuse_ffn = resident + ffn_bytes + 4 * io_block <= budget
    if fuse_ffn:
        scratch_shapes.insert(0, ((1, T, D_MODEL), F32))
        resident += ffn_bytes
    in_bufs = 2 if resident + 3 * io_block <= budget else 1
    out_bufs = 2 if resident + 4 * io_block <= budget else 1
    io_mode = pl.Buffered(out_bufs)
    in_specs = [
        pl.BlockSpec((1, T, D_MODEL), lambda b: (b, 0, 0), pipeline_mode=pl.Buffered(in_bufs)),
        pl.BlockSpec((1, 1, 6 * D_MODEL), lambda b: (m_off + m_step * b, 0, 0)),
    ]
    args = [x, m3]
    if has_pos:
        in_specs += [_const_spec(p.shape) for p in pos]
        args += list(pos)
    in_specs += [_const_spec(c.shape) for c in consts]
    args += list(consts)
    if fuse_ffn:
        in_specs += [_const_spec(c.shape) for c in ffn_consts]
        args += list(ffn_consts)
    if has_state:
        hs0, ls0 = states
        in_specs += [
            pl.BlockSpec((1, 8, HG_DK, HG_DK), lambda b: (b, 0, 0, 0)),
            pl.BlockSpec((1, N_DIR, LRU_WIDTH), lambda b: (b, 0, 0)),
        ]
        args += [hs0, ls0]
    out_shape = [jax.ShapeDtypeStruct((B, T, D_MODEL), F32)]
    out_specs = [pl.BlockSpec((1, T, D_MODEL), lambda b: (b, 0, 0), pipeline_mode=io_mode)]
    if emit_state:
        out_shape += [jax.ShapeDtypeStruct((B, 8, HG_DK, HG_DK), F32),
                      jax.ShapeDtypeStruct((B, N_DIR, LRU_WIDTH), F32)]
        out_specs += [pl.BlockSpec((1, 8, HG_DK, HG_DK), lambda b: (b, 0, 0, 0)),
                      pl.BlockSpec((1, N_DIR, LRU_WIDTH), lambda b: (b, 0, 0))]
    scratch = [pltpu.VMEM(s, d) for s, d in scratch_shapes]
    outs = pl.pallas_call(
        functools.partial(_mixer_kernel, T=T, has_pos=has_pos, has_state=has_state, emit_state=emit_state,
                          fuse_scan=fuse_scan, fuse_ffn=fuse_ffn),
        grid=(B,),
        in_specs=in_specs,
        out_specs=out_specs,
        out_shape=out_shape,
        scratch_shapes=scratch,
        compiler_params=pltpu.CompilerParams(
            dimension_semantics=("arbitrary",), vmem_limit_bytes=VMEM_LIMIT),
        name=f"mixer_t{T}",
    )(*args)
    return (outs[0], fuse_ffn) + tuple(outs[1:])


def _ffn_rows(x, mrow, n2g, fgain, w1_ref, w2_ref):
    sh2 = mrow[:, 3 * D_MODEL:4 * D_MODEL]
    sc2 = mrow[:, 4 * D_MODEL:5 * D_MODEL]
    g2 = mrow[:, 5 * D_MODEL:6 * D_MODEL]
    ms = jnp.mean(x * x, axis=-1, keepdims=True)
    hb = (x * lax.rsqrt(ms + EPS) * (n2g * (1.0 + sc2)) + sh2).astype(BF16)
    ff = jnp.zeros(x.shape, F32)
    for c in range(D_FF // FF_CHUNK):
        a = jnp.dot(hb, w1_ref[:, c * FF_CHUNK:(c + 1) * FF_CHUNK], preferred_element_type=F32)
        a = jnp.maximum(a, 0.0)
        ff = ff + jnp.dot((a * a).astype(BF16), w2_ref[c * FF_CHUNK:(c + 1) * FF_CHUNK, :],
                          preferred_element_type=F32)
    x2 = x + g2 * ff
    ms2 = jnp.mean(x2 * x2, axis=-1, keepdims=True)
    return x2 * lax.rsqrt(ms2 + EPS) * fgain


def _ffn_kernel(x_ref, m_ref, n2g_ref, fg_ref, w1_ref, w2_ref, y_ref):
    y_ref[...] = _ffn_rows(x_ref[...], m_ref[0], n2g_ref[...], fg_ref[...], w1_ref, w2_ref)


def _ffn(x1, m3, m_off, tiles_per_cond, n2g, fgain, w1, w2):
    n = x1.shape[0]

    def m_index(i):
        if tiles_per_cond is None:
            return (m_off, 0, 0)
        return (m_off + i // tiles_per_cond, 0, 0)

    return pl.pallas_call(
        _ffn_kernel,
        grid=(n // FFN_TILE,),
        in_specs=[
            pl.BlockSpec((FFN_TILE, D_MODEL), lambda i: (i, 0)),
            pl.BlockSpec((1, 1, 6 * D_MODEL), m_index),
            _const_spec(n2g.shape),
            _const_spec(fgain.shape),
            _const_spec(w1.shape),
            _const_spec(w2.shape),
        ],
        out_specs=pl.BlockSpec((FFN_TILE, D_MODEL), lambda i: (i, 0)),
        out_shape=jax.ShapeDtypeStruct((n, D_MODEL), F32),
        compiler_params=pltpu.CompilerParams(
            dimension_semantics=("arbitrary",), vmem_limit_bytes=VMEM_LIMIT),
        name="ffn",
    )(x1, m3, n2g, fgain, w1, w2)


def _grid_pos_tables(n_tok):
    quarter = D_MODEL // 4
    omega = (1.0 / (np.float32(POS_BASE) ** (np.arange(quarter, dtype=np.float32) / np.float32(quarter)))
             ).astype(np.float32)

    def emb(n):
        ang = np.arange(n).reshape(-1, 1).astype(np.float32) * omega
        return np.concatenate([np.sin(ang), np.cos(ang)], axis=-1)

    rows = np.repeat(emb(n_tok // GRID_W)[:, None, :], 8, axis=1)
    return jnp.asarray(rows, dtype=F32), jnp.asarray(emb(GRID_W), dtype=F32)


def kernel(x_prompt, x_sample, c, state_hgrn, state_rglru, c_ctx, w_ada, b_ada, norm1_gain, norm2_gain,
           w_in, hg_lb_logits, hg_norm_gain, conv_w, conv_b, lru_wa, lru_ba, lru_wx, lru_bx, lru_lambda,
           w_out, w_ff1, w_ff2, final_gain):
    bp, tp, _ = x_prompt.shape
    bs_, ts, _ = x_sample.shape

    m3, (w_in_b, w_out_b, w1, w2) = _modulation(
        c_ctx, c, w_ada[0], b_ada, [w_in[0], w_out[0], w_ff1[0], w_ff2[0]])

    masks_np, tri_np = _level_tables()
    consts = [
        norm1_gain,
        w_in_b,
        hg_lb_logits.reshape(2, N_DIR * HG_HEADS, HG_DK),
        hg_norm_gain[0].reshape(1, HG_WIDTH),
        jnp.asarray(masks_np),
        jnp.asarray(tri_np, dtype=BF16),
        conv_w[0],
        conv_b,
        lru_wa[0].reshape(N_DIR * LRU_BLOCKS, LRU_BLOCK, LRU_BLOCK),
        lru_wx[0].reshape(N_DIR * LRU_BLOCKS, LRU_BLOCK, LRU_BLOCK),
        lru_ba[0],
        lru_bx[0],
        lru_lambda[0],
        w_out_b,
    ]
    fgain = final_gain.reshape(1, D_MODEL)
    ffn_consts = [norm2_gain, fgain, w1, w2]

    y_prompt, done, hs, ls = _mixer(x_prompt, m3, 0, 0, None, consts, ffn_consts, None, True)
    if not done:
        y_prompt = _ffn(y_prompt.reshape(bp * tp, D_MODEL), m3, 0, None, *ffn_consts)

    y_sample, done = _mixer(x_sample, m3, 1, 1, _grid_pos_tables(ts), consts, ffn_consts,
                            (state_hgrn.reshape(bs_, N_DIR * HG_HEADS, HG_DK, HG_DK),
                             state_rglru.reshape(bs_, N_DIR, LRU_WIDTH)), False)
    if not done:
        y_sample = _ffn(y_sample.reshape(bs_ * ts, D_MODEL), m3, 1, ts // FFN_TILE, *ffn_consts)

    return (y_prompt.reshape(bp, tp, D_MODEL),
            y_sample.reshape(bs_, ts, D_MODEL),
            hs.reshape(bp, 1, N_DIR, HG_HEADS, HG_DK, HG_DK),
            ls.reshape(bp, 1, N_DIR, LRU_WIDTH))
```

```python
import functools

import numpy as np
import jax
import jax.numpy as jnp
from jax import lax
from jax.experimental import pallas as pl
from jax.experimental.pallas import tpu as pltpu

F32 = jnp.float32
BF16 = jnp.bfloat16

D_MODEL = 1024
N_DIR = 2
HG_HEADS = 4
HG_DK = 128
HG_WIDTH = 512
LRU_WIDTH = 512
LRU_BLOCKS = 8
LRU_BLOCK = 64
LRU_C = 8.0
D_FF = 4096
IN_COLS = 4096
EPS = 1e-6
LOG2E = 1.4426950408889634
GRID_W = 64
POS_BASE = 10000.0

COL_Q = 0
COL_F = 1024
COL_V = 2048
COL_GATE = 2560
COL_LX = 3072
COL_LG = 3584
PROJ_LG = COL_LX
PROJ_COLS = IN_COLS - LRU_WIDTH

CHUNK = 128
LEVELS = (64, 32, 16, 8, 4)
LV_4 = LEVELS.index(4)
LV_21 = len(LEVELS)
LV_DIAG = LV_21 + 1
LV_STATE = LV_DIAG + 1
N_MASKS = LV_DIAG + 1
ROW_TILE = 256
LRU_SEGS = 8
CONV_TILE = 32
SCAN_UNROLL = 4
FIX_TILE = 64
STAGE_SLOTS = 1
MAX_INLINE_TRIPS = 2
LRU_GROUP = 256
GATE_D = 2 * LRU_GROUP
GATE_G = N_DIR * GATE_D
LRS_SEG_COL = 2 * N_DIR * LRU_WIDTH
PROJ_STEP = 512
FFN_TILE = 512
FF_CHUNK = 1024
MOD_STEPS = 8
MOD_ROWS = 8
VMEM_LIMIT = 58 * 1024 * 1024
VMEM_HEADROOM = 6 * 1024 * 1024


def _sig(x):
    return 0.5 * jnp.tanh(0.5 * x) + 0.5


def _silu(x):
    h = 0.5 * x
    return h * jnp.tanh(h) + h


def _nt_dot(a, b):
    return lax.dot_general(a, b, (((1,), (1,)), ((), ())), preferred_element_type=F32)


def _staged_transpose(xb, slot_ref):
    slot_ref[...] = xb.T
    return slot_ref[...]


def _gram(x, slot_ref):
    xb = x.astype(BF16)
    return jnp.dot(xb, _staged_transpose(xb, slot_ref), preferred_element_type=F32)


def _level_tables():
    t = np.arange(CHUNK)[:, None]
    s = np.arange(CHUNK)[None, :]
    masks = np.zeros((N_DIR, N_MASKS, CHUNK, CHUNK), np.float32)
    for li, h in enumerate(LEVELS):
        same = (t // (2 * h)) == (s // (2 * h))
        t_hi = (t // h) % 2 == 1
        s_hi = (s // h) % 2 == 1
        masks[0, li] = same & t_hi & ~s_hi
        masks[1, li] = same & ~t_hi & s_hi
    same4 = (t // 4) == (s // 4)
    masks[0, LV_21] = same4 & (s < t)
    masks[1, LV_21] = same4 & (s > t)
    masks[:, LV_DIAG] = (t == s)
    tri = np.stack([(s <= t), (s >= t)]).astype(np.float32)
    return masks, tri


def _mod_kernel(cctx_ref, c_ref, w_ref, b_ref, *rest):
    n_w = (len(rest) - 2) // 2
    f32_refs, o_ref, bf16_refs, cond = rest[:n_w], rest[n_w], rest[n_w + 1:2 * n_w + 1], rest[-1]
    n = c_ref.shape[0]
    cond[...] = jnp.zeros(cond.shape, F32)
    cond[0:1, :] = cctx_ref[...]
    cond[1:1 + n, :] = c_ref[...]
    c = cond[...]
    a = _silu(c).astype(BF16)
    m = jnp.dot(a, w_ref[...].astype(BF16), preferred_element_type=F32) + b_ref[...]
    for r in range(MOD_ROWS):
        o_ref[r] = m[r:r + 1]
    for src, dst in zip(f32_refs, bf16_refs):
        dst[...] = src[...].astype(BF16)


def _modulation(c_ctx, c, w_ada, b_ada, weights):
    n = w_ada.shape[1]
    assert 1 + c.shape[0] <= MOD_ROWS
    tile = n // MOD_STEPS
    slab = lambda w: pl.BlockSpec((w.shape[0] // MOD_STEPS, w.shape[1]), lambda j: (j, 0))
    outs = pl.pallas_call(
        _mod_kernel,
        grid=(MOD_STEPS,),
        in_specs=[
            pl.BlockSpec((1, D_MODEL), lambda j: (0, 0)),
            pl.BlockSpec(c.shape, lambda j: (0, 0)),
            pl.BlockSpec((D_MODEL, tile), lambda j: (0, j)),
            pl.BlockSpec((1, tile), lambda j: (0, j)),
        ] + [slab(w) for w in weights],
        out_specs=[pl.BlockSpec((MOD_ROWS, 1, tile), lambda j: (0, 0, j))] + [slab(w) for w in weights],
        out_shape=[jax.ShapeDtypeStruct((MOD_ROWS, 1, n), F32)] + [
            jax.ShapeDtypeStruct(w.shape, BF16) for w in weights],
        scratch_shapes=[pltpu.VMEM((MOD_ROWS, D_MODEL), F32)],
        compiler_params=pltpu.CompilerParams(
            dimension_semantics=("arbitrary",), vmem_limit_bytes=VMEM_LIMIT),
        name="adaln_modulation",
    )(c_ctx.reshape(1, D_MODEL), c, w_ada, b_ada, *weights)
    return outs[0], outs[1:]


def _mixer_kernel(*refs, T, has_pos, has_state, emit_state, fuse_scan, fuse_ffn):
    it = iter(refs)
    x_ref = next(it)
    m_ref = next(it)
    if has_pos:
        posr_ref = next(it)
        posc_ref = next(it)
    n1g_ref = next(it)
    win_ref = next(it)
    lbl_ref = next(it)
    hgg_ref = next(it)
    masks_ref = next(it)
    tri_ref = next(it)
    convw_ref = next(it)
    convb_ref = next(it)
    wa_ref = next(it)
    wx_ref = next(it)
    ba_ref = next(it)
    bx_ref = next(it)
    lam_ref = next(it)
    wout_ref = next(it)
    if fuse_ffn:
        n2g_ref = next(it)
        fg_ref = next(it)
        w1_ref = next(it)
        w2_ref = next(it)
    if has_state:
        hs0_ref = next(it)
        ls0_ref = next(it)
    y_ref = next(it)
    if emit_state:
        hs_ref = next(it)
        ls_ref = next(it)
    x1_ref = next(it) if fuse_ffn else y_ref
    proj = next(it)
    lxp = next(it)
    mixin = next(it)
    st = next(it)
    ksc = next(it)
    xci = next(it)
    vts = next(it)
    rowc = next(it)
    tsl = next(it)
    wg = next(it)
    if fuse_scan:
        lrs = next(it)

    L = CHUNK

    @pl.when(pl.program_id(0) == 0)
    def _build_gate_weights():
        per_group = LRU_GROUP // LRU_BLOCK
        wg[...] = jnp.zeros(wg.shape, BF16)
        for g in range(LRU_WIDTH // LRU_GROUP):
            for p in range(2 * N_DIR):
                src = wa_ref if p % 2 == 0 else wx_ref
                for n in range(per_group):
                    r = n * LRU_BLOCK
                    col = p * LRU_GROUP + r
                    wg[g, r:r + LRU_BLOCK, col:col + LRU_BLOCK] = src[
                        (p // 2) * LRU_BLOCKS + g * per_group + n].astype(BF16)

    mrow = m_ref[0]
    sh1 = mrow[:, 0:D_MODEL]
    sc1 = mrow[:, D_MODEL:2 * D_MODEL]
    g1 = mrow[:, 2 * D_MODEL:3 * D_MODEL]
    gain1 = n1g_ref[...] * (1.0 + sc1)

    zrows = jnp.zeros((8, LRU_WIDTH), F32)
    lxp[0:8, :] = zrows
    lxp[T + 8:T + 16, :] = zrows

    def load_x(i):
        xt = x_ref[0, pl.ds(pl.multiple_of(i * ROW_TILE, ROW_TILE), ROW_TILE), :]
        if has_pos:
            per_tile = ROW_TILE // GRID_W
            tiles = []
            for s in range(per_tile):
                row_emb = jnp.concatenate([posr_ref[i * per_tile + s]] * (GRID_W // 8), axis=0)
                tiles.append(jnp.concatenate([row_emb, posc_ref[...]], axis=1))
            xt = xt + jnp.concatenate(tiles, axis=0)
        return xt

    def proj_body(i, carry):
        r0 = pl.multiple_of(i * ROW_TILE, ROW_TILE)
        xt = load_x(i)
        ms = jnp.mean(xt * xt, axis=-1, keepdims=True)
        hb = (xt * lax.rsqrt(ms + EPS) * gain1 + sh1).astype(BF16)
        starts = range(0, IN_COLS, PROJ_STEP)
        order = [COL_LX] + [c0 for c0 in starts if c0 < COL_GATE] + [
            c0 for c0 in starts if c0 >= COL_GATE and c0 != COL_LX]
        for c0 in order:
            res = jnp.dot(hb, win_ref[:, c0:c0 + PROJ_STEP], preferred_element_type=F32)
            if c0 == COL_LX:
                lxp[pl.ds(pl.multiple_of(r0 + 8, 8), ROW_TILE), :] = res
            elif c0 == COL_LG:
                proj[pl.ds(r0, ROW_TILE), PROJ_LG:PROJ_LG + LRU_WIDTH] = res
            else:
                proj[pl.ds(r0, ROW_TILE), c0:c0 + PROJ_STEP] = res
        return carry

    l0 = lbl_ref[0]
    l1 = lbl_ref[1]
    lmx = jnp.maximum(l0, l1)
    e0 = jnp.exp(l0 - lmx)
    e1 = jnp.exp(l1 - lmx)
    lb_all = e0 / (e0 + e1)

    def hg_prep(c, carry):
        rows = pl.ds(pl.multiple_of(c * L, L), L)
        for hd in range(HG_HEADS):
            vts[c, hd] = proj[rows, COL_V + hd * HG_DK:COL_V + (hd + 1) * HG_DK].T.astype(BF16)
        for d in range(N_DIR):
            for hd in range(HG_HEADS):
                idx = d * HG_HEADS + hd
                cq = COL_Q + idx * HG_DK
                cf = COL_F + idx * HG_DK
                hq = proj[rows, cq:cq + HG_DK]
                fz = proj[rows, cf:cf + HG_DK]
                proj[rows, cq:cq + HG_DK] = _silu(hq)
                sg = _sig(fz)
                lb = lb_all[idx:idx + 1, :]
                oml = 1.0 - lb
                ksc[rows, idx * HG_DK:(idx + 1) * HG_DK] = oml * (1.0 - sg)
                logf = jnp.log(lb + oml * sg)
                p1 = logf.astype(BF16)
                p2 = (logf - p1.astype(F32)).astype(BF16)
                bb = jnp.dot(tri_ref[d], jnp.concatenate([p1, p2], axis=1), preferred_element_type=F32)
                proj[rows, cf:cf + HG_DK] = (bb[:, 0:HG_DK] + bb[:, HG_DK:2 * HG_DK]) * LOG2E
        return carry

    lax.fori_loop(0, T // ROW_TILE, proj_body, 0)

    S = T // LRU_SEGS
    cw = convw_ref[...]
    cb = convb_ref[...]

    def irows(i0, j, n):
        return pl.ds(pl.multiple_of(LRU_SEGS * i0, 8) + j, n, stride=LRU_SEGS)

    for j in range(LRU_SEGS):
        def conv_body(ti, carry, j=j):
            i0 = pl.multiple_of(ti * CONV_TILE, CONV_TILE)
            win = lxp[pl.ds(pl.multiple_of(j * S + i0, 8), CONV_TILE + 16), :]
            xc = cb
            for tap in range(4):
                xc = xc + win[6 + tap:6 + tap + CONV_TILE] * cw[tap:tap + 1]
            for l in range(LRU_WIDTH // 128):
                xci[l, irows(i0, j, CONV_TILE), :] = xc[:, l * 128:(l + 1) * 128]
            return carry

        lax.fori_loop(0, S // CONV_TILE, conv_body, 0, unroll=S // CONV_TILE <= MAX_INLINE_TRIPS)

    n_grp = LRU_WIDTH // LRU_GROUP
    slabs = LRU_GROUP // 128
    gates = lrs if fuse_scan else proj

    def xc_tile(rows, g):
        return jnp.concatenate([xci[g * slabs + l, rows, :] for l in range(slabs)], axis=1)

    def gate_body(i, carry):
        rows = pl.ds(pl.multiple_of(i * ROW_TILE, ROW_TILE), ROW_TILE)
        for g in range(n_grp):
            gates[rows, g * GATE_G:(g + 1) * GATE_G] = jnp.dot(
                xc_tile(rows, g).astype(BF16), wg[g], preferred_element_type=F32)
        return carry

    lam = lam_ref[...]
    nl = -lam
    c8 = -LRU_C * (jnp.maximum(nl, 0.0) + jnp.log1p(jnp.exp(-jnp.abs(nl))))
    rowi2 = lax.broadcasted_iota(jnp.int32, (8, LRU_GROUP), 0)

    for d in range(N_DIR):
        for g in range(n_grp):
            k3 = 3 * (d * n_grp + g)
            chans = slice(g * LRU_GROUP, (g + 1) * LRU_GROUP)
            for r, row in enumerate((ba_ref[d:d + 1, chans], bx_ref[d:d + 1, chans], c8[d:d + 1, chans])):
                rowc[k3 + r] = jnp.broadcast_to(row, (LRU_SEGS, LRU_GROUP))

    def seg_out(d, rows, g, part):
        if fuse_scan:
            col = LRS_SEG_COL + ((d * n_grp + g) * 2 + part) * LRU_GROUP
            return lrs.at[rows, col:col + LRU_GROUP]
        cols = slice((g * 2 + part) * LRU_GROUP, (g * 2 + part + 1) * LRU_GROUP)
        return (ksc.at[rows, cols] if d == 0 else x1_ref.at[0, rows, cols])

    def lru_inputs(rows8, g, d):
        base = g * GATE_G + d * GATE_D
        k3 = 3 * (d * n_grp + g)
        ga = gates[rows8, base:base + LRU_GROUP] + rowc[k3]
        gx = gates[rows8, base + LRU_GROUP:base + 2 * LRU_GROUP] + rowc[k3 + 1]
        xc8 = xc_tile(rows8, g)
        log_a = rowc[k3 + 2] * _sig(ga)
        a = jnp.exp(log_a)
        z = jnp.tanh(-log_a) * (1.0 + a * a)
        mult = jnp.where(z > 0.0, z * lax.rsqrt(z), 0.0)
        return a, mult * (_sig(gx) * xc8)

    def scan8(a, u, d):
        for sft in (1, 2, 4):
            if d == 0:
                keep = rowi2 >= sft
                amt = sft
            else:
                keep = rowi2 < 8 - sft
                amt = 8 - sft
            ash = jnp.where(keep, pltpu.roll(a, amt, 0), 1.0)
            ush = jnp.where(keep, pltpu.roll(u, amt, 0), 0.0)
            u = a * ush + u
            a = a * ash
        return a, u

    def scan_steps(first, count, carry):
        hs, ds = [list(c) for c in carry[:2]], [list(c) for c in carry[2:]]
        for u in range(count):
            i_f = first + u
            for d, i in ((0, i_f), (1, S - 1 - i_f)):
                rows8 = pl.ds(pl.multiple_of(LRU_SEGS * i, 8), 8)
                for g in range(n_grp):
                    a, uu = lru_inputs(rows8, g, d)
                    hs[d][g] = a * hs[d][g] + uu
                    ds[d][g] = a * ds[d][g]
                    seg_out(d, rows8, g, 0)[...] = hs[d][g]
                    seg_out(d, rows8, g, 1)[...] = ds[d][g]
        return tuple(tuple(c) for c in hs + ds)

    zero8 = jnp.zeros((LRU_SEGS, LRU_GROUP), F32)
    one8 = jnp.ones((LRU_SEGS, LRU_GROUP), F32)
    scan_init = ((zero8,) * n_grp, (zero8,) * n_grp, (one8,) * n_grp, (one8,) * n_grp)
    if fuse_scan:
        lax.fori_loop(0, T // ROW_TILE, gate_body, 0)

    rowi = lax.broadcasted_iota(jnp.int32, (8, HG_DK), 0)
    r4 = rowi & 3
    is_r0 = r4 == 0
    is_r1 = r4 == 1
    is_r2 = r4 == 2
    hi4 = rowi >= 4

    for i in range(N_DIR * HG_HEADS):
        if has_state:
            st[i] = hs0_ref[0, i].T
        else:
            st[i] = jnp.zeros((HG_DK, HG_DK), F32)

    def hg_chunk(c, d):
        r0 = pl.multiple_of(c * L, L)
        rows = pl.ds(r0, L)
        for hd in range(HG_HEADS):
            idx = d * HG_HEADS + hd
            cq = COL_Q + idx * HG_DK
            cf = COL_F + idx * HG_DK
            ck = idx * HG_DK
            slot = idx % STAGE_SLOTS

            def ldq(lo, n, cq=cq):
                return proj[pl.ds(pl.multiple_of(r0 + lo, 8), n), cq:cq + HG_DK]

            def ldb(lo, n, cf=cf):
                return proj[pl.ds(pl.multiple_of(r0 + lo, 8), n), cf:cf + HG_DK]

            def ldk(lo, n, ck=ck):
                return ksc[pl.ds(pl.multiple_of(r0 + lo, 8), n), ck:ck + HG_DK]

            def bline(r, cf=cf):
                grp = proj[pl.ds(pl.multiple_of(r0 + 8 * (r // 8), 8), 8), cf:cf + HG_DK]
                return grp[r % 8:r % 8 + 1, :]

            def brow(r, n):
                return jnp.broadcast_to(bline(r), (n, HG_DK))

            acc_rows = [None] * (L // 8)

            def accumulate(li, p, row0):
                for i in range(p.shape[0] // 8):
                    g = row0 // 8 + i
                    term = masks_ref[d, li, 8 * g:8 * g + 8, :] * p[8 * i:8 * i + 8]
                    acc_rows[g] = term if acc_rows[g] is None else acc_rows[g] + term

            accumulate(LV_DIAG, jnp.dot(
                ldq(0, L).astype(BF16), _staged_transpose(ldk(0, L).astype(BF16), tsl.at[slot, LV_DIAG]),
                preferred_element_type=F32), 0)
            for li, h in enumerate(LEVELS[:LV_4]):
                pieces, q_pieces, q_starts = [], [], []
                for j in range(L // (2 * h)):
                    lo = j * 2 * h
                    mid = lo + h
                    if d == 0:
                        bm = brow(mid - 1, h)
                        kp = ldk(lo, h) * jnp.exp2(bm - ldb(lo, h))
                        qp = ldq(mid, h) * jnp.exp2(ldb(mid, h) - bm)
                        pieces += [kp, qp]
                        q_starts.append(mid)
                    else:
                        bm = brow(mid, h)
                        qp = ldq(lo, h) * jnp.exp2(ldb(lo, h) - bm)
                        kp = ldk(mid, h) * jnp.exp2(bm - ldb(mid, h))
                        pieces += [qp, kp]
                        q_starts.append(lo)
                    q_pieces.append(qp)
                xt = _staged_transpose(jnp.concatenate(pieces, axis=0).astype(BF16), tsl.at[slot,li])
                p = jnp.dot(jnp.concatenate(q_pieces, axis=0).astype(BF16), xt, preferred_element_type=F32)
                for j, row0 in enumerate(q_starts):
                    accumulate(li, p[j * h:(j + 1) * h], row0)
            x4, xq21, xk21 = [], [], []
            for g in range(L // 8):
                qg, kg, bg = ldq(8 * g, 8), ldk(8 * g, 8), ldb(8 * g, 8)
                fg = 1.0 - kg
                qfg = qg * fg
                mid = 8 * g + (3 if d == 0 else 4)
                e4 = jnp.exp2(-jnp.abs(bg - brow(mid, 8)))
                fnx = pltpu.roll(fg, 7, 0)
                fpv = pltpu.roll(fg, 1, 0)
                k_over_f = kg / fg
                if d == 0:
                    x4.append(jnp.where(hi4, qg, kg) * e4)
                    xq21.append(jnp.where(is_r0, 0.0, jnp.where(is_r1, qg, jnp.where(is_r2, qfg, qfg * fpv))))
                    xk21.append(jnp.where(is_r0, kg * fnx, jnp.where(is_r1, kg, jnp.where(is_r2, k_over_f, 0.0))))
                else:
                    x4.append(jnp.where(hi4, kg, qg) * e4)
                    xq21.append(jnp.where(is_r0, qfg * fnx, jnp.where(is_r1, qfg, jnp.where(is_r2, qg, 0.0))))
                    xk21.append(jnp.where(is_r0, 0.0, jnp.where(is_r1, k_over_f, jnp.where(is_r2, kg, kg * fpv))))
            accumulate(LV_4, _gram(jnp.concatenate(x4, axis=0), tsl.at[slot, LV_4]), 0)
            accumulate(LV_21, jnp.dot(
                jnp.concatenate(xq21, axis=0).astype(BF16),
                _staged_transpose(jnp.concatenate(xk21, axis=0).astype(BF16), tsl.at[slot, LV_21]),
                preferred_element_type=F32), 0)
            acc = jnp.concatenate(acc_rows, axis=0)

            vt = vts[c, hd]
            st_t = st[idx]
            b = ldb(0, L)
            qt = (ldq(0, L) * jnp.exp2(b)).astype(BF16)
            vb = proj[rows, COL_V + hd * HG_DK:COL_V + (hd + 1) * HG_DK].astype(BF16)
            o = jnp.dot(jnp.concatenate([acc.astype(BF16), qt], axis=1),
                        jnp.concatenate([vb, _staged_transpose(st_t.astype(BF16), tsl.at[slot, LV_STATE])], axis=0),
                        preferred_element_type=F32)
            btot = bline(L - 1 if d == 0 else 0)
            kt = (ldk(0, L) * jnp.exp2(btot - b)).astype(BF16)
            st[idx] = st_t * jnp.exp2(btot) + jnp.dot(vt, kt, preferred_element_type=F32)
            x1_ref[0, rows, d * HG_WIDTH + hd * HG_DK:d * HG_WIDTH + (hd + 1) * HG_DK] = o

    n_chunks = T // L

    def hg_both(c, carry):
        hg_chunk(c, 0)
        hg_chunk(n_chunks - 1 - c, 1)
        if fuse_scan:
            steps = S // n_chunks
            carry = scan_steps(c * steps, steps, carry)
        return carry

    lax.fori_loop(0, n_chunks, hg_prep, 0, unroll=n_chunks <= MAX_INLINE_TRIPS)
    scan_state = lax.fori_loop(0, n_chunks, hg_both, scan_init if fuse_scan else 0)

    if emit_state:
        for i in range(N_DIR * HG_HEADS):
            hs_ref[0, i] = st[i].T

    def hg_fin(i, carry):
        rows = pl.ds(pl.multiple_of(i * L, L), L)
        for hd in range(HG_HEADS):
            cs = slice(hd * HG_DK, (hd + 1) * HG_DK)
            o = x1_ref[0, rows, cs] + x1_ref[0, rows, HG_WIDTH + hd * HG_DK:HG_WIDTH + (hd + 1) * HG_DK]
            ms = jnp.mean(o * o, axis=-1, keepdims=True)
            y = o * lax.rsqrt(ms + EPS) * hgg_ref[:, cs]
            gz = proj[rows, COL_GATE + hd * HG_DK:COL_GATE + (hd + 1) * HG_DK]
            mixin[rows, cs] = (y * _silu(gz)).astype(BF16)
        return carry

    lax.fori_loop(0, n_chunks, hg_fin, 0, unroll=n_chunks <= MAX_INLINE_TRIPS)

    if not fuse_scan:
        lax.fori_loop(0, T // ROW_TILE, gate_body, 0)
        scan_state = lax.fori_loop(
            0, S // SCAN_UNROLL, lambda n, carry: scan_steps(n * SCAN_UNROLL, SCAN_UNROLL, carry), scan_init)
    h_end, d_end = scan_state[:2], scan_state[2:]

    carry_in = [[None] * n_grp for _ in range(N_DIR)]
    for d in range(N_DIR):
        for g in range(n_grp):
            if has_state:
                h0 = ls0_ref[0, d:d + 1, g * LRU_GROUP:(g + 1) * LRU_GROUP]
            else:
                h0 = jnp.zeros((1, LRU_GROUP), F32)
            dd, hh = scan8(d_end[d][g], h_end[d][g], d)
            seg_end = hh + dd * h0
            if d == 0:
                carry_in[d][g] = jnp.where(rowi2 >= 1, pltpu.roll(seg_end, 1, 0), h0)
                last = seg_end[LRU_SEGS - 1:LRU_SEGS, :]
            else:
                carry_in[d][g] = jnp.where(rowi2 < LRU_SEGS - 1, pltpu.roll(seg_end, LRU_SEGS - 1, 0), h0)
                last = seg_end[0:1, :]
            if emit_state:
                ls_ref[0, d:d + 1, g * LRU_GROUP:(g + 1) * LRU_GROUP] = last

    def fix_body(i, carry):
        rows = pl.ds(pl.multiple_of(i * FIX_TILE, FIX_TILE), FIX_TILE)
        for g in range(n_grp):
            tot = None
            for d in range(N_DIR):
                cin = jnp.concatenate([carry_in[d][g]] * (FIX_TILE // LRU_SEGS), axis=0)
                h = seg_out(d, rows, g, 0)[...] + seg_out(d, rows, g, 1)[...] * cin
                tot = h if tot is None else tot + h
            for l in range(slabs):
                xci[g * slabs + l, rows, :] = tot[:, l * 128:(l + 1) * 128]
        return carry

    lax.fori_loop(0, T // FIX_TILE, fix_body, 0)

    for j in range(LRU_SEGS):
        def lru_fin(ti, carry, j=j):
            i0 = pl.multiple_of(ti * CONV_TILE, CONV_TILE)
            rows = pl.ds(pl.multiple_of(j * S + i0, CONV_TILE), CONV_TILE)
            hsum = jnp.concatenate([xci[l, irows(i0, j, CONV_TILE), :] for l in range(LRU_WIDTH // 128)], axis=1)
            lg = proj[rows, PROJ_LG:PROJ_LG + LRU_WIDTH]
            gl = lg * (0.5 * (1.0 + jnp.tanh(0.7978845608028654 * (lg + 0.044715 * (lg * lg * lg)))))
            mixin[rows, HG_WIDTH:HG_WIDTH + LRU_WIDTH] = (hsum * gl).astype(BF16)
            return carry

        lax.fori_loop(0, S // CONV_TILE, lru_fin, 0, unroll=S // CONV_TILE <= MAX_INLINE_TRIPS)

    def out_body(i, carry):
        rows = pl.ds(pl.multiple_of(i * ROW_TILE, ROW_TILE), ROW_TILE)
        mix = jnp.dot(mixin[rows, :], wout_ref[...], preferred_element_type=F32)
        x1_ref[0, rows, :] = load_x(i) + g1 * mix
        return carry

    lax.fori_loop(0, T // ROW_TILE, out_body, 0)

    if fuse_ffn:
        def ffn_body(i, carry):
            rows = pl.ds(pl.multiple_of(i * ROW_TILE, ROW_TILE), ROW_TILE)
            y_ref[0, rows, :] = _ffn_rows(x1_ref[0, rows, :], mrow, n2g_ref[...], fg_ref[...], w1_ref, w2_ref)
            return carry

        lax.fori_loop(0, T // ROW_TILE, ffn_body, 0)


def _const_spec(shape):
    nd = len(shape)
    return pl.BlockSpec(shape, lambda b, _n=nd: (0,) * _n, pipeline_mode=pl.Buffered(1))


def _nbytes(shape, dtype):
    return int(np.prod(shape)) * jnp.dtype(dtype).itemsize


def _mixer(x, m3, m_off, m_step, pos, consts, ffn_consts, states, emit_state):
    B, T, _ = x.shape
    has_pos = pos is not None
    has_state = states is not None
    scratch_shapes = [
        ((T, PROJ_COLS), F32),
        ((T + 16, LRU_WIDTH), F32),
        ((T, D_MODEL), BF16),
        ((8, HG_DK, HG_DK), F32),
        ((T, N_DIR * HG_HEADS * HG_DK), F32),
        ((LRU_WIDTH // 128, T, 128), F32),
        ((T // CHUNK, HG_HEADS, HG_DK, CHUNK), BF16),
        ((3 * N_DIR * (LRU_WIDTH // LRU_GROUP), LRU_SEGS, LRU_GROUP), F32),
        ((STAGE_SLOTS, LV_STATE + 1, HG_DK, CHUNK), BF16),
        ((LRU_WIDTH // LRU_GROUP, LRU_GROUP, 2 * N_DIR * LRU_GROUP), BF16),
    ]
    resident = (sum(_nbytes(s, d) for s, d in scratch_shapes)
                + sum(_nbytes(c.shape, c.dtype) for c in consts)
                + (sum(_nbytes(p.shape, p.dtype) for p in pos) if has_pos else 0))
    io_block = _nbytes((T, D_MODEL), F32)
    budget = VMEM_LIMIT - VMEM_HEADROOM
    lrs_shape = ((T, 2 * LRS_SEG_COL), F32)
    fuse_scan = resident + _nbytes(*lrs_shape) + 4 * io_block <= budget
    if fuse_scan:
        scratch_shapes.append(lrs_shape)
        resident += _nbytes(*lrs_shape)
    ffn_bytes = sum(_nbytes(c.shape, c.dtype) for c in ffn_consts) + io_block
    fuse_ffn = resident + ffn_bytes + 4 * io_block <= budget
    if fuse_ffn:
        scratch_shapes.insert(0, ((1, T, D_MODEL), F32))
        resident += ffn_bytes
    in_bufs = 2 if resident + 3 * io_block <= budget else 1
    out_bufs = 2 if resident + 4 * io_block <= budget else 1
    io_mode = pl.Buffered(out_bufs)
    in_specs = [
        pl.BlockSpec((1, T, D_MODEL), lambda b: (b, 0, 0), pipeline_mode=pl.Buffered(in_bufs)),
        pl.BlockSpec((1, 1, 6 * D_MODEL), lambda b: (m_off + m_step * b, 0, 0)),
    ]
    args = [x, m3]
    if has_pos:
        in_specs += [_const_spec(p.shape) for p in pos]
        args += list(pos)
    in_specs += [_const_spec(c.shape) for c in consts]
    args += list(consts)
    if fuse_ffn:
        in_specs += [_const_spec(c.shape) for c in ffn_consts]
        args += list(ffn_consts)
    if has_state:
        hs0, ls0 = states
        in_specs += [
            pl.BlockSpec((1, 8, HG_DK, HG_DK), lambda b: (b, 0, 0, 0)),
            pl.BlockSpec((1, N_DIR, LRU_WIDTH), lambda b: (b, 0, 0)),
        ]
        args += [hs0, ls0]
    out_shape = [jax.ShapeDtypeStruct((B, T, D_MODEL), F32)]
    out_specs = [pl.BlockSpec((1, T, D_MODEL), lambda b: (b, 0, 0), pipeline_mode=io_mode)]
    if emit_state:
        out_shape += [jax.ShapeDtypeStruct((B, 8, HG_DK, HG_DK), F32),
                      jax.ShapeDtypeStruct((B, N_DIR, LRU_WIDTH), F32)]
        out_specs += [pl.BlockSpec((1, 8, HG_DK, HG_DK), lambda b: (b, 0, 0, 0)),
                      pl.BlockSpec((1, N_DIR, LRU_WIDTH), lambda b: (b, 0, 0))]
    scratch = [pltpu.VMEM(s, d) for s, d in scratch_shapes]
    outs = pl.pallas_call(
        functools.partial(_mixer_kernel, T=T, has_pos=has_pos, has_state=has_state, emit_state=emit_state,
                          fuse_scan=fuse_scan, fuse_ffn=fuse_ffn),
        grid=(B,),
        in_specs=in_specs,
        out_specs=out_specs,
        out_shape=out_shape,
        scratch_shapes=scratch,
        compiler_params=pltpu.CompilerParams(
            dimension_semantics=("arbitrary",), vmem_limit_bytes=VMEM_LIMIT),
        name=f"mixer_t{T}",
    )(*args)
    return (outs[0], fuse_ffn) + tuple(outs[1:])


def _ffn_rows(x, mrow, n2g, fgain, w1_ref, w2_ref):
    sh2 = mrow[:, 3 * D_MODEL:4 * D_MODEL]
    sc2 = mrow[:, 4 * D_MODEL:5 * D_MODEL]
    g2 = mrow[:, 5 * D_MODEL:6 * D_MODEL]
    ms = jnp.mean(x * x, axis=-1, keepdims=True)
    hb = (x * lax.rsqrt(ms + EPS) * (n2g * (1.0 + sc2)) + sh2).astype(BF16)
    ff = jnp.zeros(x.shape, F32)
    for c in range(D_FF // FF_CHUNK):
        a = jnp.dot(hb, w1_ref[:, c * FF_CHUNK:(c + 1) * FF_CHUNK], preferred_element_type=F32)
        a = jnp.maximum(a, 0.0)
        ff = ff + jnp.dot((a * a).astype(BF16), w2_ref[c * FF_CHUNK:(c + 1) * FF_CHUNK, :],
                          preferred_element_type=F32)
    x2 = x + g2 * ff
    ms2 = jnp.mean(x2 * x2, axis=-1, keepdims=True)
    return x2 * lax.rsqrt(ms2 + EPS) * fgain


def _ffn_kernel(x_ref, m_ref, n2g_ref, fg_ref, w1_ref, w2_ref, y_ref):
    y_ref[...] = _ffn_rows(x_ref[...], m_ref[0], n2g_ref[...], fg_ref[...], w1_ref, w2_ref)


def _ffn(x1, m3, m_off, tiles_per_cond, n2g, fgain, w1, w2):
    n = x1.shape[0]

    def m_index(i):
        if tiles_per_cond is None:
            return (m_off, 0, 0)
        return (m_off + i // tiles_per_cond, 0, 0)

    return pl.pallas_call(
        _ffn_kernel,
        grid=(n // FFN_TILE,),
        in_specs=[
            pl.BlockSpec((FFN_TILE, D_MODEL), lambda i: (i, 0)),
            pl.BlockSpec((1, 1, 6 * D_MODEL), m_index),
            _const_spec(n2g.shape),
            _const_spec(fgain.shape),
            _const_spec(w1.shape),
            _const_spec(w2.shape),
        ],
        out_specs=pl.BlockSpec((FFN_TILE, D_MODEL), lambda i: (i, 0)),
        out_shape=jax.ShapeDtypeStruct((n, D_MODEL), F32),
        compiler_params=pltpu.CompilerParams(
            dimension_semantics=("arbitrary",), vmem_limit_bytes=VMEM_LIMIT),
        name="ffn",
    )(x1, m3, n2g, fgain, w1, w2)


def _grid_pos_tables(n_tok):
    quarter = D_MODEL // 4
    omega = (1.0 / (np.float32(POS_BASE) ** (np.arange(quarter, dtype=np.float32) / np.float32(quarter)))
             ).astype(np.float32)

    def emb(n):
        ang = np.arange(n).reshape(-1, 1).astype(np.float32) * omega
        return np.concatenate([np.sin(ang), np.cos(ang)], axis=-1)

    rows = np.repeat(emb(n_tok // GRID_W)[:, None, :], 8, axis=1)
    return jnp.asarray(rows, dtype=F32), jnp.asarray(emb(GRID_W), dtype=F32)


def kernel(x_prompt, x_sample, c, state_hgrn, state_rglru, c_ctx, w_ada, b_ada, norm1_gain, norm2_gain,
           w_in, hg_lb_logits, hg_norm_gain, conv_w, conv_b, lru_wa, lru_ba, lru_wx, lru_bx, lru_lambda,
           w_out, w_ff1, w_ff2, final_gain):
    bp, tp, _ = x_prompt.shape
    bs_, ts, _ = x_sample.shape

    m3, (w_in_b, w_out_b, w1, w2) = _modulation(
        c_ctx, c, w_ada[0], b_ada, [w_in[0], w_out[0], w_ff1[0], w_ff2[0]])

    masks_np, tri_np = _level_tables()
    consts = [
        norm1_gain,
        w_in_b,
        hg_lb_logits.reshape(2, N_DIR * HG_HEADS, HG_DK),
        hg_norm_gain[0].reshape(1, HG_WIDTH),
        jnp.asarray(masks_np),
        jnp.asarray(tri_np, dtype=BF16),
        conv_w[0],
        conv_b,
        lru_wa[0].reshape(N_DIR * LRU_BLOCKS, LRU_BLOCK, LRU_BLOCK),
        lru_wx[0].reshape(N_DIR * LRU_BLOCKS, LRU_BLOCK, LRU_BLOCK),
        lru_ba[0],
        lru_bx[0],
        lru_lambda[0],
        w_out_b,
    ]
    fgain = final_gain.reshape(1, D_MODEL)
    ffn_consts = [norm2_gain, fgain, w1, w2]

    y_prompt, done, hs, ls = _mixer(x_prompt, m3, 0, 0, None, consts, ffn_consts, None, True)
    if not done:
        y_prompt = _ffn(y_prompt.reshape(bp * tp, D_MODEL), m3, 0, None, *ffn_consts)

    y_sample, done = _mixer(x_sample, m3, 1, 1, _grid_pos_tables(ts), consts, ffn_consts,
                            (state_hgrn.reshape(bs_, N_DIR * HG_HEADS, HG_DK, HG_DK),
                             state_rglru.reshape(bs_, N_DIR, LRU_WIDTH)), False)
    if not done:
        y_sample = _ffn(y_sample.reshape(bs_ * ts, D_MODEL), m3, 1, ts // FFN_TILE, *ffn_consts)

    return (y_prompt.reshape(bp, tp, D_MODEL),
            y_sample.reshape(bs_, ts, D_MODEL),
            hs.reshape(bp, 1, N_DIR, HG_HEADS, HG_DK, HG_DK),
            ls.reshape(bp, 1, N_DIR, LRU_WIDTH))
```

```python
import functools

import numpy as np
import jax
import jax.numpy as jnp
from jax import lax
from jax.experimental import pallas as pl
from jax.experimental.pallas import tpu as pltpu

F32 = jnp.float32
BF16 = jnp.bfloat16

D_MODEL = 1024
N_DIR = 2
HG_HEADS = 4
HG_DK = 128
HG_WIDTH = 512
LRU_WIDTH = 512
LRU_BLOCKS = 8
LRU_BLOCK = 64
LRU_C = 8.0
D_FF = 4096
IN_COLS = 4096
EPS = 1e-6
LOG2E = 1.4426950408889634
GRID_W = 64
POS_BASE = 10000.0

COL_Q = 0
COL_F = 1024
COL_V = 2048
COL_GATE = 2560
COL_LX = 3072
COL_LG = 3584
PROJ_LG = COL_LX
PROJ_COLS = IN_COLS - LRU_WIDTH

CHUNK = 128
LEVELS = (64, 32, 16, 8, 4)
LV_4 = LEVELS.index(4)
LV_21 = len(LEVELS)
LV_DIAG = LV_21 + 1
LV_STATE = LV_DIAG + 1
N_MASKS = LV_DIAG + 1
ROW_TILE = 256
LRU_SEGS = 8
CONV_TILE = 32
SCAN_UNROLL = 4
FIX_TILE = 64
STAGE_SLOTS = 1
MAX_INLINE_TRIPS = 2
LRU_GROUP = 256
GATE_D = 2 * LRU_GROUP
GATE_G = N_DIR * GATE_D
LRS_SEG_COL = 2 * N_DIR * LRU_WIDTH
PROJ_STEP = 512
FFN_TILE = 512
FF_CHUNK = 1024
MOD_STEPS = 8
MOD_ROWS = 8
VMEM_LIMIT = 58 * 1024 * 1024
VMEM_HEADROOM = 6 * 1024 * 1024


def _sig(x):
    return 0.5 * jnp.tanh(0.5 * x) + 0.5


def _silu(x):
    h = 0.5 * x
    return h * jnp.tanh(h) + h


def _nt_dot(a, b):
    return lax.dot_general(a, b, (((1,), (1,)), ((), ())), preferred_element_type=F32)


def _staged_transpose(xb, slot_ref):
    slot_ref[...] = xb.T
    return slot_ref[...]


def _gram(x, slot_ref):
    xb = x.astype(BF16)
    return jnp.dot(xb, _staged_transpose(xb, slot_ref), preferred_element_type=F32)


def _level_tables():
    t = np.arange(CHUNK)[:, None]
    s = np.arange(CHUNK)[None, :]
    masks = np.zeros((N_DIR, N_MASKS, CHUNK, CHUNK), np.float32)
    for li, h in enumerate(LEVELS):
        same = (t // (2 * h)) == (s // (2 * h))
        t_hi = (t // h) % 2 == 1
        s_hi = (s // h) % 2 == 1
        masks[0, li] = same & t_hi & ~s_hi
        masks[1, li] = same & ~t_hi & s_hi
    same4 = (t // 4) == (s // 4)
    masks[0, LV_21] = same4 & (s < t)
    masks[1, LV_21] = same4 & (s > t)
    masks[:, LV_DIAG] = (t == s)
    tri = np.stack([(s <= t), (s >= t)]).astype(np.float32)
    return masks, tri


def _mod_kernel(cctx_ref, c_ref, w_ref, b_ref, *rest):
    n_w = (len(rest) - 2) // 2
    f32_refs, o_ref, bf16_refs, cond = rest[:n_w], rest[n_w], rest[n_w + 1:2 * n_w + 1], rest[-1]
    n = c_ref.shape[0]
    cond[...] = jnp.zeros(cond.shape, F32)
    cond[0:1, :] = cctx_ref[...]
    cond[1:1 + n, :] = c_ref[...]
    c = cond[...]
    a = _silu(c).astype(BF16)
    m = jnp.dot(a, w_ref[...].astype(BF16), preferred_element_type=F32) + b_ref[...]
    for r in range(MOD_ROWS):
        o_ref[r] = m[r:r + 1]
    for src, dst in zip(f32_refs, bf16_refs):
        dst[...] = src[...].astype(BF16)


def _modulation(c_ctx, c, w_ada, b_ada, weights):
    n = w_ada.shape[1]
    assert 1 + c.shape[0] <= MOD_ROWS
    tile = n // MOD_STEPS
    slab = lambda w: pl.BlockSpec((w.shape[0] // MOD_STEPS, w.shape[1]), lambda j: (j, 0))
    outs = pl.pallas_call(
        _mod_kernel,
        grid=(MOD_STEPS,),
        in_specs=[
            pl.BlockSpec((1, D_MODEL), lambda j: (0, 0)),
            pl.BlockSpec(c.shape, lambda j: (0, 0)),
            pl.BlockSpec((D_MODEL, tile), lambda j: (0, j)),
            pl.BlockSpec((1, tile), lambda j: (0, j)),
        ] + [slab(w) for w in weights],
        out_specs=[pl.BlockSpec((MOD_ROWS, 1, tile), lambda j: (0, 0, j))] + [slab(w) for w in weights],
        out_shape=[jax.ShapeDtypeStruct((MOD_ROWS, 1, n), F32)] + [
            jax.ShapeDtypeStruct(w.shape, BF16) for w in weights],
        scratch_shapes=[pltpu.VMEM((MOD_ROWS, D_MODEL), F32)],
        compiler_params=pltpu.CompilerParams(
            dimension_semantics=("arbitrary",), vmem_limit_bytes=VMEM_LIMIT),
        name="adaln_modulation",
    )(c_ctx.reshape(1, D_MODEL), c, w_ada, b_ada, *weights)
    return outs[0], outs[1:]


def _mixer_kernel(*refs, T, has_pos, has_state, emit_state, fuse_scan, fuse_ffn):
    it = iter(refs)
    x_ref = next(it)
    m_ref = next(it)
    if has_pos:
        posr_ref = next(it)
        posc_ref = next(it)
    n1g_ref = next(it)
    win_ref = next(it)
    lbl_ref = next(it)
    hgg_ref = next(it)
    masks_ref = next(it)
    tri_ref = next(it)
    convw_ref = next(it)
    convb_ref = next(it)
    wa_ref = next(it)
    wx_ref = next(it)
    ba_ref = next(it)
    bx_ref = next(it)
    lam_ref = next(it)
    late_hbm = [next(it)]
    if fuse_ffn:
        n2g_ref = next(it)
        fg_ref = next(it)
        late_hbm += [next(it), next(it)]
    if has_state:
        hs0_ref = next(it)
        ls0_ref = next(it)
    y_ref = next(it)
    if emit_state:
        hs_ref = next(it)
        ls_ref = next(it)
    x1_ref = next(it) if fuse_ffn else y_ref
    proj = next(it)
    lxp = next(it)
    mixin = next(it)
    st = next(it)
    ksc = next(it)
    xci = next(it)
    vts = next(it)
    rowc = next(it)
    tsl = next(it)
    wg = next(it)
    if fuse_scan:
        lrs = next(it)
    late_vmem = [next(it) for _ in late_hbm]
    late_sem = next(it)
    wout_ref = late_vmem[0]
    if fuse_ffn:
        w1_ref, w2_ref = late_vmem[1:]

    L = CHUNK

    def late_copy(k):
        return pltpu.make_async_copy(late_hbm[k], late_vmem[k], late_sem.at[k])

    @pl.when(pl.program_id(0) == 0)
    def _first_step():
        for k in range(len(late_hbm)):
            late_copy(k).start()
        per_group = LRU_GROUP // LRU_BLOCK
        wg[...] = jnp.zeros(wg.shape, BF16)
        for g in range(LRU_WIDTH // LRU_GROUP):
            for p in range(2 * N_DIR):
                src = wa_ref if p % 2 == 0 else wx_ref
                for n in range(per_group):
                    r = n * LRU_BLOCK
                    col = p * LRU_GROUP + r
                    wg[g, r:r + LRU_BLOCK, col:col + LRU_BLOCK] = src[
                        (p // 2) * LRU_BLOCKS + g * per_group + n].astype(BF16)

    mrow = m_ref[0]
    sh1 = mrow[:, 0:D_MODEL]
    sc1 = mrow[:, D_MODEL:2 * D_MODEL]
    g1 = mrow[:, 2 * D_MODEL:3 * D_MODEL]
    gain1 = n1g_ref[...] * (1.0 + sc1)

    zrows = jnp.zeros((8, LRU_WIDTH), F32)
    lxp[0:8, :] = zrows
    lxp[T + 8:T + 16, :] = zrows

    def load_x(i):
        xt = x_ref[0, pl.ds(pl.multiple_of(i * ROW_TILE, ROW_TILE), ROW_TILE), :]
        if has_pos:
            per_tile = ROW_TILE // GRID_W
            tiles = []
            for s in range(per_tile):
                row_emb = jnp.concatenate([posr_ref[i * per_tile + s]] * (GRID_W // 8), axis=0)
                tiles.append(jnp.concatenate([row_emb, posc_ref[...]], axis=1))
            xt = xt + jnp.concatenate(tiles, axis=0)
        return xt

    def proj_body(i, carry):
        r0 = pl.multiple_of(i * ROW_TILE, ROW_TILE)
        xt = load_x(i)
        ms = jnp.mean(xt * xt, axis=-1, keepdims=True)
        hb = (xt * lax.rsqrt(ms + EPS) * gain1 + sh1).astype(BF16)
        starts = range(0, IN_COLS, PROJ_STEP)
        order = [COL_LX] + [c0 for c0 in starts if c0 < COL_GATE] + [
            c0 for c0 in starts if c0 >= COL_GATE and c0 != COL_LX]
        for c0 in order:
            res = jnp.dot(hb, win_ref[:, c0:c0 + PROJ_STEP], preferred_element_type=F32)
            if c0 == COL_LX:
                lxp[pl.ds(pl.multiple_of(r0 + 8, 8), ROW_TILE), :] = res
            elif c0 == COL_LG:
                proj[pl.ds(r0, ROW_TILE), PROJ_LG:PROJ_LG + LRU_WIDTH] = res
            else:
                proj[pl.ds(r0, ROW_TILE), c0:c0 + PROJ_STEP] = res
        return carry

    l0 = lbl_ref[0]
    l1 = lbl_ref[1]
    lmx = jnp.maximum(l0, l1)
    e0 = jnp.exp(l0 - lmx)
    e1 = jnp.exp(l1 - lmx)
    lb_all = e0 / (e0 + e1)

    def hg_prep(c, carry):
        rows = pl.ds(pl.multiple_of(c * L, L), L)
        for hd in range(HG_HEADS):
            vts[c, hd] = proj[rows, COL_V + hd * HG_DK:COL_V + (hd + 1) * HG_DK].T.astype(BF16)
        for d in range(N_DIR):
            for hd in range(HG_HEADS):
                idx = d * HG_HEADS + hd
                cq = COL_Q + idx * HG_DK
                cf = COL_F + idx * HG_DK
                hq = proj[rows, cq:cq + HG_DK]
                fz = proj[rows, cf:cf + HG_DK]
                proj[rows, cq:cq + HG_DK] = _silu(hq)
                sg = _sig(fz)
                lb = lb_all[idx:idx + 1, :]
                oml = 1.0 - lb
                ksc[rows, idx * HG_DK:(idx + 1) * HG_DK] = oml * (1.0 - sg)
                logf = jnp.log(lb + oml * sg)
                p1 = logf.astype(BF16)
                p2 = (logf - p1.astype(F32)).astype(BF16)
                bb = jnp.dot(tri_ref[d], jnp.concatenate([p1, p2], axis=1), preferred_element_type=F32)
                proj[rows, cf:cf + HG_DK] = (bb[:, 0:HG_DK] + bb[:, HG_DK:2 * HG_DK]) * LOG2E
        return carry

    lax.fori_loop(0, T // ROW_TILE, proj_body, 0)

    S = T // LRU_SEGS
    cw = convw_ref[...]
    cb = convb_ref[...]

    def irows(i0, j, n):
        return pl.ds(pl.multiple_of(LRU_SEGS * i0, 8) + j, n, stride=LRU_SEGS)

    for j in range(LRU_SEGS):
        def conv_body(ti, carry, j=j):
            i0 = pl.multiple_of(ti * CONV_TILE, CONV_TILE)
            win = lxp[pl.ds(pl.multiple_of(j * S + i0, 8), CONV_TILE + 16), :]
            xc = cb
            for tap in range(4):
                xc = xc + win[6 + tap:6 + tap + CONV_TILE] * cw[tap:tap + 1]
            for l in range(LRU_WIDTH // 128):
                xci[l, irows(i0, j, CONV_TILE), :] = xc[:, l * 128:(l + 1) * 128]
            return carry

        lax.fori_loop(0, S // CONV_TILE, conv_body, 0, unroll=S // CONV_TILE <= MAX_INLINE_TRIPS)

    n_grp = LRU_WIDTH // LRU_GROUP
    slabs = LRU_GROUP // 128
    gates = lrs if fuse_scan else proj

    def xc_tile(rows, g):
        return jnp.concatenate([xci[g * slabs + l, rows, :] for l in range(slabs)], axis=1)

    def gate_body(i, carry):
        rows = pl.ds(pl.multiple_of(i * ROW_TILE, ROW_TILE), ROW_TILE)
        for g in range(n_grp):
            gates[rows, g * GATE_G:(g + 1) * GATE_G] = jnp.dot(
                xc_tile(rows, g).astype(BF16), wg[g], preferred_element_type=F32)
        return carry

    lam = lam_ref[...]
    nl = -lam
    c8 = -LRU_C * (jnp.maximum(nl, 0.0) + jnp.log1p(jnp.exp(-jnp.abs(nl))))
    rowi2 = lax.broadcasted_iota(jnp.int32, (8, LRU_GROUP), 0)

    for d in range(N_DIR):
        for g in range(n_grp):
            k3 = 3 * (d * n_grp + g)
            chans = slice(g * LRU_GROUP, (g + 1) * LRU_GROUP)
            for r, row in enumerate((ba_ref[d:d + 1, chans], bx_ref[d:d + 1, chans], c8[d:d + 1, chans])):
                rowc[k3 + r] = jnp.broadcast_to(row, (LRU_SEGS, LRU_GROUP))

    def seg_out(d, rows, g, part):
        if fuse_scan:
            col = LRS_SEG_COL + ((d * n_grp + g) * 2 + part) * LRU_GROUP
            return lrs.at[rows, col:col + LRU_GROUP]
        cols = slice((g * 2 + part) * LRU_GROUP, (g * 2 + part + 1) * LRU_GROUP)
        return (ksc.at[rows, cols] if d == 0 else x1_ref.at[0, rows, cols])

    def lru_inputs(rows8, g, d):
        base = g * GATE_G + d * GATE_D
        k3 = 3 * (d * n_grp + g)
        ga = gates[rows8, base:base + LRU_GROUP] + rowc[k3]
        gx = gates[rows8, base + LRU_GROUP:base + 2 * LRU_GROUP] + rowc[k3 + 1]
        xc8 = xc_tile(rows8, g)
        log_a = rowc[k3 + 2] * _sig(ga)
        a = jnp.exp(log_a)
        z = jnp.tanh(-log_a) * (1.0 + a * a)
        mult = jnp.where(z > 0.0, z * lax.rsqrt(z), 0.0)
        return a, mult * (_sig(gx) * xc8)

    def scan8(a, u, d):
        for sft in (1, 2, 4):
            if d == 0:
                keep = rowi2 >= sft
                amt = sft
            else:
                keep = rowi2 < 8 - sft
                amt = 8 - sft
            ash = jnp.where(keep, pltpu.roll(a, amt, 0), 1.0)
            ush = jnp.where(keep, pltpu.roll(u, amt, 0), 0.0)
            u = a * ush + u
            a = a * ash
        return a, u

    def scan_steps(first, count, carry):
        hs, ds = [list(c) for c in carry[:2]], [list(c) for c in carry[2:]]
        for u in range(count):
            i_f = first + u
            for d, i in ((0, i_f), (1, S - 1 - i_f)):
                rows8 = pl.ds(pl.multiple_of(LRU_SEGS * i, 8), 8)
                for g in range(n_grp):
                    a, uu = lru_inputs(rows8, g, d)
                    hs[d][g] = a * hs[d][g] + uu
                    ds[d][g] = a * ds[d][g]
                    seg_out(d, rows8, g, 0)[...] = hs[d][g]
                    seg_out(d, rows8, g, 1)[...] = ds[d][g]
        return tuple(tuple(c) for c in hs + ds)

    zero8 = jnp.zeros((LRU_SEGS, LRU_GROUP), F32)
    one8 = jnp.ones((LRU_SEGS, LRU_GROUP), F32)
    scan_init = ((zero8,) * n_grp, (zero8,) * n_grp, (one8,) * n_grp, (one8,) * n_grp)
    if fuse_scan:
        lax.fori_loop(0, T // ROW_TILE, gate_body, 0)

    rowi = lax.broadcasted_iota(jnp.int32, (8, HG_DK), 0)
    r4 = rowi & 3
    is_r0 = r4 == 0
    is_r1 = r4 == 1
    is_r2 = r4 == 2
    hi4 = rowi >= 4

    for i in range(N_DIR * HG_HEADS):
        if has_state:
            st[i] = hs0_ref[0, i].T
        else:
            st[i] = jnp.zeros((HG_DK, HG_DK), F32)

    def hg_chunk(c, d):
        r0 = pl.multiple_of(c * L, L)
        rows = pl.ds(r0, L)
        for hd in range(HG_HEADS):
            idx = d * HG_HEADS + hd
            cq = COL_Q + idx * HG_DK
            cf = COL_F + idx * HG_DK
            ck = idx * HG_DK
            slot = idx % STAGE_SLOTS

            def ldq(lo, n, cq=cq):
                return proj[pl.ds(pl.multiple_of(r0 + lo, 8), n), cq:cq + HG_DK]

            def ldb(lo, n, cf=cf):
                return proj[pl.ds(pl.multiple_of(r0 + lo, 8), n), cf:cf + HG_DK]

            def ldk(lo, n, ck=ck):
                return ksc[pl.ds(pl.multiple_of(r0 + lo, 8), n), ck:ck + HG_DK]

            def bline(r, cf=cf):
                grp = proj[pl.ds(pl.multiple_of(r0 + 8 * (r // 8), 8), 8), cf:cf + HG_DK]
                return grp[r % 8:r % 8 + 1, :]

            def brow(r, n):
                return jnp.broadcast_to(bline(r), (n, HG_DK))

            acc_rows = [None] * (L // 8)

            def accumulate(li, p, row0):
                for i in range(p.shape[0] // 8):
                    g = row0 // 8 + i
                    term = masks_ref[d, li, 8 * g:8 * g + 8, :] * p[8 * i:8 * i + 8]
                    acc_rows[g] = term if acc_rows[g] is None else acc_rows[g] + term

            accumulate(LV_DIAG, jnp.dot(
                ldq(0, L).astype(BF16), _staged_transpose(ldk(0, L).astype(BF16), tsl.at[slot, LV_DIAG]),
                preferred_element_type=F32), 0)
            for li, h in enumerate(LEVELS[:LV_4]):
                pieces, q_pieces, q_starts = [], [], []
                for j in range(L // (2 * h)):
                    lo = j * 2 * h
                    mid = lo + h
                    if d == 0:
                        bm = brow(mid - 1, h)
                        kp = ldk(lo, h) * jnp.exp2(bm - ldb(lo, h))
                        qp = ldq(mid, h) * jnp.exp2(ldb(mid, h) - bm)
                        pieces += [kp, qp]
                        q_starts.append(mid)
                    else:
                        bm = brow(mid, h)
                        qp = ldq(lo, h) * jnp.exp2(ldb(lo, h) - bm)
                        kp = ldk(mid, h) * jnp.exp2(bm - ldb(mid, h))
                        pieces += [qp, kp]
                        q_starts.append(lo)
                    q_pieces.append(qp)
                xt = _staged_transpose(jnp.concatenate(pieces, axis=0).astype(BF16), tsl.at[slot,li])
                p = jnp.dot(jnp.concatenate(q_pieces, axis=0).astype(BF16), xt, preferred_element_type=F32)
                for j, row0 in enumerate(q_starts):
                    accumulate(li, p[j * h:(j + 1) * h], row0)
            x4, xq21, xk21 = [], [], []
            for g in range(L // 8):
                qg, kg, bg = ldq(8 * g, 8), ldk(8 * g, 8), ldb(8 * g, 8)
                fg = 1.0 - kg
                qfg = qg * fg
                mid = 8 * g + (3 if d == 0 else 4)
                e4 = jnp.exp2(-jnp.abs(bg - brow(mid, 8)))
                fnx = pltpu.roll(fg, 7, 0)
                fpv = pltpu.roll(fg, 1, 0)
                k_over_f = kg / fg
                if d == 0:
                    x4.append(jnp.where(hi4, qg, kg) * e4)
                    xq21.append(jnp.where(is_r0, 0.0, jnp.where(is_r1, qg, jnp.where(is_r2, qfg, qfg * fpv))))
                    xk21.append(jnp.where(is_r0, kg * fnx, jnp.where(is_r1, kg, jnp.where(is_r2, k_over_f, 0.0))))
                else:
                    x4.append(jnp.where(hi4, kg, qg) * e4)
                    xq21.append(jnp.where(is_r0, qfg * fnx, jnp.where(is_r1, qfg, jnp.where(is_r2, qg, 0.0))))
                    xk21.append(jnp.where(is_r0, 0.0, jnp.where(is_r1, k_over_f, jnp.where(is_r2, kg, kg * fpv))))
            accumulate(LV_4, _gram(jnp.concatenate(x4, axis=0), tsl.at[slot, LV_4]), 0)
            accumulate(LV_21, jnp.dot(
                jnp.concatenate(xq21, axis=0).astype(BF16),
                _staged_transpose(jnp.concatenate(xk21, axis=0).astype(BF16), tsl.at[slot, LV_21]),
                preferred_element_type=F32), 0)
            acc = jnp.concatenate(acc_rows, axis=0)

            vt = vts[c, hd]
            st_t = st[idx]
            b = ldb(0, L)
            qt = (ldq(0, L) * jnp.exp2(b)).astype(BF16)
            vb = proj[rows, COL_V + hd * HG_DK:COL_V + (hd + 1) * HG_DK].astype(BF16)
            o = jnp.dot(jnp.concatenate([acc.astype(BF16), qt], axis=1),
                        jnp.concatenate([vb, _staged_transpose(st_t.astype(BF16), tsl.at[slot, LV_STATE])], axis=0),
                        preferred_element_type=F32)
            btot = bline(L - 1 if d == 0 else 0)
            kt = (ldk(0, L) * jnp.exp2(btot - b)).astype(BF16)
            st[idx] = st_t * jnp.exp2(btot) + jnp.dot(vt, kt, preferred_element_type=F32)
            x1_ref[0, rows, d * HG_WIDTH + hd * HG_DK:d * HG_WIDTH + (hd + 1) * HG_DK] = o

    n_chunks = T // L

    def hg_both(c, carry):
        hg_chunk(c, 0)
        hg_chunk(n_chunks - 1 - c, 1)
        if fuse_scan:
            steps = S // n_chunks
            carry = scan_steps(c * steps, steps, carry)
        return carry

    lax.fori_loop(0, n_chunks, hg_prep, 0, unroll=n_chunks <= MAX_INLINE_TRIPS)
    scan_state = lax.fori_loop(0, n_chunks, hg_both, scan_init if fuse_scan else 0)

    @pl.when(pl.program_id(0) == 0)
    def _late_weights_ready():
        for k in range(len(late_hbm)):
            late_copy(k).wait()

    if emit_state:
        for i in range(N_DIR * HG_HEADS):
            hs_ref[0, i] = st[i].T

    def hg_fin(i, carry):
        rows = pl.ds(pl.multiple_of(i * L, L), L)
        for hd in range(HG_HEADS):
            cs = slice(hd * HG_DK, (hd + 1) * HG_DK)
            o = x1_ref[0, rows, cs] + x1_ref[0, rows, HG_WIDTH + hd * HG_DK:HG_WIDTH + (hd + 1) * HG_DK]
            ms = jnp.mean(o * o, axis=-1, keepdims=True)
            y = o * lax.rsqrt(ms + EPS) * hgg_ref[:, cs]
            gz = proj[rows, COL_GATE + hd * HG_DK:COL_GATE + (hd + 1) * HG_DK]
            mixin[rows, cs] = (y * _silu(gz)).astype(BF16)
        return carry

    lax.fori_loop(0, n_chunks, hg_fin, 0, unroll=n_chunks <= MAX_INLINE_TRIPS)

    if not fuse_scan:
        lax.fori_loop(0, T // ROW_TILE, gate_body, 0)
        scan_state = lax.fori_loop(
            0, S // SCAN_UNROLL, lambda n, carry: scan_steps(n * SCAN_UNROLL, SCAN_UNROLL, carry), scan_init)
    h_end, d_end = scan_state[:2], scan_state[2:]

    carry_in = [[None] * n_grp for _ in range(N_DIR)]
    for d in range(N_DIR):
        for g in range(n_grp):
            if has_state:
                h0 = ls0_ref[0, d:d + 1, g * LRU_GROUP:(g + 1) * LRU_GROUP]
            else:
                h0 = jnp.zeros((1, LRU_GROUP), F32)
            dd, hh = scan8(d_end[d][g], h_end[d][g], d)
            seg_end = hh + dd * h0
            if d == 0:
                carry_in[d][g] = jnp.where(rowi2 >= 1, pltpu.roll(seg_end, 1, 0), h0)
                last = seg_end[LRU_SEGS - 1:LRU_SEGS, :]
            else:
                carry_in[d][g] = jnp.where(rowi2 < LRU_SEGS - 1, pltpu.roll(seg_end, LRU_SEGS - 1, 0), h0)
                last = seg_end[0:1, :]
            if emit_state:
                ls_ref[0, d:d + 1, g * LRU_GROUP:(g + 1) * LRU_GROUP] = last

    def fix_body(i, carry):
        rows = pl.ds(pl.multiple_of(i * FIX_TILE, FIX_TILE), FIX_TILE)
        for g in range(n_grp):
            tot = None
            for d in range(N_DIR):
                cin = jnp.concatenate([carry_in[d][g]] * (FIX_TILE // LRU_SEGS), axis=0)
                h = seg_out(d, rows, g, 0)[...] + seg_out(d, rows, g, 1)[...] * cin
                tot = h if tot is None else tot + h
            for l in range(slabs):
                xci[g * slabs + l, rows, :] = tot[:, l * 128:(l + 1) * 128]
        return carry

    lax.fori_loop(0, T // FIX_TILE, fix_body, 0)

    for j in range(LRU_SEGS):
        def lru_fin(ti, carry, j=j):
            i0 = pl.multiple_of(ti * CONV_TILE, CONV_TILE)
            rows = pl.ds(pl.multiple_of(j * S + i0, CONV_TILE), CONV_TILE)
            hsum = jnp.concatenate([xci[l, irows(i0, j, CONV_TILE), :] for l in range(LRU_WIDTH // 128)], axis=1)
            lg = proj[rows, PROJ_LG:PROJ_LG + LRU_WIDTH]
            gl = lg * (0.5 * (1.0 + jnp.tanh(0.7978845608028654 * (lg + 0.044715 * (lg * lg * lg)))))
            mixin[rows, HG_WIDTH:HG_WIDTH + LRU_WIDTH] = (hsum * gl).astype(BF16)
            return carry

        lax.fori_loop(0, S // CONV_TILE, lru_fin, 0, unroll=S // CONV_TILE <= MAX_INLINE_TRIPS)

    def out_body(i, carry):
        rows = pl.ds(pl.multiple_of(i * ROW_TILE, ROW_TILE), ROW_TILE)
        mix = jnp.dot(mixin[rows, :], wout_ref[...], preferred_element_type=F32)
        x1_ref[0, rows, :] = load_x(i) + g1 * mix
        return carry

    lax.fori_loop(0, T // ROW_TILE, out_body, 0)

    if fuse_ffn:
        def ffn_body(i, carry):
            rows = pl.ds(pl.multiple_of(i * ROW_TILE, ROW_TILE), ROW_TILE)
            y_ref[0, rows, :] = _ffn_rows(x1_ref[0, rows, :], mrow, n2g_ref[...], fg_ref[...], w1_ref, w2_ref)
            return carry

        lax.fori_loop(0, T // ROW_TILE, ffn_body, 0)


def _const_spec(shape):
    nd = len(shape)
    return pl.BlockSpec(shape, lambda b, _n=nd: (0,) * _n, pipeline_mode=pl.Buffered(1))


def _nbytes(shape, dtype):
    return int(np.prod(shape)) * jnp.dtype(dtype).itemsize


def _mixer(x, m3, m_off, m_step, pos, consts, ffn_consts, states, emit_state):
    B, T, _ = x.shape
    has_pos = pos is not None
    has_state = states is not None
    scratch_shapes = [
        ((T, PROJ_COLS), F32),
        ((T + 16, LRU_WIDTH), F32),
        ((T, D_MODEL), BF16),
        ((8, HG_DK, HG_DK), F32),
        ((T, N_DIR * HG_HEADS * HG_DK), F32),
        ((LRU_WIDTH // 128, T, 128), F32),
        ((T // CHUNK, HG_HEADS, HG_DK, CHUNK), BF16),
        ((3 * N_DIR * (LRU_WIDTH // LRU_GROUP), LRU_SEGS, LRU_GROUP), F32),
        ((STAGE_SLOTS, LV_STATE + 1, HG_DK, CHUNK), BF16),
        ((LRU_WIDTH // LRU_GROUP, LRU_GROUP, 2 * N_DIR * LRU_GROUP), BF16),
    ]
    resident = (sum(_nbytes(s, d) for s, d in scratch_shapes)
                + sum(_nbytes(c.shape, c.dtype) for c in consts)
                + (sum(_nbytes(p.shape, p.dtype) for p in pos) if has_pos else 0))
    io_block = _nbytes((T, D_MODEL), F32)
    budget = VMEM_LIMIT - VMEM_HEADROOM
    lrs_shape = ((T, 2 * LRS_SEG_COL), F32)
    fuse_scan = resident + _nbytes(*lrs_shape) + 4 * io_block <= budget
    if fuse_scan:
        scratch_shapes.append(lrs_shape)
        resident += _nbytes(*lrs_shape)
    ffn_bytes = sum(_nbytes(c.shape, c.dtype) for c in ffn_consts) + io_block
    fuse_ffn = resident + ffn_bytes + 4 * io_block <= budget
    if fuse_ffn:
        scratch_shapes.insert(0, ((1, T, D_MODEL), F32))
        resident += ffn_bytes
    in_bufs = 2 if resident + 3 * io_block <= budget else 1
    out_bufs = 2 if resident + 4 * io_block <= budget else 1
    io_mode = pl.Buffered(out_bufs)
    in_specs = [
        pl.BlockSpec((1, T, D_MODEL), lambda b: (b, 0, 0), pipeline_mode=pl.Buffered(in_bufs)),
        pl.BlockSpec((1, 1, 6 * D_MODEL), lambda b: (m_off + m_step * b, 0, 0)),
    ]
    args = [x, m3]
    if has_pos:
        in_specs += [_const_spec(p.shape) for p in pos]
        args += list(pos)
    hbm_spec = pl.BlockSpec(memory_space=pl.ANY)
    late = [consts[-1]]
    in_specs += [_const_spec(c.shape) for c in consts[:-1]] + [hbm_spec]
    args += list(consts)
    if fuse_ffn:
        late += list(ffn_consts[-2:])
        in_specs += [_const_spec(c.shape) for c in ffn_consts[:-2]] + [hbm_spec, hbm_spec]
        args += list(ffn_consts)
    if has_state:
        hs0, ls0 = states
        in_specs += [
            pl.BlockSpec((1, 8, HG_DK, HG_DK), lambda b: (b, 0, 0, 0)),
            pl.BlockSpec((1, N_DIR, LRU_WIDTH), lambda b: (b, 0, 0)),
        ]
        args += [hs0, ls0]
    out_shape = [jax.ShapeDtypeStruct((B, T, D_MODEL), F32)]
    out_specs = [pl.BlockSpec((1, T, D_MODEL), lambda b: (b, 0, 0), pipeline_mode=io_mode)]
    if emit_state:
        out_shape += [jax.ShapeDtypeStruct((B, 8, HG_DK, HG_DK), F32),
                      jax.ShapeDtypeStruct((B, N_DIR, LRU_WIDTH), F32)]
        out_specs += [pl.BlockSpec((1, 8, HG_DK, HG_DK), lambda b: (b, 0, 0, 0)),
                      pl.BlockSpec((1, N_DIR, LRU_WIDTH), lambda b: (b, 0, 0))]
    scratch = [pltpu.VMEM(s, d) for s, d in scratch_shapes]
    scratch += [pltpu.VMEM(w.shape, w.dtype) for w in late] + [pltpu.SemaphoreType.DMA((len(late),))]
    outs = pl.pallas_call(
        functools.partial(_mixer_kernel, T=T, has_pos=has_pos, has_state=has_state, emit_state=emit_state,
                          fuse_scan=fuse_scan, fuse_ffn=fuse_ffn),
        grid=(B,),
        in_specs=in_specs,
        out_specs=out_specs,
        out_shape=out_shape,
        scratch_shapes=scratch,
        compiler_params=pltpu.CompilerParams(
            dimension_semantics=("arbitrary",), vmem_limit_bytes=VMEM_LIMIT),
        name=f"mixer_t{T}",
    )(*args)
    return (outs[0], fuse_ffn) + tuple(outs[1:])


def _ffn_rows(x, mrow, n2g, fgain, w1_ref, w2_ref):
    sh2 = mrow[:, 3 * D_MODEL:4 * D_MODEL]
    sc2 = mrow[:, 4 * D_MODEL:5 * D_MODEL]
    g2 = mrow[:, 5 * D_MODEL:6 * D_MODEL]
    ms = jnp.mean(x * x, axis=-1, keepdims=True)
    hb = (x * lax.rsqrt(ms + EPS) * (n2g * (1.0 + sc2)) + sh2).astype(BF16)
    ff = jnp.zeros(x.shape, F32)
    for c in range(D_FF // FF_CHUNK):
        a = jnp.dot(hb, w1_ref[:, c * FF_CHUNK:(c + 1) * FF_CHUNK], preferred_element_type=F32)
        a = jnp.maximum(a, 0.0)
        ff = ff + jnp.dot((a * a).astype(BF16), w2_ref[c * FF_CHUNK:(c + 1) * FF_CHUNK, :],
                          preferred_element_type=F32)
    x2 = x + g2 * ff
    ms2 = jnp.mean(x2 * x2, axis=-1, keepdims=True)
    return x2 * lax.rsqrt(ms2 + EPS) * fgain


def _ffn_kernel(x_ref, m_ref, n2g_ref, fg_ref, w1_ref, w2_ref, y_ref):
    y_ref[...] = _ffn_rows(x_ref[...], m_ref[0], n2g_ref[...], fg_ref[...], w1_ref, w2_ref)


def _ffn(x1, m3, m_off, tiles_per_cond, n2g, fgain, w1, w2):
    n = x1.shape[0]

    def m_index(i):
        if tiles_per_cond is None:
            return (m_off, 0, 0)
        return (m_off + i // tiles_per_cond, 0, 0)

    return pl.pallas_call(
        _ffn_kernel,
        grid=(n // FFN_TILE,),
        in_specs=[
            pl.BlockSpec((FFN_TILE, D_MODEL), lambda i: (i, 0)),
            pl.BlockSpec((1, 1, 6 * D_MODEL), m_index),
            _const_spec(n2g.shape),
            _const_spec(fgain.shape),
            _const_spec(w1.shape),
            _const_spec(w2.shape),
        ],
        out_specs=pl.BlockSpec((FFN_TILE, D_MODEL), lambda i: (i, 0)),
        out_shape=jax.ShapeDtypeStruct((n, D_MODEL), F32),
        compiler_params=pltpu.CompilerParams(
            dimension_semantics=("arbitrary",), vmem_limit_bytes=VMEM_LIMIT),
        name="ffn",
    )(x1, m3, n2g, fgain, w1, w2)


def _grid_pos_tables(n_tok):
    quarter = D_MODEL // 4
    omega = (1.0 / (np.float32(POS_BASE) ** (np.arange(quarter, dtype=np.float32) / np.float32(quarter)))
             ).astype(np.float32)

    def emb(n):
        ang = np.arange(n).reshape(-1, 1).astype(np.float32) * omega
        return np.concatenate([np.sin(ang), np.cos(ang)], axis=-1)

    rows = np.repeat(emb(n_tok // GRID_W)[:, None, :], 8, axis=1)
    return jnp.asarray(rows, dtype=F32), jnp.asarray(emb(GRID_W), dtype=F32)


def kernel(x_prompt, x_sample, c, state_hgrn, state_rglru, c_ctx, w_ada, b_ada, norm1_gain, norm2_gain,
           w_in, hg_lb_logits, hg_norm_gain, conv_w, conv_b, lru_wa, lru_ba, lru_wx, lru_bx, lru_lambda,
           w_out, w_ff1, w_ff2, final_gain):
    bp, tp, _ = x_prompt.shape
    bs_, ts, _ = x_sample.shape

    m3, (w_in_b, w_out_b, w1, w2) = _modulation(
        c_ctx, c, w_ada[0], b_ada, [w_in[0], w_out[0], w_ff1[0], w_ff2[0]])

    masks_np, tri_np = _level_tables()
    consts = [
        norm1_gain,
        w_in_b,
        hg_lb_logits.reshape(2, N_DIR * HG_HEADS, HG_DK),
        hg_norm_gain[0].reshape(1, HG_WIDTH),
        jnp.asarray(masks_np),
        jnp.asarray(tri_np, dtype=BF16),
        conv_w[0],
        conv_b,
        lru_wa[0].reshape(N_DIR * LRU_BLOCKS, LRU_BLOCK, LRU_BLOCK),
        lru_wx[0].reshape(N_DIR * LRU_BLOCKS, LRU_BLOCK, LRU_BLOCK),
        lru_ba[0],
        lru_bx[0],
        lru_lambda[0],
        w_out_b,
    ]
    fgain = final_gain.reshape(1, D_MODEL)
    ffn_consts = [norm2_gain, fgain, w1, w2]

    y_prompt, done, hs, ls = _mixer(x_prompt, m3, 0, 0, None, consts, ffn_consts, None, True)
    if not done:
        y_prompt = _ffn(y_prompt.reshape(bp * tp, D_MODEL), m3, 0, None, *ffn_consts)

    y_sample, done = _mixer(x_sample, m3, 1, 1, _grid_pos_tables(ts), consts, ffn_consts,
                            (state_hgrn.reshape(bs_, N_DIR * HG_HEADS, HG_DK, HG_DK),
                             state_rglru.reshape(bs_, N_DIR, LRU_WIDTH)), False)
    if not done:
        y_sample = _ffn(y_sample.reshape(bs_ * ts, D_MODEL), m3, 1, ts // FFN_TILE, *ffn_consts)

    return (y_prompt.reshape(bp, tp, D_MODEL),
            y_sample.reshape(bs_, ts, D_MODEL),
            hs.reshape(bp, 1, N_DIR, HG_HEADS, HG_DK, HG_DK),
            ls.reshape(bp, 1, N_DIR, LRU_WIDTH))
```

```python
import functools

import numpy as np
import jax
import jax.numpy as jnp
from jax import lax
from jax.experimental import pallas as pl
from jax.experimental.pallas import tpu as pltpu

F32 = jnp.float32
BF16 = jnp.bfloat16

D_MODEL = 1024
N_DIR = 2
HG_HEADS = 4
HG_DK = 128
HG_WIDTH = 512
LRU_WIDTH = 512
LRU_BLOCKS = 8
LRU_BLOCK = 64
LRU_C = 8.0
D_FF = 4096
IN_COLS = 4096
EPS = 1e-6
LOG2E = 1.4426950408889634
GRID_W = 64
POS_BASE = 10000.0

COL_Q = 0
COL_F = 1024
COL_V = 2048
COL_GATE = 2560
COL_LX = 3072
COL_LG = 3584
PROJ_LG = COL_LX
PROJ_COLS = IN_COLS - LRU_WIDTH

CHUNK = 128
LEVELS = (64, 32, 16, 8, 4)
LV_4 = LEVELS.index(4)
LV_21 = len(LEVELS)
LV_DIAG = LV_21 + 1
LV_STATE = LV_DIAG + 1
N_MASKS = LV_DIAG + 1
ROW_TILE = 256
LRU_SEGS = 8
CONV_TILE = 32
SCAN_UNROLL = 8
FIX_TILE = 64
STAGE_SLOTS = 1
MAX_INLINE_TRIPS = 2
LRU_GROUP = 256
GATE_D = 2 * LRU_GROUP
GATE_G = N_DIR * GATE_D
LRS_SEG_COL = 2 * N_DIR * LRU_WIDTH
PROJ_STEP = 512
FFN_TILE = 512
FF_CHUNK = 1024
MOD_STEPS = 8
MOD_ROWS = 8
VMEM_LIMIT = 58 * 1024 * 1024
VMEM_HEADROOM = 6 * 1024 * 1024


def _sig(x):
    return 0.5 * jnp.tanh(0.5 * x) + 0.5


def _silu(x):
    h = 0.5 * x
    return h * jnp.tanh(h) + h


def _nt_dot(a, b):
    return lax.dot_general(a, b, (((1,), (1,)), ((), ())), preferred_element_type=F32)


def _staged_transpose(xb, slot_ref):
    slot_ref[...] = xb.T
    return slot_ref[...]


def _gram(x, slot_ref):
    xb = x.astype(BF16)
    return jnp.dot(xb, _staged_transpose(xb, slot_ref), preferred_element_type=F32)


def _level_tables():
    t = np.arange(CHUNK)[:, None]
    s = np.arange(CHUNK)[None, :]
    masks = np.zeros((N_DIR, N_MASKS, CHUNK, CHUNK), np.float32)
    for li, h in enumerate(LEVELS):
        same = (t // (2 * h)) == (s // (2 * h))
        t_hi = (t // h) % 2 == 1
        s_hi = (s // h) % 2 == 1
        masks[0, li] = same & t_hi & ~s_hi
        masks[1, li] = same & ~t_hi & s_hi
    same4 = (t // 4) == (s // 4)
    masks[0, LV_21] = same4 & (s < t)
    masks[1, LV_21] = same4 & (s > t)
    masks[:, LV_DIAG] = (t == s)
    tri = np.stack([(s <= t), (s >= t)]).astype(np.float32)
    return masks, tri


def _mod_kernel(cctx_ref, c_ref, w_ref, b_ref, *rest):
    n_w = (len(rest) - 2) // 2
    f32_refs, o_ref, bf16_refs, cond = rest[:n_w], rest[n_w], rest[n_w + 1:2 * n_w + 1], rest[-1]
    n = c_ref.shape[0]
    cond[...] = jnp.zeros(cond.shape, F32)
    cond[0:1, :] = cctx_ref[...]
    cond[1:1 + n, :] = c_ref[...]
    c = cond[...]
    a = _silu(c).astype(BF16)
    m = jnp.dot(a, w_ref[...].astype(BF16), preferred_element_type=F32) + b_ref[...]
    for r in range(MOD_ROWS):
        o_ref[r] = m[r:r + 1]
    for src, dst in zip(f32_refs, bf16_refs):
        dst[...] = src[...].astype(BF16)


def _modulation(c_ctx, c, w_ada, b_ada, weights):
    n = w_ada.shape[1]
    assert 1 + c.shape[0] <= MOD_ROWS
    tile = n // MOD_STEPS
    slab = lambda w: pl.BlockSpec((w.shape[0] // MOD_STEPS, w.shape[1]), lambda j: (j, 0))
    outs = pl.pallas_call(
        _mod_kernel,
        grid=(MOD_STEPS,),
        in_specs=[
            pl.BlockSpec((1, D_MODEL), lambda j: (0, 0)),
            pl.BlockSpec(c.shape, lambda j: (0, 0)),
            pl.BlockSpec((D_MODEL, tile), lambda j: (0, j)),
            pl.BlockSpec((1, tile), lambda j: (0, j)),
        ] + [slab(w) for w in weights],
        out_specs=[pl.BlockSpec((MOD_ROWS, 1, tile), lambda j: (0, 0, j))] + [slab(w) for w in weights],
        out_shape=[jax.ShapeDtypeStruct((MOD_ROWS, 1, n), F32)] + [
            jax.ShapeDtypeStruct(w.shape, BF16) for w in weights],
        scratch_shapes=[pltpu.VMEM((MOD_ROWS, D_MODEL), F32)],
        compiler_params=pltpu.CompilerParams(
            dimension_semantics=("arbitrary",), vmem_limit_bytes=VMEM_LIMIT),
        name="adaln_modulation",
    )(c_ctx.reshape(1, D_MODEL), c, w_ada, b_ada, *weights)
    return outs[0], outs[1:]


def _mixer_kernel(*refs, T, has_pos, has_state, emit_state, fuse_scan, fuse_ffn):
    it = iter(refs)
    x_ref = next(it)
    m_ref = next(it)
    if has_pos:
        posr_ref = next(it)
        posc_ref = next(it)
    n1g_ref = next(it)
    win_ref = next(it)
    lbl_ref = next(it)
    hgg_ref = next(it)
    masks_ref = next(it)
    tri_ref = next(it)
    convw_ref = next(it)
    convb_ref = next(it)
    wa_ref = next(it)
    wx_ref = next(it)
    ba_ref = next(it)
    bx_ref = next(it)
    lam_ref = next(it)
    late_hbm = [next(it)]
    if fuse_ffn:
        n2g_ref = next(it)
        fg_ref = next(it)
        late_hbm += [next(it), next(it)]
    if has_state:
        hs0_ref = next(it)
        ls0_ref = next(it)
    y_ref = next(it)
    if emit_state:
        hs_ref = next(it)
        ls_ref = next(it)
    x1_ref = next(it) if fuse_ffn else y_ref
    proj = next(it)
    lxp = next(it)
    mixin = next(it)
    st = next(it)
    ksc = next(it)
    xci = next(it)
    vts = next(it)
    rowc = next(it)
    tsl = next(it)
    wg = next(it)
    if fuse_scan:
        lrs = next(it)
    late_vmem = [next(it) for _ in late_hbm]
    late_sem = next(it)
    wout_ref = late_vmem[0]
    if fuse_ffn:
        w1_ref, w2_ref = late_vmem[1:]

    L = CHUNK

    def late_copy(k):
        return pltpu.make_async_copy(late_hbm[k], late_vmem[k], late_sem.at[k])

    @pl.when(pl.program_id(0) == 0)
    def _first_step():
        for k in range(len(late_hbm)):
            late_copy(k).start()
        per_group = LRU_GROUP // LRU_BLOCK
        wg[...] = jnp.zeros(wg.shape, BF16)
        for g in range(LRU_WIDTH // LRU_GROUP):
            for p in range(2 * N_DIR):
                src = wa_ref if p % 2 == 0 else wx_ref
                for n in range(per_group):
                    r = n * LRU_BLOCK
                    col = p * LRU_GROUP + r
                    wg[g, r:r + LRU_BLOCK, col:col + LRU_BLOCK] = src[
                        (p // 2) * LRU_BLOCKS + g * per_group + n].astype(BF16)

    mrow = m_ref[0]
    sh1 = mrow[:, 0:D_MODEL]
    sc1 = mrow[:, D_MODEL:2 * D_MODEL]
    g1 = mrow[:, 2 * D_MODEL:3 * D_MODEL]
    gain1 = n1g_ref[...] * (1.0 + sc1)

    zrows = jnp.zeros((8, LRU_WIDTH), F32)
    lxp[0:8, :] = zrows
    lxp[T + 8:T + 16, :] = zrows

    def load_x(i):
        xt = x_ref[0, pl.ds(pl.multiple_of(i * ROW_TILE, ROW_TILE), ROW_TILE), :]
        if has_pos:
            per_tile = ROW_TILE // GRID_W
            tiles = []
            for s in range(per_tile):
                row_emb = jnp.concatenate([posr_ref[i * per_tile + s]] * (GRID_W // 8), axis=0)
                tiles.append(jnp.concatenate([row_emb, posc_ref[...]], axis=1))
            xt = xt + jnp.concatenate(tiles, axis=0)
        return xt

    def proj_body(i, carry):
        r0 = pl.multiple_of(i * ROW_TILE, ROW_TILE)
        xt = load_x(i)
        ms = jnp.mean(xt * xt, axis=-1, keepdims=True)
        hb = (xt * lax.rsqrt(ms + EPS) * gain1 + sh1).astype(BF16)
        starts = range(0, IN_COLS, PROJ_STEP)
        order = [COL_LX] + [c0 for c0 in starts if c0 < COL_GATE] + [
            c0 for c0 in starts if c0 >= COL_GATE and c0 != COL_LX]
        for c0 in order:
            res = jnp.dot(hb, win_ref[:, c0:c0 + PROJ_STEP], preferred_element_type=F32)
            if c0 == COL_LX:
                lxp[pl.ds(pl.multiple_of(r0 + 8, 8), ROW_TILE), :] = res
            elif c0 == COL_LG:
                proj[pl.ds(r0, ROW_TILE), PROJ_LG:PROJ_LG + LRU_WIDTH] = res
            else:
                proj[pl.ds(r0, ROW_TILE), c0:c0 + PROJ_STEP] = res
        return carry

    l0 = lbl_ref[0]
    l1 = lbl_ref[1]
    lmx = jnp.maximum(l0, l1)
    e0 = jnp.exp(l0 - lmx)
    e1 = jnp.exp(l1 - lmx)
    lb_all = e0 / (e0 + e1)

    def hg_prep(c, carry):
        rows = pl.ds(pl.multiple_of(c * L, L), L)
        for hd in range(HG_HEADS):
            vts[c, hd] = proj[rows, COL_V + hd * HG_DK:COL_V + (hd + 1) * HG_DK].T.astype(BF16)
        for d in range(N_DIR):
            for hd in range(HG_HEADS):
                idx = d * HG_HEADS + hd
                cq = COL_Q + idx * HG_DK
                cf = COL_F + idx * HG_DK
                hq = proj[rows, cq:cq + HG_DK]
                fz = proj[rows, cf:cf + HG_DK]
                proj[rows, cq:cq + HG_DK] = _silu(hq)
                sg = _sig(fz)
                lb = lb_all[idx:idx + 1, :]
                oml = 1.0 - lb
                ksc[rows, idx * HG_DK:(idx + 1) * HG_DK] = oml * (1.0 - sg)
                logf = jnp.log(lb + oml * sg)
                p1 = logf.astype(BF16)
                p2 = (logf - p1.astype(F32)).astype(BF16)
                bb = jnp.dot(tri_ref[d], jnp.concatenate([p1, p2], axis=1), preferred_element_type=F32)
                proj[rows, cf:cf + HG_DK] = (bb[:, 0:HG_DK] + bb[:, HG_DK:2 * HG_DK]) * LOG2E
        return carry

    lax.fori_loop(0, T // ROW_TILE, proj_body, 0)

    S = T // LRU_SEGS
    cw = convw_ref[...]
    cb = convb_ref[...]

    def irows(i0, j, n):
        return pl.ds(pl.multiple_of(LRU_SEGS * i0, 8) + j, n, stride=LRU_SEGS)

    for j in range(LRU_SEGS):
        def conv_body(ti, carry, j=j):
            i0 = pl.multiple_of(ti * CONV_TILE, CONV_TILE)
            win = lxp[pl.ds(pl.multiple_of(j * S + i0, 8), CONV_TILE + 16), :]
            xc = cb
            for tap in range(4):
                xc = xc + win[6 + tap:6 + tap + CONV_TILE] * cw[tap:tap + 1]
            for l in range(LRU_WIDTH // 128):
                xci[l, irows(i0, j, CONV_TILE), :] = xc[:, l * 128:(l + 1) * 128]
            return carry

        lax.fori_loop(0, S // CONV_TILE, conv_body, 0, unroll=S // CONV_TILE <= MAX_INLINE_TRIPS)

    n_grp = LRU_WIDTH // LRU_GROUP
    slabs = LRU_GROUP // 128
    gates = lrs if fuse_scan else proj

    def xc_tile(rows, g):
        return jnp.concatenate([xci[g * slabs + l, rows, :] for l in range(slabs)], axis=1)

    def gate_body(i, carry):
        rows = pl.ds(pl.multiple_of(i * ROW_TILE, ROW_TILE), ROW_TILE)
        for g in range(n_grp):
            gates[rows, g * GATE_G:(g + 1) * GATE_G] = jnp.dot(
                xc_tile(rows, g).astype(BF16), wg[g], preferred_element_type=F32)
        return carry

    lam = lam_ref[...]
    nl = -lam
    c8 = -LRU_C * (jnp.maximum(nl, 0.0) + jnp.log1p(jnp.exp(-jnp.abs(nl))))
    rowi2 = lax.broadcasted_iota(jnp.int32, (8, LRU_GROUP), 0)

    for d in range(N_DIR):
        for g in range(n_grp):
            k3 = 3 * (d * n_grp + g)
            chans = slice(g * LRU_GROUP, (g + 1) * LRU_GROUP)
            for r, row in enumerate((ba_ref[d:d + 1, chans], bx_ref[d:d + 1, chans], c8[d:d + 1, chans])):
                rowc[k3 + r] = jnp.broadcast_to(row, (LRU_SEGS, LRU_GROUP))

    def seg_out(d, rows, g, part):
        if fuse_scan:
            col = LRS_SEG_COL + ((d * n_grp + g) * 2 + part) * LRU_GROUP
            return lrs.at[rows, col:col + LRU_GROUP]
        cols = slice((g * 2 + part) * LRU_GROUP, (g * 2 + part + 1) * LRU_GROUP)
        return (ksc.at[rows, cols] if d == 0 else x1_ref.at[0, rows, cols])

    def lru_inputs(rows8, g, d):
        base = g * GATE_G + d * GATE_D
        k3 = 3 * (d * n_grp + g)
        ga = gates[rows8, base:base + LRU_GROUP] + rowc[k3]
        gx = gates[rows8, base + LRU_GROUP:base + 2 * LRU_GROUP] + rowc[k3 + 1]
        xc8 = xc_tile(rows8, g)
        log_a = rowc[k3 + 2] * _sig(ga)
        a = jnp.exp(log_a)
        z = jnp.tanh(-log_a) * (1.0 + a * a)
        mult = jnp.where(z > 0.0, z * lax.rsqrt(z), 0.0)
        return a, mult * (_sig(gx) * xc8)

    def scan8(a, u, d):
        for sft in (1, 2, 4):
            if d == 0:
                keep = rowi2 >= sft
                amt = sft
            else:
                keep = rowi2 < 8 - sft
                amt = 8 - sft
            ash = jnp.where(keep, pltpu.roll(a, amt, 0), 1.0)
            ush = jnp.where(keep, pltpu.roll(u, amt, 0), 0.0)
            u = a * ush + u
            a = a * ash
        return a, u

    def scan_steps(first, count, carry):
        hs, ds = [list(c) for c in carry[:2]], [list(c) for c in carry[2:]]
        for u in range(count):
            i_f = first + u
            for d, i in ((0, i_f), (1, S - 1 - i_f)):
                rows8 = pl.ds(pl.multiple_of(LRU_SEGS * i, 8), 8)
                for g in range(n_grp):
                    a, uu = lru_inputs(rows8, g, d)
                    hs[d][g] = a * hs[d][g] + uu
                    ds[d][g] = a * ds[d][g]
                    seg_out(d, rows8, g, 0)[...] = hs[d][g]
                    seg_out(d, rows8, g, 1)[...] = ds[d][g]
        return tuple(tuple(c) for c in hs + ds)

    zero8 = jnp.zeros((LRU_SEGS, LRU_GROUP), F32)
    one8 = jnp.ones((LRU_SEGS, LRU_GROUP), F32)
    scan_init = ((zero8,) * n_grp, (zero8,) * n_grp, (one8,) * n_grp, (one8,) * n_grp)
    if fuse_scan:
        lax.fori_loop(0, T // ROW_TILE, gate_body, 0)

    rowi = lax.broadcasted_iota(jnp.int32, (8, HG_DK), 0)
    r4 = rowi & 3
    is_r0 = r4 == 0
    is_r1 = r4 == 1
    is_r2 = r4 == 2
    hi4 = rowi >= 4

    for i in range(N_DIR * HG_HEADS):
        if has_state:
            st[i] = hs0_ref[0, i].T
        else:
            st[i] = jnp.zeros((HG_DK, HG_DK), F32)

    def hg_chunk(c, d):
        r0 = pl.multiple_of(c * L, L)
        rows = pl.ds(r0, L)
        for hd in range(HG_HEADS):
            idx = d * HG_HEADS + hd
            cq = COL_Q + idx * HG_DK
            cf = COL_F + idx * HG_DK
            ck = idx * HG_DK
            slot = idx % STAGE_SLOTS

            def ldq(lo, n, cq=cq):
                return proj[pl.ds(pl.multiple_of(r0 + lo, 8), n), cq:cq + HG_DK]

            def ldb(lo, n, cf=cf):
                return proj[pl.ds(pl.multiple_of(r0 + lo, 8), n), cf:cf + HG_DK]

            def ldk(lo, n, ck=ck):
                return ksc[pl.ds(pl.multiple_of(r0 + lo, 8), n), ck:ck + HG_DK]

            def bline(r, cf=cf):
                grp = proj[pl.ds(pl.multiple_of(r0 + 8 * (r // 8), 8), 8), cf:cf + HG_DK]
                return grp[r % 8:r % 8 + 1, :]

            def brow(r, n):
                return jnp.broadcast_to(bline(r), (n, HG_DK))

            acc_rows = [None] * (L // 8)

            def accumulate(li, p, row0):
                for i in range(p.shape[0] // 8):
                    g = row0 // 8 + i
                    term = masks_ref[d, li, 8 * g:8 * g + 8, :] * p[8 * i:8 * i + 8]
                    acc_rows[g] = term if acc_rows[g] is None else acc_rows[g] + term

            accumulate(LV_DIAG, jnp.dot(
                ldq(0, L).astype(BF16), _staged_transpose(ldk(0, L).astype(BF16), tsl.at[slot, LV_DIAG]),
                preferred_element_type=F32), 0)
            for li, h in enumerate(LEVELS[:LV_4]):
                pieces, q_pieces, q_starts = [], [], []
                for j in range(L // (2 * h)):
                    lo = j * 2 * h
                    mid = lo + h
                    if d == 0:
                        bm = brow(mid - 1, h)
                        kp = ldk(lo, h) * jnp.exp2(bm - ldb(lo, h))
                        qp = ldq(mid, h) * jnp.exp2(ldb(mid, h) - bm)
                        pieces += [kp, qp]
                        q_starts.append(mid)
                    else:
                        bm = brow(mid, h)
                        qp = ldq(lo, h) * jnp.exp2(ldb(lo, h) - bm)
                        kp = ldk(mid, h) * jnp.exp2(bm - ldb(mid, h))
                        pieces += [qp, kp]
                        q_starts.append(lo)
                    q_pieces.append(qp)
                xt = _staged_transpose(jnp.concatenate(pieces, axis=0).astype(BF16), tsl.at[slot,li])
                p = jnp.dot(jnp.concatenate(q_pieces, axis=0).astype(BF16), xt, preferred_element_type=F32)
                for j, row0 in enumerate(q_starts):
                    accumulate(li, p[j * h:(j + 1) * h], row0)
            x4, xq21, xk21 = [], [], []
            for g in range(L // 8):
                qg, kg, bg = ldq(8 * g, 8), ldk(8 * g, 8), ldb(8 * g, 8)
                fg = 1.0 - kg
                qfg = qg * fg
                mid = 8 * g + (3 if d == 0 else 4)
                e4 = jnp.exp2(-jnp.abs(bg - brow(mid, 8)))
                fnx = pltpu.roll(fg, 7, 0)
                fpv = pltpu.roll(fg, 1, 0)
                k_over_f = kg / fg
                if d == 0:
                    x4.append(jnp.where(hi4, qg, kg) * e4)
                    xq21.append(jnp.where(is_r0, 0.0, jnp.where(is_r1, qg, jnp.where(is_r2, qfg, qfg * fpv))))
                    xk21.append(jnp.where(is_r0, kg * fnx, jnp.where(is_r1, kg, jnp.where(is_r2, k_over_f, 0.0))))
                else:
                    x4.append(jnp.where(hi4, kg, qg) * e4)
                    xq21.append(jnp.where(is_r0, qfg * fnx, jnp.where(is_r1, qfg, jnp.where(is_r2, qg, 0.0))))
                    xk21.append(jnp.where(is_r0, 0.0, jnp.where(is_r1, k_over_f, jnp.where(is_r2, kg, kg * fpv))))
            accumulate(LV_4, _gram(jnp.concatenate(x4, axis=0), tsl.at[slot, LV_4]), 0)
            accumulate(LV_21, jnp.dot(
                jnp.concatenate(xq21, axis=0).astype(BF16),
                _staged_transpose(jnp.concatenate(xk21, axis=0).astype(BF16), tsl.at[slot, LV_21]),
                preferred_element_type=F32), 0)
            acc = jnp.concatenate(acc_rows, axis=0)

            vt = vts[c, hd]
            st_t = st[idx]
            b = ldb(0, L)
            qt = (ldq(0, L) * jnp.exp2(b)).astype(BF16)
            vb = proj[rows, COL_V + hd * HG_DK:COL_V + (hd + 1) * HG_DK].astype(BF16)
            o = jnp.dot(jnp.concatenate([acc.astype(BF16), qt], axis=1),
                        jnp.concatenate([vb, _staged_transpose(st_t.astype(BF16), tsl.at[slot, LV_STATE])], axis=0),
                        preferred_element_type=F32)
            btot = bline(L - 1 if d == 0 else 0)
            kt = (ldk(0, L) * jnp.exp2(btot - b)).astype(BF16)
            st[idx] = st_t * jnp.exp2(btot) + jnp.dot(vt, kt, preferred_element_type=F32)
            x1_ref[0, rows, d * HG_WIDTH + hd * HG_DK:d * HG_WIDTH + (hd + 1) * HG_DK] = o

    n_chunks = T // L

    def hg_both(c, carry):
        hg_chunk(c, 0)
        hg_chunk(n_chunks - 1 - c, 1)
        if fuse_scan:
            steps = S // n_chunks
            carry = scan_steps(c * steps, steps, carry)
        return carry

    lax.fori_loop(0, n_chunks, hg_prep, 0, unroll=True if n_chunks <= MAX_INLINE_TRIPS else 2)
    scan_state = lax.fori_loop(0, n_chunks, hg_both, scan_init if fuse_scan else 0)

    @pl.when(pl.program_id(0) == 0)
    def _late_weights_ready():
        for k in range(len(late_hbm)):
            late_copy(k).wait()

    if emit_state:
        for i in range(N_DIR * HG_HEADS):
            hs_ref[0, i] = st[i].T

    def hg_fin(i, carry):
        rows = pl.ds(pl.multiple_of(i * L, L), L)
        for hd in range(HG_HEADS):
            cs = slice(hd * HG_DK, (hd + 1) * HG_DK)
            o = x1_ref[0, rows, cs] + x1_ref[0, rows, HG_WIDTH + hd * HG_DK:HG_WIDTH + (hd + 1) * HG_DK]
            ms = jnp.mean(o * o, axis=-1, keepdims=True)
            y = o * lax.rsqrt(ms + EPS) * hgg_ref[:, cs]
            gz = proj[rows, COL_GATE + hd * HG_DK:COL_GATE + (hd + 1) * HG_DK]
            mixin[rows, cs] = (y * _silu(gz)).astype(BF16)
        return carry

    lax.fori_loop(0, n_chunks, hg_fin, 0, unroll=True if n_chunks <= MAX_INLINE_TRIPS else 2)

    if not fuse_scan:
        lax.fori_loop(0, T // ROW_TILE, gate_body, 0)
        scan_state = lax.fori_loop(
            0, S // SCAN_UNROLL, lambda n, carry: scan_steps(n * SCAN_UNROLL, SCAN_UNROLL, carry), scan_init)
    h_end, d_end = scan_state[:2], scan_state[2:]

    carry_in = [[None] * n_grp for _ in range(N_DIR)]
    for d in range(N_DIR):
        for g in range(n_grp):
            if has_state:
                h0 = ls0_ref[0, d:d + 1, g * LRU_GROUP:(g + 1) * LRU_GROUP]
            else:
                h0 = jnp.zeros((1, LRU_GROUP), F32)
            dd, hh = scan8(d_end[d][g], h_end[d][g], d)
            seg_end = hh + dd * h0
            if d == 0:
                carry_in[d][g] = jnp.where(rowi2 >= 1, pltpu.roll(seg_end, 1, 0), h0)
                last = seg_end[LRU_SEGS - 1:LRU_SEGS, :]
            else:
                carry_in[d][g] = jnp.where(rowi2 < LRU_SEGS - 1, pltpu.roll(seg_end, LRU_SEGS - 1, 0), h0)
                last = seg_end[0:1, :]
            if emit_state:
                ls_ref[0, d:d + 1, g * LRU_GROUP:(g + 1) * LRU_GROUP] = last

    def fix_body(i, carry):
        rows = pl.ds(pl.multiple_of(i * FIX_TILE, FIX_TILE), FIX_TILE)
        for g in range(n_grp):
            tot = None
            for d in range(N_DIR):
                cin = jnp.concatenate([carry_in[d][g]] * (FIX_TILE // LRU_SEGS), axis=0)
                h = seg_out(d, rows, g, 0)[...] + seg_out(d, rows, g, 1)[...] * cin
                tot = h if tot is None else tot + h
            for l in range(slabs):
                xci[g * slabs + l, rows, :] = tot[:, l * 128:(l + 1) * 128]
        return carry

    lax.fori_loop(0, T // FIX_TILE, fix_body, 0)

    for j in range(LRU_SEGS):
        def lru_fin(ti, carry, j=j):
            i0 = pl.multiple_of(ti * CONV_TILE, CONV_TILE)
            rows = pl.ds(pl.multiple_of(j * S + i0, CONV_TILE), CONV_TILE)
            hsum = jnp.concatenate([xci[l, irows(i0, j, CONV_TILE), :] for l in range(LRU_WIDTH // 128)], axis=1)
            lg = proj[rows, PROJ_LG:PROJ_LG + LRU_WIDTH]
            gl = lg * (0.5 * (1.0 + jnp.tanh(0.7978845608028654 * (lg + 0.044715 * (lg * lg * lg)))))
            mixin[rows, HG_WIDTH:HG_WIDTH + LRU_WIDTH] = (hsum * gl).astype(BF16)
            return carry

        lax.fori_loop(0, S // CONV_TILE, lru_fin, 0, unroll=S // CONV_TILE <= MAX_INLINE_TRIPS)

    def out_body(i, carry):
        rows = pl.ds(pl.multiple_of(i * ROW_TILE, ROW_TILE), ROW_TILE)
        mix = jnp.dot(mixin[rows, :], wout_ref[...], preferred_element_type=F32)
        x1_ref[0, rows, :] = load_x(i) + g1 * mix
        return carry

    lax.fori_loop(0, T // ROW_TILE, out_body, 0)

    if fuse_ffn:
        def ffn_body(i, carry):
            rows = pl.ds(pl.multiple_of(i * ROW_TILE, ROW_TILE), ROW_TILE)
            y_ref[0, rows, :] = _ffn_rows(x1_ref[0, rows, :], mrow, n2g_ref[...], fg_ref[...], w1_ref, w2_ref)
            return carry

        lax.fori_loop(0, T // ROW_TILE, ffn_body, 0)


def _const_spec(shape):
    nd = len(shape)
    return pl.BlockSpec(shape, lambda b, _n=nd: (0,) * _n, pipeline_mode=pl.Buffered(1))


def _nbytes(shape, dtype):
    return int(np.prod(shape)) * jnp.dtype(dtype).itemsize


def _mixer(x, m3, m_off, m_step, pos, consts, ffn_consts, states, emit_state):
    B, T, _ = x.shape
    has_pos = pos is not None
    has_state = states is not None
    scratch_shapes = [
        ((T, PROJ_COLS), F32),
        ((T + 16, LRU_WIDTH), F32),
        ((T, D_MODEL), BF16),
        ((8, HG_DK, HG_DK), F32),
        ((T, N_DIR * HG_HEADS * HG_DK), F32),
        ((LRU_WIDTH // 128, T, 128), F32),
        ((T // CHUNK, HG_HEADS, HG_DK, CHUNK), BF16),
        ((3 * N_DIR * (LRU_WIDTH // LRU_GROUP), LRU_SEGS, LRU_GROUP), F32),
        ((STAGE_SLOTS, LV_STATE + 1, HG_DK, CHUNK), BF16),
        ((LRU_WIDTH // LRU_GROUP, LRU_GROUP, 2 * N_DIR * LRU_GROUP), BF16),
    ]
    resident = (sum(_nbytes(s, d) for s, d in scratch_shapes)
                + sum(_nbytes(c.shape, c.dtype) for c in consts)
                + (sum(_nbytes(p.shape, p.dtype) for p in pos) if has_pos else 0))
    io_block = _nbytes((T, D_MODEL), F32)
    budget = VMEM_LIMIT - VMEM_HEADROOM
    lrs_shape = ((T, 2 * LRS_SEG_COL), F32)
    fuse_scan = resident + _nbytes(*lrs_shape) + 4 * io_block <= budget
    if fuse_scan:
        scratch_shapes.append(lrs_shape)
        resident += _nbytes(*lrs_shape)
    ffn_bytes = sum(_nbytes(c.shape, c.dtype) for c in ffn_consts) + io_block
    fuse_ffn = resident + ffn_bytes + 4 * io_block <= budget
    if fuse_ffn:
        scratch_shapes.insert(0, ((1, T, D_MODEL), F32))
        resident += ffn_bytes
    in_bufs = 2 if resident + 3 * io_block <= budget else 1
    out_bufs = 2 if resident + 4 * io_block <= budget else 1
    io_mode = pl.Buffered(out_bufs)
    in_specs = [
        pl.BlockSpec((1, T, D_MODEL), lambda b: (b, 0, 0), pipeline_mode=pl.Buffered(in_bufs)),
        pl.BlockSpec((1, 1, 6 * D_MODEL), lambda b: (m_off + m_step * b, 0, 0)),
    ]
    args = [x, m3]
    if has_pos:
        in_specs += [_const_spec(p.shape) for p in pos]
        args += list(pos)
    hbm_spec = pl.BlockSpec(memory_space=pl.ANY)
    late = [consts[-1]]
    in_specs += [_const_spec(c.shape) for c in consts[:-1]] + [hbm_spec]
    args += list(consts)
    if fuse_ffn:
        late += list(ffn_consts[-2:])
        in_specs += [_const_spec(c.shape) for c in ffn_consts[:-2]] + [hbm_spec, hbm_spec]
        args += list(ffn_consts)
    if has_state:
        hs0, ls0 = states
        in_specs += [
            pl.BlockSpec((1, 8, HG_DK, HG_DK), lambda b: (b, 0, 0, 0)),
            pl.BlockSpec((1, N_DIR, LRU_WIDTH), lambda b: (b, 0, 0)),
        ]
        args += [hs0, ls0]
    out_shape = [jax.ShapeDtypeStruct((B, T, D_MODEL), F32)]
    out_specs = [pl.BlockSpec((1, T, D_MODEL), lambda b: (b, 0, 0), pipeline_mode=io_mode)]
    if emit_state:
        out_shape += [jax.ShapeDtypeStruct((B, 8, HG_DK, HG_DK), F32),
                      jax.ShapeDtypeStruct((B, N_DIR, LRU_WIDTH), F32)]
        out_specs += [pl.BlockSpec((1, 8, HG_DK, HG_DK), lambda b: (b, 0, 0, 0)),
                      pl.BlockSpec((1, N_DIR, LRU_WIDTH), lambda b: (b, 0, 0))]
    scratch = [pltpu.VMEM(s, d) for s, d in scratch_shapes]
    scratch += [pltpu.VMEM(w.shape, w.dtype) for w in late] + [pltpu.SemaphoreType.DMA((len(late),))]
    outs = pl.pallas_call(
        functools.partial(_mixer_kernel, T=T, has_pos=has_pos, has_state=has_state, emit_state=emit_state,
                          fuse_scan=fuse_scan, fuse_ffn=fuse_ffn),
        grid=(B,),
        in_specs=in_specs,
        out_specs=out_specs,
        out_shape=out_shape,
        scratch_shapes=scratch,
        compiler_params=pltpu.CompilerParams(
            dimension_semantics=("arbitrary",), vmem_limit_bytes=VMEM_LIMIT),
        name=f"mixer_t{T}",
    )(*args)
    return (outs[0], fuse_ffn) + tuple(outs[1:])


def _ffn_rows(x, mrow, n2g, fgain, w1_ref, w2_ref):
    sh2 = mrow[:, 3 * D_MODEL:4 * D_MODEL]
    sc2 = mrow[:, 4 * D_MODEL:5 * D_MODEL]
    g2 = mrow[:, 5 * D_MODEL:6 * D_MODEL]
    ms = jnp.mean(x * x, axis=-1, keepdims=True)
    hb = (x * lax.rsqrt(ms + EPS) * (n2g * (1.0 + sc2)) + sh2).astype(BF16)
    ff = jnp.zeros(x.shape, F32)
    for c in range(D_FF // FF_CHUNK):
        a = jnp.dot(hb, w1_ref[:, c * FF_CHUNK:(c + 1) * FF_CHUNK], preferred_element_type=F32)
        a = jnp.maximum(a, 0.0)
        ff = ff + jnp.dot((a * a).astype(BF16), w2_ref[c * FF_CHUNK:(c + 1) * FF_CHUNK, :],
                          preferred_element_type=F32)
    x2 = x + g2 * ff
    ms2 = jnp.mean(x2 * x2, axis=-1, keepdims=True)
    return x2 * lax.rsqrt(ms2 + EPS) * fgain


def _ffn_kernel(x_ref, m_ref, n2g_ref, fg_ref, w1_ref, w2_ref, y_ref):
    y_ref[...] = _ffn_rows(x_ref[...], m_ref[0], n2g_ref[...], fg_ref[...], w1_ref, w2_ref)


def _ffn(x1, m3, m_off, tiles_per_cond, n2g, fgain, w1, w2):
    n = x1.shape[0]

    def m_index(i):
        if tiles_per_cond is None:
            return (m_off, 0, 0)
        return (m_off + i // tiles_per_cond, 0, 0)

    return pl.pallas_call(
        _ffn_kernel,
        grid=(n // FFN_TILE,),
        in_specs=[
            pl.BlockSpec((FFN_TILE, D_MODEL), lambda i: (i, 0)),
            pl.BlockSpec((1, 1, 6 * D_MODEL), m_index),
            _const_spec(n2g.shape),
            _const_spec(fgain.shape),
            _const_spec(w1.shape),
            _const_spec(w2.shape),
        ],
        out_specs=pl.BlockSpec((FFN_TILE, D_MODEL), lambda i: (i, 0)),
        out_shape=jax.ShapeDtypeStruct((n, D_MODEL), F32),
        compiler_params=pltpu.CompilerParams(
            dimension_semantics=("arbitrary",), vmem_limit_bytes=VMEM_LIMIT),
        name="ffn",
    )(x1, m3, n2g, fgain, w1, w2)


def _grid_pos_tables(n_tok):
    quarter = D_MODEL // 4
    omega = (1.0 / (np.float32(POS_BASE) ** (np.arange(quarter, dtype=np.float32) / np.float32(quarter)))
             ).astype(np.float32)

    def emb(n):
        ang = np.arange(n).reshape(-1, 1).astype(np.float32) * omega
        return np.concatenate([np.sin(ang), np.cos(ang)], axis=-1)

    rows = np.repeat(emb(n_tok // GRID_W)[:, None, :], 8, axis=1)
    return jnp.asarray(rows, dtype=F32), jnp.asarray(emb(GRID_W), dtype=F32)


def kernel(x_prompt, x_sample, c, state_hgrn, state_rglru, c_ctx, w_ada, b_ada, norm1_gain, norm2_gain,
           w_in, hg_lb_logits, hg_norm_gain, conv_w, conv_b, lru_wa, lru_ba, lru_wx, lru_bx, lru_lambda,
           w_out, w_ff1, w_ff2, final_gain):
    bp, tp, _ = x_prompt.shape
    bs_, ts, _ = x_sample.shape

    m3, (w_in_b, w_out_b, w1, w2) = _modulation(
        c_ctx, c, w_ada[0], b_ada, [w_in[0], w_out[0], w_ff1[0], w_ff2[0]])

    masks_np, tri_np = _level_tables()
    consts = [
        norm1_gain,
        w_in_b,
        hg_lb_logits.reshape(2, N_DIR * HG_HEADS, HG_DK),
        hg_norm_gain[0].reshape(1, HG_WIDTH),
        jnp.asarray(masks_np),
        jnp.asarray(tri_np, dtype=BF16),
        conv_w[0],
        conv_b,
        lru_wa[0].reshape(N_DIR * LRU_BLOCKS, LRU_BLOCK, LRU_BLOCK),
        lru_wx[0].reshape(N_DIR * LRU_BLOCKS, LRU_BLOCK, LRU_BLOCK),
        lru_ba[0],
        lru_bx[0],
        lru_lambda[0],
        w_out_b,
    ]
    fgain = final_gain.reshape(1, D_MODEL)
    ffn_consts = [norm2_gain, fgain, w1, w2]

    y_prompt, done, hs, ls = _mixer(x_prompt, m3, 0, 0, None, consts, ffn_consts, None, True)
    if not done:
        y_prompt = _ffn(y_prompt.reshape(bp * tp, D_MODEL), m3, 0, None, *ffn_consts)

    y_sample, done = _mixer(x_sample, m3, 1, 1, _grid_pos_tables(ts), consts, ffn_consts,
                            (state_hgrn.reshape(bs_, N_DIR * HG_HEADS, HG_DK, HG_DK),
                             state_rglru.reshape(bs_, N_DIR, LRU_WIDTH)), False)
    if not done:
        y_sample = _ffn(y_sample.reshape(bs_ * ts, D_MODEL), m3, 1, ts // FFN_TILE, *ffn_consts)

    return (y_prompt.reshape(bp, tp, D_MODEL),
            y_sample.reshape(bs_, ts, D_MODEL),
            hs.reshape(bp, 1, N_DIR, HG_HEADS, HG_DK, HG_DK),
            ls.reshape(bp, 1, N_DIR, LRU_WIDTH))
```

```python
import functools

import numpy as np
import jax
import jax.numpy as jnp
from jax import lax
from jax.experimental import pallas as pl
from jax.experimental.pallas import tpu as pltpu

F32 = jnp.float32
BF16 = jnp.bfloat16

D_MODEL = 1024
N_DIR = 2
HG_HEADS = 4
HG_DK = 128
HG_WIDTH = 512
LRU_WIDTH = 512
LRU_BLOCKS = 8
LRU_BLOCK = 64
LRU_C = 8.0
D_FF = 4096
IN_COLS = 4096
EPS = 1e-6
LOG2E = 1.4426950408889634
GRID_W = 64
POS_BASE = 10000.0

COL_Q = 0
COL_F = 1024
COL_V = 2048
COL_GATE = 2560
COL_LX = 3072
COL_LG = 3584
PROJ_LG = COL_LX
PROJ_COLS = IN_COLS - LRU_WIDTH

CHUNK = 128
LEVELS = (64, 32, 16, 8, 4)
LV_4 = LEVELS.index(4)
LV_21 = len(LEVELS)
LV_DIAG = LV_21 + 1
LV_STATE = LV_DIAG + 1
N_MASKS = LV_DIAG + 1
ROW_TILE = 256
LRU_SEGS = 8
CONV_TILE = 32
SCAN_UNROLL = 8
FIX_TILE = 64
STAGE_SLOTS = 1
MAX_INLINE_TRIPS = 2
LOOP_UNROLL = 2
LRU_GROUP = 256
GATE_D = 2 * LRU_GROUP
GATE_G = N_DIR * GATE_D
LRS_SEG_COL = 2 * N_DIR * LRU_WIDTH
PROJ_STEP = 512
FFN_TILE = 512
FF_CHUNK = 1024
MOD_STEPS = 8
MOD_ROWS = 8
VMEM_LIMIT = 58 * 1024 * 1024
VMEM_HEADROOM = 6 * 1024 * 1024


def _sig(x):
    return 0.5 * jnp.tanh(0.5 * x) + 0.5


def _silu(x):
    h = 0.5 * x
    return h * jnp.tanh(h) + h


def _unroll(trips):
    return True if trips <= MAX_INLINE_TRIPS else LOOP_UNROLL


def _nt_dot(a, b):
    return lax.dot_general(a, b, (((1,), (1,)), ((), ())), preferred_element_type=F32)


def _staged_transpose(xb, slot_ref):
    slot_ref[...] = xb.T
    return slot_ref[...]


def _gram(x, slot_ref):
    xb = x.astype(BF16)
    return jnp.dot(xb, _staged_transpose(xb, slot_ref), preferred_element_type=F32)


def _level_tables():
    t = np.arange(CHUNK)[:, None]
    s = np.arange(CHUNK)[None, :]
    masks = np.zeros((N_DIR, N_MASKS, CHUNK, CHUNK), np.float32)
    for li, h in enumerate(LEVELS):
        same = (t // (2 * h)) == (s // (2 * h))
        t_hi = (t // h) % 2 == 1
        s_hi = (s // h) % 2 == 1
        masks[0, li] = same & t_hi & ~s_hi
        masks[1, li] = same & ~t_hi & s_hi
    same4 = (t // 4) == (s // 4)
    masks[0, LV_21] = same4 & (s < t)
    masks[1, LV_21] = same4 & (s > t)
    masks[:, LV_DIAG] = (t == s)
    tri = np.stack([(s <= t), (s >= t)]).astype(np.float32)
    return masks, tri


def _mod_kernel(cctx_ref, c_ref, w_ref, b_ref, *rest):
    n_w = (len(rest) - 2) // 2
    f32_refs, o_ref, bf16_refs, cond = rest[:n_w], rest[n_w], rest[n_w + 1:2 * n_w + 1], rest[-1]
    n = c_ref.shape[0]
    cond[...] = jnp.zeros(cond.shape, F32)
    cond[0:1, :] = cctx_ref[...]
    cond[1:1 + n, :] = c_ref[...]
    c = cond[...]
    a = _silu(c).astype(BF16)
    m = jnp.dot(a, w_ref[...].astype(BF16), preferred_element_type=F32) + b_ref[...]
    for r in range(MOD_ROWS):
        o_ref[r] = m[r:r + 1]
    for src, dst in zip(f32_refs, bf16_refs):
        dst[...] = src[...].astype(BF16)


def _modulation(c_ctx, c, w_ada, b_ada, weights):
    n = w_ada.shape[1]
    assert 1 + c.shape[0] <= MOD_ROWS
    tile = n // MOD_STEPS
    slab = lambda w: pl.BlockSpec((w.shape[0] // MOD_STEPS, w.shape[1]), lambda j: (j, 0))
    outs = pl.pallas_call(
        _mod_kernel,
        grid=(MOD_STEPS,),
        in_specs=[
            pl.BlockSpec((1, D_MODEL), lambda j: (0, 0)),
            pl.BlockSpec(c.shape, lambda j: (0, 0)),
            pl.BlockSpec((D_MODEL, tile), lambda j: (0, j)),
            pl.BlockSpec((1, tile), lambda j: (0, j)),
        ] + [slab(w) for w in weights],
        out_specs=[pl.BlockSpec((MOD_ROWS, 1, tile), lambda j: (0, 0, j))] + [slab(w) for w in weights],
        out_shape=[jax.ShapeDtypeStruct((MOD_ROWS, 1, n), F32)] + [
            jax.ShapeDtypeStruct(w.shape, BF16) for w in weights],
        scratch_shapes=[pltpu.VMEM((MOD_ROWS, D_MODEL), F32)],
        compiler_params=pltpu.CompilerParams(
            dimension_semantics=("arbitrary",), vmem_limit_bytes=VMEM_LIMIT),
        name="adaln_modulation",
    )(c_ctx.reshape(1, D_MODEL), c, w_ada, b_ada, *weights)
    return outs[0], outs[1:]


def _mixer_kernel(*refs, T, has_pos, has_state, emit_state, fuse_scan, fuse_ffn):
    it = iter(refs)
    x_ref = next(it)
    m_ref = next(it)
    if has_pos:
        posr_ref = next(it)
        posc_ref = next(it)
    n1g_ref = next(it)
    win_ref = next(it)
    lbl_ref = next(it)
    hgg_ref = next(it)
    masks_ref = next(it)
    tri_ref = next(it)
    convw_ref = next(it)
    convb_ref = next(it)
    wa_ref = next(it)
    wx_ref = next(it)
    ba_ref = next(it)
    bx_ref = next(it)
    lam_ref = next(it)
    late_hbm = [next(it)]
    if fuse_ffn:
        n2g_ref = next(it)
        fg_ref = next(it)
        late_hbm += [next(it), next(it)]
    if has_state:
        hs0_ref = next(it)
        ls0_ref = next(it)
    y_ref = next(it)
    if emit_state:
        hs_ref = next(it)
        ls_ref = next(it)
    x1_ref = next(it) if fuse_ffn else y_ref
    proj = next(it)
    lxp = next(it)
    mixin = next(it)
    st = next(it)
    ksc = next(it)
    xci = next(it)
    vts = next(it)
    rowc = next(it)
    tsl = next(it)
    wg = next(it)
    if fuse_scan:
        lrs = next(it)
    late_vmem = [next(it) for _ in late_hbm]
    late_sem = next(it)
    wout_ref = late_vmem[0]
    if fuse_ffn:
        w1_ref, w2_ref = late_vmem[1:]

    L = CHUNK

    def late_copy(k):
        return pltpu.make_async_copy(late_hbm[k], late_vmem[k], late_sem.at[k])

    @pl.when(pl.program_id(0) == 0)
    def _first_step():
        for k in range(len(late_hbm)):
            late_copy(k).start()
        per_group = LRU_GROUP // LRU_BLOCK
        wg[...] = jnp.zeros(wg.shape, BF16)
        for g in range(LRU_WIDTH // LRU_GROUP):
            for p in range(2 * N_DIR):
                src = wa_ref if p % 2 == 0 else wx_ref
                for n in range(per_group):
                    r = n * LRU_BLOCK
                    col = p * LRU_GROUP + r
                    wg[g, r:r + LRU_BLOCK, col:col + LRU_BLOCK] = src[
                        (p // 2) * LRU_BLOCKS + g * per_group + n].astype(BF16)

    mrow = m_ref[0]
    sh1 = mrow[:, 0:D_MODEL]
    sc1 = mrow[:, D_MODEL:2 * D_MODEL]
    g1 = mrow[:, 2 * D_MODEL:3 * D_MODEL]
    gain1 = n1g_ref[...] * (1.0 + sc1)

    zrows = jnp.zeros((8, LRU_WIDTH), F32)
    lxp[0:8, :] = zrows
    lxp[T + 8:T + 16, :] = zrows

    def load_x(i):
        xt = x_ref[0, pl.ds(pl.multiple_of(i * ROW_TILE, ROW_TILE), ROW_TILE), :]
        if has_pos:
            per_tile = ROW_TILE // GRID_W
            tiles = []
            for s in range(per_tile):
                row_emb = jnp.concatenate([posr_ref[i * per_tile + s]] * (GRID_W // 8), axis=0)
                tiles.append(jnp.concatenate([row_emb, posc_ref[...]], axis=1))
            xt = xt + jnp.concatenate(tiles, axis=0)
        return xt

    def proj_body(i, carry):
        r0 = pl.multiple_of(i * ROW_TILE, ROW_TILE)
        xt = load_x(i)
        ms = jnp.mean(xt * xt, axis=-1, keepdims=True)
        hb = (xt * lax.rsqrt(ms + EPS) * gain1 + sh1).astype(BF16)
        starts = range(0, IN_COLS, PROJ_STEP)
        order = [COL_LX] + [c0 for c0 in starts if c0 < COL_GATE] + [
            c0 for c0 in starts if c0 >= COL_GATE and c0 != COL_LX]
        for c0 in order:
            res = jnp.dot(hb, win_ref[:, c0:c0 + PROJ_STEP], preferred_element_type=F32)
            if c0 == COL_LX:
                lxp[pl.ds(pl.multiple_of(r0 + 8, 8), ROW_TILE), :] = res
            elif c0 == COL_LG:
                proj[pl.ds(r0, ROW_TILE), PROJ_LG:PROJ_LG + LRU_WIDTH] = res
            else:
                proj[pl.ds(r0, ROW_TILE), c0:c0 + PROJ_STEP] = res
        return carry

    l0 = lbl_ref[0]
    l1 = lbl_ref[1]
    lmx = jnp.maximum(l0, l1)
    e0 = jnp.exp(l0 - lmx)
    e1 = jnp.exp(l1 - lmx)
    lb_all = e0 / (e0 + e1)

    def hg_prep(c, carry):
        rows = pl.ds(pl.multiple_of(c * L, L), L)
        for hd in range(HG_HEADS):
            vts[c, hd] = proj[rows, COL_V + hd * HG_DK:COL_V + (hd + 1) * HG_DK].T.astype(BF16)
        for d in range(N_DIR):
            for hd in range(HG_HEADS):
                idx = d * HG_HEADS + hd
                cq = COL_Q + idx * HG_DK
                cf = COL_F + idx * HG_DK
                hq = proj[rows, cq:cq + HG_DK]
                fz = proj[rows, cf:cf + HG_DK]
                proj[rows, cq:cq + HG_DK] = _silu(hq)
                sg = _sig(fz)
                lb = lb_all[idx:idx + 1, :]
                oml = 1.0 - lb
                ksc[rows, idx * HG_DK:(idx + 1) * HG_DK] = oml * (1.0 - sg)
                logf = jnp.log(lb + oml * sg)
                p1 = logf.astype(BF16)
                p2 = (logf - p1.astype(F32)).astype(BF16)
                bb = jnp.dot(tri_ref[d], jnp.concatenate([p1, p2], axis=1), preferred_element_type=F32)
                proj[rows, cf:cf + HG_DK] = (bb[:, 0:HG_DK] + bb[:, HG_DK:2 * HG_DK]) * LOG2E
        return carry

    lax.fori_loop(0, T // ROW_TILE, proj_body, 0)

    S = T // LRU_SEGS
    cw = convw_ref[...]
    cb = convb_ref[...]

    def irows(i0, j, n):
        return pl.ds(pl.multiple_of(LRU_SEGS * i0, 8) + j, n, stride=LRU_SEGS)

    for j in range(LRU_SEGS):
        def conv_body(ti, carry, j=j):
            i0 = pl.multiple_of(ti * CONV_TILE, CONV_TILE)
            win = lxp[pl.ds(pl.multiple_of(j * S + i0, 8), CONV_TILE + 16), :]
            xc = cb
            for tap in range(4):
                xc = xc + win[6 + tap:6 + tap + CONV_TILE] * cw[tap:tap + 1]
            for l in range(LRU_WIDTH // 128):
                xci[l, irows(i0, j, CONV_TILE), :] = xc[:, l * 128:(l + 1) * 128]
            return carry

        lax.fori_loop(0, S // CONV_TILE, conv_body, 0, unroll=_unroll(S // CONV_TILE))

    n_grp = LRU_WIDTH // LRU_GROUP
    slabs = LRU_GROUP // 128
    gates = lrs if fuse_scan else proj

    def xc_tile(rows, g):
        return jnp.concatenate([xci[g * slabs + l, rows, :] for l in range(slabs)], axis=1)

    def gate_body(i, carry):
        rows = pl.ds(pl.multiple_of(i * ROW_TILE, ROW_TILE), ROW_TILE)
        for g in range(n_grp):
            gates[rows, g * GATE_G:(g + 1) * GATE_G] = jnp.dot(
                xc_tile(rows, g).astype(BF16), wg[g], preferred_element_type=F32)
        return carry

    lam = lam_ref[...]
    nl = -lam
    c8 = -LRU_C * (jnp.maximum(nl, 0.0) + jnp.log1p(jnp.exp(-jnp.abs(nl))))
    rowi2 = lax.broadcasted_iota(jnp.int32, (8, LRU_GROUP), 0)

    for d in range(N_DIR):
        for g in range(n_grp):
            k3 = 3 * (d * n_grp + g)
            chans = slice(g * LRU_GROUP, (g + 1) * LRU_GROUP)
            for r, row in enumerate((ba_ref[d:d + 1, chans], bx_ref[d:d + 1, chans], c8[d:d + 1, chans])):
                rowc[k3 + r] = jnp.broadcast_to(row, (LRU_SEGS, LRU_GROUP))

    def seg_out(d, rows, g, part):
        if fuse_scan:
            col = LRS_SEG_COL + ((d * n_grp + g) * 2 + part) * LRU_GROUP
            return lrs.at[rows, col:col + LRU_GROUP]
        cols = slice((g * 2 + part) * LRU_GROUP, (g * 2 + part + 1) * LRU_GROUP)
        return (ksc.at[rows, cols] if d == 0 else x1_ref.at[0, rows, cols])

    def lru_inputs(rows8, g, d):
        base = g * GATE_G + d * GATE_D
        k3 = 3 * (d * n_grp + g)
        ga = gates[rows8, base:base + LRU_GROUP] + rowc[k3]
        gx = gates[rows8, base + LRU_GROUP:base + 2 * LRU_GROUP] + rowc[k3 + 1]
        xc8 = xc_tile(rows8, g)
        log_a = rowc[k3 + 2] * _sig(ga)
        a = jnp.exp(log_a)
        z = jnp.tanh(-log_a) * (1.0 + a * a)
        mult = jnp.where(z > 0.0, z * lax.rsqrt(z), 0.0)
        return a, mult * (_sig(gx) * xc8)

    def scan8(a, u, d):
        for sft in (1, 2, 4):
            if d == 0:
                keep = rowi2 >= sft
                amt = sft
            else:
                keep = rowi2 < 8 - sft
                amt = 8 - sft
            ash = jnp.where(keep, pltpu.roll(a, amt, 0), 1.0)
            ush = jnp.where(keep, pltpu.roll(u, amt, 0), 0.0)
            u = a * ush + u
            a = a * ash
        return a, u

    def scan_steps(first, count, carry):
        hs, ds = [list(c) for c in carry[:2]], [list(c) for c in carry[2:]]
        for u in range(count):
            i_f = first + u
            for d, i in ((0, i_f), (1, S - 1 - i_f)):
                rows8 = pl.ds(pl.multiple_of(LRU_SEGS * i, 8), 8)
                for g in range(n_grp):
                    a, uu = lru_inputs(rows8, g, d)
                    hs[d][g] = a * hs[d][g] + uu
                    ds[d][g] = a * ds[d][g]
                    seg_out(d, rows8, g, 0)[...] = hs[d][g]
                    seg_out(d, rows8, g, 1)[...] = ds[d][g]
        return tuple(tuple(c) for c in hs + ds)

    zero8 = jnp.zeros((LRU_SEGS, LRU_GROUP), F32)
    one8 = jnp.ones((LRU_SEGS, LRU_GROUP), F32)
    scan_init = ((zero8,) * n_grp, (zero8,) * n_grp, (one8,) * n_grp, (one8,) * n_grp)
    if fuse_scan:
        lax.fori_loop(0, T // ROW_TILE, gate_body, 0)

    rowi = lax.broadcasted_iota(jnp.int32, (8, HG_DK), 0)
    r4 = rowi & 3
    is_r0 = r4 == 0
    is_r1 = r4 == 1
    is_r2 = r4 == 2
    hi4 = rowi >= 4

    for i in range(N_DIR * HG_HEADS):
        if has_state:
            st[i] = hs0_ref[0, i].T
        else:
            st[i] = jnp.zeros((HG_DK, HG_DK), F32)

    def hg_chunk(c, d):
        r0 = pl.multiple_of(c * L, L)
        rows = pl.ds(r0, L)
        for hd in range(HG_HEADS):
            idx = d * HG_HEADS + hd
            cq = COL_Q + idx * HG_DK
            cf = COL_F + idx * HG_DK
            ck = idx * HG_DK
            slot = idx % STAGE_SLOTS

            def ldq(lo, n, cq=cq):
                return proj[pl.ds(pl.multiple_of(r0 + lo, 8), n), cq:cq + HG_DK]

            def ldb(lo, n, cf=cf):
                return proj[pl.ds(pl.multiple_of(r0 + lo, 8), n), cf:cf + HG_DK]

            def ldk(lo, n, ck=ck):
                return ksc[pl.ds(pl.multiple_of(r0 + lo, 8), n), ck:ck + HG_DK]

            def bline(r, cf=cf):
                grp = proj[pl.ds(pl.multiple_of(r0 + 8 * (r // 8), 8), 8), cf:cf + HG_DK]
                return grp[r % 8:r % 8 + 1, :]

            def brow(r, n):
                return jnp.broadcast_to(bline(r), (n, HG_DK))

            acc_rows = [None] * (L // 8)

            def accumulate(li, p, row0):
                for i in range(p.shape[0] // 8):
                    g = row0 // 8 + i
                    term = masks_ref[d, li, 8 * g:8 * g + 8, :] * p[8 * i:8 * i + 8]
                    acc_rows[g] = term if acc_rows[g] is None else acc_rows[g] + term

            accumulate(LV_DIAG, jnp.dot(
                ldq(0, L).astype(BF16), _staged_transpose(ldk(0, L).astype(BF16), tsl.at[slot, LV_DIAG]),
                preferred_element_type=F32), 0)
            for li, h in enumerate(LEVELS[:LV_4]):
                pieces, q_pieces, q_starts = [], [], []
                for j in range(L // (2 * h)):
                    lo = j * 2 * h
                    mid = lo + h
                    if d == 0:
                        bm = brow(mid - 1, h)
                        kp = ldk(lo, h) * jnp.exp2(bm - ldb(lo, h))
                        qp = ldq(mid, h) * jnp.exp2(ldb(mid, h) - bm)
                        pieces += [kp, qp]
                        q_starts.append(mid)
                    else:
                        bm = brow(mid, h)
                        qp = ldq(lo, h) * jnp.exp2(ldb(lo, h) - bm)
                        kp = ldk(mid, h) * jnp.exp2(bm - ldb(mid, h))
                        pieces += [qp, kp]
                        q_starts.append(lo)
                    q_pieces.append(qp)
                xt = _staged_transpose(jnp.concatenate(pieces, axis=0).astype(BF16), tsl.at[slot,li])
                p = jnp.dot(jnp.concatenate(q_pieces, axis=0).astype(BF16), xt, preferred_element_type=F32)
                for j, row0 in enumerate(q_starts):
                    accumulate(li, p[j * h:(j + 1) * h], row0)
            x4, xq21, xk21 = [], [], []
            for g in range(L // 8):
                qg, kg, bg = ldq(8 * g, 8), ldk(8 * g, 8), ldb(8 * g, 8)
                fg = 1.0 - kg
                qfg = qg * fg
                mid = 8 * g + (3 if d == 0 else 4)
                e4 = jnp.exp2(-jnp.abs(bg - brow(mid, 8)))
                fnx = pltpu.roll(fg, 7, 0)
                fpv = pltpu.roll(fg, 1, 0)
                k_over_f = kg / fg
                if d == 0:
                    x4.append(jnp.where(hi4, qg, kg) * e4)
                    xq21.append(jnp.where(is_r0, 0.0, jnp.where(is_r1, qg, jnp.where(is_r2, qfg, qfg * fpv))))
                    xk21.append(jnp.where(is_r0, kg * fnx, jnp.where(is_r1, kg, jnp.where(is_r2, k_over_f, 0.0))))
                else:
                    x4.append(jnp.where(hi4, kg, qg) * e4)
                    xq21.append(jnp.where(is_r0, qfg * fnx, jnp.where(is_r1, qfg, jnp.where(is_r2, qg, 0.0))))
                    xk21.append(jnp.where(is_r0, 0.0, jnp.where(is_r1, k_over_f, jnp.where(is_r2, kg, kg * fpv))))
            accumulate(LV_4, _gram(jnp.concatenate(x4, axis=0), tsl.at[slot, LV_4]), 0)
            accumulate(LV_21, jnp.dot(
                jnp.concatenate(xq21, axis=0).astype(BF16),
                _staged_transpose(jnp.concatenate(xk21, axis=0).astype(BF16), tsl.at[slot, LV_21]),
                preferred_element_type=F32), 0)
            acc = jnp.concatenate(acc_rows, axis=0)

            vt = vts[c, hd]
            st_t = st[idx]
            b = ldb(0, L)
            qt = (ldq(0, L) * jnp.exp2(b)).astype(BF16)
            vb = proj[rows, COL_V + hd * HG_DK:COL_V + (hd + 1) * HG_DK].astype(BF16)
            o = jnp.dot(jnp.concatenate([acc.astype(BF16), qt], axis=1),
                        jnp.concatenate([vb, _staged_transpose(st_t.astype(BF16), tsl.at[slot, LV_STATE])], axis=0),
                        preferred_element_type=F32)
            btot = bline(L - 1 if d == 0 else 0)
            kt = (ldk(0, L) * jnp.exp2(btot - b)).astype(BF16)
            st[idx] = st_t * jnp.exp2(btot) + jnp.dot(vt, kt, preferred_element_type=F32)
            x1_ref[0, rows, d * HG_WIDTH + hd * HG_DK:d * HG_WIDTH + (hd + 1) * HG_DK] = o

    n_chunks = T // L

    def hg_both(c, carry):
        hg_chunk(c, 0)
        hg_chunk(n_chunks - 1 - c, 1)
        if fuse_scan:
            steps = S // n_chunks
            carry = scan_steps(c * steps, steps, carry)
        return carry

    lax.fori_loop(0, n_chunks, hg_prep, 0, unroll=_unroll(n_chunks))
    scan_state = lax.fori_loop(0, n_chunks, hg_both, scan_init if fuse_scan else 0)

    @pl.when(pl.program_id(0) == 0)
    def _late_weights_ready():
        for k in range(len(late_hbm)):
            late_copy(k).wait()

    if emit_state:
        for i in range(N_DIR * HG_HEADS):
            hs_ref[0, i] = st[i].T

    def hg_fin(i, carry):
        rows = pl.ds(pl.multiple_of(i * L, L), L)
        for hd in range(HG_HEADS):
            cs = slice(hd * HG_DK, (hd + 1) * HG_DK)
            o = x1_ref[0, rows, cs] + x1_ref[0, rows, HG_WIDTH + hd * HG_DK:HG_WIDTH + (hd + 1) * HG_DK]
            ms = jnp.mean(o * o, axis=-1, keepdims=True)
            y = o * lax.rsqrt(ms + EPS) * hgg_ref[:, cs]
            gz = proj[rows, COL_GATE + hd * HG_DK:COL_GATE + (hd + 1) * HG_DK]
            mixin[rows, cs] = (y * _silu(gz)).astype(BF16)
        return carry

    lax.fori_loop(0, n_chunks, hg_fin, 0, unroll=_unroll(n_chunks))

    if not fuse_scan:
        lax.fori_loop(0, T // ROW_TILE, gate_body, 0)
        scan_state = lax.fori_loop(
            0, S // SCAN_UNROLL, lambda n, carry: scan_steps(n * SCAN_UNROLL, SCAN_UNROLL, carry), scan_init)
    h_end, d_end = scan_state[:2], scan_state[2:]

    carry_in = [[None] * n_grp for _ in range(N_DIR)]
    for d in range(N_DIR):
        for g in range(n_grp):
            if has_state:
                h0 = ls0_ref[0, d:d + 1, g * LRU_GROUP:(g + 1) * LRU_GROUP]
            else:
                h0 = jnp.zeros((1, LRU_GROUP), F32)
            dd, hh = scan8(d_end[d][g], h_end[d][g], d)
            seg_end = hh + dd * h0
            if d == 0:
                carry_in[d][g] = jnp.where(rowi2 >= 1, pltpu.roll(seg_end, 1, 0), h0)
                last = seg_end[LRU_SEGS - 1:LRU_SEGS, :]
            else:
                carry_in[d][g] = jnp.where(rowi2 < LRU_SEGS - 1, pltpu.roll(seg_end, LRU_SEGS - 1, 0), h0)
                last = seg_end[0:1, :]
            if emit_state:
                ls_ref[0, d:d + 1, g * LRU_GROUP:(g + 1) * LRU_GROUP] = last

    def fix_body(i, carry):
        rows = pl.ds(pl.multiple_of(i * FIX_TILE, FIX_TILE), FIX_TILE)
        for g in range(n_grp):
            tot = None
            for d in range(N_DIR):
                cin = jnp.concatenate([carry_in[d][g]] * (FIX_TILE // LRU_SEGS), axis=0)
                h = seg_out(d, rows, g, 0)[...] + seg_out(d, rows, g, 1)[...] * cin
                tot = h if tot is None else tot + h
            for l in range(slabs):
                xci[g * slabs + l, rows, :] = tot[:, l * 128:(l + 1) * 128]
        return carry

    lax.fori_loop(0, T // FIX_TILE, fix_body, 0)

    for j in range(LRU_SEGS):
        def lru_fin(ti, carry, j=j):
            i0 = pl.multiple_of(ti * CONV_TILE, CONV_TILE)
            rows = pl.ds(pl.multiple_of(j * S + i0, CONV_TILE), CONV_TILE)
            hsum = jnp.concatenate([xci[l, irows(i0, j, CONV_TILE), :] for l in range(LRU_WIDTH // 128)], axis=1)
            lg = proj[rows, PROJ_LG:PROJ_LG + LRU_WIDTH]
            gl = lg * (0.5 * (1.0 + jnp.tanh(0.7978845608028654 * (lg + 0.044715 * (lg * lg * lg)))))
            mixin[rows, HG_WIDTH:HG_WIDTH + LRU_WIDTH] = (hsum * gl).astype(BF16)
            return carry

        lax.fori_loop(0, S // CONV_TILE, lru_fin, 0, unroll=_unroll(S // CONV_TILE))

    def out_body(i, carry):
        rows = pl.ds(pl.multiple_of(i * ROW_TILE, ROW_TILE), ROW_TILE)
        mix = jnp.dot(mixin[rows, :], wout_ref[...], preferred_element_type=F32)
        x1_ref[0, rows, :] = load_x(i) + g1 * mix
        return carry

    lax.fori_loop(0, T // ROW_TILE, out_body, 0)

    if fuse_ffn:
        def ffn_body(i, carry):
            rows = pl.ds(pl.multiple_of(i * ROW_TILE, ROW_TILE), ROW_TILE)
            y_ref[0, rows, :] = _ffn_rows(x1_ref[0, rows, :], mrow, n2g_ref[...], fg_ref[...], w1_ref, w2_ref)
            return carry

        lax.fori_loop(0, T // ROW_TILE, ffn_body, 0)


def _const_spec(shape):
    nd = len(shape)
    return pl.BlockSpec(shape, lambda b, _n=nd: (0,) * _n, pipeline_mode=pl.Buffered(1))


def _nbytes(shape, dtype):
    return int(np.prod(shape)) * jnp.dtype(dtype).itemsize


def _mixer(x, m3, m_off, m_step, pos, consts, ffn_consts, states, emit_state):
    B, T, _ = x.shape
    has_pos = pos is not None
    has_state = states is not None
    scratch_shapes = [
        ((T, PROJ_COLS), F32),
        ((T + 16, LRU_WIDTH), F32),
        ((T, D_MODEL), BF16),
        ((8, HG_DK, HG_DK), F32),
        ((T, N_DIR * HG_HEADS * HG_DK), F32),
        ((LRU_WIDTH // 128, T, 128), F32),
        ((T // CHUNK, HG_HEADS, HG_DK, CHUNK), BF16),
        ((3 * N_DIR * (LRU_WIDTH // LRU_GROUP), LRU_SEGS, LRU_GROUP), F32),
        ((STAGE_SLOTS, LV_STATE + 1, HG_DK, CHUNK), BF16),
        ((LRU_WIDTH // LRU_GROUP, LRU_GROUP, 2 * N_DIR * LRU_GROUP), BF16),
    ]
    resident = (sum(_nbytes(s, d) for s, d in scratch_shapes)
                + sum(_nbytes(c.shape, c.dtype) for c in consts)
                + (sum(_nbytes(p.shape, p.dtype) for p in pos) if has_pos else 0))
    io_block = _nbytes((T, D_MODEL), F32)
    budget = VMEM_LIMIT - VMEM_HEADROOM
    lrs_shape = ((T, 2 * LRS_SEG_COL), F32)
    fuse_scan = resident + _nbytes(*lrs_shape) + 4 * io_block <= budget
    if fuse_scan:
        scratch_shapes.append(lrs_shape)
        resident += _nbytes(*lrs_shape)
    ffn_bytes = sum(_nbytes(c.shape, c.dtype) for c in ffn_consts) + io_block
    fuse_ffn = resident + ffn_bytes + 4 * io_block <= budget
    if fuse_ffn:
        scratch_shapes.insert(0, ((1, T, D_MODEL), F32))
        resident += ffn_bytes
    in_bufs = 2 if resident + 3 * io_block <= budget else 1
    out_bufs = 2 if resident + 4 * io_block <= budget else 1
    io_mode = pl.Buffered(out_bufs)
    in_specs = [
        pl.BlockSpec((1, T, D_MODEL), lambda b: (b, 0, 0), pipeline_mode=pl.Buffered(in_bufs)),
        pl.BlockSpec((1, 1, 6 * D_MODEL), lambda b: (m_off + m_step * b, 0, 0)),
    ]
    args = [x, m3]
    if has_pos:
        in_specs += [_const_spec(p.shape) for p in pos]
        args += list(pos)
    hbm_spec = pl.BlockSpec(memory_space=pl.ANY)
    late = [consts[-1]]
    in_specs += [_const_spec(c.shape) for c in consts[:-1]] + [hbm_spec]
    args += list(consts)
    if fuse_ffn:
        late += list(ffn_consts[-2:])
        in_specs += [_const_spec(c.shape) for c in ffn_consts[:-2]] + [hbm_spec, hbm_spec]
        args += list(ffn_consts)
    if has_state:
        hs0, ls0 = states
        in_specs += [
            pl.BlockSpec((1, 8, HG_DK, HG_DK), lambda b: (b, 0, 0, 0)),
            pl.BlockSpec((1, N_DIR, LRU_WIDTH), lambda b: (b, 0, 0)),
        ]
        args += [hs0, ls0]
    out_shape = [jax.ShapeDtypeStruct((B, T, D_MODEL), F32)]
    out_specs = [pl.BlockSpec((1, T, D_MODEL), lambda b: (b, 0, 0), pipeline_mode=io_mode)]
    if emit_state:
        out_shape += [jax.ShapeDtypeStruct((B, 8, HG_DK, HG_DK), F32),
                      jax.ShapeDtypeStruct((B, N_DIR, LRU_WIDTH), F32)]
        out_specs += [pl.BlockSpec((1, 8, HG_DK, HG_DK), lambda b: (b, 0, 0, 0)),
                      pl.BlockSpec((1, N_DIR, LRU_WIDTH), lambda b: (b, 0, 0))]
    scratch = [pltpu.VMEM(s, d) for s, d in scratch_shapes]
    scratch += [pltpu.VMEM(w.shape, w.dtype) for w in late] + [pltpu.SemaphoreType.DMA((len(late),))]
    outs = pl.pallas_call(
        functools.partial(_mixer_kernel, T=T, has_pos=has_pos, has_state=has_state, emit_state=emit_state,
                          fuse_scan=fuse_scan, fuse_ffn=fuse_ffn),
        grid=(B,),
        in_specs=in_specs,
        out_specs=out_specs,
        out_shape=out_shape,
        scratch_shapes=scratch,
        compiler_params=pltpu.CompilerParams(
            dimension_semantics=("arbitrary",), vmem_limit_bytes=VMEM_LIMIT),
        name=f"mixer_t{T}",
    )(*args)
    return (outs[0], fuse_ffn) + tuple(outs[1:])


def _ffn_rows(x, mrow, n2g, fgain, w1_ref, w2_ref):
    sh2 = mrow[:, 3 * D_MODEL:4 * D_MODEL]
    sc2 = mrow[:, 4 * D_MODEL:5 * D_MODEL]
    g2 = mrow[:, 5 * D_MODEL:6 * D_MODEL]
    ms = jnp.mean(x * x, axis=-1, keepdims=True)
    hb = (x * lax.rsqrt(ms + EPS) * (n2g * (1.0 + sc2)) + sh2).astype(BF16)
    ff = jnp.zeros(x.shape, F32)
    for c in range(D_FF // FF_CHUNK):
        a = jnp.dot(hb, w1_ref[:, c * FF_CHUNK:(c + 1) * FF_CHUNK], preferred_element_type=F32)
        a = jnp.maximum(a, 0.0)
        ff = ff + jnp.dot((a * a).astype(BF16), w2_ref[c * FF_CHUNK:(c + 1) * FF_CHUNK, :],
                          preferred_element_type=F32)
    x2 = x + g2 * ff
    ms2 = jnp.mean(x2 * x2, axis=-1, keepdims=True)
    return x2 * lax.rsqrt(ms2 + EPS) * fgain


def _ffn_kernel(x_ref, m_ref, n2g_ref, fg_ref, w1_ref, w2_ref, y_ref):
    y_ref[...] = _ffn_rows(x_ref[...], m_ref[0], n2g_ref[...], fg_ref[...], w1_ref, w2_ref)


def _ffn(x1, m3, m_off, tiles_per_cond, n2g, fgain, w1, w2):
    n = x1.shape[0]

    def m_index(i):
        if tiles_per_cond is None:
            return (m_off, 0, 0)
        return (m_off + i // tiles_per_cond, 0, 0)

    return pl.pallas_call(
        _ffn_kernel,
        grid=(n // FFN_TILE,),
        in_specs=[
            pl.BlockSpec((FFN_TILE, D_MODEL), lambda i: (i, 0)),
            pl.BlockSpec((1, 1, 6 * D_MODEL), m_index),
            _const_spec(n2g.shape),
            _const_spec(fgain.shape),
            _const_spec(w1.shape),
            _const_spec(w2.shape),
        ],
        out_specs=pl.BlockSpec((FFN_TILE, D_MODEL), lambda i: (i, 0)),
        out_shape=jax.ShapeDtypeStruct((n, D_MODEL), F32),
        compiler_params=pltpu.CompilerParams(
            dimension_semantics=("arbitrary",), vmem_limit_bytes=VMEM_LIMIT),
        name="ffn",
    )(x1, m3, n2g, fgain, w1, w2)


def _grid_pos_tables(n_tok):
    quarter = D_MODEL // 4
    omega = (1.0 / (np.float32(POS_BASE) ** (np.arange(quarter, dtype=np.float32) / np.float32(quarter)))
             ).astype(np.float32)

    def emb(n):
        ang = np.arange(n).reshape(-1, 1).astype(np.float32) * omega
        return np.concatenate([np.sin(ang), np.cos(ang)], axis=-1)

    rows = np.repeat(emb(n_tok // GRID_W)[:, None, :], 8, axis=1)
    return jnp.asarray(rows, dtype=F32), jnp.asarray(emb(GRID_W), dtype=F32)


def kernel(x_prompt, x_sample, c, state_hgrn, state_rglru, c_ctx, w_ada, b_ada, norm1_gain, norm2_gain,
           w_in, hg_lb_logits, hg_norm_gain, conv_w, conv_b, lru_wa, lru_ba, lru_wx, lru_bx, lru_lambda,
           w_out, w_ff1, w_ff2, final_gain):
    bp, tp, _ = x_prompt.shape
    bs_, ts, _ = x_sample.shape

    m3, (w_in_b, w_out_b, w1, w2) = _modulation(
        c_ctx, c, w_ada[0], b_ada, [w_in[0], w_out[0], w_ff1[0], w_ff2[0]])

    masks_np, tri_np = _level_tables()
    consts = [
        norm1_gain,
        w_in_b,
        hg_lb_logits.reshape(2, N_DIR * HG_HEADS, HG_DK),
        hg_norm_gain[0].reshape(1, HG_WIDTH),
        jnp.asarray(masks_np),
        jnp.asarray(tri_np, dtype=BF16),
        conv_w[0],
        conv_b,
        lru_wa[0].reshape(N_DIR * LRU_BLOCKS, LRU_BLOCK, LRU_BLOCK),
        lru_wx[0].reshape(N_DIR * LRU_BLOCKS, LRU_BLOCK, LRU_BLOCK),
        lru_ba[0],
        lru_bx[0],
        lru_lambda[0],
        w_out_b,
    ]
    fgain = final_gain.reshape(1, D_MODEL)
    ffn_consts = [norm2_gain, fgain, w1, w2]

    y_prompt, done, hs, ls = _mixer(x_prompt, m3, 0, 0, None, consts, ffn_consts, None, True)
    if not done:
        y_prompt = _ffn(y_prompt.reshape(bp * tp, D_MODEL), m3, 0, None, *ffn_consts)

    y_sample, done = _mixer(x_sample, m3, 1, 1, _grid_pos_tables(ts), consts, ffn_consts,
                            (state_hgrn.reshape(bs_, N_DIR * HG_HEADS, HG_DK, HG_DK),
                             state_rglru.reshape(bs_, N_DIR, LRU_WIDTH)), False)
    if not done:
        y_sample = _ffn(y_sample.reshape(bs_ * ts, D_MODEL), m3, 1, ts // FFN_TILE, *ffn_consts)

    return (y_prompt.reshape(bp, tp, D_MODEL),
            y_sample.reshape(bs_, ts, D_MODEL),
            hs.reshape(bp, 1, N_DIR, HG_HEADS, HG_DK, HG_DK),
            ls.reshape(bp, 1, N_DIR, LRU_WIDTH))
```

```python
import functools

import numpy as np
import jax
import jax.numpy as jnp
from jax import lax
from jax.experimental import pallas as pl
from jax.experimental.pallas import tpu as pltpu

F32 = jnp.float32
BF16 = jnp.bfloat16

D_MODEL = 1024
N_DIR = 2
HG_HEADS = 4
HG_DK = 128
HG_WIDTH = 512
LRU_WIDTH = 512
LRU_BLOCKS = 8
LRU_BLOCK = 64
LRU_C = 8.0
D_FF = 4096
IN_COLS = 4096
EPS = 1e-6
LOG2E = 1.4426950408889634
GRID_W = 64
POS_BASE = 10000.0

COL_Q = 0
COL_F = 1024
COL_V = 2048
COL_GATE = 2560
COL_LX = 3072
COL_LG = 3584
PROJ_LG = COL_LX
PROJ_COLS = IN_COLS - LRU_WIDTH

CHUNK = 128
LEVELS = (64, 32, 16, 8, 4)
LV_4 = LEVELS.index(4)
LV_21 = len(LEVELS)
LV_DIAG = LV_21 + 1
LV_STATE = LV_DIAG + 1
N_MASKS = LV_DIAG + 1
ROW_TILE = 256
LRU_SEGS = 8
CONV_TILE = 32
SCAN_UNROLL = 8
FIX_TILE = 64
STAGE_SLOTS = 1
MAX_INLINE_TRIPS = 2
LOOP_UNROLL = 4
LRU_GROUP = 256
GATE_D = 2 * LRU_GROUP
GATE_G = N_DIR * GATE_D
LRS_SEG_COL = 2 * N_DIR * LRU_WIDTH
PROJ_STEP = 512
FFN_TILE = 512
FF_CHUNK = 1024
MOD_STEPS = 8
MOD_ROWS = 8
VMEM_LIMIT = 58 * 1024 * 1024
VMEM_HEADROOM = 6 * 1024 * 1024


def _sig(x):
    return 0.5 * jnp.tanh(0.5 * x) + 0.5


def _silu(x):
    h = 0.5 * x
    return h * jnp.tanh(h) + h


def _unroll(trips):
    return True if trips <= MAX_INLINE_TRIPS else LOOP_UNROLL


def _nt_dot(a, b):
    return lax.dot_general(a, b, (((1,), (1,)), ((), ())), preferred_element_type=F32)


def _staged_transpose(xb, slot_ref):
    slot_ref[...] = xb.T
    return slot_ref[...]


def _gram(x, slot_ref):
    xb = x.astype(BF16)
    return jnp.dot(xb, _staged_transpose(xb, slot_ref), preferred_element_type=F32)


def _level_tables():
    t = np.arange(CHUNK)[:, None]
    s = np.arange(CHUNK)[None, :]
    masks = np.zeros((N_DIR, N_MASKS, CHUNK, CHUNK), np.float32)
    for li, h in enumerate(LEVELS):
        same = (t // (2 * h)) == (s // (2 * h))
        t_hi = (t // h) % 2 == 1
        s_hi = (s // h) % 2 == 1
        masks[0, li] = same & t_hi & ~s_hi
        masks[1, li] = same & ~t_hi & s_hi
    same4 = (t // 4) == (s // 4)
    masks[0, LV_21] = same4 & (s < t)
    masks[1, LV_21] = same4 & (s > t)
    masks[:, LV_DIAG] = (t == s)
    tri = np.stack([(s <= t), (s >= t)]).astype(np.float32)
    return masks, tri


def _mod_kernel(cctx_ref, c_ref, w_ref, b_ref, *rest):
    n_w = (len(rest) - 2) // 2
    f32_refs, o_ref, bf16_refs, cond = rest[:n_w], rest[n_w], rest[n_w + 1:2 * n_w + 1], rest[-1]
    n = c_ref.shape[0]
    cond[...] = jnp.zeros(cond.shape, F32)
    cond[0:1, :] = cctx_ref[...]
    cond[1:1 + n, :] = c_ref[...]
    c = cond[...]
    a = _silu(c).astype(BF16)
    m = jnp.dot(a, w_ref[...].astype(BF16), preferred_element_type=F32) + b_ref[...]
    for r in range(MOD_ROWS):
        o_ref[r] = m[r:r + 1]
    for src, dst in zip(f32_refs, bf16_refs):
        dst[...] = src[...].astype(BF16)


def _modulation(c_ctx, c, w_ada, b_ada, weights):
    n = w_ada.shape[1]
    assert 1 + c.shape[0] <= MOD_ROWS
    tile = n // MOD_STEPS
    slab = lambda w: pl.BlockSpec((w.shape[0] // MOD_STEPS, w.shape[1]), lambda j: (j, 0))
    outs = pl.pallas_call(
        _mod_kernel,
        grid=(MOD_STEPS,),
        in_specs=[
            pl.BlockSpec((1, D_MODEL), lambda j: (0, 0)),
            pl.BlockSpec(c.shape, lambda j: (0, 0)),
            pl.BlockSpec((D_MODEL, tile), lambda j: (0, j)),
            pl.BlockSpec((1, tile), lambda j: (0, j)),
        ] + [slab(w) for w in weights],
        out_specs=[pl.BlockSpec((MOD_ROWS, 1, tile), lambda j: (0, 0, j))] + [slab(w) for w in weights],
        out_shape=[jax.ShapeDtypeStruct((MOD_ROWS, 1, n), F32)] + [
            jax.ShapeDtypeStruct(w.shape, BF16) for w in weights],
        scratch_shapes=[pltpu.VMEM((MOD_ROWS, D_MODEL), F32)],
        compiler_params=pltpu.CompilerParams(
            dimension_semantics=("arbitrary",), vmem_limit_bytes=VMEM_LIMIT),
        name="adaln_modulation",
    )(c_ctx.reshape(1, D_MODEL), c, w_ada, b_ada, *weights)
    return outs[0], outs[1:]


def _mixer_kernel(*refs, T, has_pos, has_state, emit_state, fuse_scan, fuse_ffn):
    it = iter(refs)
    x_ref = next(it)
    m_ref = next(it)
    if has_pos:
        posr_ref = next(it)
        posc_ref = next(it)
    n1g_ref = next(it)
    win_ref = next(it)
    lbl_ref = next(it)
    hgg_ref = next(it)
    masks_ref = next(it)
    tri_ref = next(it)
    convw_ref = next(it)
    convb_ref = next(it)
    wa_ref = next(it)
    wx_ref = next(it)
    ba_ref = next(it)
    bx_ref = next(it)
    lam_ref = next(it)
    late_hbm = [next(it)]
    if fuse_ffn:
        n2g_ref = next(it)
        fg_ref = next(it)
        late_hbm += [next(it), next(it)]
    if has_state:
        hs0_ref = next(it)
        ls0_ref = next(it)
    y_ref = next(it)
    if emit_state:
        hs_ref = next(it)
        ls_ref = next(it)
    x1_ref = next(it) if fuse_ffn else y_ref
    proj = next(it)
    lxp = next(it)
    mixin = next(it)
    st = next(it)
    ksc = next(it)
    xci = next(it)
    vts = next(it)
    rowc = next(it)
    tsl = next(it)
    wg = next(it)
    if fuse_scan:
        lrs = next(it)
    late_vmem = [next(it) for _ in late_hbm]
    late_sem = next(it)
    wout_ref = late_vmem[0]
    if fuse_ffn:
        w1_ref, w2_ref = late_vmem[1:]

    L = CHUNK

    def late_copy(k):
        return pltpu.make_async_copy(late_hbm[k], late_vmem[k], late_sem.at[k])

    @pl.when(pl.program_id(0) == 0)
    def _first_step():
        for k in range(len(late_hbm)):
            late_copy(k).start()
        per_group = LRU_GROUP // LRU_BLOCK
        wg[...] = jnp.zeros(wg.shape, BF16)
        for g in range(LRU_WIDTH // LRU_GROUP):
            for p in range(2 * N_DIR):
                src = wa_ref if p % 2 == 0 else wx_ref
                for n in range(per_group):
                    r = n * LRU_BLOCK
                    col = p * LRU_GROUP + r
                    wg[g, r:r + LRU_BLOCK, col:col + LRU_BLOCK] = src[
                        (p // 2) * LRU_BLOCKS + g * per_group + n].astype(BF16)

    mrow = m_ref[0]
    sh1 = mrow[:, 0:D_MODEL]
    sc1 = mrow[:, D_MODEL:2 * D_MODEL]
    g1 = mrow[:, 2 * D_MODEL:3 * D_MODEL]
    gain1 = n1g_ref[...] * (1.0 + sc1)

    zrows = jnp.zeros((8, LRU_WIDTH), F32)
    lxp[0:8, :] = zrows
    lxp[T + 8:T + 16, :] = zrows

    def load_x(i):
        xt = x_ref[0, pl.ds(pl.multiple_of(i * ROW_TILE, ROW_TILE), ROW_TILE), :]
        if has_pos:
            per_tile = ROW_TILE // GRID_W
            tiles = []
            for s in range(per_tile):
                row_emb = jnp.concatenate([posr_ref[i * per_tile + s]] * (GRID_W // 8), axis=0)
                tiles.append(jnp.concatenate([row_emb, posc_ref[...]], axis=1))
            xt = xt + jnp.concatenate(tiles, axis=0)
        return xt

    def proj_body(i, carry):
        r0 = pl.multiple_of(i * ROW_TILE, ROW_TILE)
        xt = load_x(i)
        ms = jnp.mean(xt * xt, axis=-1, keepdims=True)
        hb = (xt * lax.rsqrt(ms + EPS) * gain1 + sh1).astype(BF16)
        starts = range(0, IN_COLS, PROJ_STEP)
        order = [COL_LX] + [c0 for c0 in starts if c0 < COL_GATE] + [
            c0 for c0 in starts if c0 >= COL_GATE and c0 != COL_LX]
        for c0 in order:
            res = jnp.dot(hb, win_ref[:, c0:c0 + PROJ_STEP], preferred_element_type=F32)
            if c0 == COL_LX:
                lxp[pl.ds(pl.multiple_of(r0 + 8, 8), ROW_TILE), :] = res
            elif c0 == COL_LG:
                proj[pl.ds(r0, ROW_TILE), PROJ_LG:PROJ_LG + LRU_WIDTH] = res
            else:
                proj[pl.ds(r0, ROW_TILE), c0:c0 + PROJ_STEP] = res
        return carry

    l0 = lbl_ref[0]
    l1 = lbl_ref[1]
    lmx = jnp.maximum(l0, l1)
    e0 = jnp.exp(l0 - lmx)
    e1 = jnp.exp(l1 - lmx)
    lb_all = e0 / (e0 + e1)

    def hg_prep(c, carry):
        rows = pl.ds(pl.multiple_of(c * L, L), L)
        for hd in range(HG_HEADS):
            vts[c, hd] = proj[rows, COL_V + hd * HG_DK:COL_V + (hd + 1) * HG_DK].T.astype(BF16)
        for d in range(N_DIR):
            for hd in range(HG_HEADS):
                idx = d * HG_HEADS + hd
                cq = COL_Q + idx * HG_DK
                cf = COL_F + idx * HG_DK
                hq = proj[rows, cq:cq + HG_DK]
                fz = proj[rows, cf:cf + HG_DK]
                proj[rows, cq:cq + HG_DK] = _silu(hq)
                sg = _sig(fz)
                lb = lb_all[idx:idx + 1, :]
                oml = 1.0 - lb
                ksc[rows, idx * HG_DK:(idx + 1) * HG_DK] = oml * (1.0 - sg)
                logf = jnp.log(lb + oml * sg)
                p1 = logf.astype(BF16)
                p2 = (logf - p1.astype(F32)).astype(BF16)
                bb = jnp.dot(tri_ref[d], jnp.concatenate([p1, p2], axis=1), preferred_element_type=F32)
                proj[rows, cf:cf + HG_DK] = (bb[:, 0:HG_DK] + bb[:, HG_DK:2 * HG_DK]) * LOG2E
        return carry

    lax.fori_loop(0, T // ROW_TILE, proj_body, 0)

    S = T // LRU_SEGS
    cw = convw_ref[...]
    cb = convb_ref[...]

    def irows(i0, j, n):
        return pl.ds(pl.multiple_of(LRU_SEGS * i0, 8) + j, n, stride=LRU_SEGS)

    for j in range(LRU_SEGS):
        def conv_body(ti, carry, j=j):
            i0 = pl.multiple_of(ti * CONV_TILE, CONV_TILE)
            win = lxp[pl.ds(pl.multiple_of(j * S + i0, 8), CONV_TILE + 16), :]
            xc = cb
            for tap in range(4):
                xc = xc + win[6 + tap:6 + tap + CONV_TILE] * cw[tap:tap + 1]
            for l in range(LRU_WIDTH // 128):
                xci[l, irows(i0, j, CONV_TILE), :] = xc[:, l * 128:(l + 1) * 128]
            return carry

        lax.fori_loop(0, S // CONV_TILE, conv_body, 0, unroll=_unroll(S // CONV_TILE))

    n_grp = LRU_WIDTH // LRU_GROUP
    slabs = LRU_GROUP // 128
    gates = lrs if fuse_scan else proj

    def xc_tile(rows, g):
        return jnp.concatenate([xci[g * slabs + l, rows, :] for l in range(slabs)], axis=1)

    def gate_body(i, carry):
        rows = pl.ds(pl.multiple_of(i * ROW_TILE, ROW_TILE), ROW_TILE)
        for g in range(n_grp):
            gates[rows, g * GATE_G:(g + 1) * GATE_G] = jnp.dot(
                xc_tile(rows, g).astype(BF16), wg[g], preferred_element_type=F32)
        return carry

    lam = lam_ref[...]
    nl = -lam
    c8 = -LRU_C * (jnp.maximum(nl, 0.0) + jnp.log1p(jnp.exp(-jnp.abs(nl))))
    rowi2 = lax.broadcasted_iota(jnp.int32, (8, LRU_GROUP), 0)

    for d in range(N_DIR):
        for g in range(n_grp):
            k3 = 3 * (d * n_grp + g)
            chans = slice(g * LRU_GROUP, (g + 1) * LRU_GROUP)
            for r, row in enumerate((ba_ref[d:d + 1, chans], bx_ref[d:d + 1, chans], c8[d:d + 1, chans])):
                rowc[k3 + r] = jnp.broadcast_to(row, (LRU_SEGS, LRU_GROUP))

    def seg_out(d, rows, g, part):
        if fuse_scan:
            col = LRS_SEG_COL + ((d * n_grp + g) * 2 + part) * LRU_GROUP
            return lrs.at[rows, col:col + LRU_GROUP]
        cols = slice((g * 2 + part) * LRU_GROUP, (g * 2 + part + 1) * LRU_GROUP)
        return (ksc.at[rows, cols] if d == 0 else x1_ref.at[0, rows, cols])

    def lru_inputs(rows8, g, d):
        base = g * GATE_G + d * GATE_D
        k3 = 3 * (d * n_grp + g)
        ga = gates[rows8, base:base + LRU_GROUP] + rowc[k3]
        gx = gates[rows8, base + LRU_GROUP:base + 2 * LRU_GROUP] + rowc[k3 + 1]
        xc8 = xc_tile(rows8, g)
        log_a = rowc[k3 + 2] * _sig(ga)
        a = jnp.exp(log_a)
        z = jnp.tanh(-log_a) * (1.0 + a * a)
        mult = jnp.where(z > 0.0, z * lax.rsqrt(z), 0.0)
        return a, mult * (_sig(gx) * xc8)

    def scan8(a, u, d):
        for sft in (1, 2, 4):
            if d == 0:
                keep = rowi2 >= sft
                amt = sft
            else:
                keep = rowi2 < 8 - sft
                amt = 8 - sft
            ash = jnp.where(keep, pltpu.roll(a, amt, 0), 1.0)
            ush = jnp.where(keep, pltpu.roll(u, amt, 0), 0.0)
            u = a * ush + u
            a = a * ash
        return a, u

    def scan_steps(first, count, carry):
        hs, ds = [list(c) for c in carry[:2]], [list(c) for c in carry[2:]]
        for u in range(count):
            i_f = first + u
            for d, i in ((0, i_f), (1, S - 1 - i_f)):
                rows8 = pl.ds(pl.multiple_of(LRU_SEGS * i, 8), 8)
                for g in range(n_grp):
                    a, uu = lru_inputs(rows8, g, d)
                    hs[d][g] = a * hs[d][g] + uu
                    ds[d][g] = a * ds[d][g]
                    seg_out(d, rows8, g, 0)[...] = hs[d][g]
                    seg_out(d, rows8, g, 1)[...] = ds[d][g]
        return tuple(tuple(c) for c in hs + ds)

    zero8 = jnp.zeros((LRU_SEGS, LRU_GROUP), F32)
    one8 = jnp.ones((LRU_SEGS, LRU_GROUP), F32)
    scan_init = ((zero8,) * n_grp, (zero8,) * n_grp, (one8,) * n_grp, (one8,) * n_grp)
    if fuse_scan:
        lax.fori_loop(0, T // ROW_TILE, gate_body, 0)

    rowi = lax.broadcasted_iota(jnp.int32, (8, HG_DK), 0)
    r4 = rowi & 3
    is_r0 = r4 == 0
    is_r1 = r4 == 1
    is_r2 = r4 == 2
    hi4 = rowi >= 4

    for i in range(N_DIR * HG_HEADS):
        if has_state:
            st[i] = hs0_ref[0, i].T
        else:
            st[i] = jnp.zeros((HG_DK, HG_DK), F32)

    def hg_chunk(c, d):
        r0 = pl.multiple_of(c * L, L)
        rows = pl.ds(r0, L)
        for hd in range(HG_HEADS):
            idx = d * HG_HEADS + hd
            cq = COL_Q + idx * HG_DK
            cf = COL_F + idx * HG_DK
            ck = idx * HG_DK
            slot = idx % STAGE_SLOTS

            def ldq(lo, n, cq=cq):
                return proj[pl.ds(pl.multiple_of(r0 + lo, 8), n), cq:cq + HG_DK]

            def ldb(lo, n, cf=cf):
                return proj[pl.ds(pl.multiple_of(r0 + lo, 8), n), cf:cf + HG_DK]

            def ldk(lo, n, ck=ck):
                return ksc[pl.ds(pl.multiple_of(r0 + lo, 8), n), ck:ck + HG_DK]

            def bline(r, cf=cf):
                grp = proj[pl.ds(pl.multiple_of(r0 + 8 * (r // 8), 8), 8), cf:cf + HG_DK]
                return grp[r % 8:r % 8 + 1, :]

            def brow(r, n):
                return jnp.broadcast_to(bline(r), (n, HG_DK))

            acc_rows = [None] * (L // 8)

            def accumulate(li, p, row0):
                for i in range(p.shape[0] // 8):
                    g = row0 // 8 + i
                    term = masks_ref[d, li, 8 * g:8 * g + 8, :] * p[8 * i:8 * i + 8]
                    acc_rows[g] = term if acc_rows[g] is None else acc_rows[g] + term

            accumulate(LV_DIAG, jnp.dot(
                ldq(0, L).astype(BF16), _staged_transpose(ldk(0, L).astype(BF16), tsl.at[slot, LV_DIAG]),
                preferred_element_type=F32), 0)
            for li, h in enumerate(LEVELS[:LV_4]):
                pieces, q_pieces, q_starts = [], [], []
                for j in range(L // (2 * h)):
                    lo = j * 2 * h
                    mid = lo + h
                    if d == 0:
                        bm = brow(mid - 1, h)
                        kp = ldk(lo, h) * jnp.exp2(bm - ldb(lo, h))
                        qp = ldq(mid, h) * jnp.exp2(ldb(mid, h) - bm)
                        pieces += [kp, qp]
                        q_starts.append(mid)
                    else:
                        bm = brow(mid, h)
                        qp = ldq(lo, h) * jnp.exp2(ldb(lo, h) - bm)
                        kp = ldk(mid, h) * jnp.exp2(bm - ldb(mid, h))
                        pieces += [qp, kp]
                        q_starts.append(lo)
                    q_pieces.append(qp)
                xt = _staged_transpose(jnp.concatenate(pieces, axis=0).astype(BF16), tsl.at[slot,li])
                p = jnp.dot(jnp.concatenate(q_pieces, axis=0).astype(BF16), xt, preferred_element_type=F32)
                for j, row0 in enumerate(q_starts):
                    accumulate(li, p[j * h:(j + 1) * h], row0)
            x4, xq21, xk21 = [], [], []
            for g in range(L // 8):
                qg, kg, bg = ldq(8 * g, 8), ldk(8 * g, 8), ldb(8 * g, 8)
                fg = 1.0 - kg
                qfg = qg * fg
                mid = 8 * g + (3 if d == 0 else 4)
                e4 = jnp.exp2(-jnp.abs(bg - brow(mid, 8)))
                fnx = pltpu.roll(fg, 7, 0)
                fpv = pltpu.roll(fg, 1, 0)
                k_over_f = kg / fg
                if d == 0:
                    x4.append(jnp.where(hi4, qg, kg) * e4)
                    xq21.append(jnp.where(is_r0, 0.0, jnp.where(is_r1, qg, jnp.where(is_r2, qfg, qfg * fpv))))
                    xk21.append(jnp.where(is_r0, kg * fnx, jnp.where(is_r1, kg, jnp.where(is_r2, k_over_f, 0.0))))
                else:
                    x4.append(jnp.where(hi4, kg, qg) * e4)
                    xq21.append(jnp.where(is_r0, qfg * fnx, jnp.where(is_r1, qfg, jnp.where(is_r2, qg, 0.0))))
                    xk21.append(jnp.where(is_r0, 0.0, jnp.where(is_r1, k_over_f, jnp.where(is_r2, kg, kg * fpv))))
            accumulate(LV_4, _gram(jnp.concatenate(x4, axis=0), tsl.at[slot, LV_4]), 0)
            accumulate(LV_21, jnp.dot(
                jnp.concatenate(xq21, axis=0).astype(BF16),
                _staged_transpose(jnp.concatenate(xk21, axis=0).astype(BF16), tsl.at[slot, LV_21]),
                preferred_element_type=F32), 0)
            acc = jnp.concatenate(acc_rows, axis=0)

            vt = vts[c, hd]
            st_t = st[idx]
            b = ldb(0, L)
            qt = (ldq(0, L) * jnp.exp2(b)).astype(BF16)
            vb = proj[rows, COL_V + hd * HG_DK:COL_V + (hd + 1) * HG_DK].astype(BF16)
            o = jnp.dot(jnp.concatenate([acc.astype(BF16), qt], axis=1),
                        jnp.concatenate([vb, _staged_transpose(st_t.astype(BF16), tsl.at[slot, LV_STATE])], axis=0),
                        preferred_element_type=F32)
            btot = bline(L - 1 if d == 0 else 0)
            kt = (ldk(0, L) * jnp.exp2(btot - b)).astype(BF16)
            st[idx] = st_t * jnp.exp2(btot) + jnp.dot(vt, kt, preferred_element_type=F32)
            x1_ref[0, rows, d * HG_WIDTH + hd * HG_DK:d * HG_WIDTH + (hd + 1) * HG_DK] = o

    n_chunks = T // L

    def hg_both(c, carry):
        hg_chunk(c, 0)
        hg_chunk(n_chunks - 1 - c, 1)
        if fuse_scan:
            steps = S // n_chunks
            carry = scan_steps(c * steps, steps, carry)
        return carry

    lax.fori_loop(0, n_chunks, hg_prep, 0, unroll=_unroll(n_chunks))
    scan_state = lax.fori_loop(0, n_chunks, hg_both, scan_init if fuse_scan else 0)

    @pl.when(pl.program_id(0) == 0)
    def _late_weights_ready():
        for k in range(len(late_hbm)):
            late_copy(k).wait()

    if emit_state:
        for i in range(N_DIR * HG_HEADS):
            hs_ref[0, i] = st[i].T

    def hg_fin(i, carry):
        rows = pl.ds(pl.multiple_of(i * L, L), L)
        for hd in range(HG_HEADS):
            cs = slice(hd * HG_DK, (hd + 1) * HG_DK)
            o = x1_ref[0, rows, cs] + x1_ref[0, rows, HG_WIDTH + hd * HG_DK:HG_WIDTH + (hd + 1) * HG_DK]
            ms = jnp.mean(o * o, axis=-1, keepdims=True)
            y = o * lax.rsqrt(ms + EPS) * hgg_ref[:, cs]
            gz = proj[rows, COL_GATE + hd * HG_DK:COL_GATE + (hd + 1) * HG_DK]
            mixin[rows, cs] = (y * _silu(gz)).astype(BF16)
        return carry

    lax.fori_loop(0, n_chunks, hg_fin, 0, unroll=_unroll(n_chunks))

    if not fuse_scan:
        lax.fori_loop(0, T // ROW_TILE, gate_body, 0)
        scan_state = lax.fori_loop(
            0, S // SCAN_UNROLL, lambda n, carry: scan_steps(n * SCAN_UNROLL, SCAN_UNROLL, carry), scan_init)
    h_end, d_end = scan_state[:2], scan_state[2:]

    carry_in = [[None] * n_grp for _ in range(N_DIR)]
    for d in range(N_DIR):
        for g in range(n_grp):
            if has_state:
                h0 = ls0_ref[0, d:d + 1, g * LRU_GROUP:(g + 1) * LRU_GROUP]
            else:
                h0 = jnp.zeros((1, LRU_GROUP), F32)
            dd, hh = scan8(d_end[d][g], h_end[d][g], d)
            seg_end = hh + dd * h0
            if d == 0:
                carry_in[d][g] = jnp.where(rowi2 >= 1, pltpu.roll(seg_end, 1, 0), h0)
                last = seg_end[LRU_SEGS - 1:LRU_SEGS, :]
            else:
                carry_in[d][g] = jnp.where(rowi2 < LRU_SEGS - 1, pltpu.roll(seg_end, LRU_SEGS - 1, 0), h0)
                last = seg_end[0:1, :]
            if emit_state:
                ls_ref[0, d:d + 1, g * LRU_GROUP:(g + 1) * LRU_GROUP] = last

    def fix_body(i, carry):
        rows = pl.ds(pl.multiple_of(i * FIX_TILE, FIX_TILE), FIX_TILE)
        for g in range(n_grp):
            tot = None
            for d in range(N_DIR):
                cin = jnp.concatenate([carry_in[d][g]] * (FIX_TILE // LRU_SEGS), axis=0)
                h = seg_out(d, rows, g, 0)[...] + seg_out(d, rows, g, 1)[...] * cin
                tot = h if tot is None else tot + h
            for l in range(slabs):
                xci[g * slabs + l, rows, :] = tot[:, l * 128:(l + 1) * 128]
        return carry

    lax.fori_loop(0, T // FIX_TILE, fix_body, 0)

    for j in range(LRU_SEGS):
        def lru_fin(ti, carry, j=j):
            i0 = pl.multiple_of(ti * CONV_TILE, CONV_TILE)
            rows = pl.ds(pl.multiple_of(j * S + i0, CONV_TILE), CONV_TILE)
            hsum = jnp.concatenate([xci[l, irows(i0, j, CONV_TILE), :] for l in range(LRU_WIDTH // 128)], axis=1)
            lg = proj[rows, PROJ_LG:PROJ_LG + LRU_WIDTH]
            gl = lg * (0.5 * (1.0 + jnp.tanh(0.7978845608028654 * (lg + 0.044715 * (lg * lg * lg)))))
            mixin[rows, HG_WIDTH:HG_WIDTH + LRU_WIDTH] = (hsum * gl).astype(BF16)
            return carry

        lax.fori_loop(0, S // CONV_TILE, lru_fin, 0, unroll=_unroll(S // CONV_TILE))

    def out_body(i, carry):
        rows = pl.ds(pl.multiple_of(i * ROW_TILE, ROW_TILE), ROW_TILE)
        mix = jnp.dot(mixin[rows, :], wout_ref[...], preferred_element_type=F32)
        x1_ref[0, rows, :] = load_x(i) + g1 * mix
        return carry

    lax.fori_loop(0, T // ROW_TILE, out_body, 0)

    if fuse_ffn:
        def ffn_body(i, carry):
            rows = pl.ds(pl.multiple_of(i * ROW_TILE, ROW_TILE), ROW_TILE)
            y_ref[0, rows, :] = _ffn_rows(x1_ref[0, rows, :], mrow, n2g_ref[...], fg_ref[...], w1_ref, w2_ref)
            return carry

        lax.fori_loop(0, T // ROW_TILE, ffn_body, 0)


def _const_spec(shape):
    nd = len(shape)
    return pl.BlockSpec(shape, lambda b, _n=nd: (0,) * _n, pipeline_mode=pl.Buffered(1))


def _nbytes(shape, dtype):
    return int(np.prod(shape)) * jnp.dtype(dtype).itemsize


def _mixer(x, m3, m_off, m_step, pos, consts, ffn_consts, states, emit_state):
    B, T, _ = x.shape
    has_pos = pos is not None
    has_state = states is not None
    scratch_shapes = [
        ((T, PROJ_COLS), F32),
        ((T + 16, LRU_WIDTH), F32),
        ((T, D_MODEL), BF16),
        ((8, HG_DK, HG_DK), F32),
        ((T, N_DIR * HG_HEADS * HG_DK), F32),
        ((LRU_WIDTH // 128, T, 128), F32),
        ((T // CHUNK, HG_HEADS, HG_DK, CHUNK), BF16),
        ((3 * N_DIR * (LRU_WIDTH // LRU_GROUP), LRU_SEGS, LRU_GROUP), F32),
        ((STAGE_SLOTS, LV_STATE + 1, HG_DK, CHUNK), BF16),
        ((LRU_WIDTH // LRU_GROUP, LRU_GROUP, 2 * N_DIR * LRU_GROUP), BF16),
    ]
    resident = (sum(_nbytes(s, d) for s, d in scratch_shapes)
                + sum(_nbytes(c.shape, c.dtype) for c in consts)
                + (sum(_nbytes(p.shape, p.dtype) for p in pos) if has_pos else 0))
    io_block = _nbytes((T, D_MODEL), F32)
    budget = VMEM_LIMIT - VMEM_HEADROOM
    lrs_shape = ((T, 2 * LRS_SEG_COL), F32)
    fuse_scan = resident + _nbytes(*lrs_shape) + 4 * io_block <= budget
    if fuse_scan:
        scratch_shapes.append(lrs_shape)
        resident += _nbytes(*lrs_shape)
    ffn_bytes = sum(_nbytes(c.shape, c.dtype) for c in ffn_consts) + io_block
    fuse_ffn = resident + ffn_bytes + 4 * io_block <= budget
    if fuse_ffn:
        scratch_shapes.insert(0, ((1, T, D_MODEL), F32))
        resident += ffn_bytes
    in_bufs = 2 if resident + 3 * io_block <= budget else 1
    out_bufs = 2 if resident + 4 * io_block <= budget else 1
    io_mode = pl.Buffered(out_bufs)
    in_specs = [
        pl.BlockSpec((1, T, D_MODEL), lambda b: (b, 0, 0), pipeline_mode=pl.Buffered(in_bufs)),
        pl.BlockSpec((1, 1, 6 * D_MODEL), lambda b: (m_off + m_step * b, 0, 0)),
    ]
    args = [x, m3]
    if has_pos:
        in_specs += [_const_spec(p.shape) for p in pos]
        args += list(pos)
    hbm_spec = pl.BlockSpec(memory_space=pl.ANY)
    late = [consts[-1]]
    in_specs += [_const_spec(c.shape) for c in consts[:-1]] + [hbm_spec]
    args += list(consts)
    if fuse_ffn:
        late += list(ffn_consts[-2:])
        in_specs += [_const_spec(c.shape) for c in ffn_consts[:-2]] + [hbm_spec, hbm_spec]
        args += list(ffn_consts)
    if has_state:
        hs0, ls0 = states
        in_specs += [
            pl.BlockSpec((1, 8, HG_DK, HG_DK), lambda b: (b, 0, 0, 0)),
            pl.BlockSpec((1, N_DIR, LRU_WIDTH), lambda b: (b, 0, 0)),
        ]
        args += [hs0, ls0]
    out_shape = [jax.ShapeDtypeStruct((B, T, D_MODEL), F32)]
    out_specs = [pl.BlockSpec((1, T, D_MODEL), lambda b: (b, 0, 0), pipeline_mode=io_mode)]
    if emit_state:
        out_shape += [jax.ShapeDtypeStruct((B, 8, HG_DK, HG_DK), F32),
                      jax.ShapeDtypeStruct((B, N_DIR, LRU_WIDTH), F32)]
        out_specs += [pl.BlockSpec((1, 8, HG_DK, HG_DK), lambda b: (b, 0, 0, 0)),
                      pl.BlockSpec((1, N_DIR, LRU_WIDTH), lambda b: (b, 0, 0))]
    scratch = [pltpu.VMEM(s, d) for s, d in scratch_shapes]
    scratch += [pltpu.VMEM(w.shape, w.dtype) for w in late] + [pltpu.SemaphoreType.DMA((len(late),))]
    outs = pl.pallas_call(
        functools.partial(_mixer_kernel, T=T, has_pos=has_pos, has_state=has_state, emit_state=emit_state,
                          fuse_scan=fuse_scan, fuse_ffn=fuse_ffn),
        grid=(B,),
        in_specs=in_specs,
        out_specs=out_specs,
        out_shape=out_shape,
        scratch_shapes=scratch,
        compiler_params=pltpu.CompilerParams(
            dimension_semantics=("arbitrary",), vmem_limit_bytes=VMEM_LIMIT),
        name=f"mixer_t{T}",
    )(*args)
    return (outs[0], fuse_ffn) + tuple(outs[1:])


def _ffn_rows(x, mrow, n2g, fgain, w1_ref, w2_ref):
    sh2 = mrow[:, 3 * D_MODEL:4 * D_MODEL]
    sc2 = mrow[:, 4 * D_MODEL:5 * D_MODEL]
    g2 = mrow[:, 5 * D_MODEL:6 * D_MODEL]
    ms = jnp.mean(x * x, axis=-1, keepdims=True)
    hb = (x * lax.rsqrt(ms + EPS) * (n2g * (1.0 + sc2)) + sh2).astype(BF16)
    ff = jnp.zeros(x.shape, F32)
    for c in range(D_FF // FF_CHUNK):
        a = jnp.dot(hb, w1_ref[:, c * FF_CHUNK:(c + 1) * FF_CHUNK], preferred_element_type=F32)
        a = jnp.maximum(a, 0.0)
        ff = ff + jnp.dot((a * a).astype(BF16), w2_ref[c * FF_CHUNK:(c + 1) * FF_CHUNK, :],
                          preferred_element_type=F32)
    x2 = x + g2 * ff
    ms2 = jnp.mean(x2 * x2, axis=-1, keepdims=True)
    return x2 * lax.rsqrt(ms2 + EPS) * fgain


def _ffn_kernel(x_ref, m_ref, n2g_ref, fg_ref, w1_ref, w2_ref, y_ref):
    y_ref[...] = _ffn_rows(x_ref[...], m_ref[0], n2g_ref[...], fg_ref[...], w1_ref, w2_ref)


def _ffn(x1, m3, m_off, tiles_per_cond, n2g, fgain, w1, w2):
    n = x1.shape[0]

    def m_index(i):
        if tiles_per_cond is None:
            return (m_off, 0, 0)
        return (m_off + i // tiles_per_cond, 0, 0)

    return pl.pallas_call(
        _ffn_kernel,
        grid=(n // FFN_TILE,),
        in_specs=[
            pl.BlockSpec((FFN_TILE, D_MODEL), lambda i: (i, 0)),
            pl.BlockSpec((1, 1, 6 * D_MODEL), m_index),
            _const_spec(n2g.shape),
            _const_spec(fgain.shape),
            _const_spec(w1.shape),
            _const_spec(w2.shape),
        ],
        out_specs=pl.BlockSpec((FFN_TILE, D_MODEL), lambda i: (i, 0)),
        out_shape=jax.ShapeDtypeStruct((n, D_MODEL), F32),
        compiler_params=pltpu.CompilerParams(
            dimension_semantics=("arbitrary",), vmem_limit_bytes=VMEM_LIMIT),
        name="ffn",
    )(x1, m3, n2g, fgain, w1, w2)


def _grid_pos_tables(n_tok):
    quarter = D_MODEL // 4
    omega = (1.0 / (np.float32(POS_BASE) ** (np.arange(quarter, dtype=np.float32) / np.float32(quarter)))
             ).astype(np.float32)

    def emb(n):
        ang = np.arange(n).reshape(-1, 1).astype(np.float32) * omega
        return np.concatenate([np.sin(ang), np.cos(ang)], axis=-1)

    rows = np.repeat(emb(n_tok // GRID_W)[:, None, :], 8, axis=1)
    return jnp.asarray(rows, dtype=F32), jnp.asarray(emb(GRID_W), dtype=F32)


def kernel(x_prompt, x_sample, c, state_hgrn, state_rglru, c_ctx, w_ada, b_ada, norm1_gain, norm2_gain,
           w_in, hg_lb_logits, hg_norm_gain, conv_w, conv_b, lru_wa, lru_ba, lru_wx, lru_bx, lru_lambda,
           w_out, w_ff1, w_ff2, final_gain):
    bp, tp, _ = x_prompt.shape
    bs_, ts, _ = x_sample.shape

    m3, (w_in_b, w_out_b, w1, w2) = _modulation(
        c_ctx, c, w_ada[0], b_ada, [w_in[0], w_out[0], w_ff1[0], w_ff2[0]])

    masks_np, tri_np = _level_tables()
    consts = [
        norm1_gain,
        w_in_b,
        hg_lb_logits.reshape(2, N_DIR * HG_HEADS, HG_DK),
        hg_norm_gain[0].reshape(1, HG_WIDTH),
        jnp.asarray(masks_np),
        jnp.asarray(tri_np, dtype=BF16),
        conv_w[0],
        conv_b,
        lru_wa[0].reshape(N_DIR * LRU_BLOCKS, LRU_BLOCK, LRU_BLOCK),
        lru_wx[0].reshape(N_DIR * LRU_BLOCKS, LRU_BLOCK, LRU_BLOCK),
        lru_ba[0],
        lru_bx[0],
        lru_lambda[0],
        w_out_b,
    ]
    fgain = final_gain.reshape(1, D_MODEL)
    ffn_consts = [norm2_gain, fgain, w1, w2]

    y_prompt, done, hs, ls = _mixer(x_prompt, m3, 0, 0, None, consts, ffn_consts, None, True)
    if not done:
        y_prompt = _ffn(y_prompt.reshape(bp * tp, D_MODEL), m3, 0, None, *ffn_consts)

    y_sample, done = _mixer(x_sample, m3, 1, 1, _grid_pos_tables(ts), consts, ffn_consts,
                            (state_hgrn.reshape(bs_, N_DIR * HG_HEADS, HG_DK, HG_DK),
                             state_rglru.reshape(bs_, N_DIR, LRU_WIDTH)), False)
    if not done:
        y_sample = _ffn(y_sample.reshape(bs_ * ts, D_MODEL), m3, 1, ts // FFN_TILE, *ffn_consts)

    return (y_prompt.reshape(bp, tp, D_MODEL),
            y_sample.reshape(bs_, ts, D_MODEL),
            hs.reshape(bp, 1, N_DIR, HG_HEADS, HG_DK, HG_DK),
            ls.reshape(bp, 1, N_DIR, LRU_WIDTH))
```

```python
import functools

import numpy as np
import jax
import jax.numpy as jnp
from jax import lax
from jax.experimental import pallas as pl
from jax.experimental.pallas import tpu as pltpu

F32 = jnp.float32
BF16 = jnp.bfloat16

D_MODEL = 1024
N_DIR = 2
HG_HEADS = 4
HG_DK = 128
HG_WIDTH = 512
LRU_WIDTH = 512
LRU_BLOCKS = 8
LRU_BLOCK = 64
LRU_C = 8.0
D_FF = 4096
IN_COLS = 4096
EPS = 1e-6
LOG2E = 1.4426950408889634
GRID_W = 64
POS_BASE = 10000.0

COL_Q = 0
COL_F = 1024
COL_V = 2048
COL_GATE = 2560
COL_LX = 3072
COL_LG = 3584
PROJ_LG = COL_LX
PROJ_COLS = IN_COLS - LRU_WIDTH

CHUNK = 128
LEVELS = (64, 32, 16, 8, 4)
LV_4 = LEVELS.index(4)
LV_21 = len(LEVELS)
LV_DIAG = LV_21 + 1
LV_STATE = LV_DIAG + 1
N_MASKS = LV_DIAG + 1
ROW_TILE = 256
LRU_SEGS = 8
CONV_TILE = 32
SCAN_UNROLL = 8
FIX_TILE = 64
STAGE_SLOTS = 1
MAX_INLINE_TRIPS = 2
LOOP_UNROLL = 2
LRU_GROUP = 256
GATE_D = 2 * LRU_GROUP
GATE_G = N_DIR * GATE_D
LRS_SEG_COL = 2 * N_DIR * LRU_WIDTH
PROJ_STEP = 512
FFN_TILE = 512
FF_CHUNK = 1024
MOD_STEPS = 8
MOD_ROWS = 8
VMEM_LIMIT = 58 * 1024 * 1024
VMEM_HEADROOM = 6 * 1024 * 1024


def _sig(x):
    return 0.5 * jnp.tanh(0.5 * x) + 0.5


def _silu(x):
    h = 0.5 * x
    return h * jnp.tanh(h) + h


def _unroll(trips):
    return True if trips <= MAX_INLINE_TRIPS else LOOP_UNROLL


def _nt_dot(a, b):
    return lax.dot_general(a, b, (((1,), (1,)), ((), ())), preferred_element_type=F32)


def _staged_transpose(xb, slot_ref):
    slot_ref[...] = xb.T
    return slot_ref[...]


def _gram(x, slot_ref):
    xb = x.astype(BF16)
    return jnp.dot(xb, _staged_transpose(xb, slot_ref), preferred_element_type=F32)


def _level_tables():
    t = np.arange(CHUNK)[:, None]
    s = np.arange(CHUNK)[None, :]
    masks = np.zeros((N_DIR, N_MASKS, CHUNK, CHUNK), np.float32)
    for li, h in enumerate(LEVELS):
        same = (t // (2 * h)) == (s // (2 * h))
        t_hi = (t // h) % 2 == 1
        s_hi = (s // h) % 2 == 1
        masks[0, li] = same & t_hi & ~s_hi
        masks[1, li] = same & ~t_hi & s_hi
    same4 = (t // 4) == (s // 4)
    masks[0, LV_21] = same4 & (s < t)
    masks[1, LV_21] = same4 & (s > t)
    masks[:, LV_DIAG] = (t == s)
    tri = np.stack([(s <= t), (s >= t)]).astype(np.float32)
    return masks, tri


def _mod_kernel(cctx_ref, c_ref, w_ref, b_ref, *rest):
    n_w = (len(rest) - 2) // 2
    f32_refs, o_ref, bf16_refs, cond = rest[:n_w], rest[n_w], rest[n_w + 1:2 * n_w + 1], rest[-1]
    n = c_ref.shape[0]
    cond[...] = jnp.zeros(cond.shape, F32)
    cond[0:1, :] = cctx_ref[...]
    cond[1:1 + n, :] = c_ref[...]
    c = cond[...]
    a = _silu(c).astype(BF16)
    m = jnp.dot(a, w_ref[...].astype(BF16), preferred_element_type=F32) + b_ref[...]
    for r in range(MOD_ROWS):
        o_ref[r] = m[r:r + 1]
    for src, dst in zip(f32_refs, bf16_refs):
        dst[...] = src[...].astype(BF16)


def _modulation(c_ctx, c, w_ada, b_ada, weights):
    n = w_ada.shape[1]
    assert 1 + c.shape[0] <= MOD_ROWS
    tile = n // MOD_STEPS
    slab = lambda w: pl.BlockSpec((w.shape[0] // MOD_STEPS, w.shape[1]), lambda j: (j, 0))
    outs = pl.pallas_call(
        _mod_kernel,
        grid=(MOD_STEPS,),
        in_specs=[
            pl.BlockSpec((1, D_MODEL), lambda j: (0, 0)),
            pl.BlockSpec(c.shape, lambda j: (0, 0)),
            pl.BlockSpec((D_MODEL, tile), lambda j: (0, j)),
            pl.BlockSpec((1, tile), lambda j: (0, j)),
        ] + [slab(w) for w in weights],
        out_specs=[pl.BlockSpec((MOD_ROWS, 1, tile), lambda j: (0, 0, j))] + [slab(w) for w in weights],
        out_shape=[jax.ShapeDtypeStruct((MOD_ROWS, 1, n), F32)] + [
            jax.ShapeDtypeStruct(w.shape, BF16) for w in weights],
        scratch_shapes=[pltpu.VMEM((MOD_ROWS, D_MODEL), F32)],
        compiler_params=pltpu.CompilerParams(
            dimension_semantics=("arbitrary",), vmem_limit_bytes=VMEM_LIMIT),
        name="adaln_modulation",
    )(c_ctx.reshape(1, D_MODEL), c, w_ada, b_ada, *weights)
    return outs[0], outs[1:]


def _mixer_kernel(*refs, T, has_pos, has_state, emit_state, fuse_scan, fuse_ffn, has_extra):
    it = iter(refs)
    x_ref = next(it)
    m_ref = next(it)
    if has_pos:
        posr_ref = next(it)
        posc_ref = next(it)
    n1g_ref = next(it)
    win_ref = next(it)
    lbl_ref = next(it)
    hgg_ref = next(it)
    masks_ref = next(it)
    tri_ref = next(it)
    convw_ref = next(it)
    convb_ref = next(it)
    wa_ref = next(it)
    wx_ref = next(it)
    ba_ref = next(it)
    bx_ref = next(it)
    lam_ref = next(it)
    late_hbm = [next(it)]
    if fuse_ffn:
        n2g_ref = next(it)
        fg_ref = next(it)
        late_hbm += [next(it), next(it)]
    if has_state:
        hs0_ref = next(it)
        ls0_ref = next(it)
    if has_extra:
        xe_ref = next(it)
        me_ref = next(it)
    y_ref = next(it)
    if emit_state:
        hs_ref = next(it)
        ls_ref = next(it)
    if has_extra:
        ye_ref = next(it)
    x1_ref = next(it) if fuse_ffn else y_ref
    proj = next(it)
    lxp = next(it)
    mixin = next(it)
    st = next(it)
    ksc = next(it)
    xci = next(it)
    vts = next(it)
    rowc = next(it)
    tsl = next(it)
    wg = next(it)
    if fuse_scan:
        lrs = next(it)
    late_vmem = [next(it) for _ in late_hbm]
    late_sem = next(it)
    wout_ref = late_vmem[0]
    if fuse_ffn:
        w1_ref, w2_ref = late_vmem[1:]

    L = CHUNK

    def late_copy(k):
        return pltpu.make_async_copy(late_hbm[k], late_vmem[k], late_sem.at[k])

    @pl.when(pl.program_id(0) == 0)
    def _first_step():
        for k in range(len(late_hbm)):
            late_copy(k).start()
        per_group = LRU_GROUP // LRU_BLOCK
        wg[...] = jnp.zeros(wg.shape, BF16)
        for g in range(LRU_WIDTH // LRU_GROUP):
            for p in range(2 * N_DIR):
                src = wa_ref if p % 2 == 0 else wx_ref
                for n in range(per_group):
                    r = n * LRU_BLOCK
                    col = p * LRU_GROUP + r
                    wg[g, r:r + LRU_BLOCK, col:col + LRU_BLOCK] = src[
                        (p // 2) * LRU_BLOCKS + g * per_group + n].astype(BF16)

    mrow = m_ref[0]
    sh1 = mrow[:, 0:D_MODEL]
    sc1 = mrow[:, D_MODEL:2 * D_MODEL]
    g1 = mrow[:, 2 * D_MODEL:3 * D_MODEL]
    gain1 = n1g_ref[...] * (1.0 + sc1)

    zrows = jnp.zeros((8, LRU_WIDTH), F32)
    lxp[0:8, :] = zrows
    lxp[T + 8:T + 16, :] = zrows

    def load_x(i):
        xt = x_ref[0, pl.ds(pl.multiple_of(i * ROW_TILE, ROW_TILE), ROW_TILE), :]
        if has_pos:
            per_tile = ROW_TILE // GRID_W
            tiles = []
            for s in range(per_tile):
                row_emb = jnp.concatenate([posr_ref[i * per_tile + s]] * (GRID_W // 8), axis=0)
                tiles.append(jnp.concatenate([row_emb, posc_ref[...]], axis=1))
            xt = xt + jnp.concatenate(tiles, axis=0)
        return xt

    def proj_body(i, carry):
        r0 = pl.multiple_of(i * ROW_TILE, ROW_TILE)
        xt = load_x(i)
        ms = jnp.mean(xt * xt, axis=-1, keepdims=True)
        hb = (xt * lax.rsqrt(ms + EPS) * gain1 + sh1).astype(BF16)
        starts = range(0, IN_COLS, PROJ_STEP)
        order = [COL_LX] + [c0 for c0 in starts if c0 < COL_GATE] + [
            c0 for c0 in starts if c0 >= COL_GATE and c0 != COL_LX]
        for c0 in order:
            res = jnp.dot(hb, win_ref[:, c0:c0 + PROJ_STEP], preferred_element_type=F32)
            if c0 == COL_LX:
                lxp[pl.ds(pl.multiple_of(r0 + 8, 8), ROW_TILE), :] = res
            elif c0 == COL_LG:
                proj[pl.ds(r0, ROW_TILE), PROJ_LG:PROJ_LG + LRU_WIDTH] = res
            else:
                proj[pl.ds(r0, ROW_TILE), c0:c0 + PROJ_STEP] = res
        return carry

    l0 = lbl_ref[0]
    l1 = lbl_ref[1]
    lmx = jnp.maximum(l0, l1)
    e0 = jnp.exp(l0 - lmx)
    e1 = jnp.exp(l1 - lmx)
    lb_all = e0 / (e0 + e1)

    def hg_prep(c, carry):
        rows = pl.ds(pl.multiple_of(c * L, L), L)
        for hd in range(HG_HEADS):
            vts[c, hd] = proj[rows, COL_V + hd * HG_DK:COL_V + (hd + 1) * HG_DK].T.astype(BF16)
        for d in range(N_DIR):
            for hd in range(HG_HEADS):
                idx = d * HG_HEADS + hd
                cq = COL_Q + idx * HG_DK
                cf = COL_F + idx * HG_DK
                hq = proj[rows, cq:cq + HG_DK]
                fz = proj[rows, cf:cf + HG_DK]
                proj[rows, cq:cq + HG_DK] = _silu(hq)
                sg = _sig(fz)
                lb = lb_all[idx:idx + 1, :]
                oml = 1.0 - lb
                ksc[rows, idx * HG_DK:(idx + 1) * HG_DK] = oml * (1.0 - sg)
                logf = jnp.log(lb + oml * sg)
                p1 = logf.astype(BF16)
                p2 = (logf - p1.astype(F32)).astype(BF16)
                bb = jnp.dot(tri_ref[d], jnp.concatenate([p1, p2], axis=1), preferred_element_type=F32)
                proj[rows, cf:cf + HG_DK] = (bb[:, 0:HG_DK] + bb[:, HG_DK:2 * HG_DK]) * LOG2E
        return carry

    lax.fori_loop(0, T // ROW_TILE, proj_body, 0)

    S = T // LRU_SEGS
    cw = convw_ref[...]
    cb = convb_ref[...]

    def irows(i0, j, n):
        return pl.ds(pl.multiple_of(LRU_SEGS * i0, 8) + j, n, stride=LRU_SEGS)

    for j in range(LRU_SEGS):
        def conv_body(ti, carry, j=j):
            i0 = pl.multiple_of(ti * CONV_TILE, CONV_TILE)
            win = lxp[pl.ds(pl.multiple_of(j * S + i0, 8), CONV_TILE + 16), :]
            xc = cb
            for tap in range(4):
                xc = xc + win[6 + tap:6 + tap + CONV_TILE] * cw[tap:tap + 1]
            for l in range(LRU_WIDTH // 128):
                xci[l, irows(i0, j, CONV_TILE), :] = xc[:, l * 128:(l + 1) * 128]
            return carry

        lax.fori_loop(0, S // CONV_TILE, conv_body, 0, unroll=_unroll(S // CONV_TILE))

    n_grp = LRU_WIDTH // LRU_GROUP
    slabs = LRU_GROUP // 128
    gates = lrs if fuse_scan else proj

    def xc_tile(rows, g):
        return jnp.concatenate([xci[g * slabs + l, rows, :] for l in range(slabs)], axis=1)

    def gate_body(i, carry):
        rows = pl.ds(pl.multiple_of(i * ROW_TILE, ROW_TILE), ROW_TILE)
        for g in range(n_grp):
            gates[rows, g * GATE_G:(g + 1) * GATE_G] = jnp.dot(
                xc_tile(rows, g).astype(BF16), wg[g], preferred_element_type=F32)
        return carry

    lam = lam_ref[...]
    nl = -lam
    c8 = -LRU_C * (jnp.maximum(nl, 0.0) + jnp.log1p(jnp.exp(-jnp.abs(nl))))
    rowi2 = lax.broadcasted_iota(jnp.int32, (8, LRU_GROUP), 0)

    for d in range(N_DIR):
        for g in range(n_grp):
            k3 = 3 * (d * n_grp + g)
            chans = slice(g * LRU_GROUP, (g + 1) * LRU_GROUP)
            for r, row in enumerate((ba_ref[d:d + 1, chans], bx_ref[d:d + 1, chans], c8[d:d + 1, chans])):
                rowc[k3 + r] = jnp.broadcast_to(row, (LRU_SEGS, LRU_GROUP))

    def seg_out(d, rows, g, part):
        if fuse_scan:
            col = LRS_SEG_COL + ((d * n_grp + g) * 2 + part) * LRU_GROUP
            return lrs.at[rows, col:col + LRU_GROUP]
        cols = slice((g * 2 + part) * LRU_GROUP, (g * 2 + part + 1) * LRU_GROUP)
        return (ksc.at[rows, cols] if d == 0 else x1_ref.at[0, rows, cols])

    def lru_inputs(rows8, g, d):
        base = g * GATE_G + d * GATE_D
        k3 = 3 * (d * n_grp + g)
        ga = gates[rows8, base:base + LRU_GROUP] + rowc[k3]
        gx = gates[rows8, base + LRU_GROUP:base + 2 * LRU_GROUP] + rowc[k3 + 1]
        xc8 = xc_tile(rows8, g)
        log_a = rowc[k3 + 2] * _sig(ga)
        a = jnp.exp(log_a)
        z = jnp.tanh(-log_a) * (1.0 + a * a)
        mult = jnp.where(z > 0.0, z * lax.rsqrt(z), 0.0)
        return a, mult * (_sig(gx) * xc8)

    def scan8(a, u, d):
        for sft in (1, 2, 4):
            if d == 0:
                keep = rowi2 >= sft
                amt = sft
            else:
                keep = rowi2 < 8 - sft
                amt = 8 - sft
            ash = jnp.where(keep, pltpu.roll(a, amt, 0), 1.0)
            ush = jnp.where(keep, pltpu.roll(u, amt, 0), 0.0)
            u = a * ush + u
            a = a * ash
        return a, u

    def scan_steps(first, count, carry):
        hs, ds = [list(c) for c in carry[:2]], [list(c) for c in carry[2:]]
        for u in range(count):
            i_f = first + u
            for d, i in ((0, i_f), (1, S - 1 - i_f)):
                rows8 = pl.ds(pl.multiple_of(LRU_SEGS * i, 8), 8)
                for g in range(n_grp):
                    a, uu = lru_inputs(rows8, g, d)
                    hs[d][g] = a * hs[d][g] + uu
                    ds[d][g] = a * ds[d][g]
                    seg_out(d, rows8, g, 0)[...] = hs[d][g]
                    seg_out(d, rows8, g, 1)[...] = ds[d][g]
        return tuple(tuple(c) for c in hs + ds)

    zero8 = jnp.zeros((LRU_SEGS, LRU_GROUP), F32)
    one8 = jnp.ones((LRU_SEGS, LRU_GROUP), F32)
    scan_init = ((zero8,) * n_grp, (zero8,) * n_grp, (one8,) * n_grp, (one8,) * n_grp)
    if fuse_scan:
        lax.fori_loop(0, T // ROW_TILE, gate_body, 0)

    rowi = lax.broadcasted_iota(jnp.int32, (8, HG_DK), 0)
    r4 = rowi & 3
    is_r0 = r4 == 0
    is_r1 = r4 == 1
    is_r2 = r4 == 2
    hi4 = rowi >= 4

    for i in range(N_DIR * HG_HEADS):
        if has_state:
            st[i] = hs0_ref[0, i].T
        else:
            st[i] = jnp.zeros((HG_DK, HG_DK), F32)

    def hg_chunk(c, d):
        r0 = pl.multiple_of(c * L, L)
        rows = pl.ds(r0, L)
        for hd in range(HG_HEADS):
            idx = d * HG_HEADS + hd
            cq = COL_Q + idx * HG_DK
            cf = COL_F + idx * HG_DK
            ck = idx * HG_DK
            slot = idx % STAGE_SLOTS

            def ldq(lo, n, cq=cq):
                return proj[pl.ds(pl.multiple_of(r0 + lo, 8), n), cq:cq + HG_DK]

            def ldb(lo, n, cf=cf):
                return proj[pl.ds(pl.multiple_of(r0 + lo, 8), n), cf:cf + HG_DK]

            def ldk(lo, n, ck=ck):
                return ksc[pl.ds(pl.multiple_of(r0 + lo, 8), n), ck:ck + HG_DK]

            def bline(r, cf=cf):
                grp = proj[pl.ds(pl.multiple_of(r0 + 8 * (r // 8), 8), 8), cf:cf + HG_DK]
                return grp[r % 8:r % 8 + 1, :]

            def brow(r, n):
                return jnp.broadcast_to(bline(r), (n, HG_DK))

            acc_rows = [None] * (L // 8)

            def accumulate(li, p, row0):
                for i in range(p.shape[0] // 8):
                    g = row0 // 8 + i
                    term = masks_ref[d, li, 8 * g:8 * g + 8, :] * p[8 * i:8 * i + 8]
                    acc_rows[g] = term if acc_rows[g] is None else acc_rows[g] + term

            accumulate(LV_DIAG, jnp.dot(
                ldq(0, L).astype(BF16), _staged_transpose(ldk(0, L).astype(BF16), tsl.at[slot, LV_DIAG]),
                preferred_element_type=F32), 0)
            for li, h in enumerate(LEVELS[:LV_4]):
                pieces, q_pieces, q_starts = [], [], []
                for j in range(L // (2 * h)):
                    lo = j * 2 * h
                    mid = lo + h
                    if d == 0:
                        bm = brow(mid - 1, h)
                        kp = ldk(lo, h) * jnp.exp2(bm - ldb(lo, h))
                        qp = ldq(mid, h) * jnp.exp2(ldb(mid, h) - bm)
                        pieces += [kp, qp]
                        q_starts.append(mid)
                    else:
                        bm = brow(mid, h)
                        qp = ldq(lo, h) * jnp.exp2(ldb(lo, h) - bm)
                        kp = ldk(mid, h) * jnp.exp2(bm - ldb(mid, h))
                        pieces += [qp, kp]
                        q_starts.append(lo)
                    q_pieces.append(qp)
                xt = _staged_transpose(jnp.concatenate(pieces, axis=0).astype(BF16), tsl.at[slot,li])
                p = jnp.dot(jnp.concatenate(q_pieces, axis=0).astype(BF16), xt, preferred_element_type=F32)
                for j, row0 in enumerate(q_starts):
                    accumulate(li, p[j * h:(j + 1) * h], row0)
            x4, xq21, xk21 = [], [], []
            for g in range(L // 8):
                qg, kg, bg = ldq(8 * g, 8), ldk(8 * g, 8), ldb(8 * g, 8)
                fg = 1.0 - kg
                qfg = qg * fg
                mid = 8 * g + (3 if d == 0 else 4)
                e4 = jnp.exp2(-jnp.abs(bg - brow(mid, 8)))
                fnx = pltpu.roll(fg, 7, 0)
                fpv = pltpu.roll(fg, 1, 0)
                k_over_f = kg / fg
                if d == 0:
                    x4.append(jnp.where(hi4, qg, kg) * e4)
                    xq21.append(jnp.where(is_r0, 0.0, jnp.where(is_r1, qg, jnp.where(is_r2, qfg, qfg * fpv))))
                    xk21.append(jnp.where(is_r0, kg * fnx, jnp.where(is_r1, kg, jnp.where(is_r2, k_over_f, 0.0))))
                else:
                    x4.append(jnp.where(hi4, kg, qg) * e4)
                    xq21.append(jnp.where(is_r0, qfg * fnx, jnp.where(is_r1, qfg, jnp.where(is_r2, qg, 0.0))))
                    xk21.append(jnp.where(is_r0, 0.0, jnp.where(is_r1, k_over_f, jnp.where(is_r2, kg, kg * fpv))))
            accumulate(LV_4, _gram(jnp.concatenate(x4, axis=0), tsl.at[slot, LV_4]), 0)
            accumulate(LV_21, jnp.dot(
                jnp.concatenate(xq21, axis=0).astype(BF16),
                _staged_transpose(jnp.concatenate(xk21, axis=0).astype(BF16), tsl.at[slot, LV_21]),
                preferred_element_type=F32), 0)
            acc = jnp.concatenate(acc_rows, axis=0)

            vt = vts[c, hd]
            st_t = st[idx]
            b = ldb(0, L)
            qt = (ldq(0, L) * jnp.exp2(b)).astype(BF16)
            vb = proj[rows, COL_V + hd * HG_DK:COL_V + (hd + 1) * HG_DK].astype(BF16)
            o = jnp.dot(jnp.concatenate([acc.astype(BF16), qt], axis=1),
                        jnp.concatenate([vb, _staged_transpose(st_t.astype(BF16), tsl.at[slot, LV_STATE])], axis=0),
                        preferred_element_type=F32)
            btot = bline(L - 1 if d == 0 else 0)
            kt = (ldk(0, L) * jnp.exp2(btot - b)).astype(BF16)
            st[idx] = st_t * jnp.exp2(btot) + jnp.dot(vt, kt, preferred_element_type=F32)
            x1_ref[0, rows, d * HG_WIDTH + hd * HG_DK:d * HG_WIDTH + (hd + 1) * HG_DK] = o

    n_chunks = T // L

    def hg_both(c, carry):
        hg_chunk(c, 0)
        hg_chunk(n_chunks - 1 - c, 1)
        if fuse_scan:
            steps = S // n_chunks
            carry = scan_steps(c * steps, steps, carry)
        return carry

    lax.fori_loop(0, n_chunks, hg_prep, 0, unroll=_unroll(n_chunks))
    scan_state = lax.fori_loop(0, n_chunks, hg_both, scan_init if fuse_scan else 0)

    @pl.when(pl.program_id(0) == 0)
    def _late_weights_ready():
        for k in range(len(late_hbm)):
            late_copy(k).wait()

    if emit_state:
        for i in range(N_DIR * HG_HEADS):
            hs_ref[0, i] = st[i].T

    def hg_fin(i, carry):
        rows = pl.ds(pl.multiple_of(i * L, L), L)
        for hd in range(HG_HEADS):
            cs = slice(hd * HG_DK, (hd + 1) * HG_DK)
            o = x1_ref[0, rows, cs] + x1_ref[0, rows, HG_WIDTH + hd * HG_DK:HG_WIDTH + (hd + 1) * HG_DK]
            ms = jnp.mean(o * o, axis=-1, keepdims=True)
            y = o * lax.rsqrt(ms + EPS) * hgg_ref[:, cs]
            gz = proj[rows, COL_GATE + hd * HG_DK:COL_GATE + (hd + 1) * HG_DK]
            mixin[rows, cs] = (y * _silu(gz)).astype(BF16)
        return carry

    lax.fori_loop(0, n_chunks, hg_fin, 0, unroll=_unroll(n_chunks))

    if not fuse_scan:
        lax.fori_loop(0, T // ROW_TILE, gate_body, 0)
        scan_state = lax.fori_loop(
            0, S // SCAN_UNROLL, lambda n, carry: scan_steps(n * SCAN_UNROLL, SCAN_UNROLL, carry), scan_init)
    h_end, d_end = scan_state[:2], scan_state[2:]

    carry_in = [[None] * n_grp for _ in range(N_DIR)]
    for d in range(N_DIR):
        for g in range(n_grp):
            if has_state:
                h0 = ls0_ref[0, d:d + 1, g * LRU_GROUP:(g + 1) * LRU_GROUP]
            else:
                h0 = jnp.zeros((1, LRU_GROUP), F32)
            dd, hh = scan8(d_end[d][g], h_end[d][g], d)
            seg_end = hh + dd * h0
            if d == 0:
                carry_in[d][g] = jnp.where(rowi2 >= 1, pltpu.roll(seg_end, 1, 0), h0)
                last = seg_end[LRU_SEGS - 1:LRU_SEGS, :]
            else:
                carry_in[d][g] = jnp.where(rowi2 < LRU_SEGS - 1, pltpu.roll(seg_end, LRU_SEGS - 1, 0), h0)
                last = seg_end[0:1, :]
            if emit_state:
                ls_ref[0, d:d + 1, g * LRU_GROUP:(g + 1) * LRU_GROUP] = last

    def fix_body(i, carry):
        rows = pl.ds(pl.multiple_of(i * FIX_TILE, FIX_TILE), FIX_TILE)
        for g in range(n_grp):
            tot = None
            for d in range(N_DIR):
                cin = jnp.concatenate([carry_in[d][g]] * (FIX_TILE // LRU_SEGS), axis=0)
                h = seg_out(d, rows, g, 0)[...] + seg_out(d, rows, g, 1)[...] * cin
                tot = h if tot is None else tot + h
            for l in range(slabs):
                xci[g * slabs + l, rows, :] = tot[:, l * 128:(l + 1) * 128]
        return carry

    lax.fori_loop(0, T // FIX_TILE, fix_body, 0)

    for j in range(LRU_SEGS):
        def lru_fin(ti, carry, j=j):
            i0 = pl.multiple_of(ti * CONV_TILE, CONV_TILE)
            rows = pl.ds(pl.multiple_of(j * S + i0, CONV_TILE), CONV_TILE)
            hsum = jnp.concatenate([xci[l, irows(i0, j, CONV_TILE), :] for l in range(LRU_WIDTH // 128)], axis=1)
            lg = proj[rows, PROJ_LG:PROJ_LG + LRU_WIDTH]
            gl = lg * (0.5 * (1.0 + jnp.tanh(0.7978845608028654 * (lg + 0.044715 * (lg * lg * lg)))))
            mixin[rows, HG_WIDTH:HG_WIDTH + LRU_WIDTH] = (hsum * gl).astype(BF16)
            return carry

        lax.fori_loop(0, S // CONV_TILE, lru_fin, 0, unroll=_unroll(S // CONV_TILE))

    def out_body(i, carry):
        rows = pl.ds(pl.multiple_of(i * ROW_TILE, ROW_TILE), ROW_TILE)
        mix = jnp.dot(mixin[rows, :], wout_ref[...], preferred_element_type=F32)
        x1_ref[0, rows, :] = load_x(i) + g1 * mix
        return carry

    lax.fori_loop(0, T // ROW_TILE, out_body, 0)

    if fuse_ffn:
        def ffn_body(i, carry):
            rows = pl.ds(pl.multiple_of(i * ROW_TILE, ROW_TILE), ROW_TILE)
            y_ref[0, rows, :] = _ffn_rows([x1_ref[0, rows, :]], [mrow], n2g_ref[...], fg_ref[...], w1_ref, w2_ref)[0]
            return carry

        n_tiles = T // ROW_TILE
        lax.fori_loop(0, n_tiles - (1 if has_extra else 0), ffn_body, 0)
        if has_extra:
            rows = pl.ds((n_tiles - 1) * ROW_TILE, ROW_TILE)
            y_ref[0, rows, :], ye_ref[...] = _ffn_rows(
                [x1_ref[0, rows, :], xe_ref[...]], [mrow, me_ref[0]], n2g_ref[...], fg_ref[...], w1_ref, w2_ref)


def _const_spec(shape):
    nd = len(shape)
    return pl.BlockSpec(shape, lambda b, _n=nd: (0,) * _n, pipeline_mode=pl.Buffered(1))


def _nbytes(shape, dtype):
    return int(np.prod(shape)) * jnp.dtype(dtype).itemsize


def _mixer(x, m3, m_off, m_step, pos, consts, ffn_consts, states, emit_state, guest=None):
    B, T, _ = x.shape
    has_pos = pos is not None
    has_state = states is not None
    scratch_shapes = [
        ((T, PROJ_COLS), F32),
        ((T + 16, LRU_WIDTH), F32),
        ((T, D_MODEL), BF16),
        ((8, HG_DK, HG_DK), F32),
        ((T, N_DIR * HG_HEADS * HG_DK), F32),
        ((LRU_WIDTH // 128, T, 128), F32),
        ((T // CHUNK, HG_HEADS, HG_DK, CHUNK), BF16),
        ((3 * N_DIR * (LRU_WIDTH // LRU_GROUP), LRU_SEGS, LRU_GROUP), F32),
        ((STAGE_SLOTS, LV_STATE + 1, HG_DK, CHUNK), BF16),
        ((LRU_WIDTH // LRU_GROUP, LRU_GROUP, 2 * N_DIR * LRU_GROUP), BF16),
    ]
    resident = (sum(_nbytes(s, d) for s, d in scratch_shapes)
                + sum(_nbytes(c.shape, c.dtype) for c in consts)
                + (sum(_nbytes(p.shape, p.dtype) for p in pos) if has_pos else 0))
    io_block = _nbytes((T, D_MODEL), F32)
    budget = VMEM_LIMIT - VMEM_HEADROOM
    lrs_shape = ((T, 2 * LRS_SEG_COL), F32)
    fuse_scan = resident + _nbytes(*lrs_shape) + 4 * io_block <= budget
    if fuse_scan:
        scratch_shapes.append(lrs_shape)
        resident += _nbytes(*lrs_shape)
    ffn_bytes = sum(_nbytes(c.shape, c.dtype) for c in ffn_consts) + io_block
    fuse_ffn = resident + ffn_bytes + 4 * io_block <= budget
    if fuse_ffn:
        scratch_shapes.insert(0, ((1, T, D_MODEL), F32))
        resident += ffn_bytes
    has_extra = fuse_ffn and guest is not None
    if has_extra:
        g_rows, g_off, g_per_m = guest
        g_tile = g_rows.shape[0] // B
        assert g_tile * B == g_rows.shape[0] and g_per_m % g_tile == 0 and g_tile % 8 == 0
        resident += 4 * _nbytes((g_tile, D_MODEL), F32)
    in_bufs = 2 if resident + 3 * io_block <= budget else 1
    out_bufs = 2 if resident + 4 * io_block <= budget else 1
    io_mode = pl.Buffered(out_bufs)
    in_specs = [
        pl.BlockSpec((1, T, D_MODEL), lambda b: (b, 0, 0), pipeline_mode=pl.Buffered(in_bufs)),
        pl.BlockSpec((1, 1, 6 * D_MODEL), lambda b: (m_off + m_step * b, 0, 0)),
    ]
    args = [x, m3]
    if has_pos:
        in_specs += [_const_spec(p.shape) for p in pos]
        args += list(pos)
    hbm_spec = pl.BlockSpec(memory_space=pl.ANY)
    late = [consts[-1]]
    in_specs += [_const_spec(c.shape) for c in consts[:-1]] + [hbm_spec]
    args += list(consts)
    if fuse_ffn:
        late += list(ffn_consts[-2:])
        in_specs += [_const_spec(c.shape) for c in ffn_consts[:-2]] + [hbm_spec, hbm_spec]
        args += list(ffn_consts)
    if has_state:
        hs0, ls0 = states
        in_specs += [
            pl.BlockSpec((1, 8, HG_DK, HG_DK), lambda b: (b, 0, 0, 0)),
            pl.BlockSpec((1, N_DIR, LRU_WIDTH), lambda b: (b, 0, 0)),
        ]
        args += [hs0, ls0]
    if has_extra:
        steps_per_m = g_per_m // g_tile
        in_specs += [pl.BlockSpec((g_tile, D_MODEL), lambda b: (b, 0)),
                     pl.BlockSpec((1, 1, 6 * D_MODEL), lambda b: (g_off + b // steps_per_m, 0, 0))]
        args += [g_rows, m3]
    out_shape = [jax.ShapeDtypeStruct((B, T, D_MODEL), F32)]
    out_specs = [pl.BlockSpec((1, T, D_MODEL), lambda b: (b, 0, 0), pipeline_mode=io_mode)]
    if emit_state:
        out_shape += [jax.ShapeDtypeStruct((B, 8, HG_DK, HG_DK), F32),
                      jax.ShapeDtypeStruct((B, N_DIR, LRU_WIDTH), F32)]
        out_specs += [pl.BlockSpec((1, 8, HG_DK, HG_DK), lambda b: (b, 0, 0, 0)),
                      pl.BlockSpec((1, N_DIR, LRU_WIDTH), lambda b: (b, 0, 0))]
    if has_extra:
        out_shape.append(jax.ShapeDtypeStruct(g_rows.shape, F32))
        out_specs.append(pl.BlockSpec((g_tile, D_MODEL), lambda b: (b, 0)))
    scratch = [pltpu.VMEM(s, d) for s, d in scratch_shapes]
    scratch += [pltpu.VMEM(w.shape, w.dtype) for w in late] + [pltpu.SemaphoreType.DMA((len(late),))]
    outs = pl.pallas_call(
        functools.partial(_mixer_kernel, T=T, has_pos=has_pos, has_state=has_state, emit_state=emit_state,
                          fuse_scan=fuse_scan, fuse_ffn=fuse_ffn, has_extra=has_extra),
        grid=(B,),
        in_specs=in_specs,
        out_specs=out_specs,
        out_shape=out_shape,
        scratch_shapes=scratch,
        compiler_params=pltpu.CompilerParams(
            dimension_semantics=("arbitrary",), vmem_limit_bytes=VMEM_LIMIT),
        name=f"mixer_t{T}",
    )(*args)
    return (outs[0], fuse_ffn) + tuple(outs[1:])


def _ffn_rows(xs, mrows, n2g, fgain, w1_ref, w2_ref):
    hbs, g2s = [], []
    for x, mrow in zip(xs, mrows):
        sh2 = mrow[:, 3 * D_MODEL:4 * D_MODEL]
        sc2 = mrow[:, 4 * D_MODEL:5 * D_MODEL]
        g2s.append(mrow[:, 5 * D_MODEL:6 * D_MODEL])
        ms = jnp.mean(x * x, axis=-1, keepdims=True)
        hbs.append((x * lax.rsqrt(ms + EPS) * (n2g * (1.0 + sc2)) + sh2).astype(BF16))
    hb = hbs[0] if len(hbs) == 1 else jnp.concatenate(hbs, axis=0)
    ff = jnp.zeros(hb.shape, F32)
    for c in range(D_FF // FF_CHUNK):
        a = jnp.dot(hb, w1_ref[:, c * FF_CHUNK:(c + 1) * FF_CHUNK], preferred_element_type=F32)
        a = jnp.maximum(a, 0.0)
        ff = ff + jnp.dot((a * a).astype(BF16), w2_ref[c * FF_CHUNK:(c + 1) * FF_CHUNK, :],
                          preferred_element_type=F32)
    outs, r = [], 0
    for x, g2 in zip(xs, g2s):
        x2 = x + g2 * ff[r:r + x.shape[0]]
        r += x.shape[0]
        ms2 = jnp.mean(x2 * x2, axis=-1, keepdims=True)
        outs.append(x2 * lax.rsqrt(ms2 + EPS) * fgain)
    return outs


def _ffn_kernel(x_ref, m_ref, n2g_ref, fg_ref, w1_ref, w2_ref, y_ref):
    y_ref[...] = _ffn_rows([x_ref[...]], [m_ref[0]], n2g_ref[...], fg_ref[...], w1_ref, w2_ref)[0]


def _ffn(x1, m3, m_off, tiles_per_cond, n2g, fgain, w1, w2):
    n = x1.shape[0]

    def m_index(i):
        if tiles_per_cond is None:
            return (m_off, 0, 0)
        return (m_off + i // tiles_per_cond, 0, 0)

    return pl.pallas_call(
        _ffn_kernel,
        grid=(n // FFN_TILE,),
        in_specs=[
            pl.BlockSpec((FFN_TILE, D_MODEL), lambda i: (i, 0)),
            pl.BlockSpec((1, 1, 6 * D_MODEL), m_index),
            _const_spec(n2g.shape),
            _const_spec(fgain.shape),
            _const_spec(w1.shape),
            _const_spec(w2.shape),
        ],
        out_specs=pl.BlockSpec((FFN_TILE, D_MODEL), lambda i: (i, 0)),
        out_shape=jax.ShapeDtypeStruct((n, D_MODEL), F32),
        compiler_params=pltpu.CompilerParams(
            dimension_semantics=("arbitrary",), vmem_limit_bytes=VMEM_LIMIT),
        name="ffn",
    )(x1, m3, n2g, fgain, w1, w2)


def _grid_pos_tables(n_tok):
    quarter = D_MODEL // 4
    omega = (1.0 / (np.float32(POS_BASE) ** (np.arange(quarter, dtype=np.float32) / np.float32(quarter)))
             ).astype(np.float32)

    def emb(n):
        ang = np.arange(n).reshape(-1, 1).astype(np.float32) * omega
        return np.concatenate([np.sin(ang), np.cos(ang)], axis=-1)

    rows = np.repeat(emb(n_tok // GRID_W)[:, None, :], 8, axis=1)
    return jnp.asarray(rows, dtype=F32), jnp.asarray(emb(GRID_W), dtype=F32)


def kernel(x_prompt, x_sample, c, state_hgrn, state_rglru, c_ctx, w_ada, b_ada, norm1_gain, norm2_gain,
           w_in, hg_lb_logits, hg_norm_gain, conv_w, conv_b, lru_wa, lru_ba, lru_wx, lru_bx, lru_lambda,
           w_out, w_ff1, w_ff2, final_gain):
    bp, tp, _ = x_prompt.shape
    bs_, ts, _ = x_sample.shape

    m3, (w_in_b, w_out_b, w1, w2) = _modulation(
        c_ctx, c, w_ada[0], b_ada, [w_in[0], w_out[0], w_ff1[0], w_ff2[0]])

    masks_np, tri_np = _level_tables()
    consts = [
        norm1_gain,
        w_in_b,
        hg_lb_logits.reshape(2, N_DIR * HG_HEADS, HG_DK),
        hg_norm_gain[0].reshape(1, HG_WIDTH),
        jnp.asarray(masks_np),
        jnp.asarray(tri_np, dtype=BF16),
        conv_w[0],
        conv_b,
        lru_wa[0].reshape(N_DIR * LRU_BLOCKS, LRU_BLOCK, LRU_BLOCK),
        lru_wx[0].reshape(N_DIR * LRU_BLOCKS, LRU_BLOCK, LRU_BLOCK),
        lru_ba[0],
        lru_bx[0],
        lru_lambda[0],
        w_out_b,
    ]
    fgain = final_gain.reshape(1, D_MODEL)
    ffn_consts = [norm2_gain, fgain, w1, w2]

    y_sample, done_s = _mixer(x_sample, m3, 1, 1, _grid_pos_tables(ts), consts, ffn_consts,
                              (state_hgrn.reshape(bs_, N_DIR * HG_HEADS, HG_DK, HG_DK),
                               state_rglru.reshape(bs_, N_DIR, LRU_WIDTH)), False)
    guest = None if done_s else (y_sample.reshape(bs_ * ts, D_MODEL), 1, ts)

    y_prompt, done, hs, ls, *rest = _mixer(x_prompt, m3, 0, 0, None, consts, ffn_consts, None, True, guest)
    if not done:
        y_prompt = _ffn(y_prompt.reshape(bp * tp, D_MODEL), m3, 0, None, *ffn_consts)
    if rest:
        y_sample = rest[0]
    elif not done_s:
        y_sample = _ffn(y_sample.reshape(bs_ * ts, D_MODEL), m3, 1, ts // FFN_TILE, *ffn_consts)

    return (y_prompt.reshape(bp, tp, D_MODEL),
            y_sample.reshape(bs_, ts, D_MODEL),
            hs.reshape(bp, 1, N_DIR, HG_HEADS, HG_DK, HG_DK),
            ls.reshape(bp, 1, N_DIR, LRU_WIDTH))
```

```python
import functools

import numpy as np
import jax
import jax.numpy as jnp
from jax import lax
from jax.experimental import pallas as pl
from jax.experimental.pallas import tpu as pltpu

F32 = jnp.float32
BF16 = jnp.bfloat16

D_MODEL = 1024
N_DIR = 2
HG_HEADS = 4
HG_DK = 128
HG_WIDTH = 512
LRU_WIDTH = 512
LRU_BLOCKS = 8
LRU_BLOCK = 64
LRU_C = 8.0
D_FF = 4096
IN_COLS = 4096
EPS = 1e-6
LOG2E = 1.4426950408889634
GRID_W = 64
POS_BASE = 10000.0

COL_Q = 0
COL_F = 1024
COL_V = 2048
COL_GATE = 2560
COL_LX = 3072
COL_LG = 3584
PROJ_LG = COL_LX
PROJ_COLS = IN_COLS - LRU_WIDTH

CHUNK = 128
LEVELS = (64, 32, 16, 8, 4)
LV_4 = LEVELS.index(4)
LV_21 = len(LEVELS)
LV_DIAG = LV_21 + 1
LV_STATE = LV_DIAG + 1
N_MASKS = LV_DIAG + 1
ROW_TILE = 256
LRU_SEGS = 8
CONV_TILE = 32
SCAN_UNROLL = 8
FIX_TILE = 64
STAGE_SLOTS = 1
MAX_INLINE_TRIPS = 2
LOOP_UNROLL = 2
LRU_GROUP = 256
GATE_D = 2 * LRU_GROUP
GATE_G = N_DIR * GATE_D
LRS_SEG_COL = 2 * N_DIR * LRU_WIDTH
PROJ_STEP = 512
FFN_TILE = 512
FF_CHUNK = 1024
MOD_STEPS = 8
MOD_ROWS = 8
VMEM_LIMIT = 58 * 1024 * 1024
VMEM_HEADROOM = 6 * 1024 * 1024


def _sig(x):
    return 0.5 * jnp.tanh(0.5 * x) + 0.5


def _silu(x):
    h = 0.5 * x
    return h * jnp.tanh(h) + h


def _unroll(trips):
    return True if trips <= MAX_INLINE_TRIPS else LOOP_UNROLL


def _nt_dot(a, b):
    return lax.dot_general(a, b, (((1,), (1,)), ((), ())), preferred_element_type=F32)


def _staged_transpose(xb, slot_ref):
    slot_ref[...] = xb.T
    return slot_ref[...]


def _gram(x, slot_ref):
    xb = x.astype(BF16)
    return jnp.dot(xb, _staged_transpose(xb, slot_ref), preferred_element_type=F32)


def _level_tables():
    t = np.arange(CHUNK)[:, None]
    s = np.arange(CHUNK)[None, :]
    masks = np.zeros((N_DIR, N_MASKS, CHUNK, CHUNK), np.float32)
    for li, h in enumerate(LEVELS):
        same = (t // (2 * h)) == (s // (2 * h))
        t_hi = (t // h) % 2 == 1
        s_hi = (s // h) % 2 == 1
        masks[0, li] = same & t_hi & ~s_hi
        masks[1, li] = same & ~t_hi & s_hi
    same4 = (t // 4) == (s // 4)
    masks[0, LV_21] = same4 & (s < t)
    masks[1, LV_21] = same4 & (s > t)
    masks[:, LV_DIAG] = (t == s)
    tri = np.stack([(s <= t), (s >= t)]).astype(np.float32)
    return masks, tri


def _mod_kernel(cctx_ref, c_ref, w_ref, b_ref, *rest):
    n_w = (len(rest) - 2) // 2
    f32_refs, o_ref, bf16_refs, cond = rest[:n_w], rest[n_w], rest[n_w + 1:2 * n_w + 1], rest[-1]
    n = c_ref.shape[0]
    cond[...] = jnp.zeros(cond.shape, F32)
    cond[0:1, :] = cctx_ref[...]
    cond[1:1 + n, :] = c_ref[...]
    c = cond[...]
    a = _silu(c).astype(BF16)
    m = jnp.dot(a, w_ref[...].astype(BF16), preferred_element_type=F32) + b_ref[...]
    for r in range(MOD_ROWS):
        o_ref[r] = m[r:r + 1]
    for src, dst in zip(f32_refs, bf16_refs):
        dst[...] = src[...].astype(BF16)


def _modulation(c_ctx, c, w_ada, b_ada, weights):
    n = w_ada.shape[1]
    assert 1 + c.shape[0] <= MOD_ROWS
    tile = n // MOD_STEPS
    slab = lambda w: pl.BlockSpec((w.shape[0] // MOD_STEPS, w.shape[1]), lambda j: (j, 0))
    outs = pl.pallas_call(
        _mod_kernel,
        grid=(MOD_STEPS,),
        in_specs=[
            pl.BlockSpec((1, D_MODEL), lambda j: (0, 0)),
            pl.BlockSpec(c.shape, lambda j: (0, 0)),
            pl.BlockSpec((D_MODEL, tile), lambda j: (0, j)),
            pl.BlockSpec((1, tile), lambda j: (0, j)),
        ] + [slab(w) for w in weights],
        out_specs=[pl.BlockSpec((MOD_ROWS, 1, tile), lambda j: (0, 0, j))] + [slab(w) for w in weights],
        out_shape=[jax.ShapeDtypeStruct((MOD_ROWS, 1, n), F32)] + [
            jax.ShapeDtypeStruct(w.shape, BF16) for w in weights],
        scratch_shapes=[pltpu.VMEM((MOD_ROWS, D_MODEL), F32)],
        compiler_params=pltpu.CompilerParams(
            dimension_semantics=("arbitrary",), vmem_limit_bytes=VMEM_LIMIT),
        name="adaln_modulation",
    )(c_ctx.reshape(1, D_MODEL), c, w_ada, b_ada, *weights)
    return outs[0], outs[1:]


def _mixer_kernel(*refs, T, has_pos, has_state, emit_state, fuse_scan, fuse_ffn, has_extra):
    it = iter(refs)
    x_ref = next(it)
    m_ref = next(it)
    if has_pos:
        posr_ref = next(it)
        posc_ref = next(it)
    n1g_ref = next(it)
    win_ref = next(it)
    lbl_ref = next(it)
    hgg_ref = next(it)
    masks_ref = next(it)
    tri_ref = next(it)
    convw_ref = next(it)
    convb_ref = next(it)
    wa_ref = next(it)
    wx_ref = next(it)
    ba_ref = next(it)
    bx_ref = next(it)
    lam_ref = next(it)
    late_hbm = [next(it)]
    if fuse_ffn:
        n2g_ref = next(it)
        fg_ref = next(it)
        late_hbm += [next(it), next(it)]
    if has_state:
        hs0_ref = next(it)
        ls0_ref = next(it)
    if has_extra:
        xe_ref = next(it)
        me_ref = next(it)
    y_ref = next(it)
    if emit_state:
        hs_ref = next(it)
        ls_ref = next(it)
    if has_extra:
        ye_ref = next(it)
    x1_ref = next(it) if fuse_ffn else y_ref
    proj = next(it)
    lxp = next(it)
    mixin = next(it)
    st = next(it)
    ksc = next(it)
    xci = next(it)
    vts = next(it)
    rowc = next(it)
    tsl = next(it)
    wg = next(it)
    if fuse_scan:
        lrs = next(it)
    late_vmem = [next(it) for _ in late_hbm]
    late_sem = next(it)
    wout_ref = late_vmem[0]
    if fuse_ffn:
        w1_ref, w2_ref = late_vmem[1:]

    L = CHUNK

    def late_copy(k):
        return pltpu.make_async_copy(late_hbm[k], late_vmem[k], late_sem.at[k])

    @pl.when(pl.program_id(0) == 0)
    def _first_step():
        for k in range(len(late_hbm)):
            late_copy(k).start(priority=1)
        per_group = LRU_GROUP // LRU_BLOCK
        wg[...] = jnp.zeros(wg.shape, BF16)
        for g in range(LRU_WIDTH // LRU_GROUP):
            for p in range(2 * N_DIR):
                src = wa_ref if p % 2 == 0 else wx_ref
                for n in range(per_group):
                    r = n * LRU_BLOCK
                    col = p * LRU_GROUP + r
                    wg[g, r:r + LRU_BLOCK, col:col + LRU_BLOCK] = src[
                        (p // 2) * LRU_BLOCKS + g * per_group + n].astype(BF16)

    mrow = m_ref[0]
    sh1 = mrow[:, 0:D_MODEL]
    sc1 = mrow[:, D_MODEL:2 * D_MODEL]
    g1 = mrow[:, 2 * D_MODEL:3 * D_MODEL]
    gain1 = n1g_ref[...] * (1.0 + sc1)

    zrows = jnp.zeros((8, LRU_WIDTH), F32)
    lxp[0:8, :] = zrows
    lxp[T + 8:T + 16, :] = zrows

    def load_x(i):
        xt = x_ref[0, pl.ds(pl.multiple_of(i * ROW_TILE, ROW_TILE), ROW_TILE), :]
        if has_pos:
            per_tile = ROW_TILE // GRID_W
            tiles = []
            for s in range(per_tile):
                row_emb = jnp.concatenate([posr_ref[i * per_tile + s]] * (GRID_W // 8), axis=0)
                tiles.append(jnp.concatenate([row_emb, posc_ref[...]], axis=1))
            xt = xt + jnp.concatenate(tiles, axis=0)
        return xt

    def proj_body(i, carry):
        r0 = pl.multiple_of(i * ROW_TILE, ROW_TILE)
        xt = load_x(i)
        ms = jnp.mean(xt * xt, axis=-1, keepdims=True)
        hb = (xt * lax.rsqrt(ms + EPS) * gain1 + sh1).astype(BF16)
        starts = range(0, IN_COLS, PROJ_STEP)
        order = [COL_LX] + [c0 for c0 in starts if c0 < COL_GATE] + [
            c0 for c0 in starts if c0 >= COL_GATE and c0 != COL_LX]
        for c0 in order:
            res = jnp.dot(hb, win_ref[:, c0:c0 + PROJ_STEP], preferred_element_type=F32)
            if c0 == COL_LX:
                lxp[pl.ds(pl.multiple_of(r0 + 8, 8), ROW_TILE), :] = res
            elif c0 == COL_LG:
                proj[pl.ds(r0, ROW_TILE), PROJ_LG:PROJ_LG + LRU_WIDTH] = res
            else:
                proj[pl.ds(r0, ROW_TILE), c0:c0 + PROJ_STEP] = res
        return carry

    l0 = lbl_ref[0]
    l1 = lbl_ref[1]
    lmx = jnp.maximum(l0, l1)
    e0 = jnp.exp(l0 - lmx)
    e1 = jnp.exp(l1 - lmx)
    lb_all = e0 / (e0 + e1)

    def hg_prep(c, carry):
        rows = pl.ds(pl.multiple_of(c * L, L), L)
        for hd in range(HG_HEADS):
            vts[c, hd] = proj[rows, COL_V + hd * HG_DK:COL_V + (hd + 1) * HG_DK].T.astype(BF16)
        for d in range(N_DIR):
            for hd in range(HG_HEADS):
                idx = d * HG_HEADS + hd
                cq = COL_Q + idx * HG_DK
                cf = COL_F + idx * HG_DK
                hq = proj[rows, cq:cq + HG_DK]
                fz = proj[rows, cf:cf + HG_DK]
                proj[rows, cq:cq + HG_DK] = _silu(hq)
                sg = _sig(fz)
                lb = lb_all[idx:idx + 1, :]
                oml = 1.0 - lb
                ksc[rows, idx * HG_DK:(idx + 1) * HG_DK] = oml * (1.0 - sg)
                logf = jnp.log(lb + oml * sg)
                p1 = logf.astype(BF16)
                p2 = (logf - p1.astype(F32)).astype(BF16)
                bb = jnp.dot(tri_ref[d], jnp.concatenate([p1, p2], axis=1), preferred_element_type=F32)
                proj[rows, cf:cf + HG_DK] = (bb[:, 0:HG_DK] + bb[:, HG_DK:2 * HG_DK]) * LOG2E
        return carry

    lax.fori_loop(0, T // ROW_TILE, proj_body, 0)

    S = T // LRU_SEGS
    cw = convw_ref[...]
    cb = convb_ref[...]

    def irows(i0, j, n):
        return pl.ds(pl.multiple_of(LRU_SEGS * i0, 8) + j, n, stride=LRU_SEGS)

    for j in range(LRU_SEGS):
        def conv_body(ti, carry, j=j):
            i0 = pl.multiple_of(ti * CONV_TILE, CONV_TILE)
            win = lxp[pl.ds(pl.multiple_of(j * S + i0, 8), CONV_TILE + 16), :]
            xc = cb
            for tap in range(4):
                xc = xc + win[6 + tap:6 + tap + CONV_TILE] * cw[tap:tap + 1]
            for l in range(LRU_WIDTH // 128):
                xci[l, irows(i0, j, CONV_TILE), :] = xc[:, l * 128:(l + 1) * 128]
            return carry

        lax.fori_loop(0, S // CONV_TILE, conv_body, 0, unroll=_unroll(S // CONV_TILE))

    n_grp = LRU_WIDTH // LRU_GROUP
    slabs = LRU_GROUP // 128
    gates = lrs if fuse_scan else proj

    def xc_tile(rows, g):
        return jnp.concatenate([xci[g * slabs + l, rows, :] for l in range(slabs)], axis=1)

    def gate_body(i, carry):
        rows = pl.ds(pl.multiple_of(i * ROW_TILE, ROW_TILE), ROW_TILE)
        for g in range(n_grp):
            gates[rows, g * GATE_G:(g + 1) * GATE_G] = jnp.dot(
                xc_tile(rows, g).astype(BF16), wg[g], preferred_element_type=F32)
        return carry

    lam = lam_ref[...]
    nl = -lam
    c8 = -LRU_C * (jnp.maximum(nl, 0.0) + jnp.log1p(jnp.exp(-jnp.abs(nl))))
    rowi2 = lax.broadcasted_iota(jnp.int32, (8, LRU_GROUP), 0)

    for d in range(N_DIR):
        for g in range(n_grp):
            k3 = 3 * (d * n_grp + g)
            chans = slice(g * LRU_GROUP, (g + 1) * LRU_GROUP)
            for r, row in enumerate((ba_ref[d:d + 1, chans], bx_ref[d:d + 1, chans], c8[d:d + 1, chans])):
                rowc[k3 + r] = jnp.broadcast_to(row, (LRU_SEGS, LRU_GROUP))

    def seg_out(d, rows, g, part):
        if fuse_scan:
            col = LRS_SEG_COL + ((d * n_grp + g) * 2 + part) * LRU_GROUP
            return lrs.at[rows, col:col + LRU_GROUP]
        cols = slice((g * 2 + part) * LRU_GROUP, (g * 2 + part + 1) * LRU_GROUP)
        return (ksc.at[rows, cols] if d == 0 else x1_ref.at[0, rows, cols])

    def lru_inputs(rows8, g, d):
        base = g * GATE_G + d * GATE_D
        k3 = 3 * (d * n_grp + g)
        ga = gates[rows8, base:base + LRU_GROUP] + rowc[k3]
        gx = gates[rows8, base + LRU_GROUP:base + 2 * LRU_GROUP] + rowc[k3 + 1]
        xc8 = xc_tile(rows8, g)
        log_a = rowc[k3 + 2] * _sig(ga)
        a = jnp.exp(log_a)
        z = jnp.tanh(-log_a) * (1.0 + a * a)
        mult = jnp.where(z > 0.0, z * lax.rsqrt(z), 0.0)
        return a, mult * (_sig(gx) * xc8)

    def scan8(a, u, d):
        for sft in (1, 2, 4):
            if d == 0:
                keep = rowi2 >= sft
                amt = sft
            else:
                keep = rowi2 < 8 - sft
                amt = 8 - sft
            ash = jnp.where(keep, pltpu.roll(a, amt, 0), 1.0)
            ush = jnp.where(keep, pltpu.roll(u, amt, 0), 0.0)
            u = a * ush + u
            a = a * ash
        return a, u

    def scan_steps(first, count, carry):
        hs, ds = [list(c) for c in carry[:2]], [list(c) for c in carry[2:]]
        for u in range(count):
            i_f = first + u
            for d, i in ((0, i_f), (1, S - 1 - i_f)):
                rows8 = pl.ds(pl.multiple_of(LRU_SEGS * i, 8), 8)
                for g in range(n_grp):
                    a, uu = lru_inputs(rows8, g, d)
                    hs[d][g] = a * hs[d][g] + uu
                    ds[d][g] = a * ds[d][g]
                    seg_out(d, rows8, g, 0)[...] = hs[d][g]
                    seg_out(d, rows8, g, 1)[...] = ds[d][g]
        return tuple(tuple(c) for c in hs + ds)

    zero8 = jnp.zeros((LRU_SEGS, LRU_GROUP), F32)
    one8 = jnp.ones((LRU_SEGS, LRU_GROUP), F32)
    scan_init = ((zero8,) * n_grp, (zero8,) * n_grp, (one8,) * n_grp, (one8,) * n_grp)
    if fuse_scan:
        lax.fori_loop(0, T // ROW_TILE, gate_body, 0)

    rowi = lax.broadcasted_iota(jnp.int32, (8, HG_DK), 0)
    r4 = rowi & 3
    is_r0 = r4 == 0
    is_r1 = r4 == 1
    is_r2 = r4 == 2
    hi4 = rowi >= 4

    for i in range(N_DIR * HG_HEADS):
        if has_state:
            st[i] = hs0_ref[0, i].T
        else:
            st[i] = jnp.zeros((HG_DK, HG_DK), F32)

    def hg_chunk(c, d):
        r0 = pl.multiple_of(c * L, L)
        rows = pl.ds(r0, L)
        for hd in range(HG_HEADS):
            idx = d * HG_HEADS + hd
            cq = COL_Q + idx * HG_DK
            cf = COL_F + idx * HG_DK
            ck = idx * HG_DK
            slot = idx % STAGE_SLOTS

            def ldq(lo, n, cq=cq):
                return proj[pl.ds(pl.multiple_of(r0 + lo, 8), n), cq:cq + HG_DK]

            def ldb(lo, n, cf=cf):
                return proj[pl.ds(pl.multiple_of(r0 + lo, 8), n), cf:cf + HG_DK]

            def ldk(lo, n, ck=ck):
                return ksc[pl.ds(pl.multiple_of(r0 + lo, 8), n), ck:ck + HG_DK]

            def bline(r, cf=cf):
                grp = proj[pl.ds(pl.multiple_of(r0 + 8 * (r // 8), 8), 8), cf:cf + HG_DK]
                return grp[r % 8:r % 8 + 1, :]

            def brow(r, n):
                return jnp.broadcast_to(bline(r), (n, HG_DK))

            acc_rows = [None] * (L // 8)

            def accumulate(li, p, row0):
                for i in range(p.shape[0] // 8):
                    g = row0 // 8 + i
                    term = masks_ref[d, li, 8 * g:8 * g + 8, :] * p[8 * i:8 * i + 8]
                    acc_rows[g] = term if acc_rows[g] is None else acc_rows[g] + term

            accumulate(LV_DIAG, jnp.dot(
                ldq(0, L).astype(BF16), _staged_transpose(ldk(0, L).astype(BF16), tsl.at[slot, LV_DIAG]),
                preferred_element_type=F32), 0)
            for li, h in enumerate(LEVELS[:LV_4]):
                pieces, q_pieces, q_starts = [], [], []
                for j in range(L // (2 * h)):
                    lo = j * 2 * h
                    mid = lo + h
                    if d == 0:
                        bm = brow(mid - 1, h)
                        kp = ldk(lo, h) * jnp.exp2(bm - ldb(lo, h))
                        qp = ldq(mid, h) * jnp.exp2(ldb(mid, h) - bm)
                        pieces += [kp, qp]
                        q_starts.append(mid)
                    else:
                        bm = brow(mid, h)
                        qp = ldq(lo, h) * jnp.exp2(ldb(lo, h) - bm)
                        kp = ldk(mid, h) * jnp.exp2(bm - ldb(mid, h))
                        pieces += [qp, kp]
                        q_starts.append(lo)
                    q_pieces.append(qp)
                xt = _staged_transpose(jnp.concatenate(pieces, axis=0).astype(BF16), tsl.at[slot,li])
                p = jnp.dot(jnp.concatenate(q_pieces, axis=0).astype(BF16), xt, preferred_element_type=F32)
                for j, row0 in enumerate(q_starts):
                    accumulate(li, p[j * h:(j + 1) * h], row0)
            x4, xq21, xk21 = [], [], []
            for g in range(L // 8):
                qg, kg, bg = ldq(8 * g, 8), ldk(8 * g, 8), ldb(8 * g, 8)
                fg = 1.0 - kg
                qfg = qg * fg
                mid = 8 * g + (3 if d == 0 else 4)
                e4 = jnp.exp2(-jnp.abs(bg - brow(mid, 8)))
                fnx = pltpu.roll(fg, 7, 0)
                fpv = pltpu.roll(fg, 1, 0)
                k_over_f = kg / fg
                if d == 0:
                    x4.append(jnp.where(hi4, qg, kg) * e4)
                    xq21.append(jnp.where(is_r0, 0.0, jnp.where(is_r1, qg, jnp.where(is_r2, qfg, qfg * fpv))))
                    xk21.append(jnp.where(is_r0, kg * fnx, jnp.where(is_r1, kg, jnp.where(is_r2, k_over_f, 0.0))))
                else:
                    x4.append(jnp.where(hi4, kg, qg) * e4)
                    xq21.append(jnp.where(is_r0, qfg * fnx, jnp.where(is_r1, qfg, jnp.where(is_r2, qg, 0.0))))
                    xk21.append(jnp.where(is_r0, 0.0, jnp.where(is_r1, k_over_f, jnp.where(is_r2, kg, kg * fpv))))
            accumulate(LV_4, _gram(jnp.concatenate(x4, axis=0), tsl.at[slot, LV_4]), 0)
            accumulate(LV_21, jnp.dot(
                jnp.concatenate(xq21, axis=0).astype(BF16),
                _staged_transpose(jnp.concatenate(xk21, axis=0).astype(BF16), tsl.at[slot, LV_21]),
                preferred_element_type=F32), 0)
            acc = jnp.concatenate(acc_rows, axis=0)

            vt = vts[c, hd]
            st_t = st[idx]
            b = ldb(0, L)
            qt = (ldq(0, L) * jnp.exp2(b)).astype(BF16)
            vb = proj[rows, COL_V + hd * HG_DK:COL_V + (hd + 1) * HG_DK].astype(BF16)
            o = jnp.dot(jnp.concatenate([acc.astype(BF16), qt], axis=1),
                        jnp.concatenate([vb, _staged_transpose(st_t.astype(BF16), tsl.at[slot, LV_STATE])], axis=0),
                        preferred_element_type=F32)
            btot = bline(L - 1 if d == 0 else 0)
            kt = (ldk(0, L) * jnp.exp2(btot - b)).astype(BF16)
            st[idx] = st_t * jnp.exp2(btot) + jnp.dot(vt, kt, preferred_element_type=F32)
            x1_ref[0, rows, d * HG_WIDTH + hd * HG_DK:d * HG_WIDTH + (hd + 1) * HG_DK] = o

    n_chunks = T // L

    def hg_both(c, carry):
        hg_chunk(c, 0)
        hg_chunk(n_chunks - 1 - c, 1)
        if fuse_scan:
            steps = S // n_chunks
            carry = scan_steps(c * steps, steps, carry)
        return carry

    lax.fori_loop(0, n_chunks, hg_prep, 0, unroll=_unroll(n_chunks))
    scan_state = lax.fori_loop(0, n_chunks, hg_both, scan_init if fuse_scan else 0)

    @pl.when(pl.program_id(0) == 0)
    def _late_weights_ready():
        for k in range(len(late_hbm)):
            late_copy(k).wait()

    if emit_state:
        for i in range(N_DIR * HG_HEADS):
            hs_ref[0, i] = st[i].T

    def hg_fin(i, carry):
        rows = pl.ds(pl.multiple_of(i * L, L), L)
        for hd in range(HG_HEADS):
            cs = slice(hd * HG_DK, (hd + 1) * HG_DK)
            o = x1_ref[0, rows, cs] + x1_ref[0, rows, HG_WIDTH + hd * HG_DK:HG_WIDTH + (hd + 1) * HG_DK]
            ms = jnp.mean(o * o, axis=-1, keepdims=True)
            y = o * lax.rsqrt(ms + EPS) * hgg_ref[:, cs]
            gz = proj[rows, COL_GATE + hd * HG_DK:COL_GATE + (hd + 1) * HG_DK]
            mixin[rows, cs] = (y * _silu(gz)).astype(BF16)
        return carry

    lax.fori_loop(0, n_chunks, hg_fin, 0, unroll=_unroll(n_chunks))

    if not fuse_scan:
        lax.fori_loop(0, T // ROW_TILE, gate_body, 0)
        scan_state = lax.fori_loop(
            0, S // SCAN_UNROLL, lambda n, carry: scan_steps(n * SCAN_UNROLL, SCAN_UNROLL, carry), scan_init)
    h_end, d_end = scan_state[:2], scan_state[2:]

    carry_in = [[None] * n_grp for _ in range(N_DIR)]
    for d in range(N_DIR):
        for g in range(n_grp):
            if has_state:
                h0 = ls0_ref[0, d:d + 1, g * LRU_GROUP:(g + 1) * LRU_GROUP]
            else:
                h0 = jnp.zeros((1, LRU_GROUP), F32)
            dd, hh = scan8(d_end[d][g], h_end[d][g], d)
            seg_end = hh + dd * h0
            if d == 0:
                carry_in[d][g] = jnp.where(rowi2 >= 1, pltpu.roll(seg_end, 1, 0), h0)
                last = seg_end[LRU_SEGS - 1:LRU_SEGS, :]
            else:
                carry_in[d][g] = jnp.where(rowi2 < LRU_SEGS - 1, pltpu.roll(seg_end, LRU_SEGS - 1, 0), h0)
                last = seg_end[0:1, :]
            if emit_state:
                ls_ref[0, d:d + 1, g * LRU_GROUP:(g + 1) * LRU_GROUP] = last

    def fix_body(i, carry):
        rows = pl.ds(pl.multiple_of(i * FIX_TILE, FIX_TILE), FIX_TILE)
        for g in range(n_grp):
            tot = None
            for d in range(N_DIR):
                cin = jnp.concatenate([carry_in[d][g]] * (FIX_TILE // LRU_SEGS), axis=0)
                h = seg_out(d, rows, g, 0)[...] + seg_out(d, rows, g, 1)[...] * cin
                tot = h if tot is None else tot + h
            for l in range(slabs):
                xci[g * slabs + l, rows, :] = tot[:, l * 128:(l + 1) * 128]
        return carry

    lax.fori_loop(0, T // FIX_TILE, fix_body, 0)

    for j in range(LRU_SEGS):
        def lru_fin(ti, carry, j=j):
            i0 = pl.multiple_of(ti * CONV_TILE, CONV_TILE)
            rows = pl.ds(pl.multiple_of(j * S + i0, CONV_TILE), CONV_TILE)
            hsum = jnp.concatenate([xci[l, irows(i0, j, CONV_TILE), :] for l in range(LRU_WIDTH // 128)], axis=1)
            lg = proj[rows, PROJ_LG:PROJ_LG + LRU_WIDTH]
            gl = lg * (0.5 * (1.0 + jnp.tanh(0.7978845608028654 * (lg + 0.044715 * (lg * lg * lg)))))
            mixin[rows, HG_WIDTH:HG_WIDTH + LRU_WIDTH] = (hsum * gl).astype(BF16)
            return carry

        lax.fori_loop(0, S // CONV_TILE, lru_fin, 0, unroll=_unroll(S // CONV_TILE))

    def out_body(i, carry):
        rows = pl.ds(pl.multiple_of(i * ROW_TILE, ROW_TILE), ROW_TILE)
        mix = jnp.dot(mixin[rows, :], wout_ref[...], preferred_element_type=F32)
        x1_ref[0, rows, :] = load_x(i) + g1 * mix
        return carry

    lax.fori_loop(0, T // ROW_TILE, out_body, 0)

    if fuse_ffn:
        def ffn_body(i, carry):
            rows = pl.ds(pl.multiple_of(i * ROW_TILE, ROW_TILE), ROW_TILE)
            y_ref[0, rows, :] = _ffn_rows([x1_ref[0, rows, :]], [mrow], n2g_ref[...], fg_ref[...], w1_ref, w2_ref)[0]
            return carry

        n_tiles = T // ROW_TILE
        lax.fori_loop(0, n_tiles - (1 if has_extra else 0), ffn_body, 0)
        if has_extra:
            rows = pl.ds((n_tiles - 1) * ROW_TILE, ROW_TILE)
            y_ref[0, rows, :], ye_ref[...] = _ffn_rows(
                [x1_ref[0, rows, :], xe_ref[...]], [mrow, me_ref[0]], n2g_ref[...], fg_ref[...], w1_ref, w2_ref)


def _const_spec(shape):
    nd = len(shape)
    return pl.BlockSpec(shape, lambda b, _n=nd: (0,) * _n, pipeline_mode=pl.Buffered(1))


def _nbytes(shape, dtype):
    return int(np.prod(shape)) * jnp.dtype(dtype).itemsize


def _mixer(x, m3, m_off, m_step, pos, consts, ffn_consts, states, emit_state, guest=None):
    B, T, _ = x.shape
    has_pos = pos is not None
    has_state = states is not None
    scratch_shapes = [
        ((T, PROJ_COLS), F32),
        ((T + 16, LRU_WIDTH), F32),
        ((T, D_MODEL), BF16),
        ((8, HG_DK, HG_DK), F32),
        ((T, N_DIR * HG_HEADS * HG_DK), F32),
        ((LRU_WIDTH // 128, T, 128), F32),
        ((T // CHUNK, HG_HEADS, HG_DK, CHUNK), BF16),
        ((3 * N_DIR * (LRU_WIDTH // LRU_GROUP), LRU_SEGS, LRU_GROUP), F32),
        ((STAGE_SLOTS, LV_STATE + 1, HG_DK, CHUNK), BF16),
        ((LRU_WIDTH // LRU_GROUP, LRU_GROUP, 2 * N_DIR * LRU_GROUP), BF16),
    ]
    resident = (sum(_nbytes(s, d) for s, d in scratch_shapes)
                + sum(_nbytes(c.shape, c.dtype) for c in consts)
                + (sum(_nbytes(p.shape, p.dtype) for p in pos) if has_pos else 0))
    io_block = _nbytes((T, D_MODEL), F32)
    budget = VMEM_LIMIT - VMEM_HEADROOM
    lrs_shape = ((T, 2 * LRS_SEG_COL), F32)
    fuse_scan = resident + _nbytes(*lrs_shape) + 4 * io_block <= budget
    if fuse_scan:
        scratch_shapes.append(lrs_shape)
        resident += _nbytes(*lrs_shape)
    ffn_bytes = sum(_nbytes(c.shape, c.dtype) for c in ffn_consts) + io_block
    fuse_ffn = resident + ffn_bytes + 4 * io_block <= budget
    if fuse_ffn:
        scratch_shapes.insert(0, ((1, T, D_MODEL), F32))
        resident += ffn_bytes
    has_extra = fuse_ffn and guest is not None
    if has_extra:
        g_rows, g_off, g_per_m = guest
        g_tile = g_rows.shape[0] // B
        assert g_tile * B == g_rows.shape[0] and g_per_m % g_tile == 0 and g_tile % 8 == 0
        resident += 4 * _nbytes((g_tile, D_MODEL), F32)
    in_bufs = 2 if resident + 3 * io_block <= budget else 1
    out_bufs = 2 if resident + 4 * io_block <= budget else 1
    io_mode = pl.Buffered(out_bufs)
    in_specs = [
        pl.BlockSpec((1, T, D_MODEL), lambda b: (b, 0, 0), pipeline_mode=pl.Buffered(in_bufs)),
        pl.BlockSpec((1, 1, 6 * D_MODEL), lambda b: (m_off + m_step * b, 0, 0)),
    ]
    args = [x, m3]
    if has_pos:
        in_specs += [_const_spec(p.shape) for p in pos]
        args += list(pos)
    hbm_spec = pl.BlockSpec(memory_space=pl.ANY)
    late = [consts[-1]]
    in_specs += [_const_spec(c.shape) for c in consts[:-1]] + [hbm_spec]
    args += list(consts)
    if fuse_ffn:
        late += list(ffn_consts[-2:])
        in_specs += [_const_spec(c.shape) for c in ffn_consts[:-2]] + [hbm_spec, hbm_spec]
        args += list(ffn_consts)
    if has_state:
        hs0, ls0 = states
        in_specs += [
            pl.BlockSpec((1, 8, HG_DK, HG_DK), lambda b: (b, 0, 0, 0)),
            pl.BlockSpec((1, N_DIR, LRU_WIDTH), lambda b: (b, 0, 0)),
        ]
        args += [hs0, ls0]
    if has_extra:
        steps_per_m = g_per_m // g_tile
        in_specs += [pl.BlockSpec((g_tile, D_MODEL), lambda b: (b, 0)),
                     pl.BlockSpec((1, 1, 6 * D_MODEL), lambda b: (g_off + b // steps_per_m, 0, 0))]
        args += [g_rows, m3]
    out_shape = [jax.ShapeDtypeStruct((B, T, D_MODEL), F32)]
    out_specs = [pl.BlockSpec((1, T, D_MODEL), lambda b: (b, 0, 0), pipeline_mode=io_mode)]
    if emit_state:
        out_shape += [jax.ShapeDtypeStruct((B, 8, HG_DK, HG_DK), F32),
                      jax.ShapeDtypeStruct((B, N_DIR, LRU_WIDTH), F32)]
        out_specs += [pl.BlockSpec((1, 8, HG_DK, HG_DK), lambda b: (b, 0, 0, 0)),
                      pl.BlockSpec((1, N_DIR, LRU_WIDTH), lambda b: (b, 0, 0))]
    if has_extra:
        out_shape.append(jax.ShapeDtypeStruct(g_rows.shape, F32))
        out_specs.append(pl.BlockSpec((g_tile, D_MODEL), lambda b: (b, 0)))
    scratch = [pltpu.VMEM(s, d) for s, d in scratch_shapes]
    scratch += [pltpu.VMEM(w.shape, w.dtype) for w in late] + [pltpu.SemaphoreType.DMA((len(late),))]
    outs = pl.pallas_call(
        functools.partial(_mixer_kernel, T=T, has_pos=has_pos, has_state=has_state, emit_state=emit_state,
                          fuse_scan=fuse_scan, fuse_ffn=fuse_ffn, has_extra=has_extra),
        grid=(B,),
        in_specs=in_specs,
        out_specs=out_specs,
        out_shape=out_shape,
        scratch_shapes=scratch,
        compiler_params=pltpu.CompilerParams(
            dimension_semantics=("arbitrary",), vmem_limit_bytes=VMEM_LIMIT),
        name=f"mixer_t{T}",
    )(*args)
    return (outs[0], fuse_ffn) + tuple(outs[1:])


def _ffn_rows(xs, mrows, n2g, fgain, w1_ref, w2_ref):
    hbs, g2s = [], []
    for x, mrow in zip(xs, mrows):
        sh2 = mrow[:, 3 * D_MODEL:4 * D_MODEL]
        sc2 = mrow[:, 4 * D_MODEL:5 * D_MODEL]
        g2s.append(mrow[:, 5 * D_MODEL:6 * D_MODEL])
        ms = jnp.mean(x * x, axis=-1, keepdims=True)
        hbs.append((x * lax.rsqrt(ms + EPS) * (n2g * (1.0 + sc2)) + sh2).astype(BF16))
    hb = hbs[0] if len(hbs) == 1 else jnp.concatenate(hbs, axis=0)
    ff = jnp.zeros(hb.shape, F32)
    for c in range(D_FF // FF_CHUNK):
        a = jnp.dot(hb, w1_ref[:, c * FF_CHUNK:(c + 1) * FF_CHUNK], preferred_element_type=F32)
        a = jnp.maximum(a, 0.0)
        ff = ff + jnp.dot((a * a).astype(BF16), w2_ref[c * FF_CHUNK:(c + 1) * FF_CHUNK, :],
                          preferred_element_type=F32)
    outs, r = [], 0
    for x, g2 in zip(xs, g2s):
        x2 = x + g2 * ff[r:r + x.shape[0]]
        r += x.shape[0]
        ms2 = jnp.mean(x2 * x2, axis=-1, keepdims=True)
        outs.append(x2 * lax.rsqrt(ms2 + EPS) * fgain)
    return outs


def _ffn_kernel(x_ref, m_ref, n2g_ref, fg_ref, w1_ref, w2_ref, y_ref):
    y_ref[...] = _ffn_rows([x_ref[...]], [m_ref[0]], n2g_ref[...], fg_ref[...], w1_ref, w2_ref)[0]


def _ffn(x1, m3, m_off, tiles_per_cond, n2g, fgain, w1, w2):
    n = x1.shape[0]

    def m_index(i):
        if tiles_per_cond is None:
            return (m_off, 0, 0)
        return (m_off + i // tiles_per_cond, 0, 0)

    return pl.pallas_call(
        _ffn_kernel,
        grid=(n // FFN_TILE,),
        in_specs=[
            pl.BlockSpec((FFN_TILE, D_MODEL), lambda i: (i, 0)),
            pl.BlockSpec((1, 1, 6 * D_MODEL), m_index),
            _const_spec(n2g.shape),
            _const_spec(fgain.shape),
            _const_spec(w1.shape),
            _const_spec(w2.shape),
        ],
        out_specs=pl.BlockSpec((FFN_TILE, D_MODEL), lambda i: (i, 0)),
        out_shape=jax.ShapeDtypeStruct((n, D_MODEL), F32),
        compiler_params=pltpu.CompilerParams(
            dimension_semantics=("arbitrary",), vmem_limit_bytes=VMEM_LIMIT),
        name="ffn",
    )(x1, m3, n2g, fgain, w1, w2)


def _grid_pos_tables(n_tok):
    quarter = D_MODEL // 4
    omega = (1.0 / (np.float32(POS_BASE) ** (np.arange(quarter, dtype=np.float32) / np.float32(quarter)))
             ).astype(np.float32)

    def emb(n):
        ang = np.arange(n).reshape(-1, 1).astype(np.float32) * omega
        return np.concatenate([np.sin(ang), np.cos(ang)], axis=-1)

    rows = np.repeat(emb(n_tok // GRID_W)[:, None, :], 8, axis=1)
    return jnp.asarray(rows, dtype=F32), jnp.asarray(emb(GRID_W), dtype=F32)


def kernel(x_prompt, x_sample, c, state_hgrn, state_rglru, c_ctx, w_ada, b_ada, norm1_gain, norm2_gain,
           w_in, hg_lb_logits, hg_norm_gain, conv_w, conv_b, lru_wa, lru_ba, lru_wx, lru_bx, lru_lambda,
           w_out, w_ff1, w_ff2, final_gain):
    bp, tp, _ = x_prompt.shape
    bs_, ts, _ = x_sample.shape

    m3, (w_in_b, w_out_b, w1, w2) = _modulation(
        c_ctx, c, w_ada[0], b_ada, [w_in[0], w_out[0], w_ff1[0], w_ff2[0]])

    masks_np, tri_np = _level_tables()
    consts = [
        norm1_gain,
        w_in_b,
        hg_lb_logits.reshape(2, N_DIR * HG_HEADS, HG_DK),
        hg_norm_gain[0].reshape(1, HG_WIDTH),
        jnp.asarray(masks_np),
        jnp.asarray(tri_np, dtype=BF16),
        conv_w[0],
        conv_b,
        lru_wa[0].reshape(N_DIR * LRU_BLOCKS, LRU_BLOCK, LRU_BLOCK),
        lru_wx[0].reshape(N_DIR * LRU_BLOCKS, LRU_BLOCK, LRU_BLOCK),
        lru_ba[0],
        lru_bx[0],
        lru_lambda[0],
        w_out_b,
    ]
    fgain = final_gain.reshape(1, D_MODEL)
    ffn_consts = [norm2_gain, fgain, w1, w2]

    y_sample, done_s = _mixer(x_sample, m3, 1, 1, _grid_pos_tables(ts), consts, ffn_consts,
                              (state_hgrn.reshape(bs_, N_DIR * HG_HEADS, HG_DK, HG_DK),
                               state_rglru.reshape(bs_, N_DIR, LRU_WIDTH)), False)
    guest = None if done_s else (y_sample.reshape(bs_ * ts, D_MODEL), 1, ts)

    y_prompt, done, hs, ls, *rest = _mixer(x_prompt, m3, 0, 0, None, consts, ffn_consts, None, True, guest)
    if not done:
        y_prompt = _ffn(y_prompt.reshape(bp * tp, D_MODEL), m3, 0, None, *ffn_consts)
    if rest:
        y_sample = rest[0]
    elif not done_s:
        y_sample = _ffn(y_sample.reshape(bs_ * ts, D_MODEL), m3, 1, ts // FFN_TILE, *ffn_consts)

    return (y_prompt.reshape(bp, tp, D_MODEL),
            y_sample.reshape(bs_, ts, D_MODEL),
            hs.reshape(bp, 1, N_DIR, HG_HEADS, HG_DK, HG_DK),
            ls.reshape(bp, 1, N_DIR, LRU_WIDTH))
```
